```python
import math
import jax, jax.numpy as jnp
from jax import lax
import numpy as np

D_MODEL = 1024
BATCH = 8
SEQ = 4096
DEPTH = 1
DEC_BATCH = 128
DEC_SEQ = 1
PAST_LEN = 8192
PAGE_SIZE = 128

HEAD_DIM = 64
HEADS_PER_GROUP = 4
DILATION_GROUPS = ((128, 1), (512, 4), (2048, 16))
N_GROUPS = len(DILATION_GROUPS)
N_ATTN_HEADS = HEADS_PER_GROUP * N_GROUPS
ATTN_WIDTH = N_ATTN_HEADS * HEAD_DIM
ATTN_OUT = HEADS_PER_GROUP * HEAD_DIM
ROPE_THETA = 10000.0
SSM_WIDTH = D_MODEL // 4
SSM_GROUP = 16
SSM_GROUPS = SSM_WIDTH // SSM_GROUP
SSM_STATE = 64
IN_WIDTH = 3 * ATTN_WIDTH + SSM_WIDTH
N_EXPERTS = 64
TOP_K = 8
EXPERT_FF = D_MODEL // 4
SHARED_FF = EXPERT_FF
ROUTED_SCALE = 2.5
MOE_BLOCK = 128
PLE_DIM = 256
DN_ALPHA = (2.0 * DEPTH) ** 0.25
DN_BETA = (8.0 * DEPTH) ** -0.25
LN_EPS = 1e-5

kernel_name = 'hybrid_s5_dilated_attn_moe_step'


def layer_norm(x, g, b):
    xf = x.astype(jnp.float32)
    mu = jnp.mean(xf, axis=-1, keepdims=True)
    var = jnp.mean(jnp.square(xf - mu), axis=-1, keepdims=True)
    return ((xf - mu) * lax.rsqrt(var + LN_EPS) * g + b).astype(x.dtype)


def rope(x, pos):
    half = HEAD_DIM // 2
    inv = ROPE_THETA ** (-jnp.arange(half, dtype=jnp.float32) / half)
    ang = pos.astype(jnp.float32)[:, None] * inv[None, :]
    cos = jnp.cos(ang)[None, :, None, :]
    sin = jnp.sin(ang)[None, :, None, :]
    xf = x.astype(jnp.float32)
    x1, x2 = xf[..., :half], xf[..., half:]
    return jnp.concatenate([x1 * cos - x2 * sin, x2 * cos + x1 * sin], axis=-1).astype(x.dtype)


def dilated_attn_prompt(q, k, v, window, dilation):
    B, S, H, Dh = q.shape
    nback = window // dilation
    span = nback * dilation
    s_pad = -(-S // span) * span
    nb = s_pad // span
    padw = ((0, 0), (0, s_pad - S), (0, 0), (0, 0))

    def blocks(t):
        return jnp.pad(t, padw).reshape(B, nb, nback, dilation, H, Dh)

    def with_prev(t):
        prev = jnp.concatenate([jnp.zeros_like(t[:, :1]), t[:, :-1]], axis=1)
        return jnp.concatenate([prev, t], axis=2)

    qb = blocks(q)
    kk = with_prev(blocks(k))
    vv = with_prev(blocks(v))
    logits = jnp.einsum('bcqrhd,bckrhd->bcrhqk', qb, kk, preferred_element_type=jnp.float32) * (Dh ** -0.5)
    qi = jnp.arange(nback)[:, None]
    kj = jnp.arange(2 * nback)[None, :]
    dist = qi + nback - kj
    band = (dist >= 0) & (dist <= nback)
    first = jnp.arange(nb)[:, None, None] > 0
    mask = band[None] & (first | (kj >= nback)[None])
    logits = jnp.where(mask[None, :, None, None], logits, -jnp.inf)
    mx = jnp.max(logits, axis=-1)
    p = jnp.exp(logits - mx[..., None])
    den = jnp.sum(p, axis=-1)
    num = jnp.einsum('bcrhqk,bckrhd->bcqrhd', p, vv.astype(jnp.float32))
    num = num.reshape(B, s_pad, H, Dh)[:, :S]
    den = jnp.transpose(den, (0, 1, 4, 2, 3)).reshape(B, s_pad, H)[:, :S]
    mx = jnp.transpose(mx, (0, 1, 4, 2, 3)).reshape(B, s_pad, H)[:, :S]
    return num, den, mx


def dilated_attn_sample(q, k_all, v_all, window, dilation):
    B, T, H, Dh = q.shape
    L = k_all.shape[1] - T
    nback = window // dilation
    idx = L + jnp.arange(T)[:, None] - dilation * jnp.arange(nback + 1)[None, :]
    valid = idx >= 0
    idx_c = jnp.maximum(idx, 0)
    kg = k_all[:, idx_c]
    vg = v_all[:, idx_c]
    logits = jnp.einsum('bthd,btjhd->bthj', q, kg, preferred_element_type=jnp.float32) * (Dh ** -0.5)
    logits = jnp.where(valid[None, :, None, :], logits, -jnp.inf)
    mx = jnp.max(logits, axis=-1)
    p = jnp.exp(logits - mx[..., None])
    den = jnp.sum(p, axis=-1)
    num = jnp.einsum('bthj,btjhd->bthd', p, vg.astype(jnp.float32))
    return num, den, mx


def merge_dilations(parts):
    m = parts[0][2]
    for _, _, mx in parts[1:]:
        m = jnp.maximum(m, mx)
    num = 0.0
    den = 0.0
    for n_g, d_g, mx in parts:
        scale = jnp.exp(mx - m)
        num = num + n_g * scale[..., None]
        den = den + d_g * scale
    o = num / den[..., None]
    B, T = o.shape[0], o.shape[1]
    return o.reshape(B, T, ATTN_OUT)


def attend_prompt(q, k, v):
    parts, kv_rows = [], []
    S = q.shape[1]
    for g, (win, dil) in enumerate(DILATION_GROUPS):
        sl = slice(g * HEADS_PER_GROUP, (g + 1) * HEADS_PER_GROUP)
        parts.append(dilated_attn_prompt(q[:, :, sl], k[:, :, sl], v[:, :, sl], win, dil))
        keep = min(win, S)
        kv_rows.append(jnp.stack([k[:, S - keep:, sl], v[:, S - keep:, sl]], axis=2))
    return merge_dilations(parts), kv_rows


def make_attend_sample(layer_caches):
    def attend(q, k, v):
        parts, kv_rows = [], []
        for g, (win, dil) in enumerate(DILATION_GROUPS):
            sl = slice(g * HEADS_PER_GROUP, (g + 1) * HEADS_PER_GROUP)
            kv_new = jnp.stack([k[:, :, sl], v[:, :, sl]], axis=2)
            kv_all = jnp.concatenate([layer_caches[g].astype(kv_new.dtype), kv_new], axis=1)
            parts.append(dilated_attn_sample(q[:, :, sl], kv_all[:, :, 0], kv_all[:, :, 1], win, dil))
            kv_rows.append(kv_new)
        return merge_dilations(parts), kv_rows
    return attend


def s5_scan(u, h0, w):
    B, T, _ = u.shape
    f32 = jnp.float32
    uf = u.astype(f32).reshape(B, T, SSM_GROUPS, SSM_GROUP)
    dt = jnp.exp(w['log_dt'].astype(f32))[:, None]
    ar = w['a_re'].astype(f32)
    ai = w['a_im'].astype(f32)
    mag = jnp.exp(ar * dt)
    abar_re = mag * jnp.cos(ai * dt)
    abar_im = mag * jnp.sin(ai * dt)
    a2 = ar * ar + ai * ai
    nr = abar_re - 1.0
    coef_re = (nr * ar + abar_im * ai) / a2
    coef_im = (abar_im * ar - nr * ai) / a2
    b_re = w['b_re'].astype(f32)
    b_im = w['b_im'].astype(f32)
    bb_re = coef_re[..., None] * b_re - coef_im[..., None] * b_im
    bb_im = coef_re[..., None] * b_im + coef_im[..., None] * b_re
    bu_re = jnp.einsum('btgc,gpc->btgp', uf, bb_re)
    bu_im = jnp.einsum('btgc,gpc->btgp', uf, bb_im)
    h0_re = h0[..., 0].astype(f32)
    h0_im = h0[..., 1].astype(f32)
    bu_re = bu_re.at[:, 0].add(abar_re * h0_re - abar_im * h0_im)
    bu_im = bu_im.at[:, 0].add(abar_re * h0_im + abar_im * h0_re)
    a_t_re = jnp.broadcast_to(abar_re, bu_re.shape)
    a_t_im = jnp.broadcast_to(abar_im, bu_im.shape)

    def combine(e1, e2):
        a1r, a1i, b1r, b1i = e1
        a2r, a2i, b2r, b2i = e2
        return (a1r * a2r - a1i * a2i, a1r * a2i + a1i * a2r,
                a2r * b1r - a2i * b1i + b2r, a2r * b1i + a2i * b1r + b2i)

    _, _, h_re, h_im = lax.associative_scan(combine, (a_t_re, a_t_im, bu_re, bu_im), axis=1)
    y = (jnp.einsum('btgp,gcp->btgc', h_re, w['c_re'].astype(f32))
         - jnp.einsum('btgp,gcp->btgc', h_im, w['c_im'].astype(f32)))
    y = y.reshape(B, T, SSM_WIDTH) + w['d_skip'].astype(f32) * uf.reshape(B, T, SSM_WIDTH)
    h_last = jnp.stack([h_re[:, -1], h_im[:, -1]], axis=-1)
    return y, h_last


def moe(x, w):
    B, T, D = x.shape
    n = B * T
    xf = x.reshape(n, D)
    scores = jax.nn.sigmoid(jnp.einsum('nd,de->ne', xf, w['w_router'], preferred_element_type=jnp.float32))
    _, top_idx = lax.top_k(scores + w['router_bias'].astype(jnp.float32), TOP_K)
    top_s = jnp.take_along_axis(scores, top_idx, axis=-1)
    gates = top_s / jnp.sum(top_s, axis=-1, keepdims=True) * ROUTED_SCALE
    flat_e = top_idx.reshape(-1)
    flat_g = gates.reshape(-1)
    order = jnp.argsort(flat_e)
    se = flat_e[order]
    stok = (order // TOP_K).astype(jnp.int32)
    sg = flat_g[order]
    counts = jnp.zeros((N_EXPERTS,), jnp.int32).at[flat_e].add(1)
    start = jnp.cumsum(counts) - counts
    pcounts = (counts + MOE_BLOCK - 1) // MOE_BLOCK * MOE_BLOCK
    pend = jnp.cumsum(pcounts)
    pstart = pend - pcounts
    dest = pstart[se] + (jnp.arange(n * TOP_K, dtype=jnp.int32) - start[se])
    n_blocks = -(-(n * TOP_K) // MOE_BLOCK) + N_EXPERTS
    rows = n_blocks * MOE_BLOCK
    row_tok = jnp.full((rows,), n, jnp.int32).at[dest].set(stok)
    row_gate = jnp.zeros((rows,), jnp.float32).at[dest].set(sg)
    block_exp = jnp.minimum(jnp.searchsorted(pend, jnp.arange(n_blocks, dtype=jnp.int32) * MOE_BLOCK, side='right'),
                            N_EXPERTS - 1)
    x_pad = jnp.concatenate([xf, jnp.zeros((1, D), xf.dtype)], axis=0)
    w1, w3, w2 = w['w1'], w['w3'], w['w2']

    def body(b, acc):
        tok = lax.dynamic_slice(row_tok, (b * MOE_BLOCK,), (MOE_BLOCK,))
        g = lax.dynamic_slice(row_gate, (b * MOE_BLOCK,), (MOE_BLOCK,))
        e = block_exp[b]
        xb = x_pad[tok]
        h = jax.nn.silu(xb @ w1[e]) * (xb @ w3[e])
        yb = (h @ w2[e]).astype(jnp.float32)
        return acc.at[tok].add(yb * g[:, None])

    acc = lax.fori_loop(0, n_blocks, body, jnp.zeros((n + 1, D), jnp.float32))
    shared = (jax.nn.silu(xf @ w['ws1']) * (xf @ w['ws3'])) @ w['ws2']
    return (acc[:n] + shared).reshape(B, T, D).astype(x.dtype)


def trunk_layer(x, p_l, pos, attend, h0, w):
    B, T, _ = x.shape
    proj = x @ w['w_in']
    a = ATTN_WIDTH
    q = rope(proj[..., :a].reshape(B, T, N_ATTN_HEADS, HEAD_DIM), pos)
    k = rope(proj[..., a:2 * a].reshape(B, T, N_ATTN_HEADS, HEAD_DIM), pos)
    v = proj[..., 2 * a:3 * a].reshape(B, T, N_ATTN_HEADS, HEAD_DIM)
    u = proj[..., 3 * a:]
    attn_o, kv_rows = attend(q, k, v)
    ssm_y, h_last = s5_scan(u, h0, w)
    s = jax.nn.gelu(ssm_y)
    s = s * jax.nn.sigmoid(s @ w['w_glu'] + w['b_glu'])
    gates = jax.nn.sigmoid(x @ w['w_gate'] + w['b_gate'])
    merged = (gates[..., :D_MODEL] * (attn_o @ w['w_attn_br'])
              + gates[..., D_MODEL:] * (s @ w['w_ssm_br']))
    mix = (merged @ w['w_out']).astype(x.dtype)
    x = layer_norm(DN_ALPHA * x + mix, w['ln1_g'], w['ln1_b'])
    ff = moe(x, w)
    ple = (jax.nn.sigmoid(x @ w['w_ple_gate']) * (p_l @ w['w_ple'])).astype(x.dtype)
    x = layer_norm(DN_ALPHA * x + ff + ple, w['ln2_g'], w['ln2_b'])
    return x, kv_rows, h_last


def setup_inputs(seed: int = 0) -> dict:
    key = jax.random.key(seed)
    ks = iter(jax.random.split(key, 48))
    f32 = jnp.float32

    def nrm(shape, scale):
        return jax.random.normal(next(ks), shape, f32) * scale

    cache_shape = lambda win: (DEPTH, DEC_BATCH, min(win, PAST_LEN), 2, HEADS_PER_GROUP, HEAD_DIM)
    inp = {}
    inp['x_prompt'] = nrm((BATCH, SEQ, D_MODEL), 1.0)
    inp['x_sample'] = nrm((DEC_BATCH, DEC_SEQ, D_MODEL), 1.0)
    inp['cache_kv_w128'] = nrm(cache_shape(128), 1.0)
    inp['cache_kv_w512'] = nrm(cache_shape(512), 1.0)
    inp['cache_kv_w2048'] = nrm(cache_shape(2048), 1.0)
    inp['state_ssm'] = nrm((DEPTH, DEC_BATCH, SSM_GROUPS, SSM_STATE, 2), 0.5)
    inp['p_prompt'] = nrm((DEPTH, BATCH, SEQ, PLE_DIM), 1.0)
    inp['p_sample'] = nrm((DEPTH, DEC_BATCH, DEC_SEQ, PLE_DIM), 1.0)
    inp['w_in'] = nrm((DEPTH, D_MODEL, IN_WIDTH), D_MODEL ** -0.5)
    inp['a_re'] = -0.5 + nrm((DEPTH, SSM_GROUPS, SSM_STATE), 0.01)
    inp['a_im'] = math.pi * jnp.arange(SSM_STATE, dtype=f32) + nrm((DEPTH, SSM_GROUPS, SSM_STATE), 0.01)
    inp['log_dt'] = jax.random.uniform(next(ks), (DEPTH, SSM_GROUPS), f32, math.log(1e-3), math.log(1e-1))
    inp['b_re'] = nrm((DEPTH, SSM_GROUPS, SSM_STATE, SSM_GROUP), (2.0 * SSM_GROUP) ** -0.5)
    inp['b_im'] = nrm((DEPTH, SSM_GROUPS, SSM_STATE, SSM_GROUP), (2.0 * SSM_GROUP) ** -0.5)
    inp['c_re'] = nrm((DEPTH, SSM_GROUPS, SSM_GROUP, SSM_STATE), (2.0 * SSM_STATE) ** -0.5)
    inp['c_im'] = nrm((DEPTH, SSM_GROUPS, SSM_GROUP, SSM_STATE), (2.0 * SSM_STATE) ** -0.5)
    inp['d_skip'] = nrm((DEPTH, SSM_WIDTH), 1.0)
    inp['w_glu'] = nrm((DEPTH, SSM_WIDTH, SSM_WIDTH), SSM_WIDTH ** -0.5)
    inp['b_glu'] = nrm((DEPTH, SSM_WIDTH), 0.02)
    inp['w_attn_br'] = nrm((DEPTH, ATTN_OUT, D_MODEL), ATTN_OUT ** -0.5)
    inp['w_ssm_br'] = nrm((DEPTH, SSM_WIDTH, D_MODEL), SSM_WIDTH ** -0.5)
    inp['w_gate'] = nrm((DEPTH, D_MODEL, 2 * D_MODEL), D_MODEL ** -0.5)
    inp['b_gate'] = nrm((DEPTH, 2 * D_MODEL), 0.02)
    inp['w_out'] = nrm((DEPTH, D_MODEL, D_MODEL), D_MODEL ** -0.5 * DN_BETA)
    inp['ln1_g'] = 1.0 + nrm((DEPTH, D_MODEL), 0.01)
    inp['ln1_b'] = nrm((DEPTH, D_MODEL), 0.01)
    inp['w_router'] = nrm((DEPTH, D_MODEL, N_EXPERTS), D_MODEL ** -0.5)
    inp['router_bias'] = nrm((DEPTH, N_EXPERTS), 0.01)
    inp['w1'] = nrm((DEPTH, N_EXPERTS, D_MODEL, EXPERT_FF), D_MODEL ** -0.5)
    inp['w3'] = nrm((DEPTH, N_EXPERTS, D_MODEL, EXPERT_FF), D_MODEL ** -0.5)
    inp['w2'] = nrm((DEPTH, N_EXPERTS, EXPERT_FF, D_MODEL), EXPERT_FF ** -0.5 * DN_BETA)
    inp['ws1'] = nrm((DEPTH, D_MODEL, SHARED_FF), D_MODEL ** -0.5)
    inp['ws3'] = nrm((DEPTH, D_MODEL, SHARED_FF), D_MODEL ** -0.5)
    inp['ws2'] = nrm((DEPTH, SHARED_FF, D_MODEL), SHARED_FF ** -0.5 * DN_BETA)
    inp['w_ple_gate'] = nrm((DEPTH, D_MODEL, D_MODEL), D_MODEL ** -0.5)
    inp['w_ple'] = nrm((DEPTH, PLE_DIM, D_MODEL), PLE_DIM ** -0.5)
    inp['ln2_g'] = 1.0 + nrm((DEPTH, D_MODEL), 0.01)
    inp['ln2_b'] = nrm((DEPTH, D_MODEL), 0.01)
    return inp


def reference(x_prompt, x_sample, cache_kv_w128, cache_kv_w512, cache_kv_w2048, state_ssm, p_prompt, p_sample,
              w_in, a_re, a_im, log_dt, b_re, b_im, c_re, c_im, d_skip, w_glu, b_glu, w_attn_br, w_ssm_br,
              w_gate, b_gate, w_out, ln1_g, ln1_b, w_router, router_bias, w1, w3, w2, ws1, ws3, ws2,
              w_ple_gate, w_ple, ln2_g, ln2_b):
    caches = (cache_kv_w128, cache_kv_w512, cache_kv_w2048)
    pos_prompt = jnp.arange(x_prompt.shape[1], dtype=jnp.int32)
    pos_sample = PAST_LEN + jnp.arange(x_sample.shape[1], dtype=jnp.int32)
    yp, ys = x_prompt, x_sample
    kv_p = [[] for _ in range(N_GROUPS)]
    kv_s = [[] for _ in range(N_GROUPS)]
    h_p, h_s = [], []
    for l in range(DEPTH):
        w = {'w_in': w_in[l], 'a_re': a_re[l], 'a_im': a_im[l], 'log_dt': log_dt[l], 'b_re': b_re[l],
             'b_im': b_im[l], 'c_re': c_re[l], 'c_im': c_im[l], 'd_skip': d_skip[l], 'w_glu': w_glu[l],
             'b_glu': b_glu[l], 'w_attn_br': w_attn_br[l], 'w_ssm_br': w_ssm_br[l], 'w_gate': w_gate[l],
             'b_gate': b_gate[l], 'w_out': w_out[l], 'ln1_g': ln1_g[l], 'ln1_b': ln1_b[l],
             'w_router': w_router[l], 'router_bias': router_bias[l], 'w1': w1[l], 'w3': w3[l], 'w2': w2[l],
             'ws1': ws1[l], 'ws3': ws3[l], 'ws2': ws2[l], 'w_ple_gate': w_ple_gate[l], 'w_ple': w_ple[l],
             'ln2_g': ln2_g[l], 'ln2_b': ln2_b[l]}
        h0_p = jnp.zeros((yp.shape[0], SSM_GROUPS, SSM_STATE, 2), jnp.float32)
        yp, rows_p, hp = trunk_layer(yp, p_prompt[l], pos_prompt, attend_prompt, h0_p, w)
        attend_s = make_attend_sample([c[l] for c in caches])
        ys, rows_s, hs = trunk_layer(ys, p_sample[l], pos_sample, attend_s, state_ssm[l], w)
        for g in range(N_GROUPS):
            kv_p[g].append(rows_p[g])
            kv_s[g].append(rows_s[g])
        h_p.append(hp)
        h_s.append(hs)
    return (yp, ys, jnp.stack(kv_p[0]), jnp.stack(kv_s[0]), jnp.stack(kv_p[1]), jnp.stack(kv_s[1]),
            jnp.stack(kv_p[2]), jnp.stack(kv_s[2]), jnp.stack(h_p), jnp.stack(h_s))
```

```python
import functools
import math

import jax
import jax.numpy as jnp
from jax import lax
from jax.experimental import pallas as pl
from jax.experimental.pallas import tpu as pltpu

F32 = jnp.float32
BF16 = jnp.bfloat16

D_MODEL = 1024
HEAD_DIM = 64
HEADS_PER_GROUP = 4
DILATION_GROUPS = ((128, 1), (512, 4), (2048, 16))
N_BACK = 128
GROUP_WIDTH = HEADS_PER_GROUP * HEAD_DIM
ATTN_WIDTH = 3 * GROUP_WIDTH
ROPE_THETA = 10000.0
SSM_WIDTH = 256
SSM_GROUP = 16
SSM_GROUPS = 16
SSM_STATE = 64
SSM_LANES = SSM_GROUPS * SSM_STATE
IN_WIDTH = 3 * ATTN_WIDTH + SSM_WIDTH
N_EXPERTS = 64
TOP_K = 8
EXPERT_FF = 256
ROUTED_SCALE = 2.5
PLE_DIM = 256
DEPTH = 1
PAST_LEN = 8192
DN_ALPHA = (2.0 * DEPTH) ** 0.25
LN_EPS = 1e-5

LANES = 128
ATTN_CHUNK = 2048
VMEM_LIMIT = 56 * 1024 * 1024


def _params(semantics):
    return pltpu.CompilerParams(dimension_semantics=semantics, vmem_limit_bytes=VMEM_LIMIT)


def _full(shape):
    return pl.BlockSpec(shape, lambda *_: (0,) * len(shape))


def _in_proj_kernel(x_ref, w_ref, cos_ref, sina_ref, sinb_ref, q_ref, k_ref, v_ref, u_ref):
    xb = x_ref[...].astype(BF16)
    cos = cos_ref[...]
    sin_a = sina_ref[...]
    sin_b = sinb_ref[...]

    def rope_store(col0, out_ref, scale):
        t = jnp.dot(xb, w_ref[:, col0:col0 + ATTN_WIDTH], preferred_element_type=F32)
        for c in range(ATTN_WIDTH // LANES):
            xc = t[:, c * LANES:(c + 1) * LANES]
            r = xc * cos + pltpu.roll(xc, LANES - 32, 1) * sin_a + pltpu.roll(xc, 32, 1) * sin_b
            out_ref[:, c * LANES:(c + 1) * LANES] = r * scale if scale != 1.0 else r

    rope_store(0, q_ref, HEAD_DIM ** -0.5)
    rope_store(ATTN_WIDTH, k_ref, 1.0)
    v_ref[...] = jnp.dot(xb, w_ref[:, 2 * ATTN_WIDTH:3 * ATTN_WIDTH], preferred_element_type=F32)
    u_ref[...] = jnp.dot(xb, w_ref[:, 3 * ATTN_WIDTH:], preferred_element_type=F32)


def _in_proj(x, w_in_bf, rope_tabs, rows_per_seq, tm):
    n = x.shape[0]
    tiles_per_seq = rows_per_seq // tm
    n_seq = n // rows_per_seq
    tab_tiles = rope_tabs[0].shape[0] // tm
    tab_spec = pl.BlockSpec((tm, LANES), lambda i: (i % tab_tiles, 0))
    row_spec = pl.BlockSpec((tm, ATTN_WIDTH), lambda i: (i, 0))
    return pl.pallas_call(
        _in_proj_kernel,
        grid=(n // tm,),
        in_specs=[pl.BlockSpec((tm, D_MODEL), lambda i: (i, 0)), _full((D_MODEL, IN_WIDTH)),
                  tab_spec, tab_spec, tab_spec],
        out_specs=[row_spec, row_spec, row_spec,
                   pl.BlockSpec((tm, SSM_WIDTH), lambda i: (i % tiles_per_seq, i // tiles_per_seq))],
        out_shape=[jax.ShapeDtypeStruct((n, ATTN_WIDTH), F32)] * 3
        + [jax.ShapeDtypeStruct((rows_per_seq, n_seq * SSM_WIDTH), F32)],
        compiler_params=_params(("parallel",)),
        name="in_proj",
    )(x, w_in_bf, *rope_tabs)


def _rope_tables(pos):
    half = HEAD_DIM // 2
    inv = ROPE_THETA ** (-jnp.arange(half, dtype=F32) / half)
    ang = pos.astype(F32)[:, None] * inv[None, :]
    cos = jnp.tile(jnp.cos(ang), (1, LANES // half))
    sin = jnp.tile(jnp.sin(ang), (1, LANES // half))
    first_half = (jnp.arange(LANES) % HEAD_DIM) < half
    sin_a = jnp.where(first_half[None, :], -sin, 0.0)
    sin_b = jnp.where(first_half[None, :], 0.0, sin)
    return cos, sin_a, sin_b


def _attn_prompt_kernel(q_ref, kp_ref, kc_ref, vp_ref, vc_ref, o_ref, kk_sc, vv_sc, m_sc, l_sc, acc_sc):
    c = pl.program_id(1)
    g = pl.program_id(3)
    ch = ATTN_CHUNK
    kk_sc[0:ch, :] = kp_ref[...]
    kk_sc[ch:2 * ch, :] = kc_ref[...]
    vv_sc[0:ch, :] = vp_ref[...]
    vv_sc[ch:2 * ch, :] = vc_ref[...]

    @pl.when(g == 0)
    def _():
        m_sc[...] = jnp.full(m_sc.shape, -jnp.inf, F32)
        l_sc[...] = jnp.zeros(l_sc.shape, F32)
        acc_sc[...] = jnp.zeros(acc_sc.shape, F32)

    qi = lax.broadcasted_iota(jnp.int32, (N_BACK, 2 * N_BACK), 0)
    kj = lax.broadcasted_iota(jnp.int32, (N_BACK, 2 * N_BACK), 1)
    dist = qi + N_BACK - kj
    band = jnp.where(dist >= 0, jnp.where(dist <= N_BACK, 0.0, -jnp.inf), -jnp.inf)
    band_first = jnp.where(kj >= N_BACK, band, -jnp.inf)

    def group_body(d):
        span = N_BACK * d

        def rows(ref_start, size):
            if d == 1:
                return pl.ds(ref_start, size)
            return pl.ds(ref_start, size, stride=d)

        def body(idx, carry):
            s = idx // d
            r = idx % d
            q0 = s * span + r
            k0 = ch + (s - 1) * span + r
            first = jnp.logical_and(c == 0, s == 0)
            q = q_ref[rows(q0, N_BACK), :]
            k = kk_sc[rows(k0, 2 * N_BACK), :]
            v = vv_sc[rows(k0, 2 * N_BACK), :]
            mask = jnp.where(first, band_first, band)
            m_parts, l_parts, pv_parts = [], [], []
            for h in range(LANES // HEAD_DIM):
                cols = slice(h * HEAD_DIM, (h + 1) * HEAD_DIM)
                logits = lax.dot_general(q[:, cols].astype(BF16), k[:, cols].astype(BF16),
                                         (((1,), (1,)), ((), ())), preferred_element_type=F32)
                logits = logits + mask
                m_h = jnp.max(logits, axis=1, keepdims=True)
                p = jnp.exp(logits - m_h)
                l_h = jnp.sum(p, axis=1, keepdims=True)
                pv = jnp.dot(p.astype(BF16), v[:, cols].astype(BF16), preferred_element_type=F32)
                m_parts.append(jnp.broadcast_to(m_h, (N_BACK, HEAD_DIM)))
                l_parts.append(jnp.broadcast_to(l_h, (N_BACK, HEAD_DIM)))
                pv_parts.append(pv)
            m_g = jnp.concatenate(m_parts, axis=1)
            l_g = jnp.concatenate(l_parts, axis=1)
            pv_g = jnp.concatenate(pv_parts, axis=1)
            m_old = m_sc[rows(q0, N_BACK), :]
            m_new = jnp.maximum(m_old, m_g)
            a_old = jnp.exp(m_old - m_new)
            a_new = jnp.exp(m_g - m_new)
            m_sc[rows(q0, N_BACK), :] = m_new
            l_sc[rows(q0, N_BACK), :] = l_sc[rows(q0, N_BACK), :] * a_old + l_g * a_new
            acc_sc[rows(q0, N_BACK), :] = acc_sc[rows(q0, N_BACK), :] * a_old + pv_g * a_new
            return carry

        lax.fori_loop(0, ch // N_BACK, body, 0)

    for gi, (_, d) in enumerate(DILATION_GROUPS):
        pl.when(g == gi)(functools.partial(group_body, d))

    @pl.when(g == len(DILATION_GROUPS) - 1)
    def _():
        o_ref[...] = acc_sc[...] / l_sc[...]


def _attn_prompt(q, k, v, batch, seq):
    ch = ATTN_CHUNK
    cps = seq // ch
    n = batch * seq
    pairs = GROUP_WIDTH // LANES
    cur = lambda b, c, hp, g: (b * cps + c, g * pairs + hp)
    prev = lambda b, c, hp, g: (b * cps + jnp.maximum(c - 1, 0), g * pairs + hp)
    blk = (ch, LANES)
    return pl.pallas_call(
        _attn_prompt_kernel,
        grid=(batch, cps, pairs, len(DILATION_GROUPS)),
        in_specs=[pl.BlockSpec(blk, cur), pl.BlockSpec(blk, prev), pl.BlockSpec(blk, cur),
                  pl.BlockSpec(blk, prev), pl.BlockSpec(blk, cur)],
        out_specs=pl.BlockSpec(blk, lambda b, c, hp, g: (b * cps + c, hp)),
        out_shape=jax.ShapeDtypeStruct((n, GROUP_WIDTH), F32),
        scratch_shapes=[pltpu.VMEM((2 * ch, LANES), F32), pltpu.VMEM((2 * ch, LANES), F32),
                        pltpu.VMEM(blk, F32), pltpu.VMEM(blk, F32), pltpu.VMEM(blk, F32)],
        compiler_params=_params(("parallel", "parallel", "parallel", "arbitrary")),
        name="attn_prompt",
    )(q, k, k, v, v)


def _attn_sample_kernel(q_ref, k_ref, v_ref, c0_ref, c1_ref, c2_ref, ee_ref, o_ref):
    ee = ee_ref[...]
    bt = q_ref.shape[0]

    def head_sum(t):
        return jnp.dot(t, ee, preferred_element_type=F32, precision=lax.Precision.HIGHEST)

    m_run = l_run = n_run = None
    for g, c_ref in enumerate((c0_ref, c1_ref, c2_ref)):
        cols = slice(g * GROUP_WIDTH, (g + 1) * GROUP_WIDTH)
        q = q_ref[:, cols]
        k_new = k_ref[:, cols]
        v_new = v_ref[:, cols]
        k_c = c_ref[:, :, 0:GROUP_WIDTH]
        v_c = c_ref[:, :, GROUP_WIDTH:2 * GROUP_WIDTH]
        s_c = head_sum((k_c * q[:, None, :]).reshape(bt * N_BACK, GROUP_WIDTH))
        s_c = s_c.reshape(bt, N_BACK, GROUP_WIDTH)
        s_new = head_sum(k_new * q)
        m_g = jnp.maximum(jnp.max(s_c, axis=1), s_new)
        p_c = jnp.exp(s_c - m_g[:, None, :])
        p_new = jnp.exp(s_new - m_g)
        l_g = jnp.sum(p_c, axis=1) + p_new
        n_g = jnp.sum(p_c * v_c, axis=1) + p_new * v_new
        if m_run is None:
            m_run, l_run, n_run = m_g, l_g, n_g
        else:
            m_new = jnp.maximum(m_run, m_g)
            a_old = jnp.exp(m_run - m_new)
            a_new = jnp.exp(m_g - m_new)
            l_run = l_run * a_old + l_g * a_new
            n_run = n_run * a_old + n_g * a_new
            m_run = m_new
    o_ref[...] = n_run / l_run


def _attn_sample(q, k, v, caches, bt):
    b = q.shape[0]
    views, specs = [], []
    for cache, (win, d) in zip(caches, DILATION_GROUPS):
        assert cache.shape[1] == win == N_BACK * d
        views.append(cache.reshape(b, N_BACK, d * 2 * GROUP_WIDTH))
        specs.append(pl.BlockSpec((bt, N_BACK, 2 * GROUP_WIDTH), lambda i: (i, 0, 0)))
    head = jnp.arange(GROUP_WIDTH) // HEAD_DIM
    ee = (head[:, None] == head[None, :]).astype(F32)
    row_spec = pl.BlockSpec((bt, ATTN_WIDTH), lambda i: (i, 0))
    return pl.pallas_call(
        _attn_sample_kernel,
        grid=(b // bt,),
        in_specs=[row_spec, row_spec, row_spec] + specs + [_full((GROUP_WIDTH, GROUP_WIDTH))],
        out_specs=pl.BlockSpec((bt, GROUP_WIDTH), lambda i: (i, 0)),
        out_shape=jax.ShapeDtypeStruct((b, GROUP_WIDTH), F32),
        compiler_params=_params(("parallel",)),
        name="attn_sample",
    )(q, k, v, *views, ee)


def _s5_scan_kernel(u_ref, bmat_ref, cmat_ref, are_ref, aim_ref, d_ref, h0re_ref, h0im_ref,
                    y_ref, hre_ref, him_ref, hist_sc, *, bg, steps):
    t_chunk = pl.program_id(0)

    @pl.when(t_chunk == 0)
    def _():
        hre_ref[...] = h0re_ref[...]
        him_ref[...] = h0im_ref[...]

    u = u_ref[...]
    hist_sc[...] = jnp.dot(u.astype(BF16), bmat_ref[...], preferred_element_type=F32)
    a_re = jnp.broadcast_to(are_ref[...], (bg, SSM_LANES))
    a_im = jnp.broadcast_to(aim_ref[...], (bg, SSM_LANES))

    def step(t, carry):
        h_re, h_im = carry
        rows = pl.ds(pl.multiple_of(t * bg, bg), bg)
        n_re = a_re * h_re - a_im * h_im + hist_sc[rows, 0:SSM_LANES]
        n_im = a_re * h_im + a_im * h_re + hist_sc[rows, SSM_LANES:2 * SSM_LANES]
        hist_sc[rows, 0:SSM_LANES] = n_re
        hist_sc[rows, SSM_LANES:2 * SSM_LANES] = n_im
        return n_re, n_im

    h_re, h_im = lax.fori_loop(0, steps, step, (hre_ref[...], him_ref[...]))
    hre_ref[...] = h_re
    him_ref[...] = h_im
    y_ref[...] = (jnp.dot(hist_sc[...].astype(BF16), cmat_ref[...], preferred_element_type=F32)
                  + d_ref[...] * u)


def _s5_scan(u_tb, ssm, h0_re, h0_im, bg, steps):
    rows = u_tb.shape[0]
    blk = steps * bg
    kern = functools.partial(_s5_scan_kernel, bg=bg, steps=steps)
    state_spec = _full((bg, SSM_LANES))
    return pl.pallas_call(
        kern,
        grid=(rows // blk,),
        in_specs=[pl.BlockSpec((blk, SSM_WIDTH), lambda i: (i, 0)),
                  _full((SSM_WIDTH, 2 * SSM_LANES)), _full((2 * SSM_LANES, SSM_WIDTH)),
                  _full((1, SSM_LANES)), _full((1, SSM_LANES)), _full((1, SSM_WIDTH)),
                  state_spec, state_spec],
        out_specs=[pl.BlockSpec((blk, SSM_WIDTH), lambda i: (i, 0)), state_spec, state_spec],
        out_shape=[jax.ShapeDtypeStruct((rows, SSM_WIDTH), F32),
                   jax.ShapeDtypeStruct((bg, SSM_LANES), F32), jax.ShapeDtypeStruct((bg, SSM_LANES), F32)],
        scratch_shapes=[pltpu.VMEM((blk, 2 * SSM_LANES), F32)],
        compiler_params=_params(("arbitrary",)),
        name="s5_scan",
    )(u_tb, ssm["bmat"], ssm["cmat"], ssm["a_re"], ssm["a_im"], ssm["d_skip"], h0_re, h0_im)


def _s5_params(a_re, a_im, log_dt, b_re, b_im, c_re, c_im, d_skip):
    dt = jnp.exp(log_dt)[:, None]
    mag = jnp.exp(a_re * dt)
    abar_re = mag * jnp.cos(a_im * dt)
    abar_im = mag * jnp.sin(a_im * dt)
    a2 = a_re * a_re + a_im * a_im
    nr = abar_re - 1.0
    coef_re = (nr * a_re + abar_im * a_im) / a2
    coef_im = (abar_im * a_re - nr * a_im) / a2
    bb_re = coef_re[..., None] * b_re - coef_im[..., None] * b_im
    bb_im = coef_re[..., None] * b_im + coef_im[..., None] * b_re
    eye = jnp.eye(SSM_GROUPS, dtype=F32)
    to_b = lambda t: jnp.einsum("gpc,gh->gchp", t, eye).reshape(SSM_WIDTH, SSM_LANES)
    to_c = lambda t: jnp.einsum("gcp,gh->gphc", t, eye).reshape(SSM_LANES, SSM_WIDTH)
    return {
        "bmat": jnp.concatenate([to_b(bb_re), to_b(bb_im)], axis=1).astype(BF16),
        "cmat": jnp.concatenate([to_c(c_re), -to_c(c_im)], axis=0).astype(BF16),
        "a_re": abar_re.reshape(1, SSM_LANES), "a_im": abar_im.reshape(1, SSM_LANES),
        "d_skip": d_skip.reshape(1, SSM_WIDTH),
    }


def _layer_norm(z, g, b):
    mu = jnp.mean(z, axis=-1, keepdims=True)
    zc = z - mu
    var = jnp.mean(zc * zc, axis=-1, keepdims=True)
    return zc * lax.rsqrt(var + LN_EPS) * g + b


def _post_mixer_kernel(x_ref, ao_ref, y_ref, wglu_ref, bglu_ref, wgate_ref, bgate_ref, wab_ref, wsb_ref,
                       wout_ref, lng_ref, lnb_ref, wr_ref, rb_ref, x1_ref, gate_ref):
    x = x_ref[...]
    xb = x.astype(BF16)
    s = jax.nn.gelu(y_ref[...])
    s = s * jax.nn.sigmoid(jnp.dot(s.astype(BF16), wglu_ref[...], preferred_element_type=F32) + bglu_ref[...])
    gates = jax.nn.sigmoid(jnp.dot(xb, wgate_ref[...], preferred_element_type=F32) + bgate_ref[...])
    attn_br = jnp.dot(ao_ref[...].astype(BF16), wab_ref[...], preferred_element_type=F32)
    ssm_br = jnp.dot(s.astype(BF16), wsb_ref[...], preferred_element_type=F32)
    merged = gates[:, :D_MODEL] * attn_br + gates[:, D_MODEL:] * ssm_br
    mix = jnp.dot(merged.astype(BF16), wout_ref[...], preferred_element_type=F32)
    x1 = _layer_norm(DN_ALPHA * x + mix, lng_ref[...], lnb_ref[...])
    x1_ref[...] = x1

    scores = jax.nn.sigmoid(jnp.dot(x1, wr_ref[...], preferred_element_type=F32,
                                    precision=lax.Precision.HIGHEST))
    sel = scores + rb_ref[...]
    lane = lax.broadcasted_iota(jnp.int32, sel.shape, 1).astype(F32)
    chosen = jnp.zeros(sel.shape, F32)
    for _ in range(TOP_K):
        top = jnp.max(sel, axis=-1, keepdims=True)
        first = jnp.min(jnp.where(sel == top, lane, float(N_EXPERTS)), axis=-1, keepdims=True)
        hit = lane == first
        chosen = jnp.where(hit, 1.0, chosen)
        sel = jnp.where(hit, -jnp.inf, sel)
    top_s = scores * chosen
    gate_ref[...] = top_s / jnp.sum(top_s, axis=-1, keepdims=True) * ROUTED_SCALE


def _post_mixer(x, attn_o, y_tb, w, rows_per_seq, tm):
    n = x.shape[0]
    tiles_per_seq = rows_per_seq // tm
    row = lambda width: pl.BlockSpec((tm, width), lambda i: (i, 0))
    return pl.pallas_call(
        _post_mixer_kernel,
        grid=(n // tm,),
        in_specs=[row(D_MODEL), row(GROUP_WIDTH),
                  pl.BlockSpec((tm, SSM_WIDTH), lambda i: (i % tiles_per_seq, i // tiles_per_seq)),
                  _full((SSM_WIDTH, SSM_WIDTH)), _full((1, SSM_WIDTH)),
                  _full((D_MODEL, 2 * D_MODEL)), _full((1, 2 * D_MODEL)),
                  _full((GROUP_WIDTH, D_MODEL)), _full((SSM_WIDTH, D_MODEL)), _full((D_MODEL, D_MODEL)),
                  _full((1, D_MODEL)), _full((1, D_MODEL)),
                  _full((D_MODEL, N_EXPERTS)), _full((1, N_EXPERTS))],
        out_specs=[row(D_MODEL), row(N_EXPERTS)],
        out_shape=[jax.ShapeDtypeStruct((n, D_MODEL), F32), jax.ShapeDtypeStruct((n, N_EXPERTS), F32)],
        compiler_params=_params(("parallel",)),
        name="post_mixer",
    )(x, attn_o, y_tb, w["w_glu"], w["b_glu"], w["w_gate"], w["b_gate"], w["w_attn_br"], w["w_ssm_br"],
      w["w_out"], w["ln1_g"], w["ln1_b"], w["w_router"], w["router_bias"])


def _moe_ffn_kernel(x_ref, gate_ref, p_ref, w13_ref, w2_ref, ws13_ref, ws2_ref, wpg_ref, wple_ref,
                    lng_ref, lnb_ref, o_ref, acc_sc, xb_sc):
    e = pl.program_id(1)

    def glu_ffn(xb, w13, w2, row_scale):
        h13 = jnp.dot(xb, w13, preferred_element_type=F32)
        h = jax.nn.silu(h13[:, :EXPERT_FF]) * h13[:, EXPERT_FF:]
        if row_scale is not None:
            h = h * row_scale
        return jnp.dot(h.astype(BF16), w2, preferred_element_type=F32)

    @pl.when(e == 0)
    def _():
        xb = x_ref[...].astype(BF16)
        xb_sc[...] = xb
        ple = (jax.nn.sigmoid(jnp.dot(xb, wpg_ref[...], preferred_element_type=F32))
               * jnp.dot(p_ref[...].astype(BF16), wple_ref[...], preferred_element_type=F32))
        acc_sc[...] = glu_ffn(xb, ws13_ref[...], ws2_ref[...], None) + ple

    gates = gate_ref[...]
    lane = lax.broadcasted_iota(jnp.int32, gates.shape, 1)
    g_col = jnp.sum(jnp.where(lane == e, gates, 0.0), axis=-1, keepdims=True)
    acc_sc[...] += glu_ffn(xb_sc[...], w13_ref[0], w2_ref[0], g_col)

    @pl.when(e == N_EXPERTS - 1)
    def _():
        o_ref[...] = _layer_norm(DN_ALPHA * x_ref[...] + acc_sc[...], lng_ref[...], lnb_ref[...])


def _moe_ffn(x1, gates, p, w, tm):
    n = x1.shape[0]
    row = lambda width: pl.BlockSpec((tm, width), lambda i, e: (i, 0))
    return pl.pallas_call(
        _moe_ffn_kernel,
        grid=(n // tm, N_EXPERTS),
        in_specs=[row(D_MODEL), row(N_EXPERTS), row(PLE_DIM),
                  pl.BlockSpec((1, D_MODEL, 2 * EXPERT_FF), lambda i, e: (e, 0, 0)),
                  pl.BlockSpec((1, EXPERT_FF, D_MODEL), lambda i, e: (e, 0, 0)),
                  _full((D_MODEL, 2 * EXPERT_FF)), _full((EXPERT_FF, D_MODEL)),
                  _full((D_MODEL, D_MODEL)), _full((PLE_DIM, D_MODEL)),
                  _full((1, D_MODEL)), _full((1, D_MODEL))],
        out_specs=row(D_MODEL),
        out_shape=jax.ShapeDtypeStruct((n, D_MODEL), F32),
        scratch_shapes=[pltpu.VMEM((tm, D_MODEL), F32), pltpu.VMEM((tm, D_MODEL), BF16)],
        compiler_params=_params(("parallel", "arbitrary")),
        name="moe_ffn",
    )(x1, gates, p, w["w13"], w["w2"], w["ws13"], w["ws2"], w["w_ple_gate"], w["w_ple"],
      w["ln2_g"], w["ln2_b"])


def _kv_rows(k, v, batch, seq, keep, g):
    cols = slice(g * GROUP_WIDTH, (g + 1) * GROUP_WIDTH)
    shape = (batch, keep, HEADS_PER_GROUP, HEAD_DIM)
    k_g = k.reshape(batch, seq, ATTN_WIDTH)[:, seq - keep:, cols].reshape(shape)
    v_g = v.reshape(batch, seq, ATTN_WIDTH)[:, seq - keep:, cols].reshape(shape)
    return jnp.stack([k_g, v_g], axis=2)


def _layer_prompt(x, p, w, ssm):
    batch, seq, _ = x.shape
    n = batch * seq
    x2 = x.reshape(n, D_MODEL)
    tabs = _rope_tables(jnp.arange(seq, dtype=jnp.int32))
    q, k, v, u = _in_proj(x2, w["w_in"], tabs, seq, 512)
    attn_o = _attn_prompt(q, k, v, batch, seq)
    zeros = jnp.zeros((batch, SSM_LANES), F32)
    y_tb, h_re, h_im = _s5_scan(u.reshape(seq * batch, SSM_WIDTH), ssm, zeros, zeros, batch, 128)
    x1, gates = _post_mixer(x2, attn_o, y_tb.reshape(seq, batch * SSM_WIDTH), w, seq, 512)
    y = _moe_ffn(x1, gates, p.reshape(n, PLE_DIM), w, 1024)
    kv = [_kv_rows(k, v, batch, seq, min(win, seq), g) for g, (win, _) in enumerate(DILATION_GROUPS)]
    h_last = jnp.stack([h_re, h_im], axis=-1).reshape(batch, SSM_GROUPS, SSM_STATE, 2)
    return y.reshape(batch, seq, D_MODEL), kv, h_last


def _layer_sample(x, p, caches, state, w, ssm):
    batch, seq, _ = x.shape
    assert seq == 1
    x2 = x.reshape(batch, D_MODEL)
    tabs = _rope_tables(jnp.full((batch,), PAST_LEN, dtype=jnp.int32))
    q, k, v, u = _in_proj(x2, w["w_in"], tabs, batch, batch)
    attn_o = _attn_sample(q, k, v, caches, 8)
    h0 = state.reshape(batch, SSM_LANES, 2)
    y_tb, h_re, h_im = _s5_scan(u, ssm, h0[..., 0], h0[..., 1], batch, 1)
    x1, gates = _post_mixer(x2, attn_o, y_tb, w, batch, batch)
    y = _moe_ffn(x1, gates, p.reshape(batch, PLE_DIM), w, batch)
    kv = [_kv_rows(k, v, batch, 1, 1, g) for g in range(len(DILATION_GROUPS))]
    h_last = jnp.stack([h_re, h_im], axis=-1).reshape(batch, SSM_GROUPS, SSM_STATE, 2)
    return y.reshape(batch, 1, D_MODEL), kv, h_last


def kernel(x_prompt, x_sample, cache_kv_w128, cache_kv_w512, cache_kv_w2048, state_ssm, p_prompt, p_sample,
           w_in, a_re, a_im, log_dt, b_re, b_im, c_re, c_im, d_skip, w_glu, b_glu, w_attn_br, w_ssm_br,
           w_gate, b_gate, w_out, ln1_g, ln1_b, w_router, router_bias, w1, w3, w2, ws1, ws3, ws2,
           w_ple_gate, w_ple, ln2_g, ln2_b):
    assert w_in.shape[0] == DEPTH == 1
    l = 0
    row = lambda t: t[l].reshape(1, -1)
    w = {
        "w_in": w_in[l].astype(BF16),
        "w_glu": w_glu[l].astype(BF16), "b_glu": row(b_glu),
        "w_gate": w_gate[l].astype(BF16), "b_gate": row(b_gate),
        "w_attn_br": w_attn_br[l].astype(BF16), "w_ssm_br": w_ssm_br[l].astype(BF16),
        "w_out": w_out[l].astype(BF16), "ln1_g": row(ln1_g), "ln1_b": row(ln1_b),
        "w_router": w_router[l], "router_bias": row(router_bias),
        "w13": jnp.concatenate([w1[l], w3[l]], axis=-1).astype(BF16), "w2": w2[l].astype(BF16),
        "ws13": jnp.concatenate([ws1[l], ws3[l]], axis=-1).astype(BF16), "ws2": ws2[l].astype(BF16),
        "w_ple_gate": w_ple_gate[l].astype(BF16), "w_ple": w_ple[l].astype(BF16),
        "ln2_g": row(ln2_g), "ln2_b": row(ln2_b),
    }
    ssm = _s5_params(a_re[l], a_im[l], log_dt[l], b_re[l], b_im[l], c_re[l], c_im[l], d_skip[l])
    yp, kv_p, h_p = _layer_prompt(x_prompt, p_prompt[l], w, ssm)
    caches = (cache_kv_w128[l], cache_kv_w512[l], cache_kv_w2048[l])
    ys, kv_s, h_s = _layer_sample(x_sample, p_sample[l], caches, state_ssm[l], w, ssm)
    return (yp, ys, kv_p[0][None], kv_s[0][None], kv_p[1][None], kv_s[1][None],
            kv_p[2][None], kv_s[2][None], h_p[None], h_s[None])
```

```python
import functools
import math

import jax
import jax.numpy as jnp
from jax import lax
from jax.experimental import pallas as pl
from jax.experimental.pallas import tpu as pltpu

F32 = jnp.float32
BF16 = jnp.bfloat16

D_MODEL = 1024
HEAD_DIM = 64
HEADS_PER_GROUP = 4
DILATION_GROUPS = ((128, 1), (512, 4), (2048, 16))
N_BACK = 128
GROUP_WIDTH = HEADS_PER_GROUP * HEAD_DIM
ATTN_WIDTH = 3 * GROUP_WIDTH
ROPE_THETA = 10000.0
SSM_WIDTH = 256
SSM_GROUP = 16
SSM_GROUPS = 16
SSM_STATE = 64
SSM_LANES = SSM_GROUPS * SSM_STATE
IN_WIDTH = 3 * ATTN_WIDTH + SSM_WIDTH
N_EXPERTS = 64
TOP_K = 8
EXPERT_FF = 256
ROUTED_SCALE = 2.5
PLE_DIM = 256
DEPTH = 1
PAST_LEN = 8192
DN_ALPHA = (2.0 * DEPTH) ** 0.25
LN_EPS = 1e-5

LANES = 128
ATTN_CHUNK = 2048
VMEM_LIMIT = 56 * 1024 * 1024


def _params(semantics):
    return pltpu.CompilerParams(dimension_semantics=semantics, vmem_limit_bytes=VMEM_LIMIT)


def _full(shape):
    return pl.BlockSpec(shape, lambda *_: (0,) * len(shape))


def _in_proj_kernel(x_ref, w_ref, cos_ref, sina_ref, sinb_ref, q_ref, k_ref, v_ref, u_ref):
    xb = x_ref[...].astype(BF16)
    cos = cos_ref[...]
    sin_a = sina_ref[...]
    sin_b = sinb_ref[...]

    def rope_store(col0, out_ref, scale):
        t = jnp.dot(xb, w_ref[:, col0:col0 + ATTN_WIDTH], preferred_element_type=F32)
        for c in range(ATTN_WIDTH // LANES):
            xc = t[:, c * LANES:(c + 1) * LANES]
            r = xc * cos + pltpu.roll(xc, LANES - 32, 1) * sin_a + pltpu.roll(xc, 32, 1) * sin_b
            out_ref[:, c * LANES:(c + 1) * LANES] = r * scale if scale != 1.0 else r

    rope_store(0, q_ref, HEAD_DIM ** -0.5)
    rope_store(ATTN_WIDTH, k_ref, 1.0)
    v_ref[...] = jnp.dot(xb, w_ref[:, 2 * ATTN_WIDTH:3 * ATTN_WIDTH], preferred_element_type=F32)
    u_ref[...] = jnp.dot(xb, w_ref[:, 3 * ATTN_WIDTH:], preferred_element_type=F32)


def _in_proj(x, w_in_bf, rope_tabs, rows_per_seq, tm):
    n = x.shape[0]
    tiles_per_seq = rows_per_seq // tm
    n_seq = n // rows_per_seq
    tab_tiles = rope_tabs[0].shape[0] // tm
    tab_spec = pl.BlockSpec((tm, LANES), lambda i: (i % tab_tiles, 0))
    row_spec = pl.BlockSpec((tm, ATTN_WIDTH), lambda i: (i, 0))
    return pl.pallas_call(
        _in_proj_kernel,
        grid=(n // tm,),
        in_specs=[pl.BlockSpec((tm, D_MODEL), lambda i: (i, 0)), _full((D_MODEL, IN_WIDTH)),
                  tab_spec, tab_spec, tab_spec],
        out_specs=[row_spec, row_spec, row_spec,
                   pl.BlockSpec((tm, SSM_WIDTH), lambda i: (i % tiles_per_seq, i // tiles_per_seq))],
        out_shape=[jax.ShapeDtypeStruct((n, ATTN_WIDTH), F32)] * 3
        + [jax.ShapeDtypeStruct((rows_per_seq, n_seq * SSM_WIDTH), F32)],
        compiler_params=_params(("parallel",)),
        name="in_proj",
    )(x, w_in_bf, *rope_tabs)


def _rope_tables(pos):
    half = HEAD_DIM // 2
    inv = ROPE_THETA ** (-jnp.arange(half, dtype=F32) / half)
    ang = pos.astype(F32)[:, None] * inv[None, :]
    cos = jnp.tile(jnp.cos(ang), (1, LANES // half))
    sin = jnp.tile(jnp.sin(ang), (1, LANES // half))
    first_half = (jnp.arange(LANES) % HEAD_DIM) < half
    sin_a = jnp.where(first_half[None, :], -sin, 0.0)
    sin_b = jnp.where(first_half[None, :], 0.0, sin)
    return cos, sin_a, sin_b


def _band_attention(q, k, v, mask):
    o_parts, lse_parts = [], []
    for h in range(LANES // HEAD_DIM):
        cols = slice(h * HEAD_DIM, (h + 1) * HEAD_DIM)
        logits = lax.dot_general(q[:, cols].astype(BF16), k[:, cols].astype(BF16),
                                 (((1,), (1,)), ((), ())), preferred_element_type=F32) + mask
        m = jnp.max(logits, axis=1, keepdims=True)
        p = jnp.exp(logits - m)
        l = jnp.sum(p, axis=1, keepdims=True)
        pv = jnp.dot(p.astype(BF16), v[:, cols].astype(BF16), preferred_element_type=F32)
        o_parts.append(pv * (1.0 / l))
        lse_parts.append(jnp.broadcast_to(m + jnp.log(l), (N_BACK, HEAD_DIM)))
    return jnp.concatenate(o_parts, axis=1), jnp.concatenate(lse_parts, axis=1)


def _attn_prompt_kernel(q_ref, kp_ref, kc_ref, vp_ref, vc_ref, o_ref, lse_ref):
    c = pl.program_id(1)
    g = pl.program_id(3)
    ch = ATTN_CHUNK
    qi = lax.broadcasted_iota(jnp.int32, (N_BACK, 2 * N_BACK), 0)
    kj = lax.broadcasted_iota(jnp.int32, (N_BACK, 2 * N_BACK), 1)
    dist = qi + N_BACK - kj
    band = jnp.where(dist >= 0, jnp.where(dist <= N_BACK, 0.0, -jnp.inf), -jnp.inf)
    band_first = jnp.where(kj >= N_BACK, band, -jnp.inf)

    def group_body(d):
        span = N_BACK * d
        n_sub = ch // N_BACK

        def rows(start, size):
            return pl.ds(start, size) if d == 1 else pl.ds(start, size, stride=d)

        def store(q0, o, lse):
            o_ref[rows(q0, N_BACK), :] = o
            lse_ref[rows(q0, N_BACK), :] = lse

        def head_block(r, carry):
            k = jnp.concatenate([kp_ref[rows(ch - span + r, N_BACK), :], kc_ref[rows(r, N_BACK), :]], axis=0)
            v = jnp.concatenate([vp_ref[rows(ch - span + r, N_BACK), :], vc_ref[rows(r, N_BACK), :]], axis=0)
            mask = jnp.where(c == 0, band_first, band)
            store(r, *_band_attention(q_ref[rows(r, N_BACK), :], k, v, mask))
            return carry

        def inner_block(idx, carry):
            s = idx // d
            r = idx % d
            k0 = (s - 1) * span + r
            store(s * span + r, *_band_attention(q_ref[rows(s * span + r, N_BACK), :],
                                                 kc_ref[rows(k0, 2 * N_BACK), :],
                                                 vc_ref[rows(k0, 2 * N_BACK), :], band))
            return carry

        lax.fori_loop(0, d, head_block, 0, unroll=min(d, 2))
        if n_sub > d:
            lax.fori_loop(d, n_sub, inner_block, 0, unroll=2 if (n_sub - d) % 2 == 0 else 3)

    for gi, (_, d) in enumerate(DILATION_GROUPS):
        pl.when(g == gi)(functools.partial(group_body, d))


def _attn_prompt(q, k, v, batch, seq):
    ch = ATTN_CHUNK
    cps = seq // ch
    n = batch * seq
    pairs = GROUP_WIDTH // LANES
    cur = lambda b, c, hp, g: (b * cps + c, g * pairs + hp)
    prev = lambda b, c, hp, g: (b * cps + jnp.maximum(c - 1, 0), g * pairs + hp)
    blk = (ch, LANES)
    return pl.pallas_call(
        _attn_prompt_kernel,
        grid=(batch, cps, pairs, len(DILATION_GROUPS)),
        in_specs=[pl.BlockSpec(blk, cur), pl.BlockSpec(blk, prev), pl.BlockSpec(blk, cur),
                  pl.BlockSpec(blk, prev), pl.BlockSpec(blk, cur)],
        out_specs=[pl.BlockSpec(blk, cur), pl.BlockSpec(blk, cur)],
        out_shape=[jax.ShapeDtypeStruct((n, ATTN_WIDTH), F32)] * 2,
        compiler_params=_params(("parallel", "parallel", "parallel", "parallel")),
        name="attn_prompt",
    )(q, k, k, v, v)


def _attn_sample_kernel(q_ref, k_ref, v_ref, c0_ref, c1_ref, c2_ref, o_ref, lse_ref):
    for g, c_ref in enumerate((c0_ref, c1_ref, c2_ref)):
        heads = slice(g * HEADS_PER_GROUP, (g + 1) * HEADS_PER_GROUP)
        q = q_ref[:, heads, :]
        k_new = k_ref[:, heads, :]
        v_new = v_ref[:, heads, :]
        k_c = c_ref[:, :, 0]
        v_c = c_ref[:, :, 1]
        s_c = jnp.sum(k_c * q[:, None], axis=-1, keepdims=True)
        s_new = jnp.sum(k_new * q, axis=-1, keepdims=True)
        m = jnp.maximum(jnp.max(s_c, axis=1), s_new)
        p_c = jnp.exp(s_c - m[:, None])
        p_new = jnp.exp(s_new - m)
        l = jnp.sum(p_c, axis=1) + p_new
        num = jnp.sum(p_c * v_c, axis=1) + p_new * v_new
        o_ref[:, heads, :] = num * (1.0 / l)
        lse_ref[:, heads, :] = jnp.broadcast_to(m + jnp.log(l), num.shape)


def _attn_sample(q, k, v, caches, bt):
    b = q.shape[0]
    n_heads = ATTN_WIDTH // HEAD_DIM
    views, specs = [], []
    for cache, (win, d) in zip(caches, DILATION_GROUPS):
        assert cache.shape[1] == win == N_BACK * d
        views.append(cache.reshape(b, N_BACK, d, 2, HEADS_PER_GROUP, HEAD_DIM))
        specs.append(pl.BlockSpec((bt, N_BACK, None, 2, HEADS_PER_GROUP, HEAD_DIM),
                                  lambda i: (i, 0, 0, 0, 0, 0)))
    row_spec = pl.BlockSpec((bt, n_heads, HEAD_DIM), lambda i: (i, 0, 0))
    heads = lambda t: t.reshape(b, n_heads, HEAD_DIM)
    o, lse = pl.pallas_call(
        _attn_sample_kernel,
        grid=(b // bt,),
        in_specs=[row_spec, row_spec, row_spec] + specs,
        out_specs=[row_spec, row_spec],
        out_shape=[jax.ShapeDtypeStruct((b, n_heads, HEAD_DIM), F32)] * 2,
        compiler_params=_params(("parallel",)),
        name="attn_sample",
    )(heads(q), heads(k), heads(v), *views)
    return o.reshape(b, ATTN_WIDTH), lse.reshape(b, ATTN_WIDTH)


def _s5_scan_kernel(u_ref, bmat_ref, cmat_ref, are_ref, aim_ref, d_ref, h0re_ref, h0im_ref,
                    y_ref, hre_ref, him_ref, hist_sc, *, bg, steps):
    t_chunk = pl.program_id(0)

    @pl.when(t_chunk == 0)
    def _():
        hre_ref[...] = h0re_ref[...]
        him_ref[...] = h0im_ref[...]

    u = u_ref[...]
    hist_sc[...] = jnp.dot(u.astype(BF16), bmat_ref[...], preferred_element_type=F32)
    a_re = jnp.broadcast_to(are_ref[...], (bg, SSM_LANES))
    a_im = jnp.broadcast_to(aim_ref[...], (bg, SSM_LANES))

    def step(t, carry):
        h_re, h_im = carry
        rows = pl.ds(pl.multiple_of(t * bg, bg), bg)
        n_re = a_re * h_re - a_im * h_im + hist_sc[rows, 0:SSM_LANES]
        n_im = a_re * h_im + a_im * h_re + hist_sc[rows, SSM_LANES:2 * SSM_LANES]
        hist_sc[rows, 0:SSM_LANES] = n_re
        hist_sc[rows, SSM_LANES:2 * SSM_LANES] = n_im
        return n_re, n_im

    h_re, h_im = lax.fori_loop(0, steps, step, (hre_ref[...], him_ref[...]))
    hre_ref[...] = h_re
    him_ref[...] = h_im
    y_ref[...] = (jnp.dot(hist_sc[...].astype(BF16), cmat_ref[...], preferred_element_type=F32)
                  + d_ref[...] * u)


def _s5_scan(u_tb, ssm, h0_re, h0_im, bg, steps):
    rows = u_tb.shape[0]
    blk = steps * bg
    kern = functools.partial(_s5_scan_kernel, bg=bg, steps=steps)
    state_spec = _full((bg, SSM_LANES))
    return pl.pallas_call(
        kern,
        grid=(rows // blk,),
        in_specs=[pl.BlockSpec((blk, SSM_WIDTH), lambda i: (i, 0)),
                  _full((SSM_WIDTH, 2 * SSM_LANES)), _full((2 * SSM_LANES, SSM_WIDTH)),
                  _full((1, SSM_LANES)), _full((1, SSM_LANES)), _full((1, SSM_WIDTH)),
                  state_spec, state_spec],
        out_specs=[pl.BlockSpec((blk, SSM_WIDTH), lambda i: (i, 0)), state_spec, state_spec],
        out_shape=[jax.ShapeDtypeStruct((rows, SSM_WIDTH), F32),
                   jax.ShapeDtypeStruct((bg, SSM_LANES), F32), jax.ShapeDtypeStruct((bg, SSM_LANES), F32)],
        scratch_shapes=[pltpu.VMEM((blk, 2 * SSM_LANES), F32)],
        compiler_params=_params(("arbitrary",)),
        name="s5_scan",
    )(u_tb, ssm["bmat"], ssm["cmat"], ssm["a_re"], ssm["a_im"], ssm["d_skip"], h0_re, h0_im)


def _s5_params(a_re, a_im, log_dt, b_re, b_im, c_re, c_im, d_skip):
    dt = jnp.exp(log_dt)[:, None]
    mag = jnp.exp(a_re * dt)
    abar_re = mag * jnp.cos(a_im * dt)
    abar_im = mag * jnp.sin(a_im * dt)
    a2 = a_re * a_re + a_im * a_im
    nr = abar_re - 1.0
    coef_re = (nr * a_re + abar_im * a_im) / a2
    coef_im = (abar_im * a_re - nr * a_im) / a2
    bb_re = coef_re[..., None] * b_re - coef_im[..., None] * b_im
    bb_im = coef_re[..., None] * b_im + coef_im[..., None] * b_re
    eye = jnp.eye(SSM_GROUPS, dtype=F32)
    to_b = lambda t: jnp.einsum("gpc,gh->gchp", t, eye).reshape(SSM_WIDTH, SSM_LANES)
    to_c = lambda t: jnp.einsum("gcp,gh->gphc", t, eye).reshape(SSM_LANES, SSM_WIDTH)
    return {
        "bmat": jnp.concatenate([to_b(bb_re), to_b(bb_im)], axis=1).astype(BF16),
        "cmat": jnp.concatenate([to_c(c_re), -to_c(c_im)], axis=0).astype(BF16),
        "a_re": abar_re.reshape(1, SSM_LANES), "a_im": abar_im.reshape(1, SSM_LANES),
        "d_skip": d_skip.reshape(1, SSM_WIDTH),
    }


def _layer_norm(z, g, b):
    mu = jnp.mean(z, axis=-1, keepdims=True)
    zc = z - mu
    var = jnp.mean(zc * zc, axis=-1, keepdims=True)
    return zc * lax.rsqrt(var + LN_EPS) * g + b


def _merge_groups(o, lse):
    parts = [slice(g * GROUP_WIDTH, (g + 1) * GROUP_WIDTH) for g in range(len(DILATION_GROUPS))]
    top = lse[:, parts[0]]
    for cols in parts[1:]:
        top = jnp.maximum(top, lse[:, cols])
    num = den = None
    for cols in parts:
        w = jnp.exp(lse[:, cols] - top)
        num = w * o[:, cols] if num is None else num + w * o[:, cols]
        den = w if den is None else den + w
    return num / den


def _post_mixer_kernel(x_ref, ao_ref, lse_ref, y_ref, wglu_ref, bglu_ref, wgate_ref, bgate_ref, wab_ref, wsb_ref,
                       wout_ref, lng_ref, lnb_ref, wr_ref, rb_ref, x1_ref, gate_ref):
    x = x_ref[...]
    xb = x.astype(BF16)
    s = jax.nn.gelu(y_ref[...])
    s = s * jax.nn.sigmoid(jnp.dot(s.astype(BF16), wglu_ref[...], preferred_element_type=F32) + bglu_ref[...])
    gates = jax.nn.sigmoid(jnp.dot(xb, wgate_ref[...], preferred_element_type=F32) + bgate_ref[...])
    attn_o = _merge_groups(ao_ref[...], lse_ref[...])
    attn_br = jnp.dot(attn_o.astype(BF16), wab_ref[...], preferred_element_type=F32)
    ssm_br = jnp.dot(s.astype(BF16), wsb_ref[...], preferred_element_type=F32)
    merged = gates[:, :D_MODEL] * attn_br + gates[:, D_MODEL:] * ssm_br
    mix = jnp.dot(merged.astype(BF16), wout_ref[...], preferred_element_type=F32)
    x1 = _layer_norm(DN_ALPHA * x + mix, lng_ref[...], lnb_ref[...])
    x1_ref[...] = x1

    scores = jax.nn.sigmoid(jnp.dot(x1, wr_ref[...], preferred_element_type=F32,
                                    precision=lax.Precision.HIGHEST))
    sel = scores + rb_ref[...]
    lane = lax.broadcasted_iota(jnp.int32, sel.shape, 1).astype(F32)
    chosen = jnp.zeros(sel.shape, F32)
    for _ in range(TOP_K):
        top = jnp.max(sel, axis=-1, keepdims=True)
        first = jnp.min(jnp.where(sel == top, lane, float(N_EXPERTS)), axis=-1, keepdims=True)
        hit = lane == first
        chosen = jnp.where(hit, 1.0, chosen)
        sel = jnp.where(hit, -jnp.inf, sel)
    top_s = scores * chosen
    gate_ref[...] = top_s / jnp.sum(top_s, axis=-1, keepdims=True) * ROUTED_SCALE


def _post_mixer(x, attn_o, attn_lse, y_tb, w, rows_per_seq, tm):
    n = x.shape[0]
    tiles_per_seq = rows_per_seq // tm
    row = lambda width: pl.BlockSpec((tm, width), lambda i: (i, 0))
    return pl.pallas_call(
        _post_mixer_kernel,
        grid=(n // tm,),
        in_specs=[row(D_MODEL), row(ATTN_WIDTH), row(ATTN_WIDTH),
                  pl.BlockSpec((tm, SSM_WIDTH), lambda i: (i % tiles_per_seq, i // tiles_per_seq)),
                  _full((SSM_WIDTH, SSM_WIDTH)), _full((1, SSM_WIDTH)),
                  _full((D_MODEL, 2 * D_MODEL)), _full((1, 2 * D_MODEL)),
                  _full((GROUP_WIDTH, D_MODEL)), _full((SSM_WIDTH, D_MODEL)), _full((D_MODEL, D_MODEL)),
                  _full((1, D_MODEL)), _full((1, D_MODEL)),
                  _full((D_MODEL, N_EXPERTS)), _full((1, N_EXPERTS))],
        out_specs=[row(D_MODEL), row(N_EXPERTS)],
        out_shape=[jax.ShapeDtypeStruct((n, D_MODEL), F32), jax.ShapeDtypeStruct((n, N_EXPERTS), F32)],
        compiler_params=_params(("parallel",)),
        name="post_mixer",
    )(x, attn_o, attn_lse, y_tb, w["w_glu"], w["b_glu"], w["w_gate"], w["b_gate"], w["w_attn_br"], w["w_ssm_br"],
      w["w_out"], w["ln1_g"], w["ln1_b"], w["w_router"], w["router_bias"])


def _moe_ffn_kernel(x_ref, gate_ref, p_ref, w13_ref, w2_ref, ws13_ref, ws2_ref, wpg_ref, wple_ref,
                    lng_ref, lnb_ref, o_ref, acc_sc, xb_sc):
    e = pl.program_id(1)

    def glu_ffn(xb, w13, w2, row_scale):
        h13 = jnp.dot(xb, w13, preferred_element_type=F32)
        h = jax.nn.silu(h13[:, :EXPERT_FF]) * h13[:, EXPERT_FF:]
        if row_scale is not None:
            h = h * row_scale
        return jnp.dot(h.astype(BF16), w2, preferred_element_type=F32)

    @pl.when(e == 0)
    def _():
        xb = x_ref[...].astype(BF16)
        xb_sc[...] = xb
        ple = (jax.nn.sigmoid(jnp.dot(xb, wpg_ref[...], preferred_element_type=F32))
               * jnp.dot(p_ref[...].astype(BF16), wple_ref[...], preferred_element_type=F32))
        acc_sc[...] = glu_ffn(xb, ws13_ref[...], ws2_ref[...], None) + ple

    gates = gate_ref[...]
    lane = lax.broadcasted_iota(jnp.int32, gates.shape, 1)
    g_col = jnp.sum(jnp.where(lane == e, gates, 0.0), axis=-1, keepdims=True)
    acc_sc[...] += glu_ffn(xb_sc[...], w13_ref[0], w2_ref[0], g_col)

    @pl.when(e == N_EXPERTS - 1)
    def _():
        o_ref[...] = _layer_norm(DN_ALPHA * x_ref[...] + acc_sc[...], lng_ref[...], lnb_ref[...])


def _moe_ffn(x1, gates, p, w, tm):
    n = x1.shape[0]
    row = lambda width: pl.BlockSpec((tm, width), lambda i, e: (i, 0))
    return pl.pallas_call(
        _moe_ffn_kernel,
        grid=(n // tm, N_EXPERTS),
        in_specs=[row(D_MODEL), row(N_EXPERTS), row(PLE_DIM),
                  pl.BlockSpec((1, D_MODEL, 2 * EXPERT_FF), lambda i, e: (e, 0, 0)),
                  pl.BlockSpec((1, EXPERT_FF, D_MODEL), lambda i, e: (e, 0, 0)),
                  _full((D_MODEL, 2 * EXPERT_FF)), _full((EXPERT_FF, D_MODEL)),
                  _full((D_MODEL, D_MODEL)), _full((PLE_DIM, D_MODEL)),
                  _full((1, D_MODEL)), _full((1, D_MODEL))],
        out_specs=row(D_MODEL),
        out_shape=jax.ShapeDtypeStruct((n, D_MODEL), F32),
        scratch_shapes=[pltpu.VMEM((tm, D_MODEL), F32), pltpu.VMEM((tm, D_MODEL), BF16)],
        compiler_params=_params(("parallel", "arbitrary")),
        name="moe_ffn",
    )(x1, gates, p, w["w13"], w["w2"], w["ws13"], w["ws2"], w["w_ple_gate"], w["w_ple"],
      w["ln2_g"], w["ln2_b"])


def _kv_rows(k, v, batch, seq, keep, g):
    cols = slice(g * GROUP_WIDTH, (g + 1) * GROUP_WIDTH)
    shape = (batch, keep, HEADS_PER_GROUP, HEAD_DIM)
    k_g = k.reshape(batch, seq, ATTN_WIDTH)[:, seq - keep:, cols].reshape(shape)
    v_g = v.reshape(batch, seq, ATTN_WIDTH)[:, seq - keep:, cols].reshape(shape)
    return jnp.stack([k_g, v_g], axis=2)


def _layer_prompt(x, p, w, ssm):
    batch, seq, _ = x.shape
    n = batch * seq
    x2 = x.reshape(n, D_MODEL)
    tabs = _rope_tables(jnp.arange(seq, dtype=jnp.int32))
    q, k, v, u = _in_proj(x2, w["w_in"], tabs, seq, 512)
    attn_o, attn_lse = _attn_prompt(q, k, v, batch, seq)
    zeros = jnp.zeros((batch, SSM_LANES), F32)
    y_tb, h_re, h_im = _s5_scan(u.reshape(seq * batch, SSM_WIDTH), ssm, zeros, zeros, batch, 128)
    x1, gates = _post_mixer(x2, attn_o, attn_lse, y_tb.reshape(seq, batch * SSM_WIDTH), w, seq, 512)
    y = _moe_ffn(x1, gates, p.reshape(n, PLE_DIM), w, 1024)
    kv = [_kv_rows(k, v, batch, seq, min(win, seq), g) for g, (win, _) in enumerate(DILATION_GROUPS)]
    h_last = jnp.stack([h_re, h_im], axis=-1).reshape(batch, SSM_GROUPS, SSM_STATE, 2)
    return y.reshape(batch, seq, D_MODEL), kv, h_last


def _layer_sample(x, p, caches, state, w, ssm):
    batch, seq, _ = x.shape
    assert seq == 1
    x2 = x.reshape(batch, D_MODEL)
    tabs = _rope_tables(jnp.full((batch,), PAST_LEN, dtype=jnp.int32))
    q, k, v, u = _in_proj(x2, w["w_in"], tabs, batch, batch)
    attn_o, attn_lse = _attn_sample(q, k, v, caches, 4)
    h0 = state.reshape(batch, SSM_LANES, 2)
    y_tb, h_re, h_im = _s5_scan(u, ssm, h0[..., 0], h0[..., 1], batch, 1)
    x1, gates = _post_mixer(x2, attn_o, attn_lse, y_tb, w, batch, batch)
    y = _moe_ffn(x1, gates, p.reshape(batch, PLE_DIM), w, batch)
    kv = [_kv_rows(k, v, batch, 1, 1, g) for g in range(len(DILATION_GROUPS))]
    h_last = jnp.stack([h_re, h_im], axis=-1).reshape(batch, SSM_GROUPS, SSM_STATE, 2)
    return y.reshape(batch, 1, D_MODEL), kv, h_last


def kernel(x_prompt, x_sample, cache_kv_w128, cache_kv_w512, cache_kv_w2048, state_ssm, p_prompt, p_sample,
           w_in, a_re, a_im, log_dt, b_re, b_im, c_re, c_im, d_skip, w_glu, b_glu, w_attn_br, w_ssm_br,
           w_gate, b_gate, w_out, ln1_g, ln1_b, w_router, router_bias, w1, w3, w2, ws1, ws3, ws2,
           w_ple_gate, w_ple, ln2_g, ln2_b):
    assert w_in.shape[0] == DEPTH == 1
    l = 0
    row = lambda t: t[l].reshape(1, -1)
    w = {
        "w_in": w_in[l].astype(BF16),
        "w_glu": w_glu[l].astype(BF16), "b_glu": row(b_glu),
        "w_gate": w_gate[l].astype(BF16), "b_gate": row(b_gate),
        "w_attn_br": w_attn_br[l].astype(BF16), "w_ssm_br": w_ssm_br[l].astype(BF16),
        "w_out": w_out[l].astype(BF16), "ln1_g": row(ln1_g), "ln1_b": row(ln1_b),
        "w_router": w_router[l], "router_bias": row(router_bias),
        "w13": jnp.concatenate([w1[l], w3[l]], axis=-1).astype(BF16), "w2": w2[l].astype(BF16),
        "ws13": jnp.concatenate([ws1[l], ws3[l]], axis=-1).astype(BF16), "ws2": ws2[l].astype(BF16),
        "w_ple_gate": w_ple_gate[l].astype(BF16), "w_ple": w_ple[l].astype(BF16),
        "ln2_g": row(ln2_g), "ln2_b": row(ln2_b),
    }
    ssm = _s5_params(a_re[l], a_im[l], log_dt[l], b_re[l], b_im[l], c_re[l], c_im[l], d_skip[l])
    yp, kv_p, h_p = _layer_prompt(x_prompt, p_prompt[l], w, ssm)
    caches = (cache_kv_w128[l], cache_kv_w512[l], cache_kv_w2048[l])
    ys, kv_s, h_s = _layer_sample(x_sample, p_sample[l], caches, state_ssm[l], w, ssm)
    return (yp, ys, kv_p[0][None], kv_s[0][None], kv_p[1][None], kv_s[1][None],
            kv_p[2][None], kv_s[2][None], h_p[None], h_s[None])
```

```python
import functools
import math

import jax
import jax.numpy as jnp
from jax import lax
from jax.experimental import pallas as pl
from jax.experimental.pallas import tpu as pltpu

F32 = jnp.float32
BF16 = jnp.bfloat16

D_MODEL = 1024
HEAD_DIM = 64
HEADS_PER_GROUP = 4
DILATION_GROUPS = ((128, 1), (512, 4), (2048, 16))
N_BACK = 128
GROUP_WIDTH = HEADS_PER_GROUP * HEAD_DIM
ATTN_WIDTH = 3 * GROUP_WIDTH
ROPE_THETA = 10000.0
SSM_WIDTH = 256
SSM_GROUP = 16
SSM_GROUPS = 16
SSM_STATE = 64
SSM_LANES = SSM_GROUPS * SSM_STATE
IN_WIDTH = 3 * ATTN_WIDTH + SSM_WIDTH
N_EXPERTS = 64
TOP_K = 8
EXPERT_FF = 256
ROUTED_SCALE = 2.5
PLE_DIM = 256
DEPTH = 1
PAST_LEN = 8192
DN_ALPHA = (2.0 * DEPTH) ** 0.25
LN_EPS = 1e-5

LANES = 128
ATTN_CHUNK = 2048
VMEM_LIMIT = 56 * 1024 * 1024


def _params(semantics):
    return pltpu.CompilerParams(dimension_semantics=semantics, vmem_limit_bytes=VMEM_LIMIT)


def _full(shape):
    return pl.BlockSpec(shape, lambda *_: (0,) * len(shape))


def _in_proj_kernel(x_ref, w_ref, cos_ref, sina_ref, sinb_ref, q_ref, k_ref, v_ref, u_ref):
    xb = x_ref[...].astype(BF16)
    cos = cos_ref[...]
    sin_a = sina_ref[...]
    sin_b = sinb_ref[...]

    def rope_store(col0, out_ref, scale):
        t = jnp.dot(xb, w_ref[:, col0:col0 + ATTN_WIDTH], preferred_element_type=F32)
        for c in range(ATTN_WIDTH // LANES):
            xc = t[:, c * LANES:(c + 1) * LANES]
            r = xc * cos + pltpu.roll(xc, LANES - 32, 1) * sin_a + pltpu.roll(xc, 32, 1) * sin_b
            out_ref[:, c * LANES:(c + 1) * LANES] = r * scale if scale != 1.0 else r

    rope_store(0, q_ref, HEAD_DIM ** -0.5)
    rope_store(ATTN_WIDTH, k_ref, 1.0)
    v_ref[...] = jnp.dot(xb, w_ref[:, 2 * ATTN_WIDTH:3 * ATTN_WIDTH], preferred_element_type=F32)
    u_ref[...] = jnp.dot(xb, w_ref[:, 3 * ATTN_WIDTH:], preferred_element_type=F32)


def _in_proj(x, w_in_bf, rope_tabs, rows_per_seq, tm):
    n = x.shape[0]
    tiles_per_seq = rows_per_seq // tm
    n_seq = n // rows_per_seq
    tab_tiles = rope_tabs[0].shape[0] // tm
    tab_spec = pl.BlockSpec((tm, LANES), lambda i: (i % tab_tiles, 0))
    row_spec = pl.BlockSpec((tm, ATTN_WIDTH), lambda i: (i, 0))
    return pl.pallas_call(
        _in_proj_kernel,
        grid=(n // tm,),
        in_specs=[pl.BlockSpec((tm, D_MODEL), lambda i: (i, 0)), _full((D_MODEL, IN_WIDTH)),
                  tab_spec, tab_spec, tab_spec],
        out_specs=[row_spec, row_spec, row_spec,
                   pl.BlockSpec((tm, SSM_WIDTH), lambda i: (i % tiles_per_seq, i // tiles_per_seq))],
        out_shape=[jax.ShapeDtypeStruct((n, ATTN_WIDTH), F32)] * 3
        + [jax.ShapeDtypeStruct((rows_per_seq, n_seq * SSM_WIDTH), F32)],
        compiler_params=_params(("parallel",)),
        name="in_proj",
    )(x, w_in_bf, *rope_tabs)


def _rope_tables(pos):
    half = HEAD_DIM // 2
    inv = ROPE_THETA ** (-jnp.arange(half, dtype=F32) / half)
    ang = pos.astype(F32)[:, None] * inv[None, :]
    cos = jnp.tile(jnp.cos(ang), (1, LANES // half))
    sin = jnp.tile(jnp.sin(ang), (1, LANES // half))
    first_half = (jnp.arange(LANES) % HEAD_DIM) < half
    sin_a = jnp.where(first_half[None, :], -sin, 0.0)
    sin_b = jnp.where(first_half[None, :], 0.0, sin)
    return cos, sin_a, sin_b


def _band_attention(q, k, v, mask):
    o_parts, lse_parts = [], []
    for h in range(LANES // HEAD_DIM):
        cols = slice(h * HEAD_DIM, (h + 1) * HEAD_DIM)
        logits = lax.dot_general(q[:, cols].astype(BF16), k[:, cols].astype(BF16),
                                 (((1,), (1,)), ((), ())), preferred_element_type=F32) + mask
        m = jnp.max(logits, axis=1, keepdims=True)
        p = jnp.exp(logits - m)
        l = jnp.sum(p, axis=1, keepdims=True)
        pv = jnp.dot(p.astype(BF16), v[:, cols].astype(BF16), preferred_element_type=F32)
        o_parts.append(pv * (1.0 / l))
        lse_parts.append(jnp.broadcast_to(m + jnp.log(l), (N_BACK, HEAD_DIM)))
    return jnp.concatenate(o_parts, axis=1), jnp.concatenate(lse_parts, axis=1)


def _attn_prompt_kernel(q_ref, kp_ref, kc_ref, vp_ref, vc_ref, o_ref, lse_ref):
    c = pl.program_id(1)
    g = pl.program_id(3)
    ch = ATTN_CHUNK
    qi = lax.broadcasted_iota(jnp.int32, (N_BACK, 2 * N_BACK), 0)
    kj = lax.broadcasted_iota(jnp.int32, (N_BACK, 2 * N_BACK), 1)
    dist = qi + N_BACK - kj
    band = jnp.where(dist >= 0, jnp.where(dist <= N_BACK, 0.0, -jnp.inf), -jnp.inf)
    band_first = jnp.where(kj >= N_BACK, band, -jnp.inf)

    def group_body(d):
        span = N_BACK * d
        n_sub = ch // N_BACK

        def rows(start, size):
            return pl.ds(start, size) if d == 1 else pl.ds(start, size, stride=d)

        def store(q0, o, lse):
            o_ref[rows(q0, N_BACK), :] = o
            lse_ref[rows(q0, N_BACK), :] = lse

        def head_block(r, carry):
            k = jnp.concatenate([kp_ref[rows(ch - span + r, N_BACK), :], kc_ref[rows(r, N_BACK), :]], axis=0)
            v = jnp.concatenate([vp_ref[rows(ch - span + r, N_BACK), :], vc_ref[rows(r, N_BACK), :]], axis=0)
            mask = jnp.where(c == 0, band_first, band)
            store(r, *_band_attention(q_ref[rows(r, N_BACK), :], k, v, mask))
            return carry

        def inner_block(idx, carry):
            s = idx // d
            r = idx % d
            k0 = (s - 1) * span + r
            store(s * span + r, *_band_attention(q_ref[rows(s * span + r, N_BACK), :],
                                                 kc_ref[rows(k0, 2 * N_BACK), :],
                                                 vc_ref[rows(k0, 2 * N_BACK), :], band))
            return carry

        lax.fori_loop(0, d, head_block, 0, unroll=min(d, 2))
        if n_sub > d:
            lax.fori_loop(d, n_sub, inner_block, 0, unroll=2 if (n_sub - d) % 2 == 0 else 3)

    for gi, (_, d) in enumerate(DILATION_GROUPS):
        pl.when(g == gi)(functools.partial(group_body, d))


def _attn_prompt(q, k, v, batch, seq):
    ch = ATTN_CHUNK
    cps = seq // ch
    n = batch * seq
    pairs = GROUP_WIDTH // LANES
    cur = lambda b, c, hp, g: (b * cps + c, g * pairs + hp)
    prev = lambda b, c, hp, g: (b * cps + jnp.maximum(c - 1, 0), g * pairs + hp)
    blk = (ch, LANES)
    return pl.pallas_call(
        _attn_prompt_kernel,
        grid=(batch, cps, pairs, len(DILATION_GROUPS)),
        in_specs=[pl.BlockSpec(blk, cur), pl.BlockSpec(blk, prev), pl.BlockSpec(blk, cur),
                  pl.BlockSpec(blk, prev), pl.BlockSpec(blk, cur)],
        out_specs=[pl.BlockSpec(blk, cur), pl.BlockSpec(blk, cur)],
        out_shape=[jax.ShapeDtypeStruct((n, ATTN_WIDTH), F32)] * 2,
        compiler_params=_params(("parallel", "parallel", "parallel", "parallel")),
        name="attn_prompt",
    )(q, k, k, v, v)


def _attn_sample_kernel(q_ref, k_ref, v_ref, c0_ref, c1_ref, c2_ref, o_ref, lse_ref):
    bt = q_ref.shape[0]
    for b in range(bt):
        for g, (c_ref, (win, d)) in enumerate(zip((c0_ref, c1_ref, c2_ref), DILATION_GROUPS)):
            pos = lax.broadcasted_iota(jnp.int32, (1, win), 1)
            off_stride = (pos % d) != 0
            for h in range(HEADS_PER_GROUP):
                j = g * HEADS_PER_GROUP + h
                q = q_ref[b, :, j:j + 1]
                k_new = k_ref[b, :, j:j + 1]
                v_new = v_ref[b, :, j:j + 1]
                s_c = jnp.sum(c_ref[b, 0, h] * q, axis=0, keepdims=True)
                s_c = jnp.where(off_stride, -jnp.inf, s_c)
                s_new = jnp.sum(k_new * q, axis=0, keepdims=True)
                m = jnp.maximum(jnp.max(s_c, axis=1, keepdims=True), s_new)
                p_c = jnp.exp(s_c - m)
                p_new = jnp.exp(s_new - m)
                l = jnp.sum(p_c, axis=1, keepdims=True) + p_new
                num = jnp.sum(c_ref[b, 1, h] * p_c, axis=1, keepdims=True) + p_new * v_new
                o_ref[b, :, j:j + 1] = num * (1.0 / l)
                lse_ref[b, :, j:j + 1] = m + jnp.log(l)


def _attn_sample(q, k, v, caches, bt):
    b = q.shape[0]
    n_heads = ATTN_WIDTH // HEAD_DIM
    views, specs = [], []
    for cache, (win, d) in zip(caches, DILATION_GROUPS):
        assert cache.shape[1] == win == N_BACK * d
        views.append(jnp.transpose(cache, (0, 2, 3, 4, 1)))
        specs.append(pl.BlockSpec((bt, 2, HEADS_PER_GROUP, HEAD_DIM, win), lambda i: (i, 0, 0, 0, 0)))
    col_spec = pl.BlockSpec((bt, HEAD_DIM, n_heads), lambda i: (i, 0, 0))
    lse_spec = pl.BlockSpec((bt, 1, n_heads), lambda i: (i, 0, 0))
    cols = lambda t: jnp.transpose(t.reshape(b, n_heads, HEAD_DIM), (0, 2, 1))
    o, lse = pl.pallas_call(
        _attn_sample_kernel,
        grid=(b // bt,),
        in_specs=[col_spec, col_spec, col_spec] + specs,
        out_specs=[col_spec, lse_spec],
        out_shape=[jax.ShapeDtypeStruct((b, HEAD_DIM, n_heads), F32),
                   jax.ShapeDtypeStruct((b, 1, n_heads), F32)],
        compiler_params=_params(("parallel",)),
        name="attn_sample",
    )(cols(q), cols(k), cols(v), *views)
    o = jnp.transpose(o, (0, 2, 1)).reshape(b, ATTN_WIDTH)
    lse = jnp.broadcast_to(jnp.transpose(lse, (0, 2, 1)), (b, n_heads, HEAD_DIM)).reshape(b, ATTN_WIDTH)
    return o, lse


def _s5_scan_kernel(u_ref, bmat_ref, cmat_ref, are_ref, aim_ref, d_ref, h0re_ref, h0im_ref,
                    y_ref, hre_ref, him_ref, hist_sc, *, bg, steps):
    t_chunk = pl.program_id(0)

    @pl.when(t_chunk == 0)
    def _():
        hre_ref[...] = h0re_ref[...]
        him_ref[...] = h0im_ref[...]

    u = u_ref[...]
    hist_sc[...] = jnp.dot(u.astype(BF16), bmat_ref[...], preferred_element_type=F32)
    a_re = jnp.broadcast_to(are_ref[...], (bg, SSM_LANES))
    a_im = jnp.broadcast_to(aim_ref[...], (bg, SSM_LANES))

    def step(t, carry):
        h_re, h_im = carry
        rows = pl.ds(pl.multiple_of(t * bg, bg), bg)
        n_re = a_re * h_re - a_im * h_im + hist_sc[rows, 0:SSM_LANES]
        n_im = a_re * h_im + a_im * h_re + hist_sc[rows, SSM_LANES:2 * SSM_LANES]
        hist_sc[rows, 0:SSM_LANES] = n_re
        hist_sc[rows, SSM_LANES:2 * SSM_LANES] = n_im
        return n_re, n_im

    h_re, h_im = lax.fori_loop(0, steps, step, (hre_ref[...], him_ref[...]))
    hre_ref[...] = h_re
    him_ref[...] = h_im
    y_ref[...] = (jnp.dot(hist_sc[...].astype(BF16), cmat_ref[...], preferred_element_type=F32)
                  + d_ref[...] * u)


def _s5_scan(u_tb, ssm, h0_re, h0_im, bg, steps):
    rows = u_tb.shape[0]
    blk = steps * bg
    kern = functools.partial(_s5_scan_kernel, bg=bg, steps=steps)
    state_spec = _full((bg, SSM_LANES))
    return pl.pallas_call(
        kern,
        grid=(rows // blk,),
        in_specs=[pl.BlockSpec((blk, SSM_WIDTH), lambda i: (i, 0)),
                  _full((SSM_WIDTH, 2 * SSM_LANES)), _full((2 * SSM_LANES, SSM_WIDTH)),
                  _full((1, SSM_LANES)), _full((1, SSM_LANES)), _full((1, SSM_WIDTH)),
                  state_spec, state_spec],
        out_specs=[pl.BlockSpec((blk, SSM_WIDTH), lambda i: (i, 0)), state_spec, state_spec],
        out_shape=[jax.ShapeDtypeStruct((rows, SSM_WIDTH), F32),
                   jax.ShapeDtypeStruct((bg, SSM_LANES), F32), jax.ShapeDtypeStruct((bg, SSM_LANES), F32)],
        scratch_shapes=[pltpu.VMEM((blk, 2 * SSM_LANES), F32)],
        compiler_params=_params(("arbitrary",)),
        name="s5_scan",
    )(u_tb, ssm["bmat"], ssm["cmat"], ssm["a_re"], ssm["a_im"], ssm["d_skip"], h0_re, h0_im)


def _s5_params(a_re, a_im, log_dt, b_re, b_im, c_re, c_im, d_skip):
    dt = jnp.exp(log_dt)[:, None]
    mag = jnp.exp(a_re * dt)
    abar_re = mag * jnp.cos(a_im * dt)
    abar_im = mag * jnp.sin(a_im * dt)
    a2 = a_re * a_re + a_im * a_im
    nr = abar_re - 1.0
    coef_re = (nr * a_re + abar_im * a_im) / a2
    coef_im = (abar_im * a_re - nr * a_im) / a2
    bb_re = coef_re[..., None] * b_re - coef_im[..., None] * b_im
    bb_im = coef_re[..., None] * b_im + coef_im[..., None] * b_re
    eye = jnp.eye(SSM_GROUPS, dtype=F32)
    to_b = lambda t: jnp.einsum("gpc,gh->gchp", t, eye).reshape(SSM_WIDTH, SSM_LANES)
    to_c = lambda t: jnp.einsum("gcp,gh->gphc", t, eye).reshape(SSM_LANES, SSM_WIDTH)
    return {
        "bmat": jnp.concatenate([to_b(bb_re), to_b(bb_im)], axis=1).astype(BF16),
        "cmat": jnp.concatenate([to_c(c_re), -to_c(c_im)], axis=0).astype(BF16),
        "a_re": abar_re.reshape(1, SSM_LANES), "a_im": abar_im.reshape(1, SSM_LANES),
        "d_skip": d_skip.reshape(1, SSM_WIDTH),
    }


def _layer_norm(z, g, b):
    mu = jnp.mean(z, axis=-1, keepdims=True)
    zc = z - mu
    var = jnp.mean(zc * zc, axis=-1, keepdims=True)
    return zc * lax.rsqrt(var + LN_EPS) * g + b


def _merge_groups(o, lse):
    parts = [slice(g * GROUP_WIDTH, (g + 1) * GROUP_WIDTH) for g in range(len(DILATION_GROUPS))]
    top = lse[:, parts[0]]
    for cols in parts[1:]:
        top = jnp.maximum(top, lse[:, cols])
    num = den = None
    for cols in parts:
        w = jnp.exp(lse[:, cols] - top)
        num = w * o[:, cols] if num is None else num + w * o[:, cols]
        den = w if den is None else den + w
    return num / den


def _post_mixer_kernel(x_ref, ao_ref, lse_ref, y_ref, wglu_ref, bglu_ref, wgate_ref, bgate_ref, wab_ref, wsb_ref,
                       wout_ref, lng_ref, lnb_ref, wr_ref, rb_ref, x1_ref, gate_ref):
    x = x_ref[...]
    xb = x.astype(BF16)
    s = jax.nn.gelu(y_ref[...])
    s = s * jax.nn.sigmoid(jnp.dot(s.astype(BF16), wglu_ref[...], preferred_element_type=F32) + bglu_ref[...])
    gates = jax.nn.sigmoid(jnp.dot(xb, wgate_ref[...], preferred_element_type=F32) + bgate_ref[...])
    attn_o = _merge_groups(ao_ref[...], lse_ref[...])
    attn_br = jnp.dot(attn_o.astype(BF16), wab_ref[...], preferred_element_type=F32)
    ssm_br = jnp.dot(s.astype(BF16), wsb_ref[...], preferred_element_type=F32)
    merged = gates[:, :D_MODEL] * attn_br + gates[:, D_MODEL:] * ssm_br
    mix = jnp.dot(merged.astype(BF16), wout_ref[...], preferred_element_type=F32)
    x1 = _layer_norm(DN_ALPHA * x + mix, lng_ref[...], lnb_ref[...])
    x1_ref[...] = x1

    scores = jax.nn.sigmoid(jnp.dot(x1, wr_ref[...], preferred_element_type=F32,
                                    precision=lax.Precision.HIGHEST))
    sel = scores + rb_ref[...]
    lane = lax.broadcasted_iota(jnp.int32, sel.shape, 1).astype(F32)
    chosen = jnp.zeros(sel.shape, F32)
    for _ in range(TOP_K):
        top = jnp.max(sel, axis=-1, keepdims=True)
        first = jnp.min(jnp.where(sel == top, lane, float(N_EXPERTS)), axis=-1, keepdims=True)
        hit = lane == first
        chosen = jnp.where(hit, 1.0, chosen)
        sel = jnp.where(hit, -jnp.inf, sel)
    top_s = scores * chosen
    gate_ref[...] = top_s / jnp.sum(top_s, axis=-1, keepdims=True) * ROUTED_SCALE


def _post_mixer(x, attn_o, attn_lse, y_tb, w, rows_per_seq, tm):
    n = x.shape[0]
    tiles_per_seq = rows_per_seq // tm
    row = lambda width: pl.BlockSpec((tm, width), lambda i: (i, 0))
    return pl.pallas_call(
        _post_mixer_kernel,
        grid=(n // tm,),
        in_specs=[row(D_MODEL), row(ATTN_WIDTH), row(ATTN_WIDTH),
                  pl.BlockSpec((tm, SSM_WIDTH), lambda i: (i % tiles_per_seq, i // tiles_per_seq)),
                  _full((SSM_WIDTH, SSM_WIDTH)), _full((1, SSM_WIDTH)),
                  _full((D_MODEL, 2 * D_MODEL)), _full((1, 2 * D_MODEL)),
                  _full((GROUP_WIDTH, D_MODEL)), _full((SSM_WIDTH, D_MODEL)), _full((D_MODEL, D_MODEL)),
                  _full((1, D_MODEL)), _full((1, D_MODEL)),
                  _full((D_MODEL, N_EXPERTS)), _full((1, N_EXPERTS))],
        out_specs=[row(D_MODEL), row(N_EXPERTS)],
        out_shape=[jax.ShapeDtypeStruct((n, D_MODEL), F32), jax.ShapeDtypeStruct((n, N_EXPERTS), F32)],
        compiler_params=_params(("parallel",)),
        name="post_mixer",
    )(x, attn_o, attn_lse, y_tb, w["w_glu"], w["b_glu"], w["w_gate"], w["b_gate"], w["w_attn_br"], w["w_ssm_br"],
      w["w_out"], w["ln1_g"], w["ln1_b"], w["w_router"], w["router_bias"])


def _moe_ffn_kernel(x_ref, gate_ref, p_ref, w13_ref, w2_ref, ws13_ref, ws2_ref, wpg_ref, wple_ref,
                    lng_ref, lnb_ref, o_ref, acc_sc, xb_sc):
    e = pl.program_id(1)

    def glu_ffn(xb, w13, w2, row_scale):
        h13 = jnp.dot(xb, w13, preferred_element_type=F32)
        h = jax.nn.silu(h13[:, :EXPERT_FF]) * h13[:, EXPERT_FF:]
        if row_scale is not None:
            h = h * row_scale
        return jnp.dot(h.astype(BF16), w2, preferred_element_type=F32)

    @pl.when(e == 0)
    def _():
        xb = x_ref[...].astype(BF16)
        xb_sc[...] = xb
        ple = (jax.nn.sigmoid(jnp.dot(xb, wpg_ref[...], preferred_element_type=F32))
               * jnp.dot(p_ref[...].astype(BF16), wple_ref[...], preferred_element_type=F32))
        acc_sc[...] = glu_ffn(xb, ws13_ref[...], ws2_ref[...], None) + ple

    gates = gate_ref[...]
    lane = lax.broadcasted_iota(jnp.int32, gates.shape, 1)
    g_col = jnp.sum(jnp.where(lane == e, gates, 0.0), axis=-1, keepdims=True)
    acc_sc[...] += glu_ffn(xb_sc[...], w13_ref[0], w2_ref[0], g_col)

    @pl.when(e == N_EXPERTS - 1)
    def _():
        o_ref[...] = _layer_norm(DN_ALPHA * x_ref[...] + acc_sc[...], lng_ref[...], lnb_ref[...])


def _moe_ffn(x1, gates, p, w, tm):
    n = x1.shape[0]
    row = lambda width: pl.BlockSpec((tm, width), lambda i, e: (i, 0))
    return pl.pallas_call(
        _moe_ffn_kernel,
        grid=(n // tm, N_EXPERTS),
        in_specs=[row(D_MODEL), row(N_EXPERTS), row(PLE_DIM),
                  pl.BlockSpec((1, D_MODEL, 2 * EXPERT_FF), lambda i, e: (e, 0, 0)),
                  pl.BlockSpec((1, EXPERT_FF, D_MODEL), lambda i, e: (e, 0, 0)),
                  _full((D_MODEL, 2 * EXPERT_FF)), _full((EXPERT_FF, D_MODEL)),
                  _full((D_MODEL, D_MODEL)), _full((PLE_DIM, D_MODEL)),
                  _full((1, D_MODEL)), _full((1, D_MODEL))],
        out_specs=row(D_MODEL),
        out_shape=jax.ShapeDtypeStruct((n, D_MODEL), F32),
        scratch_shapes=[pltpu.VMEM((tm, D_MODEL), F32), pltpu.VMEM((tm, D_MODEL), BF16)],
        compiler_params=_params(("parallel", "arbitrary")),
        name="moe_ffn",
    )(x1, gates, p, w["w13"], w["w2"], w["ws13"], w["ws2"], w["w_ple_gate"], w["w_ple"],
      w["ln2_g"], w["ln2_b"])


def _kv_rows(k, v, batch, seq, keep, g):
    cols = slice(g * GROUP_WIDTH, (g + 1) * GROUP_WIDTH)
    shape = (batch, keep, HEADS_PER_GROUP, HEAD_DIM)
    k_g = k.reshape(batch, seq, ATTN_WIDTH)[:, seq - keep:, cols].reshape(shape)
    v_g = v.reshape(batch, seq, ATTN_WIDTH)[:, seq - keep:, cols].reshape(shape)
    return jnp.stack([k_g, v_g], axis=2)


def _layer_prompt(x, p, w, ssm):
    batch, seq, _ = x.shape
    n = batch * seq
    x2 = x.reshape(n, D_MODEL)
    tabs = _rope_tables(jnp.arange(seq, dtype=jnp.int32))
    q, k, v, u = _in_proj(x2, w["w_in"], tabs, seq, 512)
    attn_o, attn_lse = _attn_prompt(q, k, v, batch, seq)
    zeros = jnp.zeros((batch, SSM_LANES), F32)
    y_tb, h_re, h_im = _s5_scan(u.reshape(seq * batch, SSM_WIDTH), ssm, zeros, zeros, batch, 128)
    x1, gates = _post_mixer(x2, attn_o, attn_lse, y_tb.reshape(seq, batch * SSM_WIDTH), w, seq, 512)
    y = _moe_ffn(x1, gates, p.reshape(n, PLE_DIM), w, 1024)
    kv = [_kv_rows(k, v, batch, seq, min(win, seq), g) for g, (win, _) in enumerate(DILATION_GROUPS)]
    h_last = jnp.stack([h_re, h_im], axis=-1).reshape(batch, SSM_GROUPS, SSM_STATE, 2)
    return y.reshape(batch, seq, D_MODEL), kv, h_last


def _layer_sample(x, p, caches, state, w, ssm):
    batch, seq, _ = x.shape
    assert seq == 1
    x2 = x.reshape(batch, D_MODEL)
    tabs = _rope_tables(jnp.full((batch,), PAST_LEN, dtype=jnp.int32))
    q, k, v, u = _in_proj(x2, w["w_in"], tabs, batch, batch)
    attn_o, attn_lse = _attn_sample(q, k, v, caches, 2)
    h0 = state.reshape(batch, SSM_LANES, 2)
    y_tb, h_re, h_im = _s5_scan(u, ssm, h0[..., 0], h0[..., 1], batch, 1)
    x1, gates = _post_mixer(x2, attn_o, attn_lse, y_tb, w, batch, batch)
    y = _moe_ffn(x1, gates, p.reshape(batch, PLE_DIM), w, batch)
    kv = [_kv_rows(k, v, batch, 1, 1, g) for g in range(len(DILATION_GROUPS))]
    h_last = jnp.stack([h_re, h_im], axis=-1).reshape(batch, SSM_GROUPS, SSM_STATE, 2)
    return y.reshape(batch, 1, D_MODEL), kv, h_last


def kernel(x_prompt, x_sample, cache_kv_w128, cache_kv_w512, cache_kv_w2048, state_ssm, p_prompt, p_sample,
           w_in, a_re, a_im, log_dt, b_re, b_im, c_re, c_im, d_skip, w_glu, b_glu, w_attn_br, w_ssm_br,
           w_gate, b_gate, w_out, ln1_g, ln1_b, w_router, router_bias, w1, w3, w2, ws1, ws3, ws2,
           w_ple_gate, w_ple, ln2_g, ln2_b):
    assert w_in.shape[0] == DEPTH == 1
    l = 0
    row = lambda t: t[l].reshape(1, -1)
    w = {
        "w_in": w_in[l].astype(BF16),
        "w_glu": w_glu[l].astype(BF16), "b_glu": row(b_glu),
        "w_gate": w_gate[l].astype(BF16), "b_gate": row(b_gate),
        "w_attn_br": w_attn_br[l].astype(BF16), "w_ssm_br": w_ssm_br[l].astype(BF16),
        "w_out": w_out[l].astype(BF16), "ln1_g": row(ln1_g), "ln1_b": row(ln1_b),
        "w_router": w_router[l], "router_bias": row(router_bias),
        "w13": jnp.concatenate([w1[l], w3[l]], axis=-1).astype(BF16), "w2": w2[l].astype(BF16),
        "ws13": jnp.concatenate([ws1[l], ws3[l]], axis=-1).astype(BF16), "ws2": ws2[l].astype(BF16),
        "w_ple_gate": w_ple_gate[l].astype(BF16), "w_ple": w_ple[l].astype(BF16),
        "ln2_g": row(ln2_g), "ln2_b": row(ln2_b),
    }
    ssm = _s5_params(a_re[l], a_im[l], log_dt[l], b_re[l], b_im[l], c_re[l], c_im[l], d_skip[l])
    yp, kv_p, h_p = _layer_prompt(x_prompt, p_prompt[l], w, ssm)
    caches = (cache_kv_w128[l], cache_kv_w512[l], cache_kv_w2048[l])
    ys, kv_s, h_s = _layer_sample(x_sample, p_sample[l], caches, state_ssm[l], w, ssm)
    return (yp, ys, kv_p[0][None], kv_s[0][None], kv_p[1][None], kv_s[1][None],
            kv_p[2][None], kv_s[2][None], h_p[None], h_s[None])
```

```python
import functools
import math

import jax
import jax.numpy as jnp
from jax import lax
from jax.experimental import pallas as pl
from jax.experimental.pallas import tpu as pltpu
from jax.experimental.pallas import tpu_sc as plsc

F32 = jnp.float32
BF16 = jnp.bfloat16

D_MODEL = 1024
HEAD_DIM = 64
HEADS_PER_GROUP = 4
DILATION_GROUPS = ((128, 1), (512, 4), (2048, 16))
N_BACK = 128
GROUP_WIDTH = HEADS_PER_GROUP * HEAD_DIM
ATTN_WIDTH = 3 * GROUP_WIDTH
ROPE_THETA = 10000.0
SSM_WIDTH = 256
SSM_GROUP = 16
SSM_GROUPS = 16
SSM_STATE = 64
SSM_LANES = SSM_GROUPS * SSM_STATE
IN_WIDTH = 3 * ATTN_WIDTH + SSM_WIDTH
N_EXPERTS = 64
TOP_K = 8
EXPERT_FF = 256
ROUTED_SCALE = 2.5
PLE_DIM = 256
DEPTH = 1
PAST_LEN = 8192
DN_ALPHA = (2.0 * DEPTH) ** 0.25
LN_EPS = 1e-5

LANES = 128
ROW_TILE = D_MODEL // LANES
SC_CORES = 2
SC_SUBCORES = 16
SC_WINDOW = 32
MOE_BLOCK = 256
ATTN_CHUNK = 2048
VMEM_LIMIT = 56 * 1024 * 1024


def _params(semantics):
    return pltpu.CompilerParams(dimension_semantics=semantics, vmem_limit_bytes=VMEM_LIMIT)


def _full(shape):
    return pl.BlockSpec(shape, lambda *_: (0,) * len(shape))


def _in_proj_kernel(x_ref, w_ref, cos_ref, sina_ref, sinb_ref, q_ref, k_ref, v_ref, u_ref):
    xb = x_ref[...].astype(BF16)
    cos = cos_ref[...]
    sin_a = sina_ref[...]
    sin_b = sinb_ref[...]

    def rope_store(col0, out_ref, scale):
        t = jnp.dot(xb, w_ref[:, col0:col0 + ATTN_WIDTH], preferred_element_type=F32)
        for c in range(ATTN_WIDTH // LANES):
            xc = t[:, c * LANES:(c + 1) * LANES]
            r = xc * cos + pltpu.roll(xc, LANES - 32, 1) * sin_a + pltpu.roll(xc, 32, 1) * sin_b
            out_ref[:, c * LANES:(c + 1) * LANES] = r * scale if scale != 1.0 else r

    rope_store(0, q_ref, HEAD_DIM ** -0.5)
    rope_store(ATTN_WIDTH, k_ref, 1.0)
    v_ref[...] = jnp.dot(xb, w_ref[:, 2 * ATTN_WIDTH:3 * ATTN_WIDTH], preferred_element_type=F32)
    u_ref[...] = jnp.dot(xb, w_ref[:, 3 * ATTN_WIDTH:], preferred_element_type=F32)


def _in_proj(x, w_in_bf, rope_tabs, rows_per_seq, tm):
    n = x.shape[0]
    tiles_per_seq = rows_per_seq // tm
    n_seq = n // rows_per_seq
    tab_tiles = rope_tabs[0].shape[0] // tm
    tab_spec = pl.BlockSpec((tm, LANES), lambda i: (i % tab_tiles, 0))
    row_spec = pl.BlockSpec((tm, ATTN_WIDTH), lambda i: (i, 0))
    return pl.pallas_call(
        _in_proj_kernel,
        grid=(n // tm,),
        in_specs=[pl.BlockSpec((tm, D_MODEL), lambda i: (i, 0)), _full((D_MODEL, IN_WIDTH)),
                  tab_spec, tab_spec, tab_spec],
        out_specs=[row_spec, row_spec, row_spec,
                   pl.BlockSpec((tm, SSM_WIDTH), lambda i: (i % tiles_per_seq, i // tiles_per_seq))],
        out_shape=[jax.ShapeDtypeStruct((n, ATTN_WIDTH), F32)] * 3
        + [jax.ShapeDtypeStruct((rows_per_seq, n_seq * SSM_WIDTH), F32)],
        compiler_params=_params(("parallel",)),
        name="in_proj",
    )(x, w_in_bf, *rope_tabs)


def _rope_tables(pos):
    half = HEAD_DIM // 2
    inv = ROPE_THETA ** (-jnp.arange(half, dtype=F32) / half)
    ang = pos.astype(F32)[:, None] * inv[None, :]
    cos = jnp.tile(jnp.cos(ang), (1, LANES // half))
    sin = jnp.tile(jnp.sin(ang), (1, LANES // half))
    first_half = (jnp.arange(LANES) % HEAD_DIM) < half
    sin_a = jnp.where(first_half[None, :], -sin, 0.0)
    sin_b = jnp.where(first_half[None, :], 0.0, sin)
    return cos, sin_a, sin_b


def _band_attention(q, k, v, mask):
    o_parts, lse_parts = [], []
    for h in range(LANES // HEAD_DIM):
        cols = slice(h * HEAD_DIM, (h + 1) * HEAD_DIM)
        logits = lax.dot_general(q[:, cols].astype(BF16), k[:, cols].astype(BF16),
                                 (((1,), (1,)), ((), ())), preferred_element_type=F32) + mask
        m = jnp.max(logits, axis=1, keepdims=True)
        p = jnp.exp(logits - m)
        l = jnp.sum(p, axis=1, keepdims=True)
        pv = jnp.dot(p.astype(BF16), v[:, cols].astype(BF16), preferred_element_type=F32)
        o_parts.append(pv * (1.0 / l))
        lse_parts.append(jnp.broadcast_to(m + jnp.log(l), (N_BACK, HEAD_DIM)))
    return jnp.concatenate(o_parts, axis=1), jnp.concatenate(lse_parts, axis=1)


def _attn_prompt_kernel(q_ref, kp_ref, kc_ref, vp_ref, vc_ref, o_ref, lse_ref):
    c = pl.program_id(1)
    g = pl.program_id(3)
    ch = ATTN_CHUNK
    qi = lax.broadcasted_iota(jnp.int32, (N_BACK, 2 * N_BACK), 0)
    kj = lax.broadcasted_iota(jnp.int32, (N_BACK, 2 * N_BACK), 1)
    dist = qi + N_BACK - kj
    band = jnp.where(dist >= 0, jnp.where(dist <= N_BACK, 0.0, -jnp.inf), -jnp.inf)
    band_first = jnp.where(kj >= N_BACK, band, -jnp.inf)

    def group_body(d):
        span = N_BACK * d
        n_sub = ch // N_BACK

        def rows(start, size):
            return pl.ds(start, size) if d == 1 else pl.ds(start, size, stride=d)

        def store(q0, o, lse):
            o_ref[rows(q0, N_BACK), :] = o
            lse_ref[rows(q0, N_BACK), :] = lse

        def head_block(r, carry):
            k = jnp.concatenate([kp_ref[rows(ch - span + r, N_BACK), :], kc_ref[rows(r, N_BACK), :]], axis=0)
            v = jnp.concatenate([vp_ref[rows(ch - span + r, N_BACK), :], vc_ref[rows(r, N_BACK), :]], axis=0)
            mask = jnp.where(c == 0, band_first, band)
            store(r, *_band_attention(q_ref[rows(r, N_BACK), :], k, v, mask))
            return carry

        def inner_block(idx, carry):
            s = idx // d
            r = idx % d
            k0 = (s - 1) * span + r
            store(s * span + r, *_band_attention(q_ref[rows(s * span + r, N_BACK), :],
                                                 kc_ref[rows(k0, 2 * N_BACK), :],
                                                 vc_ref[rows(k0, 2 * N_BACK), :], band))
            return carry

        lax.fori_loop(0, d, head_block, 0, unroll=min(d, 2))
        if n_sub > d:
            lax.fori_loop(d, n_sub, inner_block, 0, unroll=2 if (n_sub - d) % 2 == 0 else 3)

    for gi, (_, d) in enumerate(DILATION_GROUPS):
        pl.when(g == gi)(functools.partial(group_body, d))


def _attn_prompt(q, k, v, batch, seq):
    ch = ATTN_CHUNK
    cps = seq // ch
    n = batch * seq
    pairs = GROUP_WIDTH // LANES
    cur = lambda b, c, hp, g: (b * cps + c, g * pairs + hp)
    prev = lambda b, c, hp, g: (b * cps + jnp.maximum(c - 1, 0), g * pairs + hp)
    blk = (ch, LANES)
    return pl.pallas_call(
        _attn_prompt_kernel,
        grid=(batch, cps, pairs, len(DILATION_GROUPS)),
        in_specs=[pl.BlockSpec(blk, cur), pl.BlockSpec(blk, prev), pl.BlockSpec(blk, cur),
                  pl.BlockSpec(blk, prev), pl.BlockSpec(blk, cur)],
        out_specs=[pl.BlockSpec(blk, cur), pl.BlockSpec(blk, cur)],
        out_shape=[jax.ShapeDtypeStruct((n, ATTN_WIDTH), F32)] * 2,
        compiler_params=_params(("parallel", "parallel", "parallel", "parallel")),
        name="attn_prompt",
    )(q, k, k, v, v)


def _attn_sample_kernel(q_ref, k_ref, v_ref, c0_ref, c1_ref, c2_ref, o_ref, lse_ref):
    bt = q_ref.shape[0]
    for b in range(bt):
        for g, (c_ref, (win, d)) in enumerate(zip((c0_ref, c1_ref, c2_ref), DILATION_GROUPS)):
            pos = lax.broadcasted_iota(jnp.int32, (1, win), 1)
            off_stride = (pos % d) != 0
            for h in range(HEADS_PER_GROUP):
                j = g * HEADS_PER_GROUP + h
                q = q_ref[b, :, j:j + 1]
                k_new = k_ref[b, :, j:j + 1]
                v_new = v_ref[b, :, j:j + 1]
                s_c = jnp.sum(c_ref[b, 0, h] * q, axis=0, keepdims=True)
                s_c = jnp.where(off_stride, -jnp.inf, s_c)
                s_new = jnp.sum(k_new * q, axis=0, keepdims=True)
                m = jnp.maximum(jnp.max(s_c, axis=1, keepdims=True), s_new)
                p_c = jnp.exp(s_c - m)
                p_new = jnp.exp(s_new - m)
                l = jnp.sum(p_c, axis=1, keepdims=True) + p_new
                num = jnp.sum(c_ref[b, 1, h] * p_c, axis=1, keepdims=True) + p_new * v_new
                o_ref[b, :, j:j + 1] = num * (1.0 / l)
                lse_ref[b, :, j:j + 1] = m + jnp.log(l)


def _attn_sample(q, k, v, caches, bt):
    b = q.shape[0]
    n_heads = ATTN_WIDTH // HEAD_DIM
    views, specs = [], []
    for cache, (win, d) in zip(caches, DILATION_GROUPS):
        assert cache.shape[1] == win == N_BACK * d
        views.append(jnp.transpose(cache, (0, 2, 3, 4, 1)))
        specs.append(pl.BlockSpec((bt, 2, HEADS_PER_GROUP, HEAD_DIM, win), lambda i: (i, 0, 0, 0, 0)))
    col_spec = pl.BlockSpec((bt, HEAD_DIM, n_heads), lambda i: (i, 0, 0))
    lse_spec = pl.BlockSpec((bt, 1, n_heads), lambda i: (i, 0, 0))
    cols = lambda t: jnp.transpose(t.reshape(b, n_heads, HEAD_DIM), (0, 2, 1))
    o, lse = pl.pallas_call(
        _attn_sample_kernel,
        grid=(b // bt,),
        in_specs=[col_spec, col_spec, col_spec] + specs,
        out_specs=[col_spec, lse_spec],
        out_shape=[jax.ShapeDtypeStruct((b, HEAD_DIM, n_heads), F32),
                   jax.ShapeDtypeStruct((b, 1, n_heads), F32)],
        compiler_params=_params(("parallel",)),
        name="attn_sample",
    )(cols(q), cols(k), cols(v), *views)
    o = jnp.transpose(o, (0, 2, 1)).reshape(b, ATTN_WIDTH)
    lse = jnp.broadcast_to(jnp.transpose(lse, (0, 2, 1)), (b, n_heads, HEAD_DIM)).reshape(b, ATTN_WIDTH)
    return o, lse


def _s5_scan_kernel(u_ref, bmat_ref, cmat_ref, are_ref, aim_ref, d_ref, h0re_ref, h0im_ref,
                    y_ref, hre_ref, him_ref, hist_sc, *, bg, steps):
    t_chunk = pl.program_id(0)

    @pl.when(t_chunk == 0)
    def _():
        hre_ref[...] = h0re_ref[...]
        him_ref[...] = h0im_ref[...]

    u = u_ref[...]
    hist_sc[...] = jnp.dot(u.astype(BF16), bmat_ref[...], preferred_element_type=F32)
    a_re = jnp.broadcast_to(are_ref[...], (bg, SSM_LANES))
    a_im = jnp.broadcast_to(aim_ref[...], (bg, SSM_LANES))

    def step(t, carry):
        h_re, h_im = carry
        rows = pl.ds(pl.multiple_of(t * bg, bg), bg)
        n_re = a_re * h_re - a_im * h_im + hist_sc[rows, 0:SSM_LANES]
        n_im = a_re * h_im + a_im * h_re + hist_sc[rows, SSM_LANES:2 * SSM_LANES]
        hist_sc[rows, 0:SSM_LANES] = n_re
        hist_sc[rows, SSM_LANES:2 * SSM_LANES] = n_im
        return n_re, n_im

    h_re, h_im = lax.fori_loop(0, steps, step, (hre_ref[...], him_ref[...]))
    hre_ref[...] = h_re
    him_ref[...] = h_im
    y_ref[...] = (jnp.dot(hist_sc[...].astype(BF16), cmat_ref[...], preferred_element_type=F32)
                  + d_ref[...] * u)


def _s5_scan(u_tb, ssm, h0_re, h0_im, bg, steps):
    rows = u_tb.shape[0]
    blk = steps * bg
    kern = functools.partial(_s5_scan_kernel, bg=bg, steps=steps)
    state_spec = _full((bg, SSM_LANES))
    return pl.pallas_call(
        kern,
        grid=(rows // blk,),
        in_specs=[pl.BlockSpec((blk, SSM_WIDTH), lambda i: (i, 0)),
                  _full((SSM_WIDTH, 2 * SSM_LANES)), _full((2 * SSM_LANES, SSM_WIDTH)),
                  _full((1, SSM_LANES)), _full((1, SSM_LANES)), _full((1, SSM_WIDTH)),
                  state_spec, state_spec],
        out_specs=[pl.BlockSpec((blk, SSM_WIDTH), lambda i: (i, 0)), state_spec, state_spec],
        out_shape=[jax.ShapeDtypeStruct((rows, SSM_WIDTH), F32),
                   jax.ShapeDtypeStruct((bg, SSM_LANES), F32), jax.ShapeDtypeStruct((bg, SSM_LANES), F32)],
        scratch_shapes=[pltpu.VMEM((blk, 2 * SSM_LANES), F32)],
        compiler_params=_params(("arbitrary",)),
        name="s5_scan",
    )(u_tb, ssm["bmat"], ssm["cmat"], ssm["a_re"], ssm["a_im"], ssm["d_skip"], h0_re, h0_im)


def _s5_params(a_re, a_im, log_dt, b_re, b_im, c_re, c_im, d_skip):
    dt = jnp.exp(log_dt)[:, None]
    mag = jnp.exp(a_re * dt)
    abar_re = mag * jnp.cos(a_im * dt)
    abar_im = mag * jnp.sin(a_im * dt)
    a2 = a_re * a_re + a_im * a_im
    nr = abar_re - 1.0
    coef_re = (nr * a_re + abar_im * a_im) / a2
    coef_im = (abar_im * a_re - nr * a_im) / a2
    bb_re = coef_re[..., None] * b_re - coef_im[..., None] * b_im
    bb_im = coef_re[..., None] * b_im + coef_im[..., None] * b_re
    eye = jnp.eye(SSM_GROUPS, dtype=F32)
    to_b = lambda t: jnp.einsum("gpc,gh->gchp", t, eye).reshape(SSM_WIDTH, SSM_LANES)
    to_c = lambda t: jnp.einsum("gcp,gh->gphc", t, eye).reshape(SSM_LANES, SSM_WIDTH)
    return {
        "bmat": jnp.concatenate([to_b(bb_re), to_b(bb_im)], axis=1).astype(BF16),
        "cmat": jnp.concatenate([to_c(c_re), -to_c(c_im)], axis=0).astype(BF16),
        "a_re": abar_re.reshape(1, SSM_LANES), "a_im": abar_im.reshape(1, SSM_LANES),
        "d_skip": d_skip.reshape(1, SSM_WIDTH),
    }


def _layer_norm(z, g, b):
    mu = jnp.mean(z, axis=-1, keepdims=True)
    zc = z - mu
    var = jnp.mean(zc * zc, axis=-1, keepdims=True)
    return zc * lax.rsqrt(var + LN_EPS) * g + b


def _merge_groups(o, lse):
    parts = [slice(g * GROUP_WIDTH, (g + 1) * GROUP_WIDTH) for g in range(len(DILATION_GROUPS))]
    top = lse[:, parts[0]]
    for cols in parts[1:]:
        top = jnp.maximum(top, lse[:, cols])
    num = den = None
    for cols in parts:
        w = jnp.exp(lse[:, cols] - top)
        num = w * o[:, cols] if num is None else num + w * o[:, cols]
        den = w if den is None else den + w
    return num / den


def _post_mixer_kernel(x_ref, ao_ref, lse_ref, y_ref, wglu_ref, bglu_ref, wgate_ref, bgate_ref, wab_ref, wsb_ref,
                       wout_ref, lng_ref, lnb_ref, wr_ref, rb_ref,
                       x1_ref, x1t_ref, gate_ref, idx_ref, topg_ref, cnt_ref):
    x = x_ref[...]
    xb = x.astype(BF16)
    s = jax.nn.gelu(y_ref[...])
    s = s * jax.nn.sigmoid(jnp.dot(s.astype(BF16), wglu_ref[...], preferred_element_type=F32) + bglu_ref[...])
    gates = jax.nn.sigmoid(jnp.dot(xb, wgate_ref[...], preferred_element_type=F32) + bgate_ref[...])
    attn_o = _merge_groups(ao_ref[...], lse_ref[...])
    attn_br = jnp.dot(attn_o.astype(BF16), wab_ref[...], preferred_element_type=F32)
    ssm_br = jnp.dot(s.astype(BF16), wsb_ref[...], preferred_element_type=F32)
    merged = gates[:, :D_MODEL] * attn_br + gates[:, D_MODEL:] * ssm_br
    mix = jnp.dot(merged.astype(BF16), wout_ref[...], preferred_element_type=F32)
    x1 = _layer_norm(DN_ALPHA * x + mix, lng_ref[...], lnb_ref[...])
    x1_ref[...] = x1
    tm = x1.shape[0]
    for j in range(ROW_TILE):
        x1t_ref[pl.ds(j, tm, stride=ROW_TILE), :] = x1[:, j * LANES:(j + 1) * LANES]

    scores = jax.nn.sigmoid(jnp.dot(x1, wr_ref[...], preferred_element_type=F32,
                                    precision=lax.Precision.HIGHEST))
    sel = scores + rb_ref[...]
    lane = lax.broadcasted_iota(jnp.int32, sel.shape, 1).astype(F32)
    slot_lane = lax.broadcasted_iota(jnp.int32, (tm, LANES), 1)
    chosen = jnp.zeros(sel.shape, F32)
    top_idx = jnp.zeros((tm, LANES), F32)
    top_s = jnp.zeros((tm, LANES), F32)
    for k in range(TOP_K):
        top = jnp.max(sel, axis=-1, keepdims=True)
        first = jnp.min(jnp.where(sel == top, lane, float(N_EXPERTS)), axis=-1, keepdims=True)
        hit = lane == first
        chosen = jnp.where(hit, 1.0, chosen)
        sel = jnp.where(hit, -jnp.inf, sel)
        top_idx = jnp.where(slot_lane == k, first, top_idx)
        top_s = jnp.where(slot_lane == k, jnp.sum(jnp.where(hit, scores, 0.0), axis=-1, keepdims=True), top_s)
    norm = ROUTED_SCALE / jnp.sum(scores * chosen, axis=-1, keepdims=True)
    gate_ref[...] = scores * chosen * norm
    idx_ref[...] = top_idx
    topg_ref[...] = top_s * norm

    @pl.when(pl.program_id(0) == 0)
    def _():
        cnt_ref[...] = jnp.zeros(cnt_ref.shape, F32)
    cnt_ref[...] += jnp.sum(chosen, axis=0, keepdims=True)


def _post_mixer(x, attn_o, attn_lse, y_tb, w, rows_per_seq, tm):
    n = x.shape[0]
    tiles_per_seq = rows_per_seq // tm
    row = lambda width: pl.BlockSpec((tm, width), lambda i: (i, 0))
    return pl.pallas_call(
        _post_mixer_kernel,
        grid=(n // tm,),
        in_specs=[row(D_MODEL), row(ATTN_WIDTH), row(ATTN_WIDTH),
                  pl.BlockSpec((tm, SSM_WIDTH), lambda i: (i % tiles_per_seq, i // tiles_per_seq)),
                  _full((SSM_WIDTH, SSM_WIDTH)), _full((1, SSM_WIDTH)),
                  _full((D_MODEL, 2 * D_MODEL)), _full((1, 2 * D_MODEL)),
                  _full((GROUP_WIDTH, D_MODEL)), _full((SSM_WIDTH, D_MODEL)), _full((D_MODEL, D_MODEL)),
                  _full((1, D_MODEL)), _full((1, D_MODEL)),
                  _full((D_MODEL, N_EXPERTS)), _full((1, N_EXPERTS))],
        out_specs=[row(D_MODEL), pl.BlockSpec((tm * ROW_TILE, LANES), lambda i: (i, 0)),
                   row(N_EXPERTS), row(LANES), row(LANES), _full((1, N_EXPERTS))],
        out_shape=[jax.ShapeDtypeStruct((n, D_MODEL), F32), jax.ShapeDtypeStruct((n * ROW_TILE, LANES), F32),
                   jax.ShapeDtypeStruct((n, N_EXPERTS), F32), jax.ShapeDtypeStruct((n, LANES), F32),
                   jax.ShapeDtypeStruct((n, LANES), F32), jax.ShapeDtypeStruct((1, N_EXPERTS), F32)],
        compiler_params=_params(("arbitrary",)),
        name="post_mixer",
    )(x, attn_o, attn_lse, y_tb, w["w_glu"], w["b_glu"], w["w_gate"], w["b_gate"], w["w_attn_br"], w["w_ssm_br"],
      w["w_out"], w["ln1_g"], w["ln1_b"], w["w_router"], w["router_bias"])


def _moe_ffn_kernel(x_ref, gate_ref, p_ref, w13_ref, w2_ref, ws13_ref, ws2_ref, wpg_ref, wple_ref,
                    lng_ref, lnb_ref, o_ref, acc_sc, xb_sc):
    e = pl.program_id(1)

    def glu_ffn(xb, w13, w2, row_scale):
        h13 = jnp.dot(xb, w13, preferred_element_type=F32)
        h = jax.nn.silu(h13[:, :EXPERT_FF]) * h13[:, EXPERT_FF:]
        if row_scale is not None:
            h = h * row_scale
        return jnp.dot(h.astype(BF16), w2, preferred_element_type=F32)

    @pl.when(e == 0)
    def _():
        xb = x_ref[...].astype(BF16)
        xb_sc[...] = xb
        ple = (jax.nn.sigmoid(jnp.dot(xb, wpg_ref[...], preferred_element_type=F32))
               * jnp.dot(p_ref[...].astype(BF16), wple_ref[...], preferred_element_type=F32))
        acc_sc[...] = glu_ffn(xb, ws13_ref[...], ws2_ref[...], None) + ple

    gates = gate_ref[...]
    lane = lax.broadcasted_iota(jnp.int32, gates.shape, 1)
    g_col = jnp.sum(jnp.where(lane == e, gates, 0.0), axis=-1, keepdims=True)
    acc_sc[...] += glu_ffn(xb_sc[...], w13_ref[0], w2_ref[0], g_col)

    @pl.when(e == N_EXPERTS - 1)
    def _():
        o_ref[...] = _layer_norm(DN_ALPHA * x_ref[...] + acc_sc[...], lng_ref[...], lnb_ref[...])


def _moe_ffn(x1, gates, p, w, tm):
    n = x1.shape[0]
    row = lambda width: pl.BlockSpec((tm, width), lambda i, e: (i, 0))
    return pl.pallas_call(
        _moe_ffn_kernel,
        grid=(n // tm, N_EXPERTS),
        in_specs=[row(D_MODEL), row(N_EXPERTS), row(PLE_DIM),
                  pl.BlockSpec((1, D_MODEL, 2 * EXPERT_FF), lambda i, e: (e, 0, 0)),
                  pl.BlockSpec((1, EXPERT_FF, D_MODEL), lambda i, e: (e, 0, 0)),
                  _full((D_MODEL, 2 * EXPERT_FF)), _full((EXPERT_FF, D_MODEL)),
                  _full((D_MODEL, D_MODEL)), _full((PLE_DIM, D_MODEL)),
                  _full((1, D_MODEL)), _full((1, D_MODEL))],
        out_specs=row(D_MODEL),
        out_shape=jax.ShapeDtypeStruct((n, D_MODEL), F32),
        scratch_shapes=[pltpu.VMEM((tm, D_MODEL), F32), pltpu.VMEM((tm, D_MODEL), BF16)],
        compiler_params=_params(("parallel", "arbitrary")),
        name="moe_ffn",
    )(x1, gates, p, w["w13"], w["w2"], w["ws13"], w["ws2"], w["w_ple_gate"], w["w_ple"],
      w["ln2_g"], w["ln2_b"])


def _route_kernel(idx_ref, pstart_ref, slot_ref, base_sc):
    @pl.when(pl.program_id(0) == 0)
    def _():
        base_sc[...] = jnp.zeros(base_sc.shape, F32)

    idx = idx_ref[...]
    tm = idx.shape[0]
    lane = lax.broadcasted_iota(jnp.int32, (tm, N_EXPERTS), 1).astype(F32)
    hits = [lane == idx[:, k:k + 1] for k in range(TOP_K)]
    member = jnp.zeros((tm, N_EXPERTS), F32)
    for hit in hits:
        member = member + jnp.where(hit, 1.0, 0.0)
    r = lax.broadcasted_iota(jnp.int32, (tm, tm), 0)
    c = lax.broadcasted_iota(jnp.int32, (tm, tm), 1)
    earlier = jnp.where(c < r, 1.0, 0.0).astype(BF16)
    row = (jnp.dot(earlier, member.astype(BF16), preferred_element_type=F32)
           + base_sc[...] + pstart_ref[...])
    slot_lane = lax.broadcasted_iota(jnp.int32, (tm, LANES), 1)
    out = jnp.zeros((tm, LANES), F32)
    for k, hit in enumerate(hits):
        out = jnp.where(slot_lane == k, jnp.sum(jnp.where(hit, row, 0.0), axis=-1, keepdims=True), out)
    slot_ref[...] = out.astype(jnp.int32)
    base_sc[...] += jnp.sum(member, axis=0, keepdims=True)


def _route(top_idx, pstart, tm):
    n = top_idx.shape[0]
    return pl.pallas_call(
        _route_kernel,
        grid=(n // tm,),
        in_specs=[pl.BlockSpec((tm, LANES), lambda i: (i, 0)), _full((1, N_EXPERTS))],
        out_specs=pl.BlockSpec((tm, LANES), lambda i: (i, 0)),
        out_shape=jax.ShapeDtypeStruct((n, LANES), jnp.int32),
        scratch_shapes=[pltpu.VMEM((1, N_EXPERTS), F32)],
        compiler_params=_params(("arbitrary",)),
        name="route",
    )(top_idx, pstart)


def _sc_mesh():
    return plsc.VectorSubcoreMesh(core_axis_name="c", subcore_axis_name="s",
                                  num_cores=SC_CORES, num_subcores=SC_SUBCORES)


def _sc_dispatch(x_tiles, slots, n_rows):
    n = x_tiles.shape[0]
    wins_per_worker = n // SC_WINDOW // (SC_CORES * SC_SUBCORES)

    def body(x_hbm, slot_hbm, xs_hbm, idx_v, rows_v):
        wid = lax.axis_index("s") * SC_CORES + lax.axis_index("c")

        @pl.loop(0, wins_per_worker)
        def _(i):
            win = wid * wins_per_worker + i
            pltpu.sync_copy(slot_hbm.at[win], idx_v)
            pltpu.sync_copy(x_hbm.at[pl.ds(win * SC_WINDOW, SC_WINDOW)], rows_v)
            for k in range(TOP_K):
                pltpu.sync_copy(rows_v, xs_hbm.at[idx_v.at[k]])

    return pl.kernel(
        body, out_type=jax.ShapeDtypeStruct((n_rows, ROW_TILE, LANES), F32), mesh=_sc_mesh(),
        scratch_types=[pltpu.VMEM((TOP_K, SC_WINDOW), jnp.int32),
                       pltpu.VMEM((SC_WINDOW, ROW_TILE, LANES), F32)],
        name="sc_dispatch",
    )(x_tiles, slots)


def _sc_combine(y_tiles, slots, n):
    wins_per_worker = n // SC_WINDOW // (SC_CORES * SC_SUBCORES)

    def body(ys_hbm, slot_hbm, yg_hbm, idx_v, rows_v):
        wid = lax.axis_index("s") * SC_CORES + lax.axis_index("c")

        @pl.loop(0, wins_per_worker)
        def _(i):
            win = wid * wins_per_worker + i
            pltpu.sync_copy(slot_hbm.at[win], idx_v)
            for k in range(TOP_K):
                pltpu.sync_copy(ys_hbm.at[idx_v.at[k]], rows_v)
                pltpu.sync_copy(rows_v, yg_hbm.at[k, pl.ds(win * SC_WINDOW, SC_WINDOW)])

    return pl.kernel(
        body, out_type=jax.ShapeDtypeStruct((TOP_K, n, ROW_TILE, LANES), F32), mesh=_sc_mesh(),
        scratch_types=[pltpu.VMEM((TOP_K, SC_WINDOW), jnp.int32),
                       pltpu.VMEM((SC_WINDOW, ROW_TILE, LANES), F32)],
        name="sc_combine",
    )(y_tiles, slots)


def _expert_ffn_kernel(bexp_ref, nused_ref, xs_ref, w13_ref, w2_ref, ys_ref):
    del bexp_ref

    @pl.when(pl.program_id(0) < nused_ref[0])
    def _():
        tb = MOE_BLOCK
        x = jnp.concatenate([xs_ref[pl.ds(j, tb, stride=ROW_TILE), :] for j in range(ROW_TILE)], axis=1)
        h13 = jnp.dot(x.astype(BF16), w13_ref[0], preferred_element_type=F32)
        h = jax.nn.silu(h13[:, :EXPERT_FF]) * h13[:, EXPERT_FF:]
        y = jnp.dot(h.astype(BF16), w2_ref[0], preferred_element_type=F32)
        for j in range(ROW_TILE):
            ys_ref[pl.ds(j, tb, stride=ROW_TILE), :] = y[:, j * LANES:(j + 1) * LANES]


def _expert_ffn(xs_rows, block_expert, n_used, w):
    n_blocks = block_expert.shape[0]
    blk = (MOE_BLOCK * ROW_TILE, LANES)
    return pl.pallas_call(
        _expert_ffn_kernel,
        grid_spec=pltpu.PrefetchScalarGridSpec(
            num_scalar_prefetch=2, grid=(n_blocks,),
            in_specs=[pl.BlockSpec(blk, lambda i, be, nu: (i, 0)),
                      pl.BlockSpec((1, D_MODEL, 2 * EXPERT_FF), lambda i, be, nu: (be[i], 0, 0)),
                      pl.BlockSpec((1, EXPERT_FF, D_MODEL), lambda i, be, nu: (be[i], 0, 0))],
            out_specs=pl.BlockSpec(blk, lambda i, be, nu: (i, 0))),
        out_shape=jax.ShapeDtypeStruct(xs_rows.shape, F32),
        compiler_params=_params(("parallel",)),
        name="expert_ffn",
    )(block_expert, n_used, xs_rows, w["w13"], w["w2"])


def _moe_out_kernel(x_ref, g_ref, p_ref, yg_ref, ws13_ref, ws2_ref, wpg_ref, wple_ref, lng_ref, lnb_ref, o_ref):
    x = x_ref[...]
    xb = x.astype(BF16)
    tm = x.shape[0]
    g = g_ref[...]
    parts = []
    for j in range(ROW_TILE):
        acc = None
        for k in range(TOP_K):
            t = g[:, k:k + 1] * yg_ref[k, pl.ds(j, tm, stride=ROW_TILE), :]
            acc = t if acc is None else acc + t
        parts.append(acc)
    routed = jnp.concatenate(parts, axis=1)
    h13 = jnp.dot(xb, ws13_ref[...], preferred_element_type=F32)
    h = jax.nn.silu(h13[:, :EXPERT_FF]) * h13[:, EXPERT_FF:]
    shared = jnp.dot(h.astype(BF16), ws2_ref[...], preferred_element_type=F32)
    ple = (jax.nn.sigmoid(jnp.dot(xb, wpg_ref[...], preferred_element_type=F32))
           * jnp.dot(p_ref[...].astype(BF16), wple_ref[...], preferred_element_type=F32))
    o_ref[...] = _layer_norm(DN_ALPHA * x + routed + shared + ple, lng_ref[...], lnb_ref[...])


def _moe_out(x1, top_gates, p, yg_rows, w, tm):
    n = x1.shape[0]
    row = lambda width: pl.BlockSpec((tm, width), lambda i: (i, 0))
    return pl.pallas_call(
        _moe_out_kernel,
        grid=(n // tm,),
        in_specs=[row(D_MODEL), row(LANES), row(PLE_DIM),
                  pl.BlockSpec((TOP_K, tm * ROW_TILE, LANES), lambda i: (0, i, 0)),
                  _full((D_MODEL, 2 * EXPERT_FF)), _full((EXPERT_FF, D_MODEL)),
                  _full((D_MODEL, D_MODEL)), _full((PLE_DIM, D_MODEL)),
                  _full((1, D_MODEL)), _full((1, D_MODEL))],
        out_specs=row(D_MODEL),
        out_shape=jax.ShapeDtypeStruct((n, D_MODEL), F32),
        compiler_params=_params(("parallel",)),
        name="moe_out",
    )(x1, top_gates, p, yg_rows, w["ws13"], w["ws2"], w["w_ple_gate"], w["w_ple"], w["ln2_g"], w["ln2_b"])


def _moe_sorted(x1, x1_tiles, top_idx, top_gates, counts, p, w):
    n = x1.shape[0]
    n_blocks = n * TOP_K // MOE_BLOCK + N_EXPERTS
    n_rows = n_blocks * MOE_BLOCK
    cnt = counts.reshape(N_EXPERTS).astype(jnp.int32)
    padded = (cnt + MOE_BLOCK - 1) // MOE_BLOCK * MOE_BLOCK
    pend = jnp.cumsum(padded)
    pstart = (pend - padded).astype(F32).reshape(1, N_EXPERTS)
    block_expert = jnp.minimum(
        jnp.searchsorted(pend, jnp.arange(n_blocks, dtype=jnp.int32) * MOE_BLOCK, side="right"),
        N_EXPERTS - 1).astype(jnp.int32)
    n_used = (pend[-1:] // MOE_BLOCK).astype(jnp.int32)
    slots = _route(top_idx, pstart, 512)[:, :TOP_K]
    slots = jnp.transpose(slots.reshape(n // SC_WINDOW, SC_WINDOW, TOP_K), (0, 2, 1))
    xs = _sc_dispatch(x1_tiles.reshape(n, ROW_TILE, LANES), slots, n_rows)
    ys = _expert_ffn(xs.reshape(n_rows * ROW_TILE, LANES), block_expert, n_used, w)
    yg = _sc_combine(ys.reshape(n_rows, ROW_TILE, LANES), slots, n)
    return _moe_out(x1, top_gates, p, yg.reshape(TOP_K, n * ROW_TILE, LANES), w, 256)


def _kv_rows(k, v, batch, seq, keep, g):
    cols = slice(g * GROUP_WIDTH, (g + 1) * GROUP_WIDTH)
    shape = (batch, keep, HEADS_PER_GROUP, HEAD_DIM)
    k_g = k.reshape(batch, seq, ATTN_WIDTH)[:, seq - keep:, cols].reshape(shape)
    v_g = v.reshape(batch, seq, ATTN_WIDTH)[:, seq - keep:, cols].reshape(shape)
    return jnp.stack([k_g, v_g], axis=2)


def _layer_prompt(x, p, w, ssm):
    batch, seq, _ = x.shape
    n = batch * seq
    x2 = x.reshape(n, D_MODEL)
    tabs = _rope_tables(jnp.arange(seq, dtype=jnp.int32))
    q, k, v, u = _in_proj(x2, w["w_in"], tabs, seq, 512)
    attn_o, attn_lse = _attn_prompt(q, k, v, batch, seq)
    zeros = jnp.zeros((batch, SSM_LANES), F32)
    y_tb, h_re, h_im = _s5_scan(u.reshape(seq * batch, SSM_WIDTH), ssm, zeros, zeros, batch, 128)
    x1, x1_tiles, _, top_idx, top_gates, counts = _post_mixer(
        x2, attn_o, attn_lse, y_tb.reshape(seq, batch * SSM_WIDTH), w, seq, 512)
    y = _moe_sorted(x1, x1_tiles, top_idx, top_gates, counts, p.reshape(n, PLE_DIM), w)
    kv = [_kv_rows(k, v, batch, seq, min(win, seq), g) for g, (win, _) in enumerate(DILATION_GROUPS)]
    h_last = jnp.stack([h_re, h_im], axis=-1).reshape(batch, SSM_GROUPS, SSM_STATE, 2)
    return y.reshape(batch, seq, D_MODEL), kv, h_last


def _layer_sample(x, p, caches, state, w, ssm):
    batch, seq, _ = x.shape
    assert seq == 1
    x2 = x.reshape(batch, D_MODEL)
    tabs = _rope_tables(jnp.full((batch,), PAST_LEN, dtype=jnp.int32))
    q, k, v, u = _in_proj(x2, w["w_in"], tabs, batch, batch)
    attn_o, attn_lse = _attn_sample(q, k, v, caches, 2)
    h0 = state.reshape(batch, SSM_LANES, 2)
    y_tb, h_re, h_im = _s5_scan(u, ssm, h0[..., 0], h0[..., 1], batch, 1)
    x1, _, gates, _, _, _ = _post_mixer(x2, attn_o, attn_lse, y_tb, w, batch, batch)
    y = _moe_ffn(x1, gates, p.reshape(batch, PLE_DIM), w, batch)
    kv = [_kv_rows(k, v, batch, 1, 1, g) for g in range(len(DILATION_GROUPS))]
    h_last = jnp.stack([h_re, h_im], axis=-1).reshape(batch, SSM_GROUPS, SSM_STATE, 2)
    return y.reshape(batch, 1, D_MODEL), kv, h_last


def kernel(x_prompt, x_sample, cache_kv_w128, cache_kv_w512, cache_kv_w2048, state_ssm, p_prompt, p_sample,
           w_in, a_re, a_im, log_dt, b_re, b_im, c_re, c_im, d_skip, w_glu, b_glu, w_attn_br, w_ssm_br,
           w_gate, b_gate, w_out, ln1_g, ln1_b, w_router, router_bias, w1, w3, w2, ws1, ws3, ws2,
           w_ple_gate, w_ple, ln2_g, ln2_b):
    assert w_in.shape[0] == DEPTH == 1
    l = 0
    row = lambda t: t[l].reshape(1, -1)
    w = {
        "w_in": w_in[l].astype(BF16),
        "w_glu": w_glu[l].astype(BF16), "b_glu": row(b_glu),
        "w_gate": w_gate[l].astype(BF16), "b_gate": row(b_gate),
        "w_attn_br": w_attn_br[l].astype(BF16), "w_ssm_br": w_ssm_br[l].astype(BF16),
        "w_out": w_out[l].astype(BF16), "ln1_g": row(ln1_g), "ln1_b": row(ln1_b),
        "w_router": w_router[l], "router_bias": row(router_bias),
        "w13": jnp.concatenate([w1[l], w3[l]], axis=-1).astype(BF16), "w2": w2[l].astype(BF16),
        "ws13": jnp.concatenate([ws1[l], ws3[l]], axis=-1).astype(BF16), "ws2": ws2[l].astype(BF16),
        "w_ple_gate": w_ple_gate[l].astype(BF16), "w_ple": w_ple[l].astype(BF16),
        "ln2_g": row(ln2_g), "ln2_b": row(ln2_b),
    }
    ssm = _s5_params(a_re[l], a_im[l], log_dt[l], b_re[l], b_im[l], c_re[l], c_im[l], d_skip[l])
    yp, kv_p, h_p = _layer_prompt(x_prompt, p_prompt[l], w, ssm)
    caches = (cache_kv_w128[l], cache_kv_w512[l], cache_kv_w2048[l])
    ys, kv_s, h_s = _layer_sample(x_sample, p_sample[l], caches, state_ssm[l], w, ssm)
    return (yp, ys, kv_p[0][None], kv_s[0][None], kv_p[1][None], kv_s[1][None],
            kv_p[2][None], kv_s[2][None], h_p[None], h_s[None])
```

```python
import functools
import math

import jax
import jax.numpy as jnp
from jax import lax
from jax.experimental import pallas as pl
from jax.experimental.pallas import tpu as pltpu
from jax.experimental.pallas import tpu_sc as plsc

F32 = jnp.float32
BF16 = jnp.bfloat16

D_MODEL = 1024
HEAD_DIM = 64
HEADS_PER_GROUP = 4
DILATION_GROUPS = ((128, 1), (512, 4), (2048, 16))
N_BACK = 128
GROUP_WIDTH = HEADS_PER_GROUP * HEAD_DIM
ATTN_WIDTH = 3 * GROUP_WIDTH
ROPE_THETA = 10000.0
SSM_WIDTH = 256
SSM_GROUP = 16
SSM_GROUPS = 16
SSM_STATE = 64
SSM_LANES = SSM_GROUPS * SSM_STATE
IN_WIDTH = 3 * ATTN_WIDTH + SSM_WIDTH
N_EXPERTS = 64
TOP_K = 8
EXPERT_FF = 256
ROUTED_SCALE = 2.5
PLE_DIM = 256
DEPTH = 1
PAST_LEN = 8192
DN_ALPHA = (2.0 * DEPTH) ** 0.25
LN_EPS = 1e-5

LANES = 128
ROW_TILE = D_MODEL // LANES
SC_CORES = 2
SC_SUBCORES = 16
SC_WINDOW = 32
MOE_BLOCK = 512
ATTN_CHUNK = 2048
VMEM_LIMIT = 56 * 1024 * 1024


def _params(semantics):
    return pltpu.CompilerParams(dimension_semantics=semantics, vmem_limit_bytes=VMEM_LIMIT)


def _full(shape):
    return pl.BlockSpec(shape, lambda *_: (0,) * len(shape))


def _in_proj_kernel(x_ref, w_ref, cos_ref, sina_ref, sinb_ref, q_ref, k_ref, v_ref, u_ref):
    xb = x_ref[...].astype(BF16)
    cos = cos_ref[...]
    sin_a = sina_ref[...]
    sin_b = sinb_ref[...]

    def rope_store(col0, out_ref, scale):
        t = jnp.dot(xb, w_ref[:, col0:col0 + ATTN_WIDTH], preferred_element_type=F32)
        for c in range(ATTN_WIDTH // LANES):
            xc = t[:, c * LANES:(c + 1) * LANES]
            r = xc * cos + pltpu.roll(xc, LANES - 32, 1) * sin_a + pltpu.roll(xc, 32, 1) * sin_b
            out_ref[:, c * LANES:(c + 1) * LANES] = r * scale if scale != 1.0 else r

    rope_store(0, q_ref, HEAD_DIM ** -0.5)
    rope_store(ATTN_WIDTH, k_ref, 1.0)
    v_ref[...] = jnp.dot(xb, w_ref[:, 2 * ATTN_WIDTH:3 * ATTN_WIDTH], preferred_element_type=F32)
    u_ref[...] = jnp.dot(xb, w_ref[:, 3 * ATTN_WIDTH:], preferred_element_type=F32)


def _in_proj(x, w_in_bf, rope_tabs, rows_per_seq, tm):
    n = x.shape[0]
    tiles_per_seq = rows_per_seq // tm
    n_seq = n // rows_per_seq
    tab_tiles = rope_tabs[0].shape[0] // tm
    tab_spec = pl.BlockSpec((tm, LANES), lambda i: (i % tab_tiles, 0))
    row_spec = pl.BlockSpec((tm, ATTN_WIDTH), lambda i: (i, 0))
    return pl.pallas_call(
        _in_proj_kernel,
        grid=(n // tm,),
        in_specs=[pl.BlockSpec((tm, D_MODEL), lambda i: (i, 0)), _full((D_MODEL, IN_WIDTH)),
                  tab_spec, tab_spec, tab_spec],
        out_specs=[row_spec, row_spec, row_spec,
                   pl.BlockSpec((tm, SSM_WIDTH), lambda i: (i % tiles_per_seq, i // tiles_per_seq))],
        out_shape=[jax.ShapeDtypeStruct((n, ATTN_WIDTH), F32)] * 3
        + [jax.ShapeDtypeStruct((rows_per_seq, n_seq * SSM_WIDTH), F32)],
        compiler_params=_params(("parallel",)),
        name="in_proj",
    )(x, w_in_bf, *rope_tabs)


def _rope_tables(pos):
    half = HEAD_DIM // 2
    inv = ROPE_THETA ** (-jnp.arange(half, dtype=F32) / half)
    ang = pos.astype(F32)[:, None] * inv[None, :]
    cos = jnp.tile(jnp.cos(ang), (1, LANES // half))
    sin = jnp.tile(jnp.sin(ang), (1, LANES // half))
    first_half = (jnp.arange(LANES) % HEAD_DIM) < half
    sin_a = jnp.where(first_half[None, :], -sin, 0.0)
    sin_b = jnp.where(first_half[None, :], 0.0, sin)
    return cos, sin_a, sin_b


def _band_attention(q, k, v, mask):
    o_parts, lse_parts = [], []
    for h in range(LANES // HEAD_DIM):
        cols = slice(h * HEAD_DIM, (h + 1) * HEAD_DIM)
        logits = lax.dot_general(q[:, cols].astype(BF16), k[:, cols].astype(BF16),
                                 (((1,), (1,)), ((), ())), preferred_element_type=F32) + mask
        m = jnp.max(logits, axis=1, keepdims=True)
        p = jnp.exp(logits - m)
        l = jnp.sum(p, axis=1, keepdims=True)
        pv = jnp.dot(p.astype(BF16), v[:, cols].astype(BF16), preferred_element_type=F32)
        o_parts.append(pv * (1.0 / l))
        lse_parts.append(jnp.broadcast_to(m + jnp.log(l), (N_BACK, HEAD_DIM)))
    return jnp.concatenate(o_parts, axis=1), jnp.concatenate(lse_parts, axis=1)


def _attn_prompt_kernel(q_ref, kp_ref, kc_ref, vp_ref, vc_ref, o_ref, lse_ref):
    c = pl.program_id(1)
    g = pl.program_id(3)
    ch = ATTN_CHUNK
    qi = lax.broadcasted_iota(jnp.int32, (N_BACK, 2 * N_BACK), 0)
    kj = lax.broadcasted_iota(jnp.int32, (N_BACK, 2 * N_BACK), 1)
    dist = qi + N_BACK - kj
    band = jnp.where(dist >= 0, jnp.where(dist <= N_BACK, 0.0, -jnp.inf), -jnp.inf)
    band_first = jnp.where(kj >= N_BACK, band, -jnp.inf)

    def group_body(d):
        span = N_BACK * d
        n_sub = ch // N_BACK

        def rows(start, size):
            return pl.ds(start, size) if d == 1 else pl.ds(start, size, stride=d)

        def store(q0, o, lse):
            o_ref[rows(q0, N_BACK), :] = o
            lse_ref[rows(q0, N_BACK), :] = lse

        def head_block(r, carry):
            k = jnp.concatenate([kp_ref[rows(ch - span + r, N_BACK), :], kc_ref[rows(r, N_BACK), :]], axis=0)
            v = jnp.concatenate([vp_ref[rows(ch - span + r, N_BACK), :], vc_ref[rows(r, N_BACK), :]], axis=0)
            mask = jnp.where(c == 0, band_first, band)
            store(r, *_band_attention(q_ref[rows(r, N_BACK), :], k, v, mask))
            return carry

        def inner_block(idx, carry):
            s = idx // d
            r = idx % d
            k0 = (s - 1) * span + r
            store(s * span + r, *_band_attention(q_ref[rows(s * span + r, N_BACK), :],
                                                 kc_ref[rows(k0, 2 * N_BACK), :],
                                                 vc_ref[rows(k0, 2 * N_BACK), :], band))
            return carry

        lax.fori_loop(0, d, head_block, 0, unroll=min(d, 2))
        if n_sub > d:
            lax.fori_loop(d, n_sub, inner_block, 0, unroll=2 if (n_sub - d) % 2 == 0 else 3)

    for gi, (_, d) in enumerate(DILATION_GROUPS):
        pl.when(g == gi)(functools.partial(group_body, d))


def _attn_prompt(q, k, v, batch, seq):
    ch = ATTN_CHUNK
    cps = seq // ch
    n = batch * seq
    pairs = GROUP_WIDTH // LANES
    cur = lambda b, c, hp, g: (b * cps + c, g * pairs + hp)
    prev = lambda b, c, hp, g: (b * cps + jnp.maximum(c - 1, 0), g * pairs + hp)
    blk = (ch, LANES)
    return pl.pallas_call(
        _attn_prompt_kernel,
        grid=(batch, cps, pairs, len(DILATION_GROUPS)),
        in_specs=[pl.BlockSpec(blk, cur), pl.BlockSpec(blk, prev), pl.BlockSpec(blk, cur),
                  pl.BlockSpec(blk, prev), pl.BlockSpec(blk, cur)],
        out_specs=[pl.BlockSpec(blk, cur), pl.BlockSpec(blk, cur)],
        out_shape=[jax.ShapeDtypeStruct((n, ATTN_WIDTH), F32)] * 2,
        compiler_params=_params(("parallel", "parallel", "parallel", "parallel")),
        name="attn_prompt",
    )(q, k, k, v, v)


def _attn_sample_kernel(q_ref, k_ref, v_ref, c0_ref, c1_ref, c2_ref, o_ref, lse_ref):
    bt = q_ref.shape[0]
    for b in range(bt):
        for g, (c_ref, (win, d)) in enumerate(zip((c0_ref, c1_ref, c2_ref), DILATION_GROUPS)):
            pos = lax.broadcasted_iota(jnp.int32, (1, win), 1)
            off_stride = (pos % d) != 0
            for h in range(HEADS_PER_GROUP):
                j = g * HEADS_PER_GROUP + h
                q = q_ref[b, :, j:j + 1]
                k_new = k_ref[b, :, j:j + 1]
                v_new = v_ref[b, :, j:j + 1]
                s_c = jnp.sum(c_ref[b, 0, h] * q, axis=0, keepdims=True)
                s_c = jnp.where(off_stride, -jnp.inf, s_c)
                s_new = jnp.sum(k_new * q, axis=0, keepdims=True)
                m = jnp.maximum(jnp.max(s_c, axis=1, keepdims=True), s_new)
                p_c = jnp.exp(s_c - m)
                p_new = jnp.exp(s_new - m)
                l = jnp.sum(p_c, axis=1, keepdims=True) + p_new
                num = jnp.sum(c_ref[b, 1, h] * p_c, axis=1, keepdims=True) + p_new * v_new
                o_ref[b, :, j:j + 1] = num * (1.0 / l)
                lse_ref[b, :, j:j + 1] = m + jnp.log(l)


def _attn_sample(q, k, v, caches, bt):
    b = q.shape[0]
    n_heads = ATTN_WIDTH // HEAD_DIM
    views, specs = [], []
    for cache, (win, d) in zip(caches, DILATION_GROUPS):
        assert cache.shape[1] == win == N_BACK * d
        views.append(jnp.transpose(cache, (0, 2, 3, 4, 1)))
        specs.append(pl.BlockSpec((bt, 2, HEADS_PER_GROUP, HEAD_DIM, win), lambda i: (i, 0, 0, 0, 0)))
    col_spec = pl.BlockSpec((bt, HEAD_DIM, n_heads), lambda i: (i, 0, 0))
    lse_spec = pl.BlockSpec((bt, 1, n_heads), lambda i: (i, 0, 0))
    cols = lambda t: jnp.transpose(t.reshape(b, n_heads, HEAD_DIM), (0, 2, 1))
    o, lse = pl.pallas_call(
        _attn_sample_kernel,
        grid=(b // bt,),
        in_specs=[col_spec, col_spec, col_spec] + specs,
        out_specs=[col_spec, lse_spec],
        out_shape=[jax.ShapeDtypeStruct((b, HEAD_DIM, n_heads), F32),
                   jax.ShapeDtypeStruct((b, 1, n_heads), F32)],
        compiler_params=_params(("parallel",)),
        name="attn_sample",
    )(cols(q), cols(k), cols(v), *views)
    o = jnp.transpose(o, (0, 2, 1)).reshape(b, ATTN_WIDTH)
    lse = jnp.broadcast_to(jnp.transpose(lse, (0, 2, 1)), (b, n_heads, HEAD_DIM)).reshape(b, ATTN_WIDTH)
    return o, lse


def _s5_scan_kernel(u_ref, bmat_ref, cmat_ref, are_ref, aim_ref, d_ref, h0re_ref, h0im_ref,
                    y_ref, hre_ref, him_ref, hist_sc, *, bg, steps):
    t_chunk = pl.program_id(0)

    @pl.when(t_chunk == 0)
    def _():
        hre_ref[...] = h0re_ref[...]
        him_ref[...] = h0im_ref[...]

    u = u_ref[...]
    hist_sc[...] = jnp.dot(u.astype(BF16), bmat_ref[...], preferred_element_type=F32)
    a_re = jnp.broadcast_to(are_ref[...], (bg, SSM_LANES))
    a_im = jnp.broadcast_to(aim_ref[...], (bg, SSM_LANES))

    def step(t, carry):
        h_re, h_im = carry
        rows = pl.ds(pl.multiple_of(t * bg, bg), bg)
        n_re = a_re * h_re - a_im * h_im + hist_sc[rows, 0:SSM_LANES]
        n_im = a_re * h_im + a_im * h_re + hist_sc[rows, SSM_LANES:2 * SSM_LANES]
        hist_sc[rows, 0:SSM_LANES] = n_re
        hist_sc[rows, SSM_LANES:2 * SSM_LANES] = n_im
        return n_re, n_im

    h_re, h_im = lax.fori_loop(0, steps, step, (hre_ref[...], him_ref[...]))
    hre_ref[...] = h_re
    him_ref[...] = h_im
    y_ref[...] = (jnp.dot(hist_sc[...].astype(BF16), cmat_ref[...], preferred_element_type=F32)
                  + d_ref[...] * u)


def _s5_scan(u_tb, ssm, h0_re, h0_im, bg, steps):
    rows = u_tb.shape[0]
    blk = steps * bg
    kern = functools.partial(_s5_scan_kernel, bg=bg, steps=steps)
    state_spec = _full((bg, SSM_LANES))
    return pl.pallas_call(
        kern,
        grid=(rows // blk,),
        in_specs=[pl.BlockSpec((blk, SSM_WIDTH), lambda i: (i, 0)),
                  _full((SSM_WIDTH, 2 * SSM_LANES)), _full((2 * SSM_LANES, SSM_WIDTH)),
                  _full((1, SSM_LANES)), _full((1, SSM_LANES)), _full((1, SSM_WIDTH)),
                  state_spec, state_spec],
        out_specs=[pl.BlockSpec((blk, SSM_WIDTH), lambda i: (i, 0)), state_spec, state_spec],
        out_shape=[jax.ShapeDtypeStruct((rows, SSM_WIDTH), F32),
                   jax.ShapeDtypeStruct((bg, SSM_LANES), F32), jax.ShapeDtypeStruct((bg, SSM_LANES), F32)],
        scratch_shapes=[pltpu.VMEM((blk, 2 * SSM_LANES), F32)],
        compiler_params=_params(("arbitrary",)),
        name="s5_scan",
    )(u_tb, ssm["bmat"], ssm["cmat"], ssm["a_re"], ssm["a_im"], ssm["d_skip"], h0_re, h0_im)


def _s5_params(a_re, a_im, log_dt, b_re, b_im, c_re, c_im, d_skip):
    dt = jnp.exp(log_dt)[:, None]
    mag = jnp.exp(a_re * dt)
    abar_re = mag * jnp.cos(a_im * dt)
    abar_im = mag * jnp.sin(a_im * dt)
    a2 = a_re * a_re + a_im * a_im
    nr = abar_re - 1.0
    coef_re = (nr * a_re + abar_im * a_im) / a2
    coef_im = (abar_im * a_re - nr * a_im) / a2
    bb_re = coef_re[..., None] * b_re - coef_im[..., None] * b_im
    bb_im = coef_re[..., None] * b_im + coef_im[..., None] * b_re
    eye = jnp.eye(SSM_GROUPS, dtype=F32)
    to_b = lambda t: jnp.einsum("gpc,gh->gchp", t, eye).reshape(SSM_WIDTH, SSM_LANES)
    to_c = lambda t: jnp.einsum("gcp,gh->gphc", t, eye).reshape(SSM_LANES, SSM_WIDTH)
    return {
        "bmat": jnp.concatenate([to_b(bb_re), to_b(bb_im)], axis=1).astype(BF16),
        "cmat": jnp.concatenate([to_c(c_re), -to_c(c_im)], axis=0).astype(BF16),
        "a_re": abar_re.reshape(1, SSM_LANES), "a_im": abar_im.reshape(1, SSM_LANES),
        "d_skip": d_skip.reshape(1, SSM_WIDTH),
    }


def _layer_norm(z, g, b):
    mu = jnp.mean(z, axis=-1, keepdims=True)
    zc = z - mu
    var = jnp.mean(zc * zc, axis=-1, keepdims=True)
    return zc * lax.rsqrt(var + LN_EPS) * g + b


def _merge_groups(o, lse):
    parts = [slice(g * GROUP_WIDTH, (g + 1) * GROUP_WIDTH) for g in range(len(DILATION_GROUPS))]
    top = lse[:, parts[0]]
    for cols in parts[1:]:
        top = jnp.maximum(top, lse[:, cols])
    num = den = None
    for cols in parts:
        w = jnp.exp(lse[:, cols] - top)
        num = w * o[:, cols] if num is None else num + w * o[:, cols]
        den = w if den is None else den + w
    return num / den


def _post_mixer_kernel(x_ref, ao_ref, lse_ref, y_ref, wglu_ref, bglu_ref, wgate_ref, bgate_ref, wab_ref, wsb_ref,
                       wout_ref, lng_ref, lnb_ref, wr_ref, rb_ref,
                       x1_ref, x1t_ref, gate_ref, idx_ref, topg_ref, cnt_ref):
    x = x_ref[...]
    xb = x.astype(BF16)
    s = jax.nn.gelu(y_ref[...])
    s = s * jax.nn.sigmoid(jnp.dot(s.astype(BF16), wglu_ref[...], preferred_element_type=F32) + bglu_ref[...])
    gates = jax.nn.sigmoid(jnp.dot(xb, wgate_ref[...], preferred_element_type=F32) + bgate_ref[...])
    attn_o = _merge_groups(ao_ref[...], lse_ref[...])
    attn_br = jnp.dot(attn_o.astype(BF16), wab_ref[...], preferred_element_type=F32)
    ssm_br = jnp.dot(s.astype(BF16), wsb_ref[...], preferred_element_type=F32)
    merged = gates[:, :D_MODEL] * attn_br + gates[:, D_MODEL:] * ssm_br
    mix = jnp.dot(merged.astype(BF16), wout_ref[...], preferred_element_type=F32)
    x1 = _layer_norm(DN_ALPHA * x + mix, lng_ref[...], lnb_ref[...])
    x1_ref[...] = x1
    tm = x1.shape[0]
    for j in range(ROW_TILE):
        x1t_ref[pl.ds(j, tm, stride=ROW_TILE), :] = x1[:, j * LANES:(j + 1) * LANES]

    scores = jax.nn.sigmoid(jnp.dot(x1, wr_ref[...], preferred_element_type=F32,
                                    precision=lax.Precision.HIGHEST))
    sel = scores + rb_ref[...]
    lane = lax.broadcasted_iota(jnp.int32, sel.shape, 1).astype(F32)
    slot_lane = lax.broadcasted_iota(jnp.int32, (tm, LANES), 1)
    chosen = jnp.zeros(sel.shape, F32)
    top_idx = jnp.zeros((tm, LANES), F32)
    top_s = jnp.zeros((tm, LANES), F32)
    for k in range(TOP_K):
        top = jnp.max(sel, axis=-1, keepdims=True)
        first = jnp.min(jnp.where(sel == top, lane, float(N_EXPERTS)), axis=-1, keepdims=True)
        hit = lane == first
        chosen = jnp.where(hit, 1.0, chosen)
        sel = jnp.where(hit, -jnp.inf, sel)
        top_idx = jnp.where(slot_lane == k, first, top_idx)
        top_s = jnp.where(slot_lane == k, jnp.sum(jnp.where(hit, scores, 0.0), axis=-1, keepdims=True), top_s)
    norm = ROUTED_SCALE / jnp.sum(scores * chosen, axis=-1, keepdims=True)
    gate_ref[...] = scores * chosen * norm
    idx_ref[...] = top_idx
    topg_ref[...] = top_s * norm

    @pl.when(pl.program_id(0) == 0)
    def _():
        cnt_ref[...] = jnp.zeros(cnt_ref.shape, F32)
    cnt_ref[...] += jnp.sum(chosen, axis=0, keepdims=True)


def _post_mixer(x, attn_o, attn_lse, y_tb, w, rows_per_seq, tm):
    n = x.shape[0]
    tiles_per_seq = rows_per_seq // tm
    row = lambda width: pl.BlockSpec((tm, width), lambda i: (i, 0))
    return pl.pallas_call(
        _post_mixer_kernel,
        grid=(n // tm,),
        in_specs=[row(D_MODEL), row(ATTN_WIDTH), row(ATTN_WIDTH),
                  pl.BlockSpec((tm, SSM_WIDTH), lambda i: (i % tiles_per_seq, i // tiles_per_seq)),
                  _full((SSM_WIDTH, SSM_WIDTH)), _full((1, SSM_WIDTH)),
                  _full((D_MODEL, 2 * D_MODEL)), _full((1, 2 * D_MODEL)),
                  _full((GROUP_WIDTH, D_MODEL)), _full((SSM_WIDTH, D_MODEL)), _full((D_MODEL, D_MODEL)),
                  _full((1, D_MODEL)), _full((1, D_MODEL)),
                  _full((D_MODEL, N_EXPERTS)), _full((1, N_EXPERTS))],
        out_specs=[row(D_MODEL), pl.BlockSpec((tm * ROW_TILE, LANES), lambda i: (i, 0)),
                   row(N_EXPERTS), row(LANES), row(LANES), _full((1, N_EXPERTS))],
        out_shape=[jax.ShapeDtypeStruct((n, D_MODEL), F32), jax.ShapeDtypeStruct((n * ROW_TILE, LANES), F32),
                   jax.ShapeDtypeStruct((n, N_EXPERTS), F32), jax.ShapeDtypeStruct((n, LANES), F32),
                   jax.ShapeDtypeStruct((n, LANES), F32), jax.ShapeDtypeStruct((1, N_EXPERTS), F32)],
        compiler_params=_params(("arbitrary",)),
        name="post_mixer",
    )(x, attn_o, attn_lse, y_tb, w["w_glu"], w["b_glu"], w["w_gate"], w["b_gate"], w["w_attn_br"], w["w_ssm_br"],
      w["w_out"], w["ln1_g"], w["ln1_b"], w["w_router"], w["router_bias"])


def _moe_ffn_kernel(x_ref, gate_ref, p_ref, w13_ref, w2_ref, ws13_ref, ws2_ref, wpg_ref, wple_ref,
                    lng_ref, lnb_ref, o_ref, acc_sc, xb_sc):
    e = pl.program_id(1)

    def glu_ffn(xb, w13, w2, row_scale):
        h13 = jnp.dot(xb, w13, preferred_element_type=F32)
        h = jax.nn.silu(h13[:, :EXPERT_FF]) * h13[:, EXPERT_FF:]
        if row_scale is not None:
            h = h * row_scale
        return jnp.dot(h.astype(BF16), w2, preferred_element_type=F32)

    @pl.when(e == 0)
    def _():
        xb = x_ref[...].astype(BF16)
        xb_sc[...] = xb
        ple = (jax.nn.sigmoid(jnp.dot(xb, wpg_ref[...], preferred_element_type=F32))
               * jnp.dot(p_ref[...].astype(BF16), wple_ref[...], preferred_element_type=F32))
        acc_sc[...] = glu_ffn(xb, ws13_ref[...], ws2_ref[...], None) + ple

    gates = gate_ref[...]
    lane = lax.broadcasted_iota(jnp.int32, gates.shape, 1)
    g_col = jnp.sum(jnp.where(lane == e, gates, 0.0), axis=-1, keepdims=True)
    acc_sc[...] += glu_ffn(xb_sc[...], w13_ref[0], w2_ref[0], g_col)

    @pl.when(e == N_EXPERTS - 1)
    def _():
        o_ref[...] = _layer_norm(DN_ALPHA * x_ref[...] + acc_sc[...], lng_ref[...], lnb_ref[...])


def _moe_ffn(x1, gates, p, w, tm):
    n = x1.shape[0]
    row = lambda width: pl.BlockSpec((tm, width), lambda i, e: (i, 0))
    return pl.pallas_call(
        _moe_ffn_kernel,
        grid=(n // tm, N_EXPERTS),
        in_specs=[row(D_MODEL), row(N_EXPERTS), row(PLE_DIM),
                  pl.BlockSpec((1, D_MODEL, 2 * EXPERT_FF), lambda i, e: (e, 0, 0)),
                  pl.BlockSpec((1, EXPERT_FF, D_MODEL), lambda i, e: (e, 0, 0)),
                  _full((D_MODEL, 2 * EXPERT_FF)), _full((EXPERT_FF, D_MODEL)),
                  _full((D_MODEL, D_MODEL)), _full((PLE_DIM, D_MODEL)),
                  _full((1, D_MODEL)), _full((1, D_MODEL))],
        out_specs=row(D_MODEL),
        out_shape=jax.ShapeDtypeStruct((n, D_MODEL), F32),
        scratch_shapes=[pltpu.VMEM((tm, D_MODEL), F32), pltpu.VMEM((tm, D_MODEL), BF16)],
        compiler_params=_params(("parallel", "arbitrary")),
        name="moe_ffn",
    )(x1, gates, p, w["w13"], w["w2"], w["ws13"], w["ws2"], w["w_ple_gate"], w["w_ple"],
      w["ln2_g"], w["ln2_b"])


def _route_kernel(idx_ref, pstart_ref, slot_ref, base_sc):
    @pl.when(pl.program_id(0) == 0)
    def _():
        base_sc[...] = jnp.zeros(base_sc.shape, F32)

    idx = idx_ref[...]
    tm = idx.shape[0]
    lane = lax.broadcasted_iota(jnp.int32, (tm, N_EXPERTS), 1).astype(F32)
    hits = [lane == idx[:, k:k + 1] for k in range(TOP_K)]
    member = jnp.zeros((tm, N_EXPERTS), F32)
    for hit in hits:
        member = member + jnp.where(hit, 1.0, 0.0)
    r = lax.broadcasted_iota(jnp.int32, (tm, tm), 0)
    c = lax.broadcasted_iota(jnp.int32, (tm, tm), 1)
    earlier = jnp.where(c < r, 1.0, 0.0).astype(BF16)
    row = (jnp.dot(earlier, member.astype(BF16), preferred_element_type=F32)
           + base_sc[...] + pstart_ref[...])
    slot_lane = lax.broadcasted_iota(jnp.int32, (tm, LANES), 1)
    out = jnp.zeros((tm, LANES), F32)
    for k, hit in enumerate(hits):
        out = jnp.where(slot_lane == k, jnp.sum(jnp.where(hit, row, 0.0), axis=-1, keepdims=True), out)
    slot_ref[...] = out.astype(jnp.int32)
    base_sc[...] += jnp.sum(member, axis=0, keepdims=True)


def _route(top_idx, pstart, tm):
    n = top_idx.shape[0]
    return pl.pallas_call(
        _route_kernel,
        grid=(n // tm,),
        in_specs=[pl.BlockSpec((tm, LANES), lambda i: (i, 0)), _full((1, N_EXPERTS))],
        out_specs=pl.BlockSpec((tm, LANES), lambda i: (i, 0)),
        out_shape=jax.ShapeDtypeStruct((n, LANES), jnp.int32),
        scratch_shapes=[pltpu.VMEM((1, N_EXPERTS), F32)],
        compiler_params=_params(("arbitrary",)),
        name="route",
    )(top_idx, pstart)


def _sc_mesh():
    return plsc.VectorSubcoreMesh(core_axis_name="c", subcore_axis_name="s",
                                  num_cores=SC_CORES, num_subcores=SC_SUBCORES)


def _sc_dispatch(x_tiles, slots, n_rows):
    n = x_tiles.shape[0]
    wins_per_worker = n // SC_WINDOW // (SC_CORES * SC_SUBCORES)

    def body(x_hbm, slot_hbm, xs_hbm, idx_v, rows_v):
        wid = lax.axis_index("s") * SC_CORES + lax.axis_index("c")

        @pl.loop(0, wins_per_worker)
        def _(i):
            win = wid * wins_per_worker + i
            pltpu.sync_copy(slot_hbm.at[win], idx_v)
            pltpu.sync_copy(x_hbm.at[pl.ds(win * SC_WINDOW, SC_WINDOW)], rows_v)
            for k in range(TOP_K):
                pltpu.sync_copy(rows_v, xs_hbm.at[idx_v.at[k]])

    return pl.kernel(
        body, out_type=jax.ShapeDtypeStruct((n_rows, ROW_TILE, LANES), F32), mesh=_sc_mesh(),
        scratch_types=[pltpu.VMEM((TOP_K, SC_WINDOW), jnp.int32),
                       pltpu.VMEM((SC_WINDOW, ROW_TILE, LANES), F32)],
        name="sc_dispatch",
    )(x_tiles, slots)


def _sc_combine(y_tiles, slots, n):
    wins_per_worker = n // SC_WINDOW // (SC_CORES * SC_SUBCORES)

    def body(ys_hbm, slot_hbm, yg_hbm, idx_v, rows_v):
        wid = lax.axis_index("s") * SC_CORES + lax.axis_index("c")

        @pl.loop(0, wins_per_worker)
        def _(i):
            win = wid * wins_per_worker + i
            pltpu.sync_copy(slot_hbm.at[win], idx_v)
            for k in range(TOP_K):
                pltpu.sync_copy(ys_hbm.at[idx_v.at[k]], rows_v)
                pltpu.sync_copy(rows_v, yg_hbm.at[k, pl.ds(win * SC_WINDOW, SC_WINDOW)])

    return pl.kernel(
        body, out_type=jax.ShapeDtypeStruct((TOP_K, n, ROW_TILE, LANES), F32), mesh=_sc_mesh(),
        scratch_types=[pltpu.VMEM((TOP_K, SC_WINDOW), jnp.int32),
                       pltpu.VMEM((SC_WINDOW, ROW_TILE, LANES), F32)],
        name="sc_combine",
    )(y_tiles, slots)


def _expert_ffn_kernel(bexp_ref, nused_ref, xs_ref, w13_ref, w2_ref, ys_ref):
    del bexp_ref

    @pl.when(pl.program_id(0) < nused_ref[0])
    def _():
        tb = MOE_BLOCK
        x = jnp.concatenate([xs_ref[pl.ds(j, tb, stride=ROW_TILE), :] for j in range(ROW_TILE)], axis=1)
        h13 = jnp.dot(x.astype(BF16), w13_ref[0], preferred_element_type=F32)
        h = jax.nn.silu(h13[:, :EXPERT_FF]) * h13[:, EXPERT_FF:]
        y = jnp.dot(h.astype(BF16), w2_ref[0], preferred_element_type=F32)
        for j in range(ROW_TILE):
            ys_ref[pl.ds(j, tb, stride=ROW_TILE), :] = y[:, j * LANES:(j + 1) * LANES]


def _expert_ffn(xs_rows, block_expert, n_used, w):
    n_blocks = block_expert.shape[0]
    blk = (MOE_BLOCK * ROW_TILE, LANES)
    return pl.pallas_call(
        _expert_ffn_kernel,
        grid_spec=pltpu.PrefetchScalarGridSpec(
            num_scalar_prefetch=2, grid=(n_blocks,),
            in_specs=[pl.BlockSpec(blk, lambda i, be, nu: (i, 0)),
                      pl.BlockSpec((1, D_MODEL, 2 * EXPERT_FF), lambda i, be, nu: (be[i], 0, 0)),
                      pl.BlockSpec((1, EXPERT_FF, D_MODEL), lambda i, be, nu: (be[i], 0, 0))],
            out_specs=pl.BlockSpec(blk, lambda i, be, nu: (i, 0))),
        out_shape=jax.ShapeDtypeStruct(xs_rows.shape, F32),
        compiler_params=_params(("parallel",)),
        name="expert_ffn",
    )(block_expert, n_used, xs_rows, w["w13"], w["w2"])


def _moe_out_kernel(x_ref, g_ref, p_ref, yg_ref, ws13_ref, ws2_ref, wpg_ref, wple_ref, lng_ref, lnb_ref, o_ref):
    x = x_ref[...]
    xb = x.astype(BF16)
    tm = x.shape[0]
    g = g_ref[...]
    parts = []
    for j in range(ROW_TILE):
        acc = None
        for k in range(TOP_K):
            t = g[:, k:k + 1] * yg_ref[k, pl.ds(j, tm, stride=ROW_TILE), :]
            acc = t if acc is None else acc + t
        parts.append(acc)
    routed = jnp.concatenate(parts, axis=1)
    h13 = jnp.dot(xb, ws13_ref[...], preferred_element_type=F32)
    h = jax.nn.silu(h13[:, :EXPERT_FF]) * h13[:, EXPERT_FF:]
    shared = jnp.dot(h.astype(BF16), ws2_ref[...], preferred_element_type=F32)
    ple = (jax.nn.sigmoid(jnp.dot(xb, wpg_ref[...], preferred_element_type=F32))
           * jnp.dot(p_ref[...].astype(BF16), wple_ref[...], preferred_element_type=F32))
    o_ref[...] = _layer_norm(DN_ALPHA * x + routed + shared + ple, lng_ref[...], lnb_ref[...])


def _moe_out(x1, top_gates, p, yg_rows, w, tm):
    n = x1.shape[0]
    row = lambda width: pl.BlockSpec((tm, width), lambda i: (i, 0))
    return pl.pallas_call(
        _moe_out_kernel,
        grid=(n // tm,),
        in_specs=[row(D_MODEL), row(LANES), row(PLE_DIM),
                  pl.BlockSpec((TOP_K, tm * ROW_TILE, LANES), lambda i: (0, i, 0)),
                  _full((D_MODEL, 2 * EXPERT_FF)), _full((EXPERT_FF, D_MODEL)),
                  _full((D_MODEL, D_MODEL)), _full((PLE_DIM, D_MODEL)),
                  _full((1, D_MODEL)), _full((1, D_MODEL))],
        out_specs=row(D_MODEL),
        out_shape=jax.ShapeDtypeStruct((n, D_MODEL), F32),
        compiler_params=_params(("parallel",)),
        name="moe_out",
    )(x1, top_gates, p, yg_rows, w["ws13"], w["ws2"], w["w_ple_gate"], w["w_ple"], w["ln2_g"], w["ln2_b"])


def _moe_sorted(x1, x1_tiles, top_idx, top_gates, counts, p, w):
    n = x1.shape[0]
    n_blocks = n * TOP_K // MOE_BLOCK + N_EXPERTS
    n_rows = n_blocks * MOE_BLOCK
    cnt = counts.reshape(N_EXPERTS).astype(jnp.int32)
    padded = (cnt + MOE_BLOCK - 1) // MOE_BLOCK * MOE_BLOCK
    pend = jnp.cumsum(padded)
    pstart = (pend - padded).astype(F32).reshape(1, N_EXPERTS)
    block_start = jnp.arange(n_blocks, dtype=jnp.int32) * MOE_BLOCK
    block_expert = jnp.minimum(jnp.sum((pend[None, :] <= block_start[:, None]).astype(jnp.int32), axis=1),
                               N_EXPERTS - 1)
    n_used = (pend[-1:] // MOE_BLOCK).astype(jnp.int32)
    slots = _route(top_idx, pstart, 512)[:, :TOP_K]
    slots = jnp.transpose(slots.reshape(n // SC_WINDOW, SC_WINDOW, TOP_K), (0, 2, 1))
    xs = _sc_dispatch(x1_tiles.reshape(n, ROW_TILE, LANES), slots, n_rows)
    ys = _expert_ffn(xs.reshape(n_rows * ROW_TILE, LANES), block_expert, n_used, w)
    yg = _sc_combine(ys.reshape(n_rows, ROW_TILE, LANES), slots, n)
    return _moe_out(x1, top_gates, p, yg.reshape(TOP_K, n * ROW_TILE, LANES), w, 256)


def _kv_rows(k, v, batch, seq, keep, g):
    cols = slice(g * GROUP_WIDTH, (g + 1) * GROUP_WIDTH)
    shape = (batch, keep, HEADS_PER_GROUP, HEAD_DIM)
    k_g = k.reshape(batch, seq, ATTN_WIDTH)[:, seq - keep:, cols].reshape(shape)
    v_g = v.reshape(batch, seq, ATTN_WIDTH)[:, seq - keep:, cols].reshape(shape)
    return jnp.stack([k_g, v_g], axis=2)


def _layer_prompt(x, p, w, ssm):
    batch, seq, _ = x.shape
    n = batch * seq
    x2 = x.reshape(n, D_MODEL)
    tabs = _rope_tables(jnp.arange(seq, dtype=jnp.int32))
    q, k, v, u = _in_proj(x2, w["w_in"], tabs, seq, 512)
    attn_o, attn_lse = _attn_prompt(q, k, v, batch, seq)
    zeros = jnp.zeros((batch, SSM_LANES), F32)
    y_tb, h_re, h_im = _s5_scan(u.reshape(seq * batch, SSM_WIDTH), ssm, zeros, zeros, batch, 128)
    x1, x1_tiles, _, top_idx, top_gates, counts = _post_mixer(
        x2, attn_o, attn_lse, y_tb.reshape(seq, batch * SSM_WIDTH), w, seq, 512)
    y = _moe_sorted(x1, x1_tiles, top_idx, top_gates, counts, p.reshape(n, PLE_DIM), w)
    kv = [_kv_rows(k, v, batch, seq, min(win, seq), g) for g, (win, _) in enumerate(DILATION_GROUPS)]
    h_last = jnp.stack([h_re, h_im], axis=-1).reshape(batch, SSM_GROUPS, SSM_STATE, 2)
    return y.reshape(batch, seq, D_MODEL), kv, h_last


def _layer_sample(x, p, caches, state, w, ssm):
    batch, seq, _ = x.shape
    assert seq == 1
    x2 = x.reshape(batch, D_MODEL)
    tabs = _rope_tables(jnp.full((batch,), PAST_LEN, dtype=jnp.int32))
    q, k, v, u = _in_proj(x2, w["w_in"], tabs, batch, batch)
    attn_o, attn_lse = _attn_sample(q, k, v, caches, 2)
    h0 = state.reshape(batch, SSM_LANES, 2)
    y_tb, h_re, h_im = _s5_scan(u, ssm, h0[..., 0], h0[..., 1], batch, 1)
    x1, _, gates, _, _, _ = _post_mixer(x2, attn_o, attn_lse, y_tb, w, batch, batch)
    y = _moe_ffn(x1, gates, p.reshape(batch, PLE_DIM), w, batch)
    kv = [_kv_rows(k, v, batch, 1, 1, g) for g in range(len(DILATION_GROUPS))]
    h_last = jnp.stack([h_re, h_im], axis=-1).reshape(batch, SSM_GROUPS, SSM_STATE, 2)
    return y.reshape(batch, 1, D_MODEL), kv, h_last


def kernel(x_prompt, x_sample, cache_kv_w128, cache_kv_w512, cache_kv_w2048, state_ssm, p_prompt, p_sample,
           w_in, a_re, a_im, log_dt, b_re, b_im, c_re, c_im, d_skip, w_glu, b_glu, w_attn_br, w_ssm_br,
           w_gate, b_gate, w_out, ln1_g, ln1_b, w_router, router_bias, w1, w3, w2, ws1, ws3, ws2,
           w_ple_gate, w_ple, ln2_g, ln2_b):
    assert w_in.shape[0] == DEPTH == 1
    l = 0
    row = lambda t: t[l].reshape(1, -1)
    w = {
        "w_in": w_in[l].astype(BF16),
        "w_glu": w_glu[l].astype(BF16), "b_glu": row(b_glu),
        "w_gate": w_gate[l].astype(BF16), "b_gate": row(b_gate),
        "w_attn_br": w_attn_br[l].astype(BF16), "w_ssm_br": w_ssm_br[l].astype(BF16),
        "w_out": w_out[l].astype(BF16), "ln1_g": row(ln1_g), "ln1_b": row(ln1_b),
        "w_router": w_router[l], "router_bias": row(router_bias),
        "w13": jnp.concatenate([w1[l], w3[l]], axis=-1).astype(BF16), "w2": w2[l].astype(BF16),
        "ws13": jnp.concatenate([ws1[l], ws3[l]], axis=-1).astype(BF16), "ws2": ws2[l].astype(BF16),
        "w_ple_gate": w_ple_gate[l].astype(BF16), "w_ple": w_ple[l].astype(BF16),
        "ln2_g": row(ln2_g), "ln2_b": row(ln2_b),
    }
    ssm = _s5_params(a_re[l], a_im[l], log_dt[l], b_re[l], b_im[l], c_re[l], c_im[l], d_skip[l])
    yp, kv_p, h_p = _layer_prompt(x_prompt, p_prompt[l], w, ssm)
    caches = (cache_kv_w128[l], cache_kv_w512[l], cache_kv_w2048[l])
    ys, kv_s, h_s = _layer_sample(x_sample, p_sample[l], caches, state_ssm[l], w, ssm)
    return (yp, ys, kv_p[0][None], kv_s[0][None], kv_p[1][None], kv_s[1][None],
            kv_p[2][None], kv_s[2][None], h_p[None], h_s[None])
```

```python
import functools
import math

import jax
import jax.numpy as jnp
from jax import lax
from jax.experimental import pallas as pl
from jax.experimental.pallas import tpu as pltpu
from jax.experimental.pallas import tpu_sc as plsc

F32 = jnp.float32
BF16 = jnp.bfloat16

D_MODEL = 1024
HEAD_DIM = 64
HEADS_PER_GROUP = 4
DILATION_GROUPS = ((128, 1), (512, 4), (2048, 16))
N_BACK = 128
GROUP_WIDTH = HEADS_PER_GROUP * HEAD_DIM
ATTN_WIDTH = 3 * GROUP_WIDTH
ROPE_THETA = 10000.0
SSM_WIDTH = 256
SSM_GROUP = 16
SSM_GROUPS = 16
SSM_STATE = 64
SSM_LANES = SSM_GROUPS * SSM_STATE
IN_WIDTH = 3 * ATTN_WIDTH + SSM_WIDTH
N_EXPERTS = 64
TOP_K = 8
EXPERT_FF = 256
ROUTED_SCALE = 2.5
PLE_DIM = 256
DEPTH = 1
PAST_LEN = 8192
DN_ALPHA = (2.0 * DEPTH) ** 0.25
LN_EPS = 1e-5

LANES = 128
ROW_TILE = D_MODEL // LANES // 2
SC_CORES = 2
SC_SUBCORES = 16
SC_WINDOW = 64
MOE_BLOCK = 512
ATTN_CHUNK = 2048
VMEM_LIMIT = 56 * 1024 * 1024


def _params(semantics):
    return pltpu.CompilerParams(dimension_semantics=semantics, vmem_limit_bytes=VMEM_LIMIT)


def _full(shape):
    return pl.BlockSpec(shape, lambda *_: (0,) * len(shape))


def _in_proj_kernel(x_ref, w_ref, cos_ref, sina_ref, sinb_ref, q_ref, k_ref, v_ref, u_ref):
    xb = x_ref[...].astype(BF16)
    cos = cos_ref[...]
    sin_a = sina_ref[...]
    sin_b = sinb_ref[...]

    def rope_store(col0, out_ref, scale):
        t = jnp.dot(xb, w_ref[:, col0:col0 + ATTN_WIDTH], preferred_element_type=F32)
        for c in range(ATTN_WIDTH // LANES):
            xc = t[:, c * LANES:(c + 1) * LANES]
            r = xc * cos + pltpu.roll(xc, LANES - 32, 1) * sin_a + pltpu.roll(xc, 32, 1) * sin_b
            out_ref[:, c * LANES:(c + 1) * LANES] = r * scale if scale != 1.0 else r

    rope_store(0, q_ref, HEAD_DIM ** -0.5)
    rope_store(ATTN_WIDTH, k_ref, 1.0)
    v_ref[...] = jnp.dot(xb, w_ref[:, 2 * ATTN_WIDTH:3 * ATTN_WIDTH], preferred_element_type=F32)
    u_ref[...] = jnp.dot(xb, w_ref[:, 3 * ATTN_WIDTH:], preferred_element_type=F32)


def _in_proj(x, w_in_bf, rope_tabs, rows_per_seq, tm):
    n = x.shape[0]
    tiles_per_seq = rows_per_seq // tm
    n_seq = n // rows_per_seq
    tab_tiles = rope_tabs[0].shape[0] // tm
    tab_spec = pl.BlockSpec((tm, LANES), lambda i: (i % tab_tiles, 0))
    row_spec = pl.BlockSpec((tm, ATTN_WIDTH), lambda i: (i, 0))
    return pl.pallas_call(
        _in_proj_kernel,
        grid=(n // tm,),
        in_specs=[pl.BlockSpec((tm, D_MODEL), lambda i: (i, 0)), _full((D_MODEL, IN_WIDTH)),
                  tab_spec, tab_spec, tab_spec],
        out_specs=[row_spec, row_spec, row_spec,
                   pl.BlockSpec((tm, SSM_WIDTH), lambda i: (i % tiles_per_seq, i // tiles_per_seq))],
        out_shape=[jax.ShapeDtypeStruct((n, ATTN_WIDTH), F32)] * 3
        + [jax.ShapeDtypeStruct((rows_per_seq, n_seq * SSM_WIDTH), F32)],
        compiler_params=_params(("parallel",)),
        name="in_proj",
    )(x, w_in_bf, *rope_tabs)


def _rope_tables(pos):
    half = HEAD_DIM // 2
    inv = ROPE_THETA ** (-jnp.arange(half, dtype=F32) / half)
    ang = pos.astype(F32)[:, None] * inv[None, :]
    cos = jnp.tile(jnp.cos(ang), (1, LANES // half))
    sin = jnp.tile(jnp.sin(ang), (1, LANES // half))
    first_half = (jnp.arange(LANES) % HEAD_DIM) < half
    sin_a = jnp.where(first_half[None, :], -sin, 0.0)
    sin_b = jnp.where(first_half[None, :], 0.0, sin)
    return cos, sin_a, sin_b


def _band_attention(q, k, v, mask):
    o_parts, lse_parts = [], []
    for h in range(LANES // HEAD_DIM):
        cols = slice(h * HEAD_DIM, (h + 1) * HEAD_DIM)
        logits = lax.dot_general(q[:, cols].astype(BF16), k[:, cols].astype(BF16),
                                 (((1,), (1,)), ((), ())), preferred_element_type=F32) + mask
        m = jnp.max(logits, axis=1, keepdims=True)
        p = jnp.exp(logits - m)
        l = jnp.sum(p, axis=1, keepdims=True)
        pv = jnp.dot(p.astype(BF16), v[:, cols].astype(BF16), preferred_element_type=F32)
        o_parts.append(pv * (1.0 / l))
        lse_parts.append(jnp.broadcast_to(m + jnp.log(l), (N_BACK, HEAD_DIM)))
    return jnp.concatenate(o_parts, axis=1), jnp.concatenate(lse_parts, axis=1)


def _attn_prompt_kernel(q_ref, kp_ref, kc_ref, vp_ref, vc_ref, o_ref, lse_ref):
    c = pl.program_id(1)
    g = pl.program_id(3)
    ch = ATTN_CHUNK
    qi = lax.broadcasted_iota(jnp.int32, (N_BACK, 2 * N_BACK), 0)
    kj = lax.broadcasted_iota(jnp.int32, (N_BACK, 2 * N_BACK), 1)
    dist = qi + N_BACK - kj
    band = jnp.where(dist >= 0, jnp.where(dist <= N_BACK, 0.0, -jnp.inf), -jnp.inf)
    band_first = jnp.where(kj >= N_BACK, band, -jnp.inf)

    def group_body(d):
        span = N_BACK * d
        n_sub = ch // N_BACK

        def rows(start, size):
            return pl.ds(start, size) if d == 1 else pl.ds(start, size, stride=d)

        def store(q0, o, lse):
            o_ref[rows(q0, N_BACK), :] = o
            lse_ref[rows(q0, N_BACK), :] = lse

        def head_block(r, carry):
            k = jnp.concatenate([kp_ref[rows(ch - span + r, N_BACK), :], kc_ref[rows(r, N_BACK), :]], axis=0)
            v = jnp.concatenate([vp_ref[rows(ch - span + r, N_BACK), :], vc_ref[rows(r, N_BACK), :]], axis=0)
            mask = jnp.where(c == 0, band_first, band)
            store(r, *_band_attention(q_ref[rows(r, N_BACK), :], k, v, mask))
            return carry

        def inner_block(idx, carry):
            s = idx // d
            r = idx % d
            k0 = (s - 1) * span + r
            store(s * span + r, *_band_attention(q_ref[rows(s * span + r, N_BACK), :],
                                                 kc_ref[rows(k0, 2 * N_BACK), :],
                                                 vc_ref[rows(k0, 2 * N_BACK), :], band))
            return carry

        lax.fori_loop(0, d, head_block, 0, unroll=min(d, 2))
        if n_sub > d:
            lax.fori_loop(d, n_sub, inner_block, 0, unroll=2 if (n_sub - d) % 2 == 0 else 3)

    for gi, (_, d) in enumerate(DILATION_GROUPS):
        pl.when(g == gi)(functools.partial(group_body, d))


def _attn_prompt(q, k, v, batch, seq):
    ch = ATTN_CHUNK
    cps = seq // ch
    n = batch * seq
    pairs = GROUP_WIDTH // LANES
    cur = lambda b, c, hp, g: (b * cps + c, g * pairs + hp)
    prev = lambda b, c, hp, g: (b * cps + jnp.maximum(c - 1, 0), g * pairs + hp)
    blk = (ch, LANES)
    return pl.pallas_call(
        _attn_prompt_kernel,
        grid=(batch, cps, pairs, len(DILATION_GROUPS)),
        in_specs=[pl.BlockSpec(blk, cur), pl.BlockSpec(blk, prev), pl.BlockSpec(blk, cur),
                  pl.BlockSpec(blk, prev), pl.BlockSpec(blk, cur)],
        out_specs=[pl.BlockSpec(blk, cur), pl.BlockSpec(blk, cur)],
        out_shape=[jax.ShapeDtypeStruct((n, ATTN_WIDTH), F32)] * 2,
        compiler_params=_params(("parallel", "parallel", "parallel", "parallel")),
        name="attn_prompt",
    )(q, k, k, v, v)


def _attn_sample_kernel(q_ref, k_ref, v_ref, c0_ref, c1_ref, c2_ref, o_ref, lse_ref):
    bt = q_ref.shape[0]
    for b in range(bt):
        for g, (c_ref, (win, d)) in enumerate(zip((c0_ref, c1_ref, c2_ref), DILATION_GROUPS)):
            pos = lax.broadcasted_iota(jnp.int32, (1, win), 1)
            off_stride = (pos % d) != 0
            for h in range(HEADS_PER_GROUP):
                j = g * HEADS_PER_GROUP + h
                q = q_ref[b, :, j:j + 1]
                k_new = k_ref[b, :, j:j + 1]
                v_new = v_ref[b, :, j:j + 1]
                s_c = jnp.sum(c_ref[b, 0, h] * q, axis=0, keepdims=True)
                s_c = jnp.where(off_stride, -jnp.inf, s_c)
                s_new = jnp.sum(k_new * q, axis=0, keepdims=True)
                m = jnp.maximum(jnp.max(s_c, axis=1, keepdims=True), s_new)
                p_c = jnp.exp(s_c - m)
                p_new = jnp.exp(s_new - m)
                l = jnp.sum(p_c, axis=1, keepdims=True) + p_new
                num = jnp.sum(c_ref[b, 1, h] * p_c, axis=1, keepdims=True) + p_new * v_new
                o_ref[b, :, j:j + 1] = num * (1.0 / l)
                lse_ref[b, :, j:j + 1] = m + jnp.log(l)


def _attn_sample(q, k, v, caches, bt):
    b = q.shape[0]
    n_heads = ATTN_WIDTH // HEAD_DIM
    views, specs = [], []
    for cache, (win, d) in zip(caches, DILATION_GROUPS):
        assert cache.shape[1] == win == N_BACK * d
        views.append(jnp.transpose(cache, (0, 2, 3, 4, 1)))
        specs.append(pl.BlockSpec((bt, 2, HEADS_PER_GROUP, HEAD_DIM, win), lambda i: (i, 0, 0, 0, 0)))
    col_spec = pl.BlockSpec((bt, HEAD_DIM, n_heads), lambda i: (i, 0, 0))
    lse_spec = pl.BlockSpec((bt, 1, n_heads), lambda i: (i, 0, 0))
    cols = lambda t: jnp.transpose(t.reshape(b, n_heads, HEAD_DIM), (0, 2, 1))
    o, lse = pl.pallas_call(
        _attn_sample_kernel,
        grid=(b // bt,),
        in_specs=[col_spec, col_spec, col_spec] + specs,
        out_specs=[col_spec, lse_spec],
        out_shape=[jax.ShapeDtypeStruct((b, HEAD_DIM, n_heads), F32),
                   jax.ShapeDtypeStruct((b, 1, n_heads), F32)],
        compiler_params=_params(("parallel",)),
        name="attn_sample",
    )(cols(q), cols(k), cols(v), *views)
    o = jnp.transpose(o, (0, 2, 1)).reshape(b, ATTN_WIDTH)
    lse = jnp.broadcast_to(jnp.transpose(lse, (0, 2, 1)), (b, n_heads, HEAD_DIM)).reshape(b, ATTN_WIDTH)
    return o, lse


def _s5_scan_kernel(u_ref, bmat_ref, cmat_ref, are_ref, aim_ref, d_ref, h0re_ref, h0im_ref,
                    y_ref, hre_ref, him_ref, hist_sc, *, bg, steps):
    t_chunk = pl.program_id(0)

    @pl.when(t_chunk == 0)
    def _():
        hre_ref[...] = h0re_ref[...]
        him_ref[...] = h0im_ref[...]

    u = u_ref[...]
    hist_sc[...] = jnp.dot(u.astype(BF16), bmat_ref[...], preferred_element_type=F32)
    a_re = jnp.broadcast_to(are_ref[...], (bg, SSM_LANES))
    a_im = jnp.broadcast_to(aim_ref[...], (bg, SSM_LANES))

    def step(t, carry):
        h_re, h_im = carry
        rows = pl.ds(pl.multiple_of(t * bg, bg), bg)
        n_re = a_re * h_re - a_im * h_im + hist_sc[rows, 0:SSM_LANES]
        n_im = a_re * h_im + a_im * h_re + hist_sc[rows, SSM_LANES:2 * SSM_LANES]
        hist_sc[rows, 0:SSM_LANES] = n_re
        hist_sc[rows, SSM_LANES:2 * SSM_LANES] = n_im
        return n_re, n_im

    h_re, h_im = lax.fori_loop(0, steps, step, (hre_ref[...], him_ref[...]))
    hre_ref[...] = h_re
    him_ref[...] = h_im
    y_ref[...] = (jnp.dot(hist_sc[...].astype(BF16), cmat_ref[...], preferred_element_type=F32)
                  + d_ref[...] * u)


def _s5_scan(u_tb, ssm, h0_re, h0_im, bg, steps):
    rows = u_tb.shape[0]
    blk = steps * bg
    kern = functools.partial(_s5_scan_kernel, bg=bg, steps=steps)
    state_spec = _full((bg, SSM_LANES))
    return pl.pallas_call(
        kern,
        grid=(rows // blk,),
        in_specs=[pl.BlockSpec((blk, SSM_WIDTH), lambda i: (i, 0)),
                  _full((SSM_WIDTH, 2 * SSM_LANES)), _full((2 * SSM_LANES, SSM_WIDTH)),
                  _full((1, SSM_LANES)), _full((1, SSM_LANES)), _full((1, SSM_WIDTH)),
                  state_spec, state_spec],
        out_specs=[pl.BlockSpec((blk, SSM_WIDTH), lambda i: (i, 0)), state_spec, state_spec],
        out_shape=[jax.ShapeDtypeStruct((rows, SSM_WIDTH), F32),
                   jax.ShapeDtypeStruct((bg, SSM_LANES), F32), jax.ShapeDtypeStruct((bg, SSM_LANES), F32)],
        scratch_shapes=[pltpu.VMEM((blk, 2 * SSM_LANES), F32)],
        compiler_params=_params(("arbitrary",)),
        name="s5_scan",
    )(u_tb, ssm["bmat"], ssm["cmat"], ssm["a_re"], ssm["a_im"], ssm["d_skip"], h0_re, h0_im)


def _s5_params(a_re, a_im, log_dt, b_re, b_im, c_re, c_im, d_skip):
    dt = jnp.exp(log_dt)[:, None]
    mag = jnp.exp(a_re * dt)
    abar_re = mag * jnp.cos(a_im * dt)
    abar_im = mag * jnp.sin(a_im * dt)
    a2 = a_re * a_re + a_im * a_im
    nr = abar_re - 1.0
    coef_re = (nr * a_re + abar_im * a_im) / a2
    coef_im = (abar_im * a_re - nr * a_im) / a2
    bb_re = coef_re[..., None] * b_re - coef_im[..., None] * b_im
    bb_im = coef_re[..., None] * b_im + coef_im[..., None] * b_re
    eye = jnp.eye(SSM_GROUPS, dtype=F32)
    to_b = lambda t: jnp.einsum("gpc,gh->gchp", t, eye).reshape(SSM_WIDTH, SSM_LANES)
    to_c = lambda t: jnp.einsum("gcp,gh->gphc", t, eye).reshape(SSM_LANES, SSM_WIDTH)
    return {
        "bmat": jnp.concatenate([to_b(bb_re), to_b(bb_im)], axis=1).astype(BF16),
        "cmat": jnp.concatenate([to_c(c_re), -to_c(c_im)], axis=0).astype(BF16),
        "a_re": abar_re.reshape(1, SSM_LANES), "a_im": abar_im.reshape(1, SSM_LANES),
        "d_skip": d_skip.reshape(1, SSM_WIDTH),
    }


def _layer_norm(z, g, b):
    mu = jnp.mean(z, axis=-1, keepdims=True)
    zc = z - mu
    var = jnp.mean(zc * zc, axis=-1, keepdims=True)
    return zc * lax.rsqrt(var + LN_EPS) * g + b


def _merge_groups(o, lse):
    parts = [slice(g * GROUP_WIDTH, (g + 1) * GROUP_WIDTH) for g in range(len(DILATION_GROUPS))]
    top = lse[:, parts[0]]
    for cols in parts[1:]:
        top = jnp.maximum(top, lse[:, cols])
    num = den = None
    for cols in parts:
        w = jnp.exp(lse[:, cols] - top)
        num = w * o[:, cols] if num is None else num + w * o[:, cols]
        den = w if den is None else den + w
    return num / den


def _store_packed_rows(ref, x):
    rows = x.shape[0]
    for j in range(ROW_TILE):
        lo = x[:, j * LANES:(j + 1) * LANES].astype(BF16).astype(F32)
        hi = x[:, (j + ROW_TILE) * LANES:(j + ROW_TILE + 1) * LANES].astype(BF16).astype(F32)
        word = (lax.bitcast_convert_type(lo, jnp.uint32) >> 16) | lax.bitcast_convert_type(hi, jnp.uint32)
        ref[pl.ds(j, rows, stride=ROW_TILE), :] = lax.bitcast_convert_type(word, jnp.int32)


def _load_packed_chunks(ref, rows, lead=None):
    lows, highs = [], []
    for j in range(ROW_TILE):
        idx = (pl.ds(j, rows, stride=ROW_TILE), slice(None))
        word = lax.bitcast_convert_type(ref[idx] if lead is None else ref[(lead,) + idx], jnp.uint32)
        lows.append(lax.bitcast_convert_type(word << 16, F32))
        highs.append(lax.bitcast_convert_type(word & jnp.uint32(0xFFFF0000), F32))
    return lows + highs


def _post_mixer_kernel(x_ref, ao_ref, lse_ref, y_ref, wglu_ref, bglu_ref, wgate_ref, bgate_ref, wab_ref, wsb_ref,
                       wout_ref, lng_ref, lnb_ref, wr_ref, rb_ref,
                       x1_ref, x1t_ref, gate_ref, idx_ref, topg_ref, cnt_ref):
    x = x_ref[...]
    xb = x.astype(BF16)
    s = jax.nn.gelu(y_ref[...])
    s = s * jax.nn.sigmoid(jnp.dot(s.astype(BF16), wglu_ref[...], preferred_element_type=F32) + bglu_ref[...])
    gates = jax.nn.sigmoid(jnp.dot(xb, wgate_ref[...], preferred_element_type=F32) + bgate_ref[...])
    attn_o = _merge_groups(ao_ref[...], lse_ref[...])
    attn_br = jnp.dot(attn_o.astype(BF16), wab_ref[...], preferred_element_type=F32)
    ssm_br = jnp.dot(s.astype(BF16), wsb_ref[...], preferred_element_type=F32)
    merged = gates[:, :D_MODEL] * attn_br + gates[:, D_MODEL:] * ssm_br
    mix = jnp.dot(merged.astype(BF16), wout_ref[...], preferred_element_type=F32)
    x1 = _layer_norm(DN_ALPHA * x + mix, lng_ref[...], lnb_ref[...])
    x1_ref[...] = x1
    tm = x1.shape[0]
    _store_packed_rows(x1t_ref, x1)

    scores = jax.nn.sigmoid(jnp.dot(x1, wr_ref[...], preferred_element_type=F32,
                                    precision=lax.Precision.HIGHEST))
    sel = scores + rb_ref[...]
    lane = lax.broadcasted_iota(jnp.int32, sel.shape, 1).astype(F32)
    slot_lane = lax.broadcasted_iota(jnp.int32, (tm, LANES), 1)
    chosen = jnp.zeros(sel.shape, F32)
    top_idx = jnp.zeros((tm, LANES), F32)
    top_s = jnp.zeros((tm, LANES), F32)
    for k in range(TOP_K):
        top = jnp.max(sel, axis=-1, keepdims=True)
        first = jnp.min(jnp.where(sel == top, lane, float(N_EXPERTS)), axis=-1, keepdims=True)
        hit = lane == first
        chosen = jnp.where(hit, 1.0, chosen)
        sel = jnp.where(hit, -jnp.inf, sel)
        top_idx = jnp.where(slot_lane == k, first, top_idx)
        top_s = jnp.where(slot_lane == k, jnp.sum(jnp.where(hit, scores, 0.0), axis=-1, keepdims=True), top_s)
    norm = ROUTED_SCALE / jnp.sum(scores * chosen, axis=-1, keepdims=True)
    gate_ref[...] = scores * chosen * norm
    idx_ref[...] = top_idx
    topg_ref[...] = top_s * norm

    @pl.when(pl.program_id(0) == 0)
    def _():
        cnt_ref[...] = jnp.zeros(cnt_ref.shape, F32)
    cnt_ref[...] += jnp.sum(chosen, axis=0, keepdims=True)


def _post_mixer(x, attn_o, attn_lse, y_tb, w, rows_per_seq, tm):
    n = x.shape[0]
    tiles_per_seq = rows_per_seq // tm
    row = lambda width: pl.BlockSpec((tm, width), lambda i: (i, 0))
    return pl.pallas_call(
        _post_mixer_kernel,
        grid=(n // tm,),
        in_specs=[row(D_MODEL), row(ATTN_WIDTH), row(ATTN_WIDTH),
                  pl.BlockSpec((tm, SSM_WIDTH), lambda i: (i % tiles_per_seq, i // tiles_per_seq)),
                  _full((SSM_WIDTH, SSM_WIDTH)), _full((1, SSM_WIDTH)),
                  _full((D_MODEL, 2 * D_MODEL)), _full((1, 2 * D_MODEL)),
                  _full((GROUP_WIDTH, D_MODEL)), _full((SSM_WIDTH, D_MODEL)), _full((D_MODEL, D_MODEL)),
                  _full((1, D_MODEL)), _full((1, D_MODEL)),
                  _full((D_MODEL, N_EXPERTS)), _full((1, N_EXPERTS))],
        out_specs=[row(D_MODEL), pl.BlockSpec((tm * ROW_TILE, LANES), lambda i: (i, 0)),
                   row(N_EXPERTS), row(LANES), row(LANES), _full((1, N_EXPERTS))],
        out_shape=[jax.ShapeDtypeStruct((n, D_MODEL), F32), jax.ShapeDtypeStruct((n * ROW_TILE, LANES), jnp.int32),
                   jax.ShapeDtypeStruct((n, N_EXPERTS), F32), jax.ShapeDtypeStruct((n, LANES), F32),
                   jax.ShapeDtypeStruct((n, LANES), F32), jax.ShapeDtypeStruct((1, N_EXPERTS), F32)],
        compiler_params=_params(("arbitrary",)),
        name="post_mixer",
    )(x, attn_o, attn_lse, y_tb, w["w_glu"], w["b_glu"], w["w_gate"], w["b_gate"], w["w_attn_br"], w["w_ssm_br"],
      w["w_out"], w["ln1_g"], w["ln1_b"], w["w_router"], w["router_bias"])


def _moe_ffn_kernel(x_ref, gate_ref, p_ref, w13_ref, w2_ref, ws13_ref, ws2_ref, wpg_ref, wple_ref,
                    lng_ref, lnb_ref, o_ref, acc_sc, xb_sc):
    e = pl.program_id(1)

    def glu_ffn(xb, w13, w2, row_scale):
        h13 = jnp.dot(xb, w13, preferred_element_type=F32)
        h = jax.nn.silu(h13[:, :EXPERT_FF]) * h13[:, EXPERT_FF:]
        if row_scale is not None:
            h = h * row_scale
        return jnp.dot(h.astype(BF16), w2, preferred_element_type=F32)

    @pl.when(e == 0)
    def _():
        xb = x_ref[...].astype(BF16)
        xb_sc[...] = xb
        ple = (jax.nn.sigmoid(jnp.dot(xb, wpg_ref[...], preferred_element_type=F32))
               * jnp.dot(p_ref[...].astype(BF16), wple_ref[...], preferred_element_type=F32))
        acc_sc[...] = glu_ffn(xb, ws13_ref[...], ws2_ref[...], None) + ple

    gates = gate_ref[...]
    lane = lax.broadcasted_iota(jnp.int32, gates.shape, 1)
    g_col = jnp.sum(jnp.where(lane == e, gates, 0.0), axis=-1, keepdims=True)
    acc_sc[...] += glu_ffn(xb_sc[...], w13_ref[0], w2_ref[0], g_col)

    @pl.when(e == N_EXPERTS - 1)
    def _():
        o_ref[...] = _layer_norm(DN_ALPHA * x_ref[...] + acc_sc[...], lng_ref[...], lnb_ref[...])


def _moe_ffn(x1, gates, p, w, tm):
    n = x1.shape[0]
    row = lambda width: pl.BlockSpec((tm, width), lambda i, e: (i, 0))
    return pl.pallas_call(
        _moe_ffn_kernel,
        grid=(n // tm, N_EXPERTS),
        in_specs=[row(D_MODEL), row(N_EXPERTS), row(PLE_DIM),
                  pl.BlockSpec((1, D_MODEL, 2 * EXPERT_FF), lambda i, e: (e, 0, 0)),
                  pl.BlockSpec((1, EXPERT_FF, D_MODEL), lambda i, e: (e, 0, 0)),
                  _full((D_MODEL, 2 * EXPERT_FF)), _full((EXPERT_FF, D_MODEL)),
                  _full((D_MODEL, D_MODEL)), _full((PLE_DIM, D_MODEL)),
                  _full((1, D_MODEL)), _full((1, D_MODEL))],
        out_specs=row(D_MODEL),
        out_shape=jax.ShapeDtypeStruct((n, D_MODEL), F32),
        scratch_shapes=[pltpu.VMEM((tm, D_MODEL), F32), pltpu.VMEM((tm, D_MODEL), BF16)],
        compiler_params=_params(("parallel", "arbitrary")),
        name="moe_ffn",
    )(x1, gates, p, w["w13"], w["w2"], w["ws13"], w["ws2"], w["w_ple_gate"], w["w_ple"],
      w["ln2_g"], w["ln2_b"])


def _route_kernel(idx_ref, pstart_ref, slot_ref, base_sc):
    @pl.when(pl.program_id(0) == 0)
    def _():
        base_sc[...] = jnp.zeros(base_sc.shape, F32)

    idx = idx_ref[...]
    tm = idx.shape[0]
    lane = lax.broadcasted_iota(jnp.int32, (tm, N_EXPERTS), 1).astype(F32)
    hits = [lane == idx[:, k:k + 1] for k in range(TOP_K)]
    member = jnp.zeros((tm, N_EXPERTS), F32)
    for hit in hits:
        member = member + jnp.where(hit, 1.0, 0.0)
    r = lax.broadcasted_iota(jnp.int32, (tm, tm), 0)
    c = lax.broadcasted_iota(jnp.int32, (tm, tm), 1)
    earlier = jnp.where(c < r, 1.0, 0.0).astype(BF16)
    row = (jnp.dot(earlier, member.astype(BF16), preferred_element_type=F32)
           + base_sc[...] + pstart_ref[...])
    slot_lane = lax.broadcasted_iota(jnp.int32, (tm, LANES), 1)
    out = jnp.zeros((tm, LANES), F32)
    for k, hit in enumerate(hits):
        out = jnp.where(slot_lane == k, jnp.sum(jnp.where(hit, row, 0.0), axis=-1, keepdims=True), out)
    slot_ref[...] = out.astype(jnp.int32)
    base_sc[...] += jnp.sum(member, axis=0, keepdims=True)


def _route(top_idx, pstart, tm):
    n = top_idx.shape[0]
    return pl.pallas_call(
        _route_kernel,
        grid=(n // tm,),
        in_specs=[pl.BlockSpec((tm, LANES), lambda i: (i, 0)), _full((1, N_EXPERTS))],
        out_specs=pl.BlockSpec((tm, LANES), lambda i: (i, 0)),
        out_shape=jax.ShapeDtypeStruct((n, LANES), jnp.int32),
        scratch_shapes=[pltpu.VMEM((1, N_EXPERTS), F32)],
        compiler_params=_params(("arbitrary",)),
        name="route",
    )(top_idx, pstart)


def _sc_mesh():
    return plsc.VectorSubcoreMesh(core_axis_name="c", subcore_axis_name="s",
                                  num_cores=SC_CORES, num_subcores=SC_SUBCORES)


def _sc_dispatch(x_tiles, slots, n_rows):
    n = x_tiles.shape[0]
    wins_per_worker = n // SC_WINDOW // (SC_CORES * SC_SUBCORES)

    def body(x_hbm, slot_hbm, xs_hbm, idx_v, rows_v):
        wid = lax.axis_index("s") * SC_CORES + lax.axis_index("c")

        @pl.loop(0, wins_per_worker)
        def _(i):
            win = wid * wins_per_worker + i
            pltpu.sync_copy(slot_hbm.at[win], idx_v)
            pltpu.sync_copy(x_hbm.at[pl.ds(win * SC_WINDOW, SC_WINDOW)], rows_v)
            for k in range(TOP_K):
                pltpu.sync_copy(rows_v, xs_hbm.at[idx_v.at[k]])

    return pl.kernel(
        body, out_type=jax.ShapeDtypeStruct((n_rows, ROW_TILE, LANES), jnp.int32), mesh=_sc_mesh(),
        scratch_types=[pltpu.VMEM((TOP_K, SC_WINDOW), jnp.int32),
                       pltpu.VMEM((SC_WINDOW, ROW_TILE, LANES), jnp.int32)],
        name="sc_dispatch",
    )(x_tiles, slots)


def _sc_combine(y_tiles, slots, n):
    wins_per_worker = n // SC_WINDOW // (SC_CORES * SC_SUBCORES)

    def body(ys_hbm, slot_hbm, yg_hbm, idx_v, rows_v):
        wid = lax.axis_index("s") * SC_CORES + lax.axis_index("c")

        @pl.loop(0, wins_per_worker)
        def _(i):
            win = wid * wins_per_worker + i
            pltpu.sync_copy(slot_hbm.at[win], idx_v)
            for k in range(TOP_K):
                pltpu.sync_copy(ys_hbm.at[idx_v.at[k]], rows_v)
                pltpu.sync_copy(rows_v, yg_hbm.at[k, pl.ds(win * SC_WINDOW, SC_WINDOW)])

    return pl.kernel(
        body, out_type=jax.ShapeDtypeStruct((TOP_K, n, ROW_TILE, LANES), jnp.int32), mesh=_sc_mesh(),
        scratch_types=[pltpu.VMEM((TOP_K, SC_WINDOW), jnp.int32),
                       pltpu.VMEM((SC_WINDOW, ROW_TILE, LANES), jnp.int32)],
        name="sc_combine",
    )(y_tiles, slots)


def _expert_ffn_kernel(bexp_ref, nused_ref, xs_ref, w13_ref, w2_ref, ys_ref):
    del bexp_ref

    @pl.when(pl.program_id(0) < nused_ref[0])
    def _():
        tb = MOE_BLOCK
        x = jnp.concatenate(_load_packed_chunks(xs_ref, tb), axis=1)
        h13 = jnp.dot(x.astype(BF16), w13_ref[0], preferred_element_type=F32)
        h = jax.nn.silu(h13[:, :EXPERT_FF]) * h13[:, EXPERT_FF:]
        _store_packed_rows(ys_ref, jnp.dot(h.astype(BF16), w2_ref[0], preferred_element_type=F32))


def _expert_ffn(xs_rows, block_expert, n_used, w):
    n_blocks = block_expert.shape[0]
    blk = (MOE_BLOCK * ROW_TILE, LANES)
    return pl.pallas_call(
        _expert_ffn_kernel,
        grid_spec=pltpu.PrefetchScalarGridSpec(
            num_scalar_prefetch=2, grid=(n_blocks,),
            in_specs=[pl.BlockSpec(blk, lambda i, be, nu: (i, 0)),
                      pl.BlockSpec((1, D_MODEL, 2 * EXPERT_FF), lambda i, be, nu: (be[i], 0, 0)),
                      pl.BlockSpec((1, EXPERT_FF, D_MODEL), lambda i, be, nu: (be[i], 0, 0))],
            out_specs=pl.BlockSpec(blk, lambda i, be, nu: (i, 0))),
        out_shape=jax.ShapeDtypeStruct(xs_rows.shape, jnp.int32),
        compiler_params=_params(("parallel",)),
        name="expert_ffn",
    )(block_expert, n_used, xs_rows, w["w13"], w["w2"])


def _moe_out_kernel(x_ref, g_ref, p_ref, yg_ref, ws13_ref, ws2_ref, wpg_ref, wple_ref, lng_ref, lnb_ref, o_ref):
    x = x_ref[...]
    xb = x.astype(BF16)
    tm = x.shape[0]
    g = g_ref[...]
    parts = None
    for k in range(TOP_K):
        chunks = [g[:, k:k + 1] * c for c in _load_packed_chunks(yg_ref, tm, lead=k)]
        parts = chunks if parts is None else [a + c for a, c in zip(parts, chunks)]
    routed = jnp.concatenate(parts, axis=1)
    h13 = jnp.dot(xb, ws13_ref[...], preferred_element_type=F32)
    h = jax.nn.silu(h13[:, :EXPERT_FF]) * h13[:, EXPERT_FF:]
    shared = jnp.dot(h.astype(BF16), ws2_ref[...], preferred_element_type=F32)
    ple = (jax.nn.sigmoid(jnp.dot(xb, wpg_ref[...], preferred_element_type=F32))
           * jnp.dot(p_ref[...].astype(BF16), wple_ref[...], preferred_element_type=F32))
    o_ref[...] = _layer_norm(DN_ALPHA * x + routed + shared + ple, lng_ref[...], lnb_ref[...])


def _moe_out(x1, top_gates, p, yg_rows, w, tm):
    n = x1.shape[0]
    row = lambda width: pl.BlockSpec((tm, width), lambda i: (i, 0))
    return pl.pallas_call(
        _moe_out_kernel,
        grid=(n // tm,),
        in_specs=[row(D_MODEL), row(LANES), row(PLE_DIM),
                  pl.BlockSpec((TOP_K, tm * ROW_TILE, LANES), lambda i: (0, i, 0)),
                  _full((D_MODEL, 2 * EXPERT_FF)), _full((EXPERT_FF, D_MODEL)),
                  _full((D_MODEL, D_MODEL)), _full((PLE_DIM, D_MODEL)),
                  _full((1, D_MODEL)), _full((1, D_MODEL))],
        out_specs=row(D_MODEL),
        out_shape=jax.ShapeDtypeStruct((n, D_MODEL), F32),
        compiler_params=_params(("parallel",)),
        name="moe_out",
    )(x1, top_gates, p, yg_rows, w["ws13"], w["ws2"], w["w_ple_gate"], w["w_ple"], w["ln2_g"], w["ln2_b"])


def _moe_sorted(x1, x1_tiles, top_idx, top_gates, counts, p, w):
    n = x1.shape[0]
    n_blocks = n * TOP_K // MOE_BLOCK + N_EXPERTS
    n_rows = n_blocks * MOE_BLOCK
    cnt = counts.reshape(N_EXPERTS).astype(jnp.int32)
    padded = (cnt + MOE_BLOCK - 1) // MOE_BLOCK * MOE_BLOCK
    pend = jnp.cumsum(padded)
    pstart = (pend - padded).astype(F32).reshape(1, N_EXPERTS)
    block_start = jnp.arange(n_blocks, dtype=jnp.int32) * MOE_BLOCK
    block_expert = jnp.minimum(jnp.sum((pend[None, :] <= block_start[:, None]).astype(jnp.int32), axis=1),
                               N_EXPERTS - 1)
    n_used = (pend[-1:] // MOE_BLOCK).astype(jnp.int32)
    slots = _route(top_idx, pstart, 512)[:, :TOP_K]
    slots = jnp.transpose(slots.reshape(n // SC_WINDOW, SC_WINDOW, TOP_K), (0, 2, 1))
    xs = _sc_dispatch(x1_tiles.reshape(n, ROW_TILE, LANES), slots, n_rows)
    ys = _expert_ffn(xs.reshape(n_rows * ROW_TILE, LANES), block_expert, n_used, w)
    yg = _sc_combine(ys.reshape(n_rows, ROW_TILE, LANES), slots, n)
    return _moe_out(x1, top_gates, p, yg.reshape(TOP_K, n * ROW_TILE, LANES), w, 256)


def _kv_rows(k, v, batch, seq, keep, g):
    cols = slice(g * GROUP_WIDTH, (g + 1) * GROUP_WIDTH)
    shape = (batch, keep, HEADS_PER_GROUP, HEAD_DIM)
    k_g = k.reshape(batch, seq, ATTN_WIDTH)[:, seq - keep:, cols].reshape(shape)
    v_g = v.reshape(batch, seq, ATTN_WIDTH)[:, seq - keep:, cols].reshape(shape)
    return jnp.stack([k_g, v_g], axis=2)


def _layer_prompt(x, p, w, ssm):
    batch, seq, _ = x.shape
    n = batch * seq
    x2 = x.reshape(n, D_MODEL)
    tabs = _rope_tables(jnp.arange(seq, dtype=jnp.int32))
    q, k, v, u = _in_proj(x2, w["w_in"], tabs, seq, 512)
    attn_o, attn_lse = _attn_prompt(q, k, v, batch, seq)
    zeros = jnp.zeros((batch, SSM_LANES), F32)
    y_tb, h_re, h_im = _s5_scan(u.reshape(seq * batch, SSM_WIDTH), ssm, zeros, zeros, batch, 128)
    x1, x1_tiles, _, top_idx, top_gates, counts = _post_mixer(
        x2, attn_o, attn_lse, y_tb.reshape(seq, batch * SSM_WIDTH), w, seq, 512)
    y = _moe_sorted(x1, x1_tiles, top_idx, top_gates, counts, p.reshape(n, PLE_DIM), w)
    kv = [_kv_rows(k, v, batch, seq, min(win, seq), g) for g, (win, _) in enumerate(DILATION_GROUPS)]
    h_last = jnp.stack([h_re, h_im], axis=-1).reshape(batch, SSM_GROUPS, SSM_STATE, 2)
    return y.reshape(batch, seq, D_MODEL), kv, h_last


def _layer_sample(x, p, caches, state, w, ssm):
    batch, seq, _ = x.shape
    assert seq == 1
    x2 = x.reshape(batch, D_MODEL)
    tabs = _rope_tables(jnp.full((batch,), PAST_LEN, dtype=jnp.int32))
    q, k, v, u = _in_proj(x2, w["w_in"], tabs, batch, batch)
    attn_o, attn_lse = _attn_sample(q, k, v, caches, 2)
    h0 = state.reshape(batch, SSM_LANES, 2)
    y_tb, h_re, h_im = _s5_scan(u, ssm, h0[..., 0], h0[..., 1], batch, 1)
    x1, _, gates, _, _, _ = _post_mixer(x2, attn_o, attn_lse, y_tb, w, batch, batch)
    y = _moe_ffn(x1, gates, p.reshape(batch, PLE_DIM), w, batch)
    kv = [_kv_rows(k, v, batch, 1, 1, g) for g in range(len(DILATION_GROUPS))]
    h_last = jnp.stack([h_re, h_im], axis=-1).reshape(batch, SSM_GROUPS, SSM_STATE, 2)
    return y.reshape(batch, 1, D_MODEL), kv, h_last


def kernel(x_prompt, x_sample, cache_kv_w128, cache_kv_w512, cache_kv_w2048, state_ssm, p_prompt, p_sample,
           w_in, a_re, a_im, log_dt, b_re, b_im, c_re, c_im, d_skip, w_glu, b_glu, w_attn_br, w_ssm_br,
           w_gate, b_gate, w_out, ln1_g, ln1_b, w_router, router_bias, w1, w3, w2, ws1, ws3, ws2,
           w_ple_gate, w_ple, ln2_g, ln2_b):
    assert w_in.shape[0] == DEPTH == 1
    l = 0
    row = lambda t: t[l].reshape(1, -1)
    w = {
        "w_in": w_in[l].astype(BF16),
        "w_glu": w_glu[l].astype(BF16), "b_glu": row(b_glu),
        "w_gate": w_gate[l].astype(BF16), "b_gate": row(b_gate),
        "w_attn_br": w_attn_br[l].astype(BF16), "w_ssm_br": w_ssm_br[l].astype(BF16),
        "w_out": w_out[l].astype(BF16), "ln1_g": row(ln1_g), "ln1_b": row(ln1_b),
        "w_router": w_router[l], "router_bias": row(router_bias),
        "w13": jnp.concatenate([w1[l], w3[l]], axis=-1).astype(BF16), "w2": w2[l].astype(BF16),
        "ws13": jnp.concatenate([ws1[l], ws3[l]], axis=-1).astype(BF16), "ws2": ws2[l].astype(BF16),
        "w_ple_gate": w_ple_gate[l].astype(BF16), "w_ple": w_ple[l].astype(BF16),
        "ln2_g": row(ln2_g), "ln2_b": row(ln2_b),
    }
    ssm = _s5_params(a_re[l], a_im[l], log_dt[l], b_re[l], b_im[l], c_re[l], c_im[l], d_skip[l])
    yp, kv_p, h_p = _layer_prompt(x_prompt, p_prompt[l], w, ssm)
    caches = (cache_kv_w128[l], cache_kv_w512[l], cache_kv_w2048[l])
    ys, kv_s, h_s = _layer_sample(x_sample, p_sample[l], caches, state_ssm[l], w, ssm)
    return (yp, ys, kv_p[0][None], kv_s[0][None], kv_p[1][None], kv_s[1][None],
            kv_p[2][None], kv_s[2][None], h_p[None], h_s[None])
```

```python
import functools
import math

import jax
import jax.numpy as jnp
from jax import lax
from jax.experimental import pallas as pl
from jax.experimental.pallas import tpu as pltpu
from jax.experimental.pallas import tpu_sc as plsc

F32 = jnp.float32
BF16 = jnp.bfloat16

D_MODEL = 1024
HEAD_DIM = 64
HEADS_PER_GROUP = 4
DILATION_GROUPS = ((128, 1), (512, 4), (2048, 16))
N_BACK = 128
GROUP_WIDTH = HEADS_PER_GROUP * HEAD_DIM
ATTN_WIDTH = 3 * GROUP_WIDTH
ROPE_THETA = 10000.0
SSM_WIDTH = 256
SSM_GROUP = 16
SSM_GROUPS = 16
SSM_STATE = 64
SSM_LANES = SSM_GROUPS * SSM_STATE
IN_WIDTH = 3 * ATTN_WIDTH + SSM_WIDTH
N_EXPERTS = 64
TOP_K = 8
EXPERT_FF = 256
ROUTED_SCALE = 2.5
PLE_DIM = 256
DEPTH = 1
PAST_LEN = 8192
DN_ALPHA = (2.0 * DEPTH) ** 0.25
LN_EPS = 1e-5

LANES = 128
ROW_TILE = D_MODEL // LANES // 2
SC_CORES = 2
SC_SUBCORES = 16
SC_WINDOW = 64
MOE_BLOCK = 1024
POST_MIXER_SUBTILES = 2
ATTN_CHUNK = 2048
VMEM_LIMIT = 56 * 1024 * 1024


def _params(semantics):
    return pltpu.CompilerParams(dimension_semantics=semantics, vmem_limit_bytes=VMEM_LIMIT)


def _full(shape):
    return pl.BlockSpec(shape, lambda *_: (0,) * len(shape))


def _in_proj_kernel(x_ref, w_ref, cos_ref, sina_ref, sinb_ref, q_ref, k_ref, v_ref, u_ref):
    xb = x_ref[...].astype(BF16)
    cos = cos_ref[...]
    sin_a = sina_ref[...]
    sin_b = sinb_ref[...]

    def rope_store(col0, out_ref, scale):
        t = jnp.dot(xb, w_ref[:, col0:col0 + ATTN_WIDTH], preferred_element_type=F32)
        for c in range(ATTN_WIDTH // LANES):
            xc = t[:, c * LANES:(c + 1) * LANES]
            r = xc * cos + pltpu.roll(xc, LANES - 32, 1) * sin_a + pltpu.roll(xc, 32, 1) * sin_b
            out_ref[:, c * LANES:(c + 1) * LANES] = r * scale if scale != 1.0 else r

    rope_store(0, q_ref, HEAD_DIM ** -0.5)
    rope_store(ATTN_WIDTH, k_ref, 1.0)
    v_ref[...] = jnp.dot(xb, w_ref[:, 2 * ATTN_WIDTH:3 * ATTN_WIDTH], preferred_element_type=F32)
    u_ref[...] = jnp.dot(xb, w_ref[:, 3 * ATTN_WIDTH:], preferred_element_type=F32)


def _in_proj(x, w_in_bf, rope_tabs, rows_per_seq, tm):
    n = x.shape[0]
    tiles_per_seq = rows_per_seq // tm
    n_seq = n // rows_per_seq
    tab_tiles = rope_tabs[0].shape[0] // tm
    tab_spec = pl.BlockSpec((tm, LANES), lambda i: (i % tab_tiles, 0))
    row_spec = pl.BlockSpec((tm, ATTN_WIDTH), lambda i: (i, 0))
    return pl.pallas_call(
        _in_proj_kernel,
        grid=(n // tm,),
        in_specs=[pl.BlockSpec((tm, D_MODEL), lambda i: (i, 0)), _full((D_MODEL, IN_WIDTH)),
                  tab_spec, tab_spec, tab_spec],
        out_specs=[row_spec, row_spec, row_spec,
                   pl.BlockSpec((tm, SSM_WIDTH), lambda i: (i % tiles_per_seq, i // tiles_per_seq))],
        out_shape=[jax.ShapeDtypeStruct((n, ATTN_WIDTH), F32)] * 3
        + [jax.ShapeDtypeStruct((rows_per_seq, n_seq * SSM_WIDTH), F32)],
        compiler_params=_params(("parallel",)),
        name="in_proj",
    )(x, w_in_bf, *rope_tabs)


def _rope_tables(pos):
    half = HEAD_DIM // 2
    inv = ROPE_THETA ** (-jnp.arange(half, dtype=F32) / half)
    ang = pos.astype(F32)[:, None] * inv[None, :]
    cos = jnp.tile(jnp.cos(ang), (1, LANES // half))
    sin = jnp.tile(jnp.sin(ang), (1, LANES // half))
    first_half = (jnp.arange(LANES) % HEAD_DIM) < half
    sin_a = jnp.where(first_half[None, :], -sin, 0.0)
    sin_b = jnp.where(first_half[None, :], 0.0, sin)
    return cos, sin_a, sin_b


def _band_attention(q, k, v, mask):
    o_parts, lse_parts = [], []
    for h in range(LANES // HEAD_DIM):
        cols = slice(h * HEAD_DIM, (h + 1) * HEAD_DIM)
        logits = lax.dot_general(q[:, cols].astype(BF16), k[:, cols].astype(BF16),
                                 (((1,), (1,)), ((), ())), preferred_element_type=F32) + mask
        m = jnp.max(logits, axis=1, keepdims=True)
        p = jnp.exp(logits - m)
        l = jnp.sum(p, axis=1, keepdims=True)
        pv = jnp.dot(p.astype(BF16), v[:, cols].astype(BF16), preferred_element_type=F32)
        o_parts.append(pv * (1.0 / l))
        lse_parts.append(jnp.broadcast_to(m + jnp.log(l), (N_BACK, HEAD_DIM)))
    return jnp.concatenate(o_parts, axis=1), jnp.concatenate(lse_parts, axis=1)


def _attn_prompt_kernel(q_ref, kp_ref, kc_ref, vp_ref, vc_ref, o_ref, lse_ref):
    c = pl.program_id(1)
    g = pl.program_id(3)
    ch = ATTN_CHUNK
    qi = lax.broadcasted_iota(jnp.int32, (N_BACK, 2 * N_BACK), 0)
    kj = lax.broadcasted_iota(jnp.int32, (N_BACK, 2 * N_BACK), 1)
    dist = qi + N_BACK - kj
    band = jnp.where(dist >= 0, jnp.where(dist <= N_BACK, 0.0, -jnp.inf), -jnp.inf)
    band_first = jnp.where(kj >= N_BACK, band, -jnp.inf)

    def group_body(d):
        span = N_BACK * d
        n_sub = ch // N_BACK

        def rows(start, size):
            return pl.ds(start, size) if d == 1 else pl.ds(start, size, stride=d)

        def store(q0, o, lse):
            o_ref[rows(q0, N_BACK), :] = o
            lse_ref[rows(q0, N_BACK), :] = lse

        def head_block(r, carry):
            k = jnp.concatenate([kp_ref[rows(ch - span + r, N_BACK), :], kc_ref[rows(r, N_BACK), :]], axis=0)
            v = jnp.concatenate([vp_ref[rows(ch - span + r, N_BACK), :], vc_ref[rows(r, N_BACK), :]], axis=0)
            mask = jnp.where(c == 0, band_first, band)
            store(r, *_band_attention(q_ref[rows(r, N_BACK), :], k, v, mask))
            return carry

        def inner_block(idx, carry):
            s = idx // d
            r = idx % d
            k0 = (s - 1) * span + r
            store(s * span + r, *_band_attention(q_ref[rows(s * span + r, N_BACK), :],
                                                 kc_ref[rows(k0, 2 * N_BACK), :],
                                                 vc_ref[rows(k0, 2 * N_BACK), :], band))
            return carry

        lax.fori_loop(0, d, head_block, 0, unroll=min(d, 2))
        if n_sub > d:
            lax.fori_loop(d, n_sub, inner_block, 0, unroll=2 if (n_sub - d) % 2 == 0 else 3)

    for gi, (_, d) in enumerate(DILATION_GROUPS):
        pl.when(g == gi)(functools.partial(group_body, d))


def _attn_prompt(q, k, v, batch, seq):
    ch = ATTN_CHUNK
    cps = seq // ch
    n = batch * seq
    pairs = GROUP_WIDTH // LANES
    cur = lambda b, c, hp, g: (b * cps + c, g * pairs + hp)
    prev = lambda b, c, hp, g: (b * cps + jnp.maximum(c - 1, 0), g * pairs + hp)
    blk = (ch, LANES)
    return pl.pallas_call(
        _attn_prompt_kernel,
        grid=(batch, cps, pairs, len(DILATION_GROUPS)),
        in_specs=[pl.BlockSpec(blk, cur), pl.BlockSpec(blk, prev), pl.BlockSpec(blk, cur),
                  pl.BlockSpec(blk, prev), pl.BlockSpec(blk, cur)],
        out_specs=[pl.BlockSpec(blk, cur), pl.BlockSpec(blk, cur)],
        out_shape=[jax.ShapeDtypeStruct((n, ATTN_WIDTH), F32)] * 2,
        compiler_params=_params(("parallel", "parallel", "parallel", "parallel")),
        name="attn_prompt",
    )(q, k, k, v, v)


def _attn_sample_kernel(q_ref, k_ref, v_ref, c0_ref, c1_ref, c2_ref, o_ref, lse_ref):
    bt = q_ref.shape[0]
    for b in range(bt):
        for g, (c_ref, (win, d)) in enumerate(zip((c0_ref, c1_ref, c2_ref), DILATION_GROUPS)):
            pos = lax.broadcasted_iota(jnp.int32, (1, win), 1)
            off_stride = (pos % d) != 0
            j0 = g * HEADS_PER_GROUP
            heads = range(HEADS_PER_GROUP)
            qs = [q_ref[b, :, j0 + h:j0 + h + 1] for h in heads]
            s_c = jnp.concatenate([jnp.sum(c_ref[b, 0, h] * qs[h], axis=0, keepdims=True) for h in heads],
                                  axis=0)
            s_c = jnp.where(off_stride, -jnp.inf, s_c)
            s_new = jnp.concatenate([jnp.sum(k_ref[b, :, j0 + h:j0 + h + 1] * qs[h], axis=0, keepdims=True)
                                     for h in heads], axis=0)
            m = jnp.maximum(jnp.max(s_c, axis=1, keepdims=True), s_new)
            p_c = jnp.exp(s_c - m)
            p_new = jnp.exp(s_new - m)
            l = jnp.sum(p_c, axis=1, keepdims=True) + p_new
            inv_l = 1.0 / l
            lse_ref[b, j0:j0 + HEADS_PER_GROUP, :] = m + jnp.log(l)
            for h in heads:
                num = (jnp.sum(c_ref[b, 1, h] * p_c[h:h + 1, :], axis=1, keepdims=True)
                       + p_new[h:h + 1, :] * v_ref[b, :, j0 + h:j0 + h + 1])
                o_ref[b, :, j0 + h:j0 + h + 1] = num * inv_l[h:h + 1, :]


def _attn_sample(q, k, v, caches, bt):
    b = q.shape[0]
    n_heads = ATTN_WIDTH // HEAD_DIM
    views, specs = [], []
    for cache, (win, d) in zip(caches, DILATION_GROUPS):
        assert cache.shape[1] == win == N_BACK * d
        views.append(jnp.transpose(cache, (0, 2, 3, 4, 1)))
        specs.append(pl.BlockSpec((bt, 2, HEADS_PER_GROUP, HEAD_DIM, win), lambda i: (i, 0, 0, 0, 0)))
    col_spec = pl.BlockSpec((bt, HEAD_DIM, n_heads), lambda i: (i, 0, 0))
    lse_spec = pl.BlockSpec((bt, n_heads, 1), lambda i: (i, 0, 0))
    cols = lambda t: jnp.transpose(t.reshape(b, n_heads, HEAD_DIM), (0, 2, 1))
    o, lse = pl.pallas_call(
        _attn_sample_kernel,
        grid=(b // bt,),
        in_specs=[col_spec, col_spec, col_spec] + specs,
        out_specs=[col_spec, lse_spec],
        out_shape=[jax.ShapeDtypeStruct((b, HEAD_DIM, n_heads), F32),
                   jax.ShapeDtypeStruct((b, n_heads, 1), F32)],
        compiler_params=_params(("parallel",)),
        name="attn_sample",
    )(cols(q), cols(k), cols(v), *views)
    o = jnp.transpose(o, (0, 2, 1)).reshape(b, ATTN_WIDTH)
    lse = jnp.broadcast_to(lse, (b, n_heads, HEAD_DIM)).reshape(b, ATTN_WIDTH)
    return o, lse


def _s5_scan_kernel(u_ref, bmat_ref, cmat_ref, are_ref, aim_ref, d_ref, h0re_ref, h0im_ref,
                    y_ref, hre_ref, him_ref, hist_sc, *, bg, steps):
    t_chunk = pl.program_id(0)

    @pl.when(t_chunk == 0)
    def _():
        hre_ref[...] = h0re_ref[...]
        him_ref[...] = h0im_ref[...]

    u = u_ref[...]
    hist_sc[...] = jnp.dot(u.astype(BF16), bmat_ref[...], preferred_element_type=F32)
    a_re = jnp.broadcast_to(are_ref[...], (bg, SSM_LANES))
    a_im = jnp.broadcast_to(aim_ref[...], (bg, SSM_LANES))

    def step(t, carry):
        h_re, h_im = carry
        rows = pl.ds(pl.multiple_of(t * bg, bg), bg)
        n_re = a_re * h_re - a_im * h_im + hist_sc[rows, 0:SSM_LANES]
        n_im = a_re * h_im + a_im * h_re + hist_sc[rows, SSM_LANES:2 * SSM_LANES]
        hist_sc[rows, 0:SSM_LANES] = n_re
        hist_sc[rows, SSM_LANES:2 * SSM_LANES] = n_im
        return n_re, n_im

    h_re, h_im = lax.fori_loop(0, steps, step, (hre_ref[...], him_ref[...]))
    hre_ref[...] = h_re
    him_ref[...] = h_im
    y_ref[...] = (jnp.dot(hist_sc[...].astype(BF16), cmat_ref[...], preferred_element_type=F32)
                  + d_ref[...] * u)


def _s5_scan(u_tb, ssm, h0_re, h0_im, bg, steps):
    rows = u_tb.shape[0]
    blk = steps * bg
    kern = functools.partial(_s5_scan_kernel, bg=bg, steps=steps)
    state_spec = _full((bg, SSM_LANES))
    return pl.pallas_call(
        kern,
        grid=(rows // blk,),
        in_specs=[pl.BlockSpec((blk, SSM_WIDTH), lambda i: (i, 0)),
                  _full((SSM_WIDTH, 2 * SSM_LANES)), _full((2 * SSM_LANES, SSM_WIDTH)),
                  _full((1, SSM_LANES)), _full((1, SSM_LANES)), _full((1, SSM_WIDTH)),
                  state_spec, state_spec],
        out_specs=[pl.BlockSpec((blk, SSM_WIDTH), lambda i: (i, 0)), state_spec, state_spec],
        out_shape=[jax.ShapeDtypeStruct((rows, SSM_WIDTH), F32),
                   jax.ShapeDtypeStruct((bg, SSM_LANES), F32), jax.ShapeDtypeStruct((bg, SSM_LANES), F32)],
        scratch_shapes=[pltpu.VMEM((blk, 2 * SSM_LANES), F32)],
        compiler_params=_params(("arbitrary",)),
        name="s5_scan",
    )(u_tb, ssm["bmat"], ssm["cmat"], ssm["a_re"], ssm["a_im"], ssm["d_skip"], h0_re, h0_im)


def _s5_params(a_re, a_im, log_dt, b_re, b_im, c_re, c_im, d_skip):
    dt = jnp.exp(log_dt)[:, None]
    mag = jnp.exp(a_re * dt)
    abar_re = mag * jnp.cos(a_im * dt)
    abar_im = mag * jnp.sin(a_im * dt)
    a2 = a_re * a_re + a_im * a_im
    nr = abar_re - 1.0
    coef_re = (nr * a_re + abar_im * a_im) / a2
    coef_im = (abar_im * a_re - nr * a_im) / a2
    bb_re = coef_re[..., None] * b_re - coef_im[..., None] * b_im
    bb_im = coef_re[..., None] * b_im + coef_im[..., None] * b_re
    eye = jnp.eye(SSM_GROUPS, dtype=F32)
    to_b = lambda t: jnp.einsum("gpc,gh->gchp", t, eye).reshape(SSM_WIDTH, SSM_LANES)
    to_c = lambda t: jnp.einsum("gcp,gh->gphc", t, eye).reshape(SSM_LANES, SSM_WIDTH)
    return {
        "bmat": jnp.concatenate([to_b(bb_re), to_b(bb_im)], axis=1).astype(BF16),
        "cmat": jnp.concatenate([to_c(c_re), -to_c(c_im)], axis=0).astype(BF16),
        "a_re": abar_re.reshape(1, SSM_LANES), "a_im": abar_im.reshape(1, SSM_LANES),
        "d_skip": d_skip.reshape(1, SSM_WIDTH),
    }


def _layer_norm(z, g, b):
    mu = jnp.mean(z, axis=-1, keepdims=True)
    zc = z - mu
    var = jnp.mean(zc * zc, axis=-1, keepdims=True)
    return zc * lax.rsqrt(var + LN_EPS) * g + b


def _merge_groups(o, lse):
    parts = [slice(g * GROUP_WIDTH, (g + 1) * GROUP_WIDTH) for g in range(len(DILATION_GROUPS))]
    top = lse[:, parts[0]]
    for cols in parts[1:]:
        top = jnp.maximum(top, lse[:, cols])
    num = den = None
    for cols in parts:
        w = jnp.exp(lse[:, cols] - top)
        num = w * o[:, cols] if num is None else num + w * o[:, cols]
        den = w if den is None else den + w
    return num / den


def _store_packed_rows(ref, x, row0=0):
    rows = x.shape[0]
    for j in range(ROW_TILE):
        lo = x[:, j * LANES:(j + 1) * LANES].astype(BF16).astype(F32)
        hi = x[:, (j + ROW_TILE) * LANES:(j + ROW_TILE + 1) * LANES].astype(BF16).astype(F32)
        word = (lax.bitcast_convert_type(lo, jnp.uint32) >> 16) | lax.bitcast_convert_type(hi, jnp.uint32)
        ref[pl.ds(row0 * ROW_TILE + j, rows, stride=ROW_TILE), :] = lax.bitcast_convert_type(word, jnp.int32)


def _load_packed_chunks(ref, rows, lead=None):
    lows, highs = [], []
    for j in range(ROW_TILE):
        idx = (pl.ds(j, rows, stride=ROW_TILE), slice(None))
        word = lax.bitcast_convert_type(ref[idx] if lead is None else ref[(lead,) + idx], jnp.uint32)
        lows.append(lax.bitcast_convert_type(word << 16, F32))
        highs.append(lax.bitcast_convert_type(word & jnp.uint32(0xFFFF0000), F32))
    return lows + highs


def _post_mixer_kernel(x_ref, ao_ref, lse_ref, y_ref, wglu_ref, bglu_ref, wgate_ref, bgate_ref, wab_ref, wsb_ref,
                       wout_ref, lng_ref, lnb_ref, wr_ref, rb_ref,
                       x1_ref, x1t_ref, gate_ref, idx_ref, topg_ref, cnt_ref):
    @pl.when(pl.program_id(0) == 0)
    def _():
        cnt_ref[...] = jnp.zeros(cnt_ref.shape, F32)

    tm = x_ref.shape[0]
    sub = tm // POST_MIXER_SUBTILES if tm % (8 * POST_MIXER_SUBTILES) == 0 else tm
    for r0 in range(0, tm, sub):
        rows = slice(r0, r0 + sub)
        x = x_ref[rows, :]
        xb = x.astype(BF16)
        s = jax.nn.gelu(y_ref[rows, :])
        s = s * jax.nn.sigmoid(jnp.dot(s.astype(BF16), wglu_ref[...], preferred_element_type=F32) + bglu_ref[...])
        gates = jax.nn.sigmoid(jnp.dot(xb, wgate_ref[...], preferred_element_type=F32) + bgate_ref[...])
        attn_o = _merge_groups(ao_ref[rows, :], lse_ref[rows, :])
        attn_br = jnp.dot(attn_o.astype(BF16), wab_ref[...], preferred_element_type=F32)
        ssm_br = jnp.dot(s.astype(BF16), wsb_ref[...], preferred_element_type=F32)
        merged = gates[:, :D_MODEL] * attn_br + gates[:, D_MODEL:] * ssm_br
        mix = jnp.dot(merged.astype(BF16), wout_ref[...], preferred_element_type=F32)
        x1 = _layer_norm(DN_ALPHA * x + mix, lng_ref[...], lnb_ref[...])
        x1_ref[rows, :] = x1
        _store_packed_rows(x1t_ref, x1, r0)

        x1_hi = x1.astype(BF16)
        x1_lo = (x1 - x1_hi.astype(F32)).astype(BF16)
        prod = jnp.dot(jnp.concatenate([x1_hi, x1_lo], axis=0), wr_ref[...], preferred_element_type=F32)
        logits = ((prod[:sub, :N_EXPERTS] + prod[:sub, N_EXPERTS:] + prod[sub:, :N_EXPERTS])
                  + prod[sub:, N_EXPERTS:])
        scores = jax.nn.sigmoid(logits)
        sel = scores + rb_ref[...]
        lane = lax.broadcasted_iota(jnp.int32, sel.shape, 1).astype(F32)
        slot_lane = lax.broadcasted_iota(jnp.int32, (sub, LANES), 1)
        chosen = jnp.zeros(sel.shape, F32)
        top_idx = jnp.zeros((sub, LANES), F32)
        top_s = jnp.zeros((sub, LANES), F32)
        for k in range(TOP_K):
            top = jnp.max(sel, axis=-1, keepdims=True)
            first = jnp.min(jnp.where(sel == top, lane, float(N_EXPERTS)), axis=-1, keepdims=True)
            hit = lane == first
            chosen = jnp.where(hit, 1.0, chosen)
            sel = jnp.where(hit, -jnp.inf, sel)
            top_idx = jnp.where(slot_lane == k, first, top_idx)
            top_s = jnp.where(slot_lane == k,
                              jnp.sum(jnp.where(hit, scores, 0.0), axis=-1, keepdims=True), top_s)
        norm = ROUTED_SCALE / jnp.sum(scores * chosen, axis=-1, keepdims=True)
        gate_ref[rows, :] = scores * chosen * norm
        idx_ref[rows, :] = top_idx
        topg_ref[rows, :] = top_s * norm
        cnt_ref[...] += jnp.sum(chosen, axis=0, keepdims=True)


def _post_mixer(x, attn_o, attn_lse, y_tb, w, rows_per_seq, tm):
    n = x.shape[0]
    tiles_per_seq = rows_per_seq // tm
    row = lambda width: pl.BlockSpec((tm, width), lambda i: (i, 0))
    return pl.pallas_call(
        _post_mixer_kernel,
        grid=(n // tm,),
        in_specs=[row(D_MODEL), row(ATTN_WIDTH), row(ATTN_WIDTH),
                  pl.BlockSpec((tm, SSM_WIDTH), lambda i: (i % tiles_per_seq, i // tiles_per_seq)),
                  _full((SSM_WIDTH, SSM_WIDTH)), _full((1, SSM_WIDTH)),
                  _full((D_MODEL, 2 * D_MODEL)), _full((1, 2 * D_MODEL)),
                  _full((GROUP_WIDTH, D_MODEL)), _full((SSM_WIDTH, D_MODEL)), _full((D_MODEL, D_MODEL)),
                  _full((1, D_MODEL)), _full((1, D_MODEL)),
                  _full((D_MODEL, 2 * N_EXPERTS)), _full((1, N_EXPERTS))],
        out_specs=[row(D_MODEL), pl.BlockSpec((tm * ROW_TILE, LANES), lambda i: (i, 0)),
                   row(N_EXPERTS), row(LANES), row(LANES), _full((1, N_EXPERTS))],
        out_shape=[jax.ShapeDtypeStruct((n, D_MODEL), F32), jax.ShapeDtypeStruct((n * ROW_TILE, LANES), jnp.int32),
                   jax.ShapeDtypeStruct((n, N_EXPERTS), F32), jax.ShapeDtypeStruct((n, LANES), F32),
                   jax.ShapeDtypeStruct((n, LANES), F32), jax.ShapeDtypeStruct((1, N_EXPERTS), F32)],
        compiler_params=_params(("arbitrary",)),
        name="post_mixer",
    )(x, attn_o, attn_lse, y_tb, w["w_glu"], w["b_glu"], w["w_gate"], w["b_gate"], w["w_attn_br"], w["w_ssm_br"],
      w["w_out"], w["ln1_g"], w["ln1_b"], w["w_router"], w["router_bias"])


def _moe_ffn_kernel(x_ref, gate_ref, p_ref, w13_ref, w2_ref, ws13_ref, ws2_ref, wpg_ref, wple_ref,
                    lng_ref, lnb_ref, o_ref, acc_sc, xb_sc):
    e = pl.program_id(1)

    def glu_ffn(xb, w13, w2, row_scale):
        h13 = jnp.dot(xb, w13, preferred_element_type=F32)
        h = jax.nn.silu(h13[:, :EXPERT_FF]) * h13[:, EXPERT_FF:]
        if row_scale is not None:
            h = h * row_scale
        return jnp.dot(h.astype(BF16), w2, preferred_element_type=F32)

    @pl.when(e == 0)
    def _():
        xb = x_ref[...].astype(BF16)
        xb_sc[...] = xb
        ple = (jax.nn.sigmoid(jnp.dot(xb, wpg_ref[...], preferred_element_type=F32))
               * jnp.dot(p_ref[...].astype(BF16), wple_ref[...], preferred_element_type=F32))
        acc_sc[...] = glu_ffn(xb, ws13_ref[...], ws2_ref[...], None) + ple

    gates = gate_ref[...]
    lane = lax.broadcasted_iota(jnp.int32, gates.shape, 1)
    g_col = jnp.sum(jnp.where(lane == e, gates, 0.0), axis=-1, keepdims=True)
    acc_sc[...] += glu_ffn(xb_sc[...], w13_ref[0], w2_ref[0], g_col)

    @pl.when(e == N_EXPERTS - 1)
    def _():
        o_ref[...] = _layer_norm(DN_ALPHA * x_ref[...] + acc_sc[...], lng_ref[...], lnb_ref[...])


def _moe_ffn(x1, gates, p, w, tm):
    n = x1.shape[0]
    row = lambda width: pl.BlockSpec((tm, width), lambda i, e: (i, 0))
    return pl.pallas_call(
        _moe_ffn_kernel,
        grid=(n // tm, N_EXPERTS),
        in_specs=[row(D_MODEL), row(N_EXPERTS), row(PLE_DIM),
                  pl.BlockSpec((1, D_MODEL, 2 * EXPERT_FF), lambda i, e: (e, 0, 0)),
                  pl.BlockSpec((1, EXPERT_FF, D_MODEL), lambda i, e: (e, 0, 0)),
                  _full((D_MODEL, 2 * EXPERT_FF)), _full((EXPERT_FF, D_MODEL)),
                  _full((D_MODEL, D_MODEL)), _full((PLE_DIM, D_MODEL)),
                  _full((1, D_MODEL)), _full((1, D_MODEL))],
        out_specs=row(D_MODEL),
        out_shape=jax.ShapeDtypeStruct((n, D_MODEL), F32),
        scratch_shapes=[pltpu.VMEM((tm, D_MODEL), F32), pltpu.VMEM((tm, D_MODEL), BF16)],
        compiler_params=_params(("parallel", "arbitrary")),
        name="moe_ffn",
    )(x1, gates, p, w["w13"], w["w2"], w["ws13"], w["ws2"], w["w_ple_gate"], w["w_ple"],
      w["ln2_g"], w["ln2_b"])


def _route_kernel(idx_ref, pstart_ref, slot_ref, base_sc):
    @pl.when(pl.program_id(0) == 0)
    def _():
        base_sc[...] = jnp.zeros(base_sc.shape, F32)

    idx = idx_ref[...]
    tm = idx.shape[0]
    lane = lax.broadcasted_iota(jnp.int32, (tm, N_EXPERTS), 1).astype(F32)
    hits = [lane == idx[:, k:k + 1] for k in range(TOP_K)]
    member = jnp.zeros((tm, N_EXPERTS), F32)
    for hit in hits:
        member = member + jnp.where(hit, 1.0, 0.0)
    r = lax.broadcasted_iota(jnp.int32, (tm, tm), 0)
    c = lax.broadcasted_iota(jnp.int32, (tm, tm), 1)
    earlier = jnp.where(c < r, 1.0, 0.0).astype(BF16)
    row = (jnp.dot(earlier, member.astype(BF16), preferred_element_type=F32)
           + base_sc[...] + pstart_ref[...])
    slot_lane = lax.broadcasted_iota(jnp.int32, (tm, LANES), 1)
    out = jnp.zeros((tm, LANES), F32)
    for k, hit in enumerate(hits):
        out = jnp.where(slot_lane == k, jnp.sum(jnp.where(hit, row, 0.0), axis=-1, keepdims=True), out)
    slot_ref[...] = out.astype(jnp.int32)
    base_sc[...] += jnp.sum(member, axis=0, keepdims=True)


def _route(top_idx, pstart, tm):
    n = top_idx.shape[0]
    return pl.pallas_call(
        _route_kernel,
        grid=(n // tm,),
        in_specs=[pl.BlockSpec((tm, LANES), lambda i: (i, 0)), _full((1, N_EXPERTS))],
        out_specs=pl.BlockSpec((tm, LANES), lambda i: (i, 0)),
        out_shape=jax.ShapeDtypeStruct((n, LANES), jnp.int32),
        scratch_shapes=[pltpu.VMEM((1, N_EXPERTS), F32)],
        compiler_params=_params(("arbitrary",)),
        name="route",
    )(top_idx, pstart)


def _sc_mesh():
    return plsc.VectorSubcoreMesh(core_axis_name="c", subcore_axis_name="s",
                                  num_cores=SC_CORES, num_subcores=SC_SUBCORES)


def _sc_dispatch(x_tiles, slots, n_rows):
    n = x_tiles.shape[0]
    wins_per_worker = n // SC_WINDOW // (SC_CORES * SC_SUBCORES)

    def body(x_hbm, slot_hbm, xs_hbm, idx_v, rows_v):
        wid = lax.axis_index("s") * SC_CORES + lax.axis_index("c")

        @pl.loop(0, wins_per_worker)
        def _(i):
            win = wid * wins_per_worker + i
            pltpu.sync_copy(slot_hbm.at[win], idx_v)
            pltpu.sync_copy(x_hbm.at[pl.ds(win * SC_WINDOW, SC_WINDOW)], rows_v)
            for k in range(TOP_K):
                pltpu.sync_copy(rows_v, xs_hbm.at[idx_v.at[k]])

    return pl.kernel(
        body, out_type=jax.ShapeDtypeStruct((n_rows, ROW_TILE, LANES), jnp.int32), mesh=_sc_mesh(),
        scratch_types=[pltpu.VMEM((TOP_K, SC_WINDOW), jnp.int32),
                       pltpu.VMEM((SC_WINDOW, ROW_TILE, LANES), jnp.int32)],
        name="sc_dispatch",
    )(x_tiles, slots)


def _sc_combine(y_tiles, slots, n):
    wins_per_worker = n // SC_WINDOW // (SC_CORES * SC_SUBCORES)

    def body(ys_hbm, slot_hbm, yg_hbm, idx_v, rows_v):
        wid = lax.axis_index("s") * SC_CORES + lax.axis_index("c")

        @pl.loop(0, wins_per_worker)
        def _(i):
            win = wid * wins_per_worker + i
            pltpu.sync_copy(slot_hbm.at[win], idx_v)
            for k in range(TOP_K):
                pltpu.sync_copy(ys_hbm.at[idx_v.at[k]], rows_v)
                pltpu.sync_copy(rows_v, yg_hbm.at[k, pl.ds(win * SC_WINDOW, SC_WINDOW)])

    return pl.kernel(
        body, out_type=jax.ShapeDtypeStruct((TOP_K, n, ROW_TILE, LANES), jnp.int32), mesh=_sc_mesh(),
        scratch_types=[pltpu.VMEM((TOP_K, SC_WINDOW), jnp.int32),
                       pltpu.VMEM((SC_WINDOW, ROW_TILE, LANES), jnp.int32)],
        name="sc_combine",
    )(y_tiles, slots)


def _expert_ffn_kernel(bexp_ref, nused_ref, xs_ref, w13_ref, w2_ref, ys_ref):
    del bexp_ref

    @pl.when(pl.program_id(0) < nused_ref[0])
    def _():
        tb = MOE_BLOCK
        x = jnp.concatenate(_load_packed_chunks(xs_ref, tb), axis=1)
        h13 = jnp.dot(x.astype(BF16), w13_ref[0], preferred_element_type=F32)
        h = jax.nn.silu(h13[:, :EXPERT_FF]) * h13[:, EXPERT_FF:]
        _store_packed_rows(ys_ref, jnp.dot(h.astype(BF16), w2_ref[0], preferred_element_type=F32))


def _expert_ffn(xs_rows, block_expert, n_used, w):
    n_blocks = block_expert.shape[0]
    blk = (MOE_BLOCK * ROW_TILE, LANES)
    return pl.pallas_call(
        _expert_ffn_kernel,
        grid_spec=pltpu.PrefetchScalarGridSpec(
            num_scalar_prefetch=2, grid=(n_blocks,),
            in_specs=[pl.BlockSpec(blk, lambda i, be, nu: (i, 0)),
                      pl.BlockSpec((1, D_MODEL, 2 * EXPERT_FF), lambda i, be, nu: (be[i], 0, 0)),
                      pl.BlockSpec((1, EXPERT_FF, D_MODEL), lambda i, be, nu: (be[i], 0, 0))],
            out_specs=pl.BlockSpec(blk, lambda i, be, nu: (i, 0))),
        out_shape=jax.ShapeDtypeStruct(xs_rows.shape, jnp.int32),
        compiler_params=_params(("parallel",)),
        name="expert_ffn",
    )(block_expert, n_used, xs_rows, w["w13"], w["w2"])


def _moe_out_kernel(x_ref, g_ref, p_ref, yg_ref, ws13_ref, ws2_ref, wpg_ref, wple_ref, lng_ref, lnb_ref, o_ref):
    x = x_ref[...]
    xb = x.astype(BF16)
    tm = x.shape[0]
    g = g_ref[...]
    parts = None
    for k in range(TOP_K):
        chunks = [g[:, k:k + 1] * c for c in _load_packed_chunks(yg_ref, tm, lead=k)]
        parts = chunks if parts is None else [a + c for a, c in zip(parts, chunks)]
    routed = jnp.concatenate(parts, axis=1)
    h13 = jnp.dot(xb, ws13_ref[...], preferred_element_type=F32)
    h = jax.nn.silu(h13[:, :EXPERT_FF]) * h13[:, EXPERT_FF:]
    shared = jnp.dot(h.astype(BF16), ws2_ref[...], preferred_element_type=F32)
    ple = (jax.nn.sigmoid(jnp.dot(xb, wpg_ref[...], preferred_element_type=F32))
           * jnp.dot(p_ref[...].astype(BF16), wple_ref[...], preferred_element_type=F32))
    o_ref[...] = _layer_norm(DN_ALPHA * x + routed + shared + ple, lng_ref[...], lnb_ref[...])


def _moe_out(x1, top_gates, p, yg_rows, w, tm):
    n = x1.shape[0]
    row = lambda width: pl.BlockSpec((tm, width), lambda i: (i, 0))
    return pl.pallas_call(
        _moe_out_kernel,
        grid=(n // tm,),
        in_specs=[row(D_MODEL), row(LANES), row(PLE_DIM),
                  pl.BlockSpec((TOP_K, tm * ROW_TILE, LANES), lambda i: (0, i, 0)),
                  _full((D_MODEL, 2 * EXPERT_FF)), _full((EXPERT_FF, D_MODEL)),
                  _full((D_MODEL, D_MODEL)), _full((PLE_DIM, D_MODEL)),
                  _full((1, D_MODEL)), _full((1, D_MODEL))],
        out_specs=row(D_MODEL),
        out_shape=jax.ShapeDtypeStruct((n, D_MODEL), F32),
        compiler_params=_params(("parallel",)),
        name="moe_out",
    )(x1, top_gates, p, yg_rows, w["ws13"], w["ws2"], w["w_ple_gate"], w["w_ple"], w["ln2_g"], w["ln2_b"])


def _moe_sorted(x1, x1_tiles, top_idx, top_gates, counts, p, w):
    n = x1.shape[0]
    n_blocks = n * TOP_K // MOE_BLOCK + N_EXPERTS
    n_rows = n_blocks * MOE_BLOCK
    cnt = counts.reshape(N_EXPERTS).astype(jnp.int32)
    padded = (cnt + MOE_BLOCK - 1) // MOE_BLOCK * MOE_BLOCK
    pend = jnp.cumsum(padded)
    pstart = (pend - padded).astype(F32).reshape(1, N_EXPERTS)
    block_start = jnp.arange(n_blocks, dtype=jnp.int32) * MOE_BLOCK
    block_expert = jnp.minimum(jnp.sum((pend[None, :] <= block_start[:, None]).astype(jnp.int32), axis=1),
                               N_EXPERTS - 1)
    n_used = (pend[-1:] // MOE_BLOCK).astype(jnp.int32)
    slots = _route(top_idx, pstart, 512)[:, :TOP_K]
    slots = jnp.transpose(slots.reshape(n // SC_WINDOW, SC_WINDOW, TOP_K), (0, 2, 1))
    xs = _sc_dispatch(x1_tiles.reshape(n, ROW_TILE, LANES), slots, n_rows)
    ys = _expert_ffn(xs.reshape(n_rows * ROW_TILE, LANES), block_expert, n_used, w)
    yg = _sc_combine(ys.reshape(n_rows, ROW_TILE, LANES), slots, n)
    return _moe_out(x1, top_gates, p, yg.reshape(TOP_K, n * ROW_TILE, LANES), w, 256)


def _kv_rows(k, v, batch, seq, keep, g):
    cols = slice(g * GROUP_WIDTH, (g + 1) * GROUP_WIDTH)
    shape = (batch, keep, HEADS_PER_GROUP, HEAD_DIM)
    k_g = k.reshape(batch, seq, ATTN_WIDTH)[:, seq - keep:, cols].reshape(shape)
    v_g = v.reshape(batch, seq, ATTN_WIDTH)[:, seq - keep:, cols].reshape(shape)
    return jnp.stack([k_g, v_g], axis=2)


def _layer_prompt(x, p, w, ssm):
    batch, seq, _ = x.shape
    n = batch * seq
    x2 = x.reshape(n, D_MODEL)
    tabs = _rope_tables(jnp.arange(seq, dtype=jnp.int32))
    q, k, v, u = _in_proj(x2, w["w_in"], tabs, seq, 512)
    attn_o, attn_lse = _attn_prompt(q, k, v, batch, seq)
    zeros = jnp.zeros((batch, SSM_LANES), F32)
    y_tb, h_re, h_im = _s5_scan(u.reshape(seq * batch, SSM_WIDTH), ssm, zeros, zeros, batch, 128)
    x1, x1_tiles, _, top_idx, top_gates, counts = _post_mixer(
        x2, attn_o, attn_lse, y_tb.reshape(seq, batch * SSM_WIDTH), w, seq, 512)
    y = _moe_sorted(x1, x1_tiles, top_idx, top_gates, counts, p.reshape(n, PLE_DIM), w)
    kv = [_kv_rows(k, v, batch, seq, min(win, seq), g) for g, (win, _) in enumerate(DILATION_GROUPS)]
    h_last = jnp.stack([h_re, h_im], axis=-1).reshape(batch, SSM_GROUPS, SSM_STATE, 2)
    return y.reshape(batch, seq, D_MODEL), kv, h_last


def _layer_sample(x, p, caches, state, w, ssm):
    batch, seq, _ = x.shape
    assert seq == 1
    x2 = x.reshape(batch, D_MODEL)
    tabs = _rope_tables(jnp.full((batch,), PAST_LEN, dtype=jnp.int32))
    q, k, v, u = _in_proj(x2, w["w_in"], tabs, batch, batch)
    attn_o, attn_lse = _attn_sample(q, k, v, caches, 2)
    h0 = state.reshape(batch, SSM_LANES, 2)
    y_tb, h_re, h_im = _s5_scan(u, ssm, h0[..., 0], h0[..., 1], batch, 1)
    x1, _, gates, _, _, _ = _post_mixer(x2, attn_o, attn_lse, y_tb, w, batch, batch)
    y = _moe_ffn(x1, gates, p.reshape(batch, PLE_DIM), w, batch)
    kv = [_kv_rows(k, v, batch, 1, 1, g) for g in range(len(DILATION_GROUPS))]
    h_last = jnp.stack([h_re, h_im], axis=-1).reshape(batch, SSM_GROUPS, SSM_STATE, 2)
    return y.reshape(batch, 1, D_MODEL), kv, h_last


def _hi_lo(t):
    hi = t.astype(BF16)
    return jnp.concatenate([hi, (t - hi.astype(F32)).astype(BF16)], axis=1)


def kernel(x_prompt, x_sample, cache_kv_w128, cache_kv_w512, cache_kv_w2048, state_ssm, p_prompt, p_sample,
           w_in, a_re, a_im, log_dt, b_re, b_im, c_re, c_im, d_skip, w_glu, b_glu, w_attn_br, w_ssm_br,
           w_gate, b_gate, w_out, ln1_g, ln1_b, w_router, router_bias, w1, w3, w2, ws1, ws3, ws2,
           w_ple_gate, w_ple, ln2_g, ln2_b):
    assert w_in.shape[0] == DEPTH == 1
    l = 0
    row = lambda t: t[l].reshape(1, -1)
    w = {
        "w_in": w_in[l].astype(BF16),
        "w_glu": w_glu[l].astype(BF16), "b_glu": row(b_glu),
        "w_gate": w_gate[l].astype(BF16), "b_gate": row(b_gate),
        "w_attn_br": w_attn_br[l].astype(BF16), "w_ssm_br": w_ssm_br[l].astype(BF16),
        "w_out": w_out[l].astype(BF16), "ln1_g": row(ln1_g), "ln1_b": row(ln1_b),
        "w_router": _hi_lo(w_router[l]), "router_bias": row(router_bias),
        "w13": jnp.concatenate([w1[l], w3[l]], axis=-1).astype(BF16), "w2": w2[l].astype(BF16),
        "ws13": jnp.concatenate([ws1[l], ws3[l]], axis=-1).astype(BF16), "ws2": ws2[l].astype(BF16),
        "w_ple_gate": w_ple_gate[l].astype(BF16), "w_ple": w_ple[l].astype(BF16),
        "ln2_g": row(ln2_g), "ln2_b": row(ln2_b),
    }
    ssm = _s5_params(a_re[l], a_im[l], log_dt[l], b_re[l], b_im[l], c_re[l], c_im[l], d_skip[l])
    yp, kv_p, h_p = _layer_prompt(x_prompt, p_prompt[l], w, ssm)
    caches = (cache_kv_w128[l], cache_kv_w512[l], cache_kv_w2048[l])
    ys, kv_s, h_s = _layer_sample(x_sample, p_sample[l], caches, state_ssm[l], w, ssm)
    return (yp, ys, kv_p[0][None], kv_s[0][None], kv_p[1][None], kv_s[1][None],
            kv_p[2][None], kv_s[2][None], h_p[None], h_s[None])
```

```python
import functools
import math

import jax
import jax.numpy as jnp
from jax import lax
from jax.experimental import pallas as pl
from jax.experimental.pallas import tpu as pltpu
from jax.experimental.pallas import tpu_sc as plsc

F32 = jnp.float32
BF16 = jnp.bfloat16

D_MODEL = 1024
HEAD_DIM = 64
HEADS_PER_GROUP = 4
DILATION_GROUPS = ((128, 1), (512, 4), (2048, 16))
N_BACK = 128
GROUP_WIDTH = HEADS_PER_GROUP * HEAD_DIM
ATTN_WIDTH = 3 * GROUP_WIDTH
ROPE_THETA = 10000.0
SSM_WIDTH = 256
SSM_GROUP = 16
SSM_GROUPS = 16
SSM_STATE = 64
SSM_LANES = SSM_GROUPS * SSM_STATE
IN_WIDTH = 3 * ATTN_WIDTH + SSM_WIDTH
N_EXPERTS = 64
TOP_K = 8
EXPERT_FF = 256
ROUTED_SCALE = 2.5
PLE_DIM = 256
DEPTH = 1
PAST_LEN = 8192
DN_ALPHA = (2.0 * DEPTH) ** 0.25
LN_EPS = 1e-5

LANES = 128
ROW_TILE = D_MODEL // LANES // 2
SC_CORES = 2
SC_SUBCORES = 16
SC_WINDOW = 64
MOE_BLOCK = 1024
POST_MIXER_SUBTILES = 2
ATTN_CHUNK = 2048
VMEM_LIMIT = 56 * 1024 * 1024


def _params(semantics):
    return pltpu.CompilerParams(dimension_semantics=semantics, vmem_limit_bytes=VMEM_LIMIT)


def _full(shape):
    return pl.BlockSpec(shape, lambda *_: (0,) * len(shape))


def _in_proj_kernel(x_ref, w_ref, cos_ref, sina_ref, sinb_ref, q_ref, k_ref, v_ref, u_ref):
    xb = x_ref[...].astype(BF16)
    cos = cos_ref[...]
    sin_a = sina_ref[...]
    sin_b = sinb_ref[...]

    def rope_store(col0, out_ref, scale):
        t = jnp.dot(xb, w_ref[:, col0:col0 + ATTN_WIDTH], preferred_element_type=F32)
        for c in range(ATTN_WIDTH // LANES):
            xc = t[:, c * LANES:(c + 1) * LANES]
            r = xc * cos + pltpu.roll(xc, LANES - 32, 1) * sin_a + pltpu.roll(xc, 32, 1) * sin_b
            out_ref[:, c * LANES:(c + 1) * LANES] = r * scale if scale != 1.0 else r

    rope_store(0, q_ref, HEAD_DIM ** -0.5)
    rope_store(ATTN_WIDTH, k_ref, 1.0)
    v_ref[...] = jnp.dot(xb, w_ref[:, 2 * ATTN_WIDTH:3 * ATTN_WIDTH], preferred_element_type=F32)
    u_ref[...] = jnp.dot(xb, w_ref[:, 3 * ATTN_WIDTH:], preferred_element_type=F32)


def _in_proj(x, w_in_bf, rope_tabs, rows_per_seq, tm):
    n = x.shape[0]
    tiles_per_seq = rows_per_seq // tm
    n_seq = n // rows_per_seq
    tab_tiles = rope_tabs[0].shape[0] // tm
    tab_spec = pl.BlockSpec((tm, LANES), lambda i: (i % tab_tiles, 0))
    row_spec = pl.BlockSpec((tm, ATTN_WIDTH), lambda i: (i, 0))
    return pl.pallas_call(
        _in_proj_kernel,
        grid=(n // tm,),
        in_specs=[pl.BlockSpec((tm, D_MODEL), lambda i: (i, 0)), _full((D_MODEL, IN_WIDTH)),
                  tab_spec, tab_spec, tab_spec],
        out_specs=[row_spec, row_spec, row_spec,
                   pl.BlockSpec((tm, SSM_WIDTH), lambda i: (i % tiles_per_seq, i // tiles_per_seq))],
        out_shape=[jax.ShapeDtypeStruct((n, ATTN_WIDTH), F32)] * 3
        + [jax.ShapeDtypeStruct((rows_per_seq, n_seq * SSM_WIDTH), F32)],
        compiler_params=_params(("parallel",)),
        name="in_proj",
    )(x, w_in_bf, *rope_tabs)


def _rope_tables(pos):
    half = HEAD_DIM // 2
    inv = ROPE_THETA ** (-jnp.arange(half, dtype=F32) / half)
    ang = pos.astype(F32)[:, None] * inv[None, :]
    cos = jnp.tile(jnp.cos(ang), (1, LANES // half))
    sin = jnp.tile(jnp.sin(ang), (1, LANES // half))
    first_half = (jnp.arange(LANES) % HEAD_DIM) < half
    sin_a = jnp.where(first_half[None, :], -sin, 0.0)
    sin_b = jnp.where(first_half[None, :], 0.0, sin)
    return cos, sin_a, sin_b


def _band_attention(q, k, v, mask):
    head_of_lane = lax.broadcasted_iota(jnp.int32, (N_BACK, LANES), 1) // HEAD_DIM
    kb = k.astype(BF16)
    vb = v.astype(BF16)
    o = lse = None
    for h in range(LANES // HEAD_DIM):
        qh = jnp.where(head_of_lane == h, q, 0.0).astype(BF16)
        logits = lax.dot_general(qh, kb, (((1,), (1,)), ((), ())), preferred_element_type=F32) + mask
        m = jnp.max(logits, axis=1, keepdims=True)
        p = jnp.exp(logits - m)
        l = jnp.sum(p, axis=1, keepdims=True)
        o_h = jnp.dot(p.astype(BF16), vb, preferred_element_type=F32) * (1.0 / l)
        lse_h = jnp.broadcast_to(m + jnp.log(l), (N_BACK, LANES))
        o = o_h if o is None else jnp.where(head_of_lane == h, o_h, o)
        lse = lse_h if lse is None else jnp.where(head_of_lane == h, lse_h, lse)
    return o, lse


def _attn_prompt_kernel(q_ref, kp_ref, kc_ref, vp_ref, vc_ref, o_ref, lse_ref):
    c = pl.program_id(1)
    g = pl.program_id(3)
    ch = ATTN_CHUNK
    qi = lax.broadcasted_iota(jnp.int32, (N_BACK, 2 * N_BACK), 0)
    kj = lax.broadcasted_iota(jnp.int32, (N_BACK, 2 * N_BACK), 1)
    dist = qi + N_BACK - kj
    band = jnp.where(dist >= 0, jnp.where(dist <= N_BACK, 0.0, -jnp.inf), -jnp.inf)
    band_first = jnp.where(kj >= N_BACK, band, -jnp.inf)

    def group_body(d):
        span = N_BACK * d
        n_sub = ch // N_BACK

        def rows(start, size):
            return pl.ds(start, size) if d == 1 else pl.ds(start, size, stride=d)

        def store(q0, o, lse):
            o_ref[rows(q0, N_BACK), :] = o
            lse_ref[rows(q0, N_BACK), :] = lse

        def head_block(r, carry):
            k = jnp.concatenate([kp_ref[rows(ch - span + r, N_BACK), :], kc_ref[rows(r, N_BACK), :]], axis=0)
            v = jnp.concatenate([vp_ref[rows(ch - span + r, N_BACK), :], vc_ref[rows(r, N_BACK), :]], axis=0)
            mask = jnp.where(c == 0, band_first, band)
            store(r, *_band_attention(q_ref[rows(r, N_BACK), :], k, v, mask))
            return carry

        def inner_block(idx, carry):
            s = idx // d
            r = idx % d
            k0 = (s - 1) * span + r
            store(s * span + r, *_band_attention(q_ref[rows(s * span + r, N_BACK), :],
                                                 kc_ref[rows(k0, 2 * N_BACK), :],
                                                 vc_ref[rows(k0, 2 * N_BACK), :], band))
            return carry

        lax.fori_loop(0, d, head_block, 0, unroll=min(d, 2))
        if n_sub > d:
            lax.fori_loop(d, n_sub, inner_block, 0, unroll=2 if (n_sub - d) % 2 == 0 else 3)

    for gi, (_, d) in enumerate(DILATION_GROUPS):
        pl.when(g == gi)(functools.partial(group_body, d))


def _attn_prompt(q, k, v, batch, seq):
    ch = ATTN_CHUNK
    cps = seq // ch
    n = batch * seq
    pairs = GROUP_WIDTH // LANES
    cur = lambda b, c, hp, g: (b * cps + c, g * pairs + hp)
    prev = lambda b, c, hp, g: (b * cps + jnp.maximum(c - 1, 0), g * pairs + hp)
    blk = (ch, LANES)
    return pl.pallas_call(
        _attn_prompt_kernel,
        grid=(batch, cps, pairs, len(DILATION_GROUPS)),
        in_specs=[pl.BlockSpec(blk, cur), pl.BlockSpec(blk, prev), pl.BlockSpec(blk, cur),
                  pl.BlockSpec(blk, prev), pl.BlockSpec(blk, cur)],
        out_specs=[pl.BlockSpec(blk, cur), pl.BlockSpec(blk, cur)],
        out_shape=[jax.ShapeDtypeStruct((n, ATTN_WIDTH), F32)] * 2,
        compiler_params=_params(("parallel", "parallel", "parallel", "parallel")),
        name="attn_prompt",
    )(q, k, k, v, v)


def _attn_sample_kernel(q_ref, k_ref, v_ref, c0_ref, c1_ref, c2_ref, o_ref, lse_ref):
    bt = q_ref.shape[0]
    for b in range(bt):
        for g, (c_ref, (win, d)) in enumerate(zip((c0_ref, c1_ref, c2_ref), DILATION_GROUPS)):
            pos = lax.broadcasted_iota(jnp.int32, (1, win), 1)
            off_stride = (pos % d) != 0
            j0 = g * HEADS_PER_GROUP
            heads = range(HEADS_PER_GROUP)
            qs = [q_ref[b, :, j0 + h:j0 + h + 1] for h in heads]
            s_c = jnp.concatenate([jnp.sum(c_ref[b, 0, h] * qs[h], axis=0, keepdims=True) for h in heads],
                                  axis=0)
            s_c = jnp.where(off_stride, -jnp.inf, s_c)
            s_new = jnp.concatenate([jnp.sum(k_ref[b, :, j0 + h:j0 + h + 1] * qs[h], axis=0, keepdims=True)
                                     for h in heads], axis=0)
            m = jnp.maximum(jnp.max(s_c, axis=1, keepdims=True), s_new)
            p_c = jnp.exp(s_c - m)
            p_new = jnp.exp(s_new - m)
            l = jnp.sum(p_c, axis=1, keepdims=True) + p_new
            inv_l = 1.0 / l
            lse_ref[b, j0:j0 + HEADS_PER_GROUP, :] = m + jnp.log(l)
            for h in heads:
                num = (jnp.sum(c_ref[b, 1, h] * p_c[h:h + 1, :], axis=1, keepdims=True)
                       + p_new[h:h + 1, :] * v_ref[b, :, j0 + h:j0 + h + 1])
                o_ref[b, :, j0 + h:j0 + h + 1] = num * inv_l[h:h + 1, :]


def _attn_sample(q, k, v, caches, bt):
    b = q.shape[0]
    n_heads = ATTN_WIDTH // HEAD_DIM
    views, specs = [], []
    for cache, (win, d) in zip(caches, DILATION_GROUPS):
        assert cache.shape[1] == win == N_BACK * d
        views.append(jnp.transpose(cache, (0, 2, 3, 4, 1)))
        specs.append(pl.BlockSpec((bt, 2, HEADS_PER_GROUP, HEAD_DIM, win), lambda i: (i, 0, 0, 0, 0)))
    col_spec = pl.BlockSpec((bt, HEAD_DIM, n_heads), lambda i: (i, 0, 0))
    lse_spec = pl.BlockSpec((bt, n_heads, 1), lambda i: (i, 0, 0))
    cols = lambda t: jnp.transpose(t.reshape(b, n_heads, HEAD_DIM), (0, 2, 1))
    o, lse = pl.pallas_call(
        _attn_sample_kernel,
        grid=(b // bt,),
        in_specs=[col_spec, col_spec, col_spec] + specs,
        out_specs=[col_spec, lse_spec],
        out_shape=[jax.ShapeDtypeStruct((b, HEAD_DIM, n_heads), F32),
                   jax.ShapeDtypeStruct((b, n_heads, 1), F32)],
        compiler_params=_params(("parallel",)),
        name="attn_sample",
    )(cols(q), cols(k), cols(v), *views)
    o = jnp.transpose(o, (0, 2, 1)).reshape(b, ATTN_WIDTH)
    lse = jnp.broadcast_to(lse, (b, n_heads, HEAD_DIM)).reshape(b, ATTN_WIDTH)
    return o, lse


def _s5_scan_kernel(u_ref, bmat_ref, cmat_ref, are_ref, aim_ref, d_ref, h0re_ref, h0im_ref,
                    y_ref, hre_ref, him_ref, hist_sc, *, bg, steps):
    t_chunk = pl.program_id(0)

    @pl.when(t_chunk == 0)
    def _():
        hre_ref[...] = h0re_ref[...]
        him_ref[...] = h0im_ref[...]

    u = u_ref[...]
    hist_sc[...] = jnp.dot(u.astype(BF16), bmat_ref[...], preferred_element_type=F32)
    a_re = jnp.broadcast_to(are_ref[...], (bg, SSM_LANES))
    a_im = jnp.broadcast_to(aim_ref[...], (bg, SSM_LANES))

    def step(t, carry):
        h_re, h_im = carry
        rows = pl.ds(pl.multiple_of(t * bg, bg), bg)
        n_re = a_re * h_re - a_im * h_im + hist_sc[rows, 0:SSM_LANES]
        n_im = a_re * h_im + a_im * h_re + hist_sc[rows, SSM_LANES:2 * SSM_LANES]
        hist_sc[rows, 0:SSM_LANES] = n_re
        hist_sc[rows, SSM_LANES:2 * SSM_LANES] = n_im
        return n_re, n_im

    h_re, h_im = lax.fori_loop(0, steps, step, (hre_ref[...], him_ref[...]))
    hre_ref[...] = h_re
    him_ref[...] = h_im
    y_ref[...] = (jnp.dot(hist_sc[...].astype(BF16), cmat_ref[...], preferred_element_type=F32)
                  + d_ref[...] * u)


def _s5_scan(u_tb, ssm, h0_re, h0_im, bg, steps):
    rows = u_tb.shape[0]
    blk = steps * bg
    kern = functools.partial(_s5_scan_kernel, bg=bg, steps=steps)
    state_spec = _full((bg, SSM_LANES))
    return pl.pallas_call(
        kern,
        grid=(rows // blk,),
        in_specs=[pl.BlockSpec((blk, SSM_WIDTH), lambda i: (i, 0)),
                  _full((SSM_WIDTH, 2 * SSM_LANES)), _full((2 * SSM_LANES, SSM_WIDTH)),
                  _full((1, SSM_LANES)), _full((1, SSM_LANES)), _full((1, SSM_WIDTH)),
                  state_spec, state_spec],
        out_specs=[pl.BlockSpec((blk, SSM_WIDTH), lambda i: (i, 0)), state_spec, state_spec],
        out_shape=[jax.ShapeDtypeStruct((rows, SSM_WIDTH), F32),
                   jax.ShapeDtypeStruct((bg, SSM_LANES), F32), jax.ShapeDtypeStruct((bg, SSM_LANES), F32)],
        scratch_shapes=[pltpu.VMEM((blk, 2 * SSM_LANES), F32)],
        compiler_params=_params(("arbitrary",)),
        name="s5_scan",
    )(u_tb, ssm["bmat"], ssm["cmat"], ssm["a_re"], ssm["a_im"], ssm["d_skip"], h0_re, h0_im)


def _s5_params(a_re, a_im, log_dt, b_re, b_im, c_re, c_im, d_skip):
    dt = jnp.exp(log_dt)[:, None]
    mag = jnp.exp(a_re * dt)
    abar_re = mag * jnp.cos(a_im * dt)
    abar_im = mag * jnp.sin(a_im * dt)
    a2 = a_re * a_re + a_im * a_im
    nr = abar_re - 1.0
    coef_re = (nr * a_re + abar_im * a_im) / a2
    coef_im = (abar_im * a_re - nr * a_im) / a2
    bb_re = coef_re[..., None] * b_re - coef_im[..., None] * b_im
    bb_im = coef_re[..., None] * b_im + coef_im[..., None] * b_re
    eye = jnp.eye(SSM_GROUPS, dtype=F32)
    to_b = lambda t: jnp.einsum("gpc,gh->gchp", t, eye).reshape(SSM_WIDTH, SSM_LANES)
    to_c = lambda t: jnp.einsum("gcp,gh->gphc", t, eye).reshape(SSM_LANES, SSM_WIDTH)
    return {
        "bmat": jnp.concatenate([to_b(bb_re), to_b(bb_im)], axis=1).astype(BF16),
        "cmat": jnp.concatenate([to_c(c_re), -to_c(c_im)], axis=0).astype(BF16),
        "a_re": abar_re.reshape(1, SSM_LANES), "a_im": abar_im.reshape(1, SSM_LANES),
        "d_skip": d_skip.reshape(1, SSM_WIDTH),
    }


def _layer_norm(z, g, b):
    mu = jnp.mean(z, axis=-1, keepdims=True)
    zc = z - mu
    var = jnp.mean(zc * zc, axis=-1, keepdims=True)
    return zc * lax.rsqrt(var + LN_EPS) * g + b


def _merge_groups(o, lse):
    parts = [slice(g * GROUP_WIDTH, (g + 1) * GROUP_WIDTH) for g in range(len(DILATION_GROUPS))]
    top = lse[:, parts[0]]
    for cols in parts[1:]:
        top = jnp.maximum(top, lse[:, cols])
    num = den = None
    for cols in parts:
        w = jnp.exp(lse[:, cols] - top)
        num = w * o[:, cols] if num is None else num + w * o[:, cols]
        den = w if den is None else den + w
    return num / den


def _store_packed_rows(ref, x, row0=0):
    rows = x.shape[0]
    for j in range(ROW_TILE):
        lo = x[:, j * LANES:(j + 1) * LANES].astype(BF16).astype(F32)
        hi = x[:, (j + ROW_TILE) * LANES:(j + ROW_TILE + 1) * LANES].astype(BF16).astype(F32)
        word = (lax.bitcast_convert_type(lo, jnp.uint32) >> 16) | lax.bitcast_convert_type(hi, jnp.uint32)
        ref[pl.ds(row0 * ROW_TILE + j, rows, stride=ROW_TILE), :] = lax.bitcast_convert_type(word, jnp.int32)


def _load_packed_chunks(ref, rows, lead=None):
    lows, highs = [], []
    for j in range(ROW_TILE):
        idx = (pl.ds(j, rows, stride=ROW_TILE), slice(None))
        word = lax.bitcast_convert_type(ref[idx] if lead is None else ref[(lead,) + idx], jnp.uint32)
        lows.append(lax.bitcast_convert_type(word << 16, F32))
        highs.append(lax.bitcast_convert_type(word & jnp.uint32(0xFFFF0000), F32))
    return lows + highs


def _post_mixer_kernel(x_ref, ao_ref, lse_ref, y_ref, wglu_ref, bglu_ref, wgate_ref, bgate_ref, wab_ref, wsb_ref,
                       wout_ref, lng_ref, lnb_ref, wr_ref, rb_ref,
                       x1_ref, x1t_ref, gate_ref, idx_ref, topg_ref, cnt_ref):
    @pl.when(pl.program_id(0) == 0)
    def _():
        cnt_ref[...] = jnp.zeros(cnt_ref.shape, F32)

    tm = x_ref.shape[0]
    sub = tm // POST_MIXER_SUBTILES if tm % (8 * POST_MIXER_SUBTILES) == 0 else tm
    for r0 in range(0, tm, sub):
        rows = slice(r0, r0 + sub)
        x = x_ref[rows, :]
        xb = x.astype(BF16)
        s = jax.nn.gelu(y_ref[rows, :])
        s = s * jax.nn.sigmoid(jnp.dot(s.astype(BF16), wglu_ref[...], preferred_element_type=F32) + bglu_ref[...])
        gates = jax.nn.sigmoid(jnp.dot(xb, wgate_ref[...], preferred_element_type=F32) + bgate_ref[...])
        attn_o = _merge_groups(ao_ref[rows, :], lse_ref[rows, :])
        attn_br = jnp.dot(attn_o.astype(BF16), wab_ref[...], preferred_element_type=F32)
        ssm_br = jnp.dot(s.astype(BF16), wsb_ref[...], preferred_element_type=F32)
        merged = gates[:, :D_MODEL] * attn_br + gates[:, D_MODEL:] * ssm_br
        mix = jnp.dot(merged.astype(BF16), wout_ref[...], preferred_element_type=F32)
        x1 = _layer_norm(DN_ALPHA * x + mix, lng_ref[...], lnb_ref[...])
        x1_ref[rows, :] = x1
        _store_packed_rows(x1t_ref, x1, r0)

        x1_hi = x1.astype(BF16)
        x1_lo = (x1 - x1_hi.astype(F32)).astype(BF16)
        prod = jnp.dot(jnp.concatenate([x1_hi, x1_lo], axis=0), wr_ref[...], preferred_element_type=F32)
        logits = ((prod[:sub, :N_EXPERTS] + prod[:sub, N_EXPERTS:] + prod[sub:, :N_EXPERTS])
                  + prod[sub:, N_EXPERTS:])
        scores = jax.nn.sigmoid(logits)
        sel = scores + rb_ref[...]
        lane = lax.broadcasted_iota(jnp.int32, sel.shape, 1).astype(F32)
        slot_lane = lax.broadcasted_iota(jnp.int32, (sub, LANES), 1)
        chosen = jnp.zeros(sel.shape, F32)
        top_idx = jnp.zeros((sub, LANES), F32)
        top_s = jnp.zeros((sub, LANES), F32)
        for k in range(TOP_K):
            top = jnp.max(sel, axis=-1, keepdims=True)
            first = jnp.min(jnp.where(sel == top, lane, float(N_EXPERTS)), axis=-1, keepdims=True)
            hit = lane == first
            chosen = jnp.where(hit, 1.0, chosen)
            sel = jnp.where(hit, -jnp.inf, sel)
            top_idx = jnp.where(slot_lane == k, first, top_idx)
            top_s = jnp.where(slot_lane == k,
                              jnp.sum(jnp.where(hit, scores, 0.0), axis=-1, keepdims=True), top_s)
        norm = ROUTED_SCALE / jnp.sum(scores * chosen, axis=-1, keepdims=True)
        gate_ref[rows, :] = scores * chosen * norm
        idx_ref[rows, :] = top_idx
        topg_ref[rows, :] = top_s * norm
        cnt_ref[...] += jnp.sum(chosen, axis=0, keepdims=True)


def _post_mixer(x, attn_o, attn_lse, y_tb, w, rows_per_seq, tm):
    n = x.shape[0]
    tiles_per_seq = rows_per_seq // tm
    row = lambda width: pl.BlockSpec((tm, width), lambda i: (i, 0))
    return pl.pallas_call(
        _post_mixer_kernel,
        grid=(n // tm,),
        in_specs=[row(D_MODEL), row(ATTN_WIDTH), row(ATTN_WIDTH),
                  pl.BlockSpec((tm, SSM_WIDTH), lambda i: (i % tiles_per_seq, i // tiles_per_seq)),
                  _full((SSM_WIDTH, SSM_WIDTH)), _full((1, SSM_WIDTH)),
                  _full((D_MODEL, 2 * D_MODEL)), _full((1, 2 * D_MODEL)),
                  _full((GROUP_WIDTH, D_MODEL)), _full((SSM_WIDTH, D_MODEL)), _full((D_MODEL, D_MODEL)),
                  _full((1, D_MODEL)), _full((1, D_MODEL)),
                  _full((D_MODEL, 2 * N_EXPERTS)), _full((1, N_EXPERTS))],
        out_specs=[row(D_MODEL), pl.BlockSpec((tm * ROW_TILE, LANES), lambda i: (i, 0)),
                   row(N_EXPERTS), row(LANES), row(LANES), _full((1, N_EXPERTS))],
        out_shape=[jax.ShapeDtypeStruct((n, D_MODEL), F32), jax.ShapeDtypeStruct((n * ROW_TILE, LANES), jnp.int32),
                   jax.ShapeDtypeStruct((n, N_EXPERTS), F32), jax.ShapeDtypeStruct((n, LANES), F32),
                   jax.ShapeDtypeStruct((n, LANES), F32), jax.ShapeDtypeStruct((1, N_EXPERTS), F32)],
        compiler_params=_params(("arbitrary",)),
        name="post_mixer",
    )(x, attn_o, attn_lse, y_tb, w["w_glu"], w["b_glu"], w["w_gate"], w["b_gate"], w["w_attn_br"], w["w_ssm_br"],
      w["w_out"], w["ln1_g"], w["ln1_b"], w["w_router"], w["router_bias"])


def _moe_ffn_kernel(x_ref, gate_ref, p_ref, w13_ref, w2_ref, ws13_ref, ws2_ref, wpg_ref, wple_ref,
                    lng_ref, lnb_ref, o_ref, acc_sc, xb_sc):
    e = pl.program_id(1)

    def glu_ffn(xb, w13, w2, row_scale):
        h13 = jnp.dot(xb, w13, preferred_element_type=F32)
        h = jax.nn.silu(h13[:, :EXPERT_FF]) * h13[:, EXPERT_FF:]
        if row_scale is not None:
            h = h * row_scale
        return jnp.dot(h.astype(BF16), w2, preferred_element_type=F32)

    @pl.when(e == 0)
    def _():
        xb = x_ref[...].astype(BF16)
        xb_sc[...] = xb
        ple = (jax.nn.sigmoid(jnp.dot(xb, wpg_ref[...], preferred_element_type=F32))
               * jnp.dot(p_ref[...].astype(BF16), wple_ref[...], preferred_element_type=F32))
        acc_sc[...] = glu_ffn(xb, ws13_ref[...], ws2_ref[...], None) + ple

    gates = gate_ref[...]
    lane = lax.broadcasted_iota(jnp.int32, gates.shape, 1)
    g_col = jnp.sum(jnp.where(lane == e, gates, 0.0), axis=-1, keepdims=True)
    acc_sc[...] += glu_ffn(xb_sc[...], w13_ref[0], w2_ref[0], g_col)

    @pl.when(e == N_EXPERTS - 1)
    def _():
        o_ref[...] = _layer_norm(DN_ALPHA * x_ref[...] + acc_sc[...], lng_ref[...], lnb_ref[...])


def _moe_ffn(x1, gates, p, w, tm):
    n = x1.shape[0]
    row = lambda width: pl.BlockSpec((tm, width), lambda i, e: (i, 0))
    return pl.pallas_call(
        _moe_ffn_kernel,
        grid=(n // tm, N_EXPERTS),
        in_specs=[row(D_MODEL), row(N_EXPERTS), row(PLE_DIM),
                  pl.BlockSpec((1, D_MODEL, 2 * EXPERT_FF), lambda i, e: (e, 0, 0)),
                  pl.BlockSpec((1, EXPERT_FF, D_MODEL), lambda i, e: (e, 0, 0)),
                  _full((D_MODEL, 2 * EXPERT_FF)), _full((EXPERT_FF, D_MODEL)),
                  _full((D_MODEL, D_MODEL)), _full((PLE_DIM, D_MODEL)),
                  _full((1, D_MODEL)), _full((1, D_MODEL))],
        out_specs=row(D_MODEL),
        out_shape=jax.ShapeDtypeStruct((n, D_MODEL), F32),
        scratch_shapes=[pltpu.VMEM((tm, D_MODEL), F32), pltpu.VMEM((tm, D_MODEL), BF16)],
        compiler_params=_params(("parallel", "arbitrary")),
        name="moe_ffn",
    )(x1, gates, p, w["w13"], w["w2"], w["ws13"], w["ws2"], w["w_ple_gate"], w["w_ple"],
      w["ln2_g"], w["ln2_b"])


def _route_kernel(idx_ref, pstart_ref, slot_ref, base_sc):
    @pl.when(pl.program_id(0) == 0)
    def _():
        base_sc[...] = jnp.zeros(base_sc.shape, F32)

    idx = idx_ref[...]
    tm = idx.shape[0]
    lane = lax.broadcasted_iota(jnp.int32, (tm, N_EXPERTS), 1).astype(F32)
    hits = [lane == idx[:, k:k + 1] for k in range(TOP_K)]
    member = jnp.zeros((tm, N_EXPERTS), F32)
    for hit in hits:
        member = member + jnp.where(hit, 1.0, 0.0)
    r = lax.broadcasted_iota(jnp.int32, (tm, tm), 0)
    c = lax.broadcasted_iota(jnp.int32, (tm, tm), 1)
    earlier = jnp.where(c < r, 1.0, 0.0).astype(BF16)
    row = (jnp.dot(earlier, member.astype(BF16), preferred_element_type=F32)
           + base_sc[...] + pstart_ref[...])
    slot_lane = lax.broadcasted_iota(jnp.int32, (tm, LANES), 1)
    out = jnp.zeros((tm, LANES), F32)
    for k, hit in enumerate(hits):
        out = jnp.where(slot_lane == k, jnp.sum(jnp.where(hit, row, 0.0), axis=-1, keepdims=True), out)
    slot_ref[...] = out.astype(jnp.int32)
    base_sc[...] += jnp.sum(member, axis=0, keepdims=True)


def _route(top_idx, pstart, tm):
    n = top_idx.shape[0]
    return pl.pallas_call(
        _route_kernel,
        grid=(n // tm,),
        in_specs=[pl.BlockSpec((tm, LANES), lambda i: (i, 0)), _full((1, N_EXPERTS))],
        out_specs=pl.BlockSpec((tm, LANES), lambda i: (i, 0)),
        out_shape=jax.ShapeDtypeStruct((n, LANES), jnp.int32),
        scratch_shapes=[pltpu.VMEM((1, N_EXPERTS), F32)],
        compiler_params=_params(("arbitrary",)),
        name="route",
    )(top_idx, pstart)


def _sc_mesh():
    return plsc.VectorSubcoreMesh(core_axis_name="c", subcore_axis_name="s",
                                  num_cores=SC_CORES, num_subcores=SC_SUBCORES)


def _sc_dispatch(x_tiles, slots, n_rows):
    n = x_tiles.shape[0]
    wins_per_worker = n // SC_WINDOW // (SC_CORES * SC_SUBCORES)

    def body(x_hbm, slot_hbm, xs_hbm, idx_v, rows_v, sem):
        wid = lax.axis_index("s") * SC_CORES + lax.axis_index("c")

        @pl.loop(0, wins_per_worker)
        def _(i):
            win = wid * wins_per_worker + i
            pltpu.sync_copy(slot_hbm.at[win], idx_v)
            pltpu.sync_copy(x_hbm.at[pl.ds(win * SC_WINDOW, SC_WINDOW)], rows_v)
            copies = [pltpu.async_copy(rows_v, xs_hbm.at[idx_v.at[k]], sem) for k in range(TOP_K)]
            for copy in copies:
                copy.wait()

    return pl.kernel(
        body, out_type=jax.ShapeDtypeStruct((n_rows, ROW_TILE, LANES), jnp.int32), mesh=_sc_mesh(),
        scratch_types=[pltpu.VMEM((TOP_K, SC_WINDOW), jnp.int32),
                       pltpu.VMEM((SC_WINDOW, ROW_TILE, LANES), jnp.int32),
                       pltpu.SemaphoreType.DMA],
        name="sc_dispatch",
    )(x_tiles, slots)


def _sc_combine(y_tiles, slots, n):
    wins_per_worker = n // SC_WINDOW // (SC_CORES * SC_SUBCORES)

    def body(ys_hbm, slot_hbm, yg_hbm, idx_v, rows_a, rows_b, gather_sems, write_sems):
        wid = lax.axis_index("s") * SC_CORES + lax.axis_index("c")
        bufs = (rows_a, rows_b)

        @pl.loop(0, wins_per_worker)
        def _(i):
            win = wid * wins_per_worker + i
            pltpu.sync_copy(slot_hbm.at[win], idx_v)

            def gather(k):
                return pltpu.async_copy(ys_hbm.at[idx_v.at[k]], bufs[k % 2], gather_sems.at[k % 2])

            def write(k):
                return pltpu.async_copy(bufs[k % 2], yg_hbm.at[k, pl.ds(win * SC_WINDOW, SC_WINDOW)],
                                        write_sems.at[k % 2])

            gathers = {0: gather(0)}
            writes = {}
            for k in range(TOP_K):
                gathers[k].wait()
                if k + 1 < TOP_K:
                    if k >= 1:
                        writes[k - 1].wait()
                    gathers[k + 1] = gather(k + 1)
                writes[k] = write(k)
            writes[TOP_K - 2].wait()
            writes[TOP_K - 1].wait()

    return pl.kernel(
        body, out_type=jax.ShapeDtypeStruct((TOP_K, n, ROW_TILE, LANES), jnp.int32), mesh=_sc_mesh(),
        scratch_types=[pltpu.VMEM((TOP_K, SC_WINDOW), jnp.int32),
                       pltpu.VMEM((SC_WINDOW, ROW_TILE, LANES), jnp.int32),
                       pltpu.VMEM((SC_WINDOW, ROW_TILE, LANES), jnp.int32),
                       pltpu.SemaphoreType.DMA((2,)), pltpu.SemaphoreType.DMA((2,))],
        name="sc_combine",
    )(y_tiles, slots)


def _expert_ffn_kernel(bexp_ref, nused_ref, xs_ref, w13_ref, w2_ref, ys_ref):
    del bexp_ref

    @pl.when(pl.program_id(0) < nused_ref[0])
    def _():
        tb = MOE_BLOCK
        x = jnp.concatenate(_load_packed_chunks(xs_ref, tb), axis=1)
        h13 = jnp.dot(x.astype(BF16), w13_ref[0], preferred_element_type=F32)
        h = jax.nn.silu(h13[:, :EXPERT_FF]) * h13[:, EXPERT_FF:]
        _store_packed_rows(ys_ref, jnp.dot(h.astype(BF16), w2_ref[0], preferred_element_type=F32))


def _expert_ffn(xs_rows, block_expert, n_used, w):
    n_blocks = block_expert.shape[0]
    blk = (MOE_BLOCK * ROW_TILE, LANES)
    return pl.pallas_call(
        _expert_ffn_kernel,
        grid_spec=pltpu.PrefetchScalarGridSpec(
            num_scalar_prefetch=2, grid=(n_blocks,),
            in_specs=[pl.BlockSpec(blk, lambda i, be, nu: (i, 0)),
                      pl.BlockSpec((1, D_MODEL, 2 * EXPERT_FF), lambda i, be, nu: (be[i], 0, 0)),
                      pl.BlockSpec((1, EXPERT_FF, D_MODEL), lambda i, be, nu: (be[i], 0, 0))],
            out_specs=pl.BlockSpec(blk, lambda i, be, nu: (i, 0))),
        out_shape=jax.ShapeDtypeStruct(xs_rows.shape, jnp.int32),
        compiler_params=_params(("parallel",)),
        name="expert_ffn",
    )(block_expert, n_used, xs_rows, w["w13"], w["w2"])


def _moe_out_kernel(x_ref, g_ref, p_ref, yg_ref, ws13_ref, ws2_ref, wpg_ref, wple_ref, lng_ref, lnb_ref, o_ref):
    x = x_ref[...]
    xb = x.astype(BF16)
    tm = x.shape[0]
    g = g_ref[...]
    parts = None
    for k in range(TOP_K):
        chunks = [g[:, k:k + 1] * c for c in _load_packed_chunks(yg_ref, tm, lead=k)]
        parts = chunks if parts is None else [a + c for a, c in zip(parts, chunks)]
    routed = jnp.concatenate(parts, axis=1)
    h13 = jnp.dot(xb, ws13_ref[...], preferred_element_type=F32)
    h = jax.nn.silu(h13[:, :EXPERT_FF]) * h13[:, EXPERT_FF:]
    shared = jnp.dot(h.astype(BF16), ws2_ref[...], preferred_element_type=F32)
    ple = (jax.nn.sigmoid(jnp.dot(xb, wpg_ref[...], preferred_element_type=F32))
           * jnp.dot(p_ref[...].astype(BF16), wple_ref[...], preferred_element_type=F32))
    o_ref[...] = _layer_norm(DN_ALPHA * x + routed + shared + ple, lng_ref[...], lnb_ref[...])


def _moe_out(x1, top_gates, p, yg_rows, w, tm):
    n = x1.shape[0]
    row = lambda width: pl.BlockSpec((tm, width), lambda i: (i, 0))
    return pl.pallas_call(
        _moe_out_kernel,
        grid=(n // tm,),
        in_specs=[row(D_MODEL), row(LANES), row(PLE_DIM),
                  pl.BlockSpec((TOP_K, tm * ROW_TILE, LANES), lambda i: (0, i, 0)),
                  _full((D_MODEL, 2 * EXPERT_FF)), _full((EXPERT_FF, D_MODEL)),
                  _full((D_MODEL, D_MODEL)), _full((PLE_DIM, D_MODEL)),
                  _full((1, D_MODEL)), _full((1, D_MODEL))],
        out_specs=row(D_MODEL),
        out_shape=jax.ShapeDtypeStruct((n, D_MODEL), F32),
        compiler_params=_params(("parallel",)),
        name="moe_out",
    )(x1, top_gates, p, yg_rows, w["ws13"], w["ws2"], w["w_ple_gate"], w["w_ple"], w["ln2_g"], w["ln2_b"])


def _moe_sorted(x1, x1_tiles, top_idx, top_gates, counts, p, w):
    n = x1.shape[0]
    n_blocks = n * TOP_K // MOE_BLOCK + N_EXPERTS
    n_rows = n_blocks * MOE_BLOCK
    cnt = counts.reshape(N_EXPERTS).astype(jnp.int32)
    padded = (cnt + MOE_BLOCK - 1) // MOE_BLOCK * MOE_BLOCK
    pend = jnp.cumsum(padded)
    pstart = (pend - padded).astype(F32).reshape(1, N_EXPERTS)
    block_start = jnp.arange(n_blocks, dtype=jnp.int32) * MOE_BLOCK
    block_expert = jnp.minimum(jnp.sum((pend[None, :] <= block_start[:, None]).astype(jnp.int32), axis=1),
                               N_EXPERTS - 1)
    n_used = (pend[-1:] // MOE_BLOCK).astype(jnp.int32)
    slots = _route(top_idx, pstart, 512)[:, :TOP_K]
    slots = jnp.transpose(slots.reshape(n // SC_WINDOW, SC_WINDOW, TOP_K), (0, 2, 1))
    xs = _sc_dispatch(x1_tiles.reshape(n, ROW_TILE, LANES), slots, n_rows)
    ys = _expert_ffn(xs.reshape(n_rows * ROW_TILE, LANES), block_expert, n_used, w)
    yg = _sc_combine(ys.reshape(n_rows, ROW_TILE, LANES), slots, n)
    return _moe_out(x1, top_gates, p, yg.reshape(TOP_K, n * ROW_TILE, LANES), w, 256)


def _kv_rows(k, v, batch, seq, keep, g):
    cols = slice(g * GROUP_WIDTH, (g + 1) * GROUP_WIDTH)
    shape = (batch, keep, HEADS_PER_GROUP, HEAD_DIM)
    k_g = k.reshape(batch, seq, ATTN_WIDTH)[:, seq - keep:, cols].reshape(shape)
    v_g = v.reshape(batch, seq, ATTN_WIDTH)[:, seq - keep:, cols].reshape(shape)
    return jnp.stack([k_g, v_g], axis=2)


def _layer_prompt(x, p, w, ssm):
    batch, seq, _ = x.shape
    n = batch * seq
    x2 = x.reshape(n, D_MODEL)
    tabs = _rope_tables(jnp.arange(seq, dtype=jnp.int32))
    q, k, v, u = _in_proj(x2, w["w_in"], tabs, seq, 512)
    attn_o, attn_lse = _attn_prompt(q, k, v, batch, seq)
    zeros = jnp.zeros((batch, SSM_LANES), F32)
    y_tb, h_re, h_im = _s5_scan(u.reshape(seq * batch, SSM_WIDTH), ssm, zeros, zeros, batch, 128)
    x1, x1_tiles, _, top_idx, top_gates, counts = _post_mixer(
        x2, attn_o, attn_lse, y_tb.reshape(seq, batch * SSM_WIDTH), w, seq, 512)
    y = _moe_sorted(x1, x1_tiles, top_idx, top_gates, counts, p.reshape(n, PLE_DIM), w)
    kv = [_kv_rows(k, v, batch, seq, min(win, seq), g) for g, (win, _) in enumerate(DILATION_GROUPS)]
    h_last = jnp.stack([h_re, h_im], axis=-1).reshape(batch, SSM_GROUPS, SSM_STATE, 2)
    return y.reshape(batch, seq, D_MODEL), kv, h_last


def _layer_sample(x, p, caches, state, w, ssm):
    batch, seq, _ = x.shape
    assert seq == 1
    x2 = x.reshape(batch, D_MODEL)
    tabs = _rope_tables(jnp.full((batch,), PAST_LEN, dtype=jnp.int32))
    q, k, v, u = _in_proj(x2, w["w_in"], tabs, batch, batch)
    attn_o, attn_lse = _attn_sample(q, k, v, caches, 2)
    h0 = state.reshape(batch, SSM_LANES, 2)
    y_tb, h_re, h_im = _s5_scan(u, ssm, h0[..., 0], h0[..., 1], batch, 1)
    x1, _, gates, _, _, _ = _post_mixer(x2, attn_o, attn_lse, y_tb, w, batch, batch)
    y = _moe_ffn(x1, gates, p.reshape(batch, PLE_DIM), w, batch)
    kv = [_kv_rows(k, v, batch, 1, 1, g) for g in range(len(DILATION_GROUPS))]
    h_last = jnp.stack([h_re, h_im], axis=-1).reshape(batch, SSM_GROUPS, SSM_STATE, 2)
    return y.reshape(batch, 1, D_MODEL), kv, h_last


def _hi_lo(t):
    hi = t.astype(BF16)
    return jnp.concatenate([hi, (t - hi.astype(F32)).astype(BF16)], axis=1)


def kernel(x_prompt, x_sample, cache_kv_w128, cache_kv_w512, cache_kv_w2048, state_ssm, p_prompt, p_sample,
           w_in, a_re, a_im, log_dt, b_re, b_im, c_re, c_im, d_skip, w_glu, b_glu, w_attn_br, w_ssm_br,
           w_gate, b_gate, w_out, ln1_g, ln1_b, w_router, router_bias, w1, w3, w2, ws1, ws3, ws2,
           w_ple_gate, w_ple, ln2_g, ln2_b):
    assert w_in.shape[0] == DEPTH == 1
    l = 0
    row = lambda t: t[l].reshape(1, -1)
    w = {
        "w_in": w_in[l].astype(BF16),
        "w_glu": w_glu[l].astype(BF16), "b_glu": row(b_glu),
        "w_gate": w_gate[l].astype(BF16), "b_gate": row(b_gate),
        "w_attn_br": w_attn_br[l].astype(BF16), "w_ssm_br": w_ssm_br[l].astype(BF16),
        "w_out": w_out[l].astype(BF16), "ln1_g": row(ln1_g), "ln1_b": row(ln1_b),
        "w_router": _hi_lo(w_router[l]), "router_bias": row(router_bias),
        "w13": jnp.concatenate([w1[l], w3[l]], axis=-1).astype(BF16), "w2": w2[l].astype(BF16),
        "ws13": jnp.concatenate([ws1[l], ws3[l]], axis=-1).astype(BF16), "ws2": ws2[l].astype(BF16),
        "w_ple_gate": w_ple_gate[l].astype(BF16), "w_ple": w_ple[l].astype(BF16),
        "ln2_g": row(ln2_g), "ln2_b": row(ln2_b),
    }
    ssm = _s5_params(a_re[l], a_im[l], log_dt[l], b_re[l], b_im[l], c_re[l], c_im[l], d_skip[l])
    yp, kv_p, h_p = _layer_prompt(x_prompt, p_prompt[l], w, ssm)
    caches = (cache_kv_w128[l], cache_kv_w512[l], cache_kv_w2048[l])
    ys, kv_s, h_s = _layer_sample(x_sample, p_sample[l], caches, state_ssm[l], w, ssm)
    return (yp, ys, kv_p[0][None], kv_s[0][None], kv_p[1][None], kv_s[1][None],
            kv_p[2][None], kv_s[2][None], h_p[None], h_s[None])
```

```python
import functools
import math

import jax
import jax.numpy as jnp
from jax import lax
from jax.experimental import pallas as pl
from jax.experimental.pallas import tpu as pltpu
from jax.experimental.pallas import tpu_sc as plsc

F32 = jnp.float32
BF16 = jnp.bfloat16

D_MODEL = 1024
HEAD_DIM = 64
HEADS_PER_GROUP = 4
DILATION_GROUPS = ((128, 1), (512, 4), (2048, 16))
N_BACK = 128
GROUP_WIDTH = HEADS_PER_GROUP * HEAD_DIM
ATTN_WIDTH = 3 * GROUP_WIDTH
ROPE_THETA = 10000.0
SSM_WIDTH = 256
SSM_GROUP = 16
SSM_GROUPS = 16
SSM_STATE = 64
SSM_LANES = SSM_GROUPS * SSM_STATE
IN_WIDTH = 3 * ATTN_WIDTH + SSM_WIDTH
N_EXPERTS = 64
TOP_K = 8
EXPERT_FF = 256
ROUTED_SCALE = 2.5
PLE_DIM = 256
DEPTH = 1
PAST_LEN = 8192
DN_ALPHA = (2.0 * DEPTH) ** 0.25
LN_EPS = 1e-5

LANES = 128
ROW_TILE = D_MODEL // LANES // 2
SC_CORES = 2
SC_SUBCORES = 16
SC_WINDOW = 64
MOE_BLOCK = 1024
POST_MIXER_SUBTILES = 2
EXPERT_FFN_SUBTILES = 1
ATTN_CHUNK = 2048
ATTN_UNROLL = 8
VMEM_LIMIT = 56 * 1024 * 1024


def _params(semantics):
    return pltpu.CompilerParams(dimension_semantics=semantics, vmem_limit_bytes=VMEM_LIMIT)


def _full(shape):
    return pl.BlockSpec(shape, lambda *_: (0,) * len(shape))


def _in_proj_kernel(x_ref, w_ref, cos_ref, sina_ref, sinb_ref, q_ref, k_ref, v_ref, u_ref):
    xb = x_ref[...].astype(BF16)
    cos = cos_ref[...]
    sin_a = sina_ref[...]
    sin_b = sinb_ref[...]

    def rope_store(col0, out_ref, scale):
        t = jnp.dot(xb, w_ref[:, col0:col0 + ATTN_WIDTH], preferred_element_type=F32)
        for c in range(ATTN_WIDTH // LANES):
            xc = t[:, c * LANES:(c + 1) * LANES]
            r = xc * cos + pltpu.roll(xc, LANES - 32, 1) * sin_a + pltpu.roll(xc, 32, 1) * sin_b
            out_ref[:, c * LANES:(c + 1) * LANES] = r * scale if scale != 1.0 else r

    rope_store(0, q_ref, HEAD_DIM ** -0.5)
    rope_store(ATTN_WIDTH, k_ref, 1.0)
    v_ref[...] = jnp.dot(xb, w_ref[:, 2 * ATTN_WIDTH:3 * ATTN_WIDTH], preferred_element_type=F32)
    u_ref[...] = jnp.dot(xb, w_ref[:, 3 * ATTN_WIDTH:], preferred_element_type=F32)


def _in_proj(x, w_in_bf, rope_tabs, rows_per_seq, tm):
    n = x.shape[0]
    tiles_per_seq = rows_per_seq // tm
    n_seq = n // rows_per_seq
    tab_tiles = rope_tabs[0].shape[0] // tm
    tab_spec = pl.BlockSpec((tm, LANES), lambda i: (i % tab_tiles, 0))
    row_spec = pl.BlockSpec((tm, ATTN_WIDTH), lambda i: (i, 0))
    return pl.pallas_call(
        _in_proj_kernel,
        grid=(n // tm,),
        in_specs=[pl.BlockSpec((tm, D_MODEL), lambda i: (i, 0)), _full((D_MODEL, IN_WIDTH)),
                  tab_spec, tab_spec, tab_spec],
        out_specs=[row_spec, row_spec, row_spec,
                   pl.BlockSpec((tm, SSM_WIDTH), lambda i: (i % tiles_per_seq, i // tiles_per_seq))],
        out_shape=[jax.ShapeDtypeStruct((n, ATTN_WIDTH), F32)] * 3
        + [jax.ShapeDtypeStruct((rows_per_seq, n_seq * SSM_WIDTH), F32)],
        compiler_params=_params(("parallel",)),
        name="in_proj",
    )(x, w_in_bf, *rope_tabs)


def _rope_tables(pos):
    half = HEAD_DIM // 2
    inv = ROPE_THETA ** (-jnp.arange(half, dtype=F32) / half)
    ang = pos.astype(F32)[:, None] * inv[None, :]
    cos = jnp.tile(jnp.cos(ang), (1, LANES // half))
    sin = jnp.tile(jnp.sin(ang), (1, LANES // half))
    first_half = (jnp.arange(LANES) % HEAD_DIM) < half
    sin_a = jnp.where(first_half[None, :], -sin, 0.0)
    sin_b = jnp.where(first_half[None, :], 0.0, sin)
    return cos, sin_a, sin_b


def _band_attention(q, k, v, mask):
    head_of_lane = lax.broadcasted_iota(jnp.int32, (N_BACK, LANES), 1) // HEAD_DIM
    kb = k.astype(BF16)
    vb = v.astype(BF16)
    o = lse = None
    for h in range(LANES // HEAD_DIM):
        qh = jnp.where(head_of_lane == h, q, 0.0).astype(BF16)
        logits = lax.dot_general(qh, kb, (((1,), (1,)), ((), ())), preferred_element_type=F32) + mask
        m = jnp.max(logits, axis=1, keepdims=True)
        p = jnp.exp(logits - m)
        l = jnp.sum(p, axis=1, keepdims=True)
        o_h = jnp.dot(p.astype(BF16), vb, preferred_element_type=F32) * (1.0 / l)
        lse_h = jnp.broadcast_to(m + jnp.log(l), (N_BACK, LANES))
        o = o_h if o is None else jnp.where(head_of_lane == h, o_h, o)
        lse = lse_h if lse is None else jnp.where(head_of_lane == h, lse_h, lse)
    return o, lse


def _attn_prompt_kernel(q_ref, kp_ref, kc_ref, vp_ref, vc_ref, o_ref, lse_ref):
    c = pl.program_id(1)
    g = pl.program_id(3)
    ch = ATTN_CHUNK
    qi = lax.broadcasted_iota(jnp.int32, (N_BACK, 2 * N_BACK), 0)
    kj = lax.broadcasted_iota(jnp.int32, (N_BACK, 2 * N_BACK), 1)
    dist = qi + N_BACK - kj
    band = jnp.where(dist >= 0, jnp.where(dist <= N_BACK, 0.0, -jnp.inf), -jnp.inf)
    band_first = jnp.where(kj >= N_BACK, band, -jnp.inf)

    def group_body(d):
        span = N_BACK * d
        n_sub = ch // N_BACK

        def rows(start, size):
            return pl.ds(start, size) if d == 1 else pl.ds(start, size, stride=d)

        def store(q0, o, lse):
            o_ref[rows(q0, N_BACK), :] = o
            lse_ref[rows(q0, N_BACK), :] = lse

        def head_block(r, carry):
            k = jnp.concatenate([kp_ref[rows(ch - span + r, N_BACK), :], kc_ref[rows(r, N_BACK), :]], axis=0)
            v = jnp.concatenate([vp_ref[rows(ch - span + r, N_BACK), :], vc_ref[rows(r, N_BACK), :]], axis=0)
            mask = jnp.where(c == 0, band_first, band)
            store(r, *_band_attention(q_ref[rows(r, N_BACK), :], k, v, mask))
            return carry

        def inner_block(idx, carry):
            s = idx // d
            r = idx % d
            k0 = (s - 1) * span + r
            store(s * span + r, *_band_attention(q_ref[rows(s * span + r, N_BACK), :],
                                                 kc_ref[rows(k0, 2 * N_BACK), :],
                                                 vc_ref[rows(k0, 2 * N_BACK), :], band))
            return carry

        lax.fori_loop(0, d, head_block, 0, unroll=min(d, ATTN_UNROLL))
        if n_sub > d:
            trips = n_sub - d
            lax.fori_loop(d, n_sub, inner_block, 0,
                          unroll=max(u for u in range(1, ATTN_UNROLL + 1) if trips % u == 0))

    for gi, (_, d) in enumerate(DILATION_GROUPS):
        pl.when(g == gi)(functools.partial(group_body, d))


def _attn_prompt(q, k, v, batch, seq):
    ch = ATTN_CHUNK
    cps = seq // ch
    n = batch * seq
    pairs = GROUP_WIDTH // LANES
    cur = lambda b, c, hp, g: (b * cps + c, g * pairs + hp)
    prev = lambda b, c, hp, g: (b * cps + jnp.maximum(c - 1, 0), g * pairs + hp)
    blk = (ch, LANES)
    return pl.pallas_call(
        _attn_prompt_kernel,
        grid=(batch, cps, pairs, len(DILATION_GROUPS)),
        in_specs=[pl.BlockSpec(blk, cur), pl.BlockSpec(blk, prev), pl.BlockSpec(blk, cur),
                  pl.BlockSpec(blk, prev), pl.BlockSpec(blk, cur)],
        out_specs=[pl.BlockSpec(blk, cur), pl.BlockSpec(blk, cur)],
        out_shape=[jax.ShapeDtypeStruct((n, ATTN_WIDTH), F32)] * 2,
        compiler_params=_params(("parallel", "parallel", "parallel", "parallel")),
        name="attn_prompt",
    )(q, k, k, v, v)


def _attn_sample_kernel(q_ref, k_ref, v_ref, c0_ref, c1_ref, c2_ref, o_ref, lse_ref):
    bt = q_ref.shape[0]
    for b in range(bt):
        for g, (c_ref, (win, d)) in enumerate(zip((c0_ref, c1_ref, c2_ref), DILATION_GROUPS)):
            pos = lax.broadcasted_iota(jnp.int32, (1, win), 1)
            off_stride = (pos % d) != 0
            j0 = g * HEADS_PER_GROUP
            heads = range(HEADS_PER_GROUP)
            qs = [q_ref[b, :, j0 + h:j0 + h + 1] for h in heads]
            s_c = jnp.concatenate([jnp.sum(c_ref[b, 0, h] * qs[h], axis=0, keepdims=True) for h in heads],
                                  axis=0)
            s_c = jnp.where(off_stride, -jnp.inf, s_c)
            s_new = jnp.concatenate([jnp.sum(k_ref[b, :, j0 + h:j0 + h + 1] * qs[h], axis=0, keepdims=True)
                                     for h in heads], axis=0)
            m = jnp.maximum(jnp.max(s_c, axis=1, keepdims=True), s_new)
            p_c = jnp.exp(s_c - m)
            p_new = jnp.exp(s_new - m)
            l = jnp.sum(p_c, axis=1, keepdims=True) + p_new
            inv_l = 1.0 / l
            lse_ref[b, j0:j0 + HEADS_PER_GROUP, :] = m + jnp.log(l)
            for h in heads:
                num = (jnp.sum(c_ref[b, 1, h] * p_c[h:h + 1, :], axis=1, keepdims=True)
                       + p_new[h:h + 1, :] * v_ref[b, :, j0 + h:j0 + h + 1])
                o_ref[b, :, j0 + h:j0 + h + 1] = num * inv_l[h:h + 1, :]


def _attn_sample(q, k, v, caches, bt):
    b = q.shape[0]
    n_heads = ATTN_WIDTH // HEAD_DIM
    views, specs = [], []
    for cache, (win, d) in zip(caches, DILATION_GROUPS):
        assert cache.shape[1] == win == N_BACK * d
        views.append(jnp.transpose(cache, (0, 2, 3, 4, 1)))
        specs.append(pl.BlockSpec((bt, 2, HEADS_PER_GROUP, HEAD_DIM, win), lambda i: (i, 0, 0, 0, 0)))
    col_spec = pl.BlockSpec((bt, HEAD_DIM, n_heads), lambda i: (i, 0, 0))
    lse_spec = pl.BlockSpec((bt, n_heads, 1), lambda i: (i, 0, 0))
    cols = lambda t: jnp.transpose(t.reshape(b, n_heads, HEAD_DIM), (0, 2, 1))
    o, lse = pl.pallas_call(
        _attn_sample_kernel,
        grid=(b // bt,),
        in_specs=[col_spec, col_spec, col_spec] + specs,
        out_specs=[col_spec, lse_spec],
        out_shape=[jax.ShapeDtypeStruct((b, HEAD_DIM, n_heads), F32),
                   jax.ShapeDtypeStruct((b, n_heads, 1), F32)],
        compiler_params=_params(("parallel",)),
        name="attn_sample",
    )(cols(q), cols(k), cols(v), *views)
    o = jnp.transpose(o, (0, 2, 1)).reshape(b, ATTN_WIDTH)
    lse = jnp.broadcast_to(lse, (b, n_heads, HEAD_DIM)).reshape(b, ATTN_WIDTH)
    return o, lse


def _s5_scan_kernel(u_ref, bmat_ref, cmat_ref, are_ref, aim_ref, d_ref, h0re_ref, h0im_ref,
                    y_ref, hre_ref, him_ref, hist_sc, *, bg, steps):
    t_chunk = pl.program_id(0)

    @pl.when(t_chunk == 0)
    def _():
        hre_ref[...] = h0re_ref[...]
        him_ref[...] = h0im_ref[...]

    u = u_ref[...]
    hist_sc[...] = jnp.dot(u.astype(BF16), bmat_ref[...], preferred_element_type=F32)
    a_re = jnp.broadcast_to(are_ref[...], (bg, SSM_LANES))
    a_im = jnp.broadcast_to(aim_ref[...], (bg, SSM_LANES))

    def step(t, carry):
        h_re, h_im = carry
        rows = pl.ds(pl.multiple_of(t * bg, bg), bg)
        n_re = a_re * h_re - a_im * h_im + hist_sc[rows, 0:SSM_LANES]
        n_im = a_re * h_im + a_im * h_re + hist_sc[rows, SSM_LANES:2 * SSM_LANES]
        hist_sc[rows, 0:SSM_LANES] = n_re
        hist_sc[rows, SSM_LANES:2 * SSM_LANES] = n_im
        return n_re, n_im

    h_re, h_im = lax.fori_loop(0, steps, step, (hre_ref[...], him_ref[...]))
    hre_ref[...] = h_re
    him_ref[...] = h_im
    y_ref[...] = (jnp.dot(hist_sc[...].astype(BF16), cmat_ref[...], preferred_element_type=F32)
                  + d_ref[...] * u)


def _s5_scan(u_tb, ssm, h0_re, h0_im, bg, steps):
    rows = u_tb.shape[0]
    blk = steps * bg
    kern = functools.partial(_s5_scan_kernel, bg=bg, steps=steps)
    state_spec = _full((bg, SSM_LANES))
    return pl.pallas_call(
        kern,
        grid=(rows // blk,),
        in_specs=[pl.BlockSpec((blk, SSM_WIDTH), lambda i: (i, 0)),
                  _full((SSM_WIDTH, 2 * SSM_LANES)), _full((2 * SSM_LANES, SSM_WIDTH)),
                  _full((1, SSM_LANES)), _full((1, SSM_LANES)), _full((1, SSM_WIDTH)),
                  state_spec, state_spec],
        out_specs=[pl.BlockSpec((blk, SSM_WIDTH), lambda i: (i, 0)), state_spec, state_spec],
        out_shape=[jax.ShapeDtypeStruct((rows, SSM_WIDTH), F32),
                   jax.ShapeDtypeStruct((bg, SSM_LANES), F32), jax.ShapeDtypeStruct((bg, SSM_LANES), F32)],
        scratch_shapes=[pltpu.VMEM((blk, 2 * SSM_LANES), F32)],
        compiler_params=_params(("arbitrary",)),
        name="s5_scan",
    )(u_tb, ssm["bmat"], ssm["cmat"], ssm["a_re"], ssm["a_im"], ssm["d_skip"], h0_re, h0_im)


def _s5_params(a_re, a_im, log_dt, b_re, b_im, c_re, c_im, d_skip):
    dt = jnp.exp(log_dt)[:, None]
    mag = jnp.exp(a_re * dt)
    abar_re = mag * jnp.cos(a_im * dt)
    abar_im = mag * jnp.sin(a_im * dt)
    a2 = a_re * a_re + a_im * a_im
    nr = abar_re - 1.0
    coef_re = (nr * a_re + abar_im * a_im) / a2
    coef_im = (abar_im * a_re - nr * a_im) / a2
    bb_re = coef_re[..., None] * b_re - coef_im[..., None] * b_im
    bb_im = coef_re[..., None] * b_im + coef_im[..., None] * b_re
    eye = jnp.eye(SSM_GROUPS, dtype=F32)
    to_b = lambda t: jnp.einsum("gpc,gh->gchp", t, eye).reshape(SSM_WIDTH, SSM_LANES)
    to_c = lambda t: jnp.einsum("gcp,gh->gphc", t, eye).reshape(SSM_LANES, SSM_WIDTH)
    return {
        "bmat": jnp.concatenate([to_b(bb_re), to_b(bb_im)], axis=1).astype(BF16),
        "cmat": jnp.concatenate([to_c(c_re), -to_c(c_im)], axis=0).astype(BF16),
        "a_re": abar_re.reshape(1, SSM_LANES), "a_im": abar_im.reshape(1, SSM_LANES),
        "d_skip": d_skip.reshape(1, SSM_WIDTH),
    }


def _layer_norm(z, g, b):
    mu = jnp.mean(z, axis=-1, keepdims=True)
    zc = z - mu
    var = jnp.mean(zc * zc, axis=-1, keepdims=True)
    return zc * lax.rsqrt(var + LN_EPS) * g + b


def _merge_groups(o, lse):
    parts = [slice(g * GROUP_WIDTH, (g + 1) * GROUP_WIDTH) for g in range(len(DILATION_GROUPS))]
    top = lse[:, parts[0]]
    for cols in parts[1:]:
        top = jnp.maximum(top, lse[:, cols])
    num = den = None
    for cols in parts:
        w = jnp.exp(lse[:, cols] - top)
        num = w * o[:, cols] if num is None else num + w * o[:, cols]
        den = w if den is None else den + w
    return num / den


def _store_packed_rows(ref, x, row0=0):
    rows = x.shape[0]
    for j in range(ROW_TILE):
        lo = x[:, j * LANES:(j + 1) * LANES].astype(BF16).astype(F32)
        hi = x[:, (j + ROW_TILE) * LANES:(j + ROW_TILE + 1) * LANES].astype(BF16).astype(F32)
        word = (lax.bitcast_convert_type(lo, jnp.uint32) >> 16) | lax.bitcast_convert_type(hi, jnp.uint32)
        ref[pl.ds(row0 * ROW_TILE + j, rows, stride=ROW_TILE), :] = lax.bitcast_convert_type(word, jnp.int32)


def _load_packed_chunks(ref, rows, lead=None, row0=0):
    lows, highs = [], []
    for j in range(ROW_TILE):
        idx = (pl.ds(row0 * ROW_TILE + j, rows, stride=ROW_TILE), slice(None))
        word = lax.bitcast_convert_type(ref[idx] if lead is None else ref[(lead,) + idx], jnp.uint32)
        lows.append(lax.bitcast_convert_type(word << 16, F32))
        highs.append(lax.bitcast_convert_type(word & jnp.uint32(0xFFFF0000), F32))
    return lows + highs


def _post_mixer_kernel(x_ref, ao_ref, lse_ref, y_ref, wglu_ref, bglu_ref, wgate_ref, bgate_ref, wab_ref, wsb_ref,
                       wout_ref, lng_ref, lnb_ref, wr_ref, rb_ref,
                       x1_ref, x1t_ref, gate_ref, idx_ref, topg_ref, cnt_ref):
    @pl.when(pl.program_id(0) == 0)
    def _():
        cnt_ref[...] = jnp.zeros(cnt_ref.shape, F32)

    tm = x_ref.shape[0]
    sub = tm // POST_MIXER_SUBTILES if tm % (8 * POST_MIXER_SUBTILES) == 0 else tm
    for r0 in range(0, tm, sub):
        rows = slice(r0, r0 + sub)
        x = x_ref[rows, :]
        xb = x.astype(BF16)
        s = jax.nn.gelu(y_ref[rows, :])
        s = s * jax.nn.sigmoid(jnp.dot(s.astype(BF16), wglu_ref[...], preferred_element_type=F32) + bglu_ref[...])
        gates = jax.nn.sigmoid(jnp.dot(xb, wgate_ref[...], preferred_element_type=F32) + bgate_ref[...])
        attn_o = _merge_groups(ao_ref[rows, :], lse_ref[rows, :])
        attn_br = jnp.dot(attn_o.astype(BF16), wab_ref[...], preferred_element_type=F32)
        ssm_br = jnp.dot(s.astype(BF16), wsb_ref[...], preferred_element_type=F32)
        merged = gates[:, :D_MODEL] * attn_br + gates[:, D_MODEL:] * ssm_br
        mix = jnp.dot(merged.astype(BF16), wout_ref[...], preferred_element_type=F32)
        x1 = _layer_norm(DN_ALPHA * x + mix, lng_ref[...], lnb_ref[...])
        x1_ref[rows, :] = x1
        _store_packed_rows(x1t_ref, x1, r0)

        x1_hi = x1.astype(BF16)
        x1_lo = (x1 - x1_hi.astype(F32)).astype(BF16)
        prod = jnp.dot(jnp.concatenate([x1_hi, x1_lo], axis=0), wr_ref[...], preferred_element_type=F32)
        logits = ((prod[:sub, :N_EXPERTS] + prod[:sub, N_EXPERTS:] + prod[sub:, :N_EXPERTS])
                  + prod[sub:, N_EXPERTS:])
        scores = jax.nn.sigmoid(logits)
        sel = scores + rb_ref[...]
        lane = lax.broadcasted_iota(jnp.int32, sel.shape, 1).astype(F32)
        slot_lane = lax.broadcasted_iota(jnp.int32, (sub, LANES), 1)
        chosen = jnp.zeros(sel.shape, F32)
        top_idx = jnp.zeros((sub, LANES), F32)
        top_s = jnp.zeros((sub, LANES), F32)
        for k in range(TOP_K):
            top = jnp.max(sel, axis=-1, keepdims=True)
            first = jnp.min(jnp.where(sel == top, lane, float(N_EXPERTS)), axis=-1, keepdims=True)
            hit = lane == first
            chosen = jnp.where(hit, 1.0, chosen)
            sel = jnp.where(hit, -jnp.inf, sel)
            top_idx = jnp.where(slot_lane == k, first, top_idx)
            top_s = jnp.where(slot_lane == k,
                              jnp.sum(jnp.where(hit, scores, 0.0), axis=-1, keepdims=True), top_s)
        norm = ROUTED_SCALE / jnp.sum(scores * chosen, axis=-1, keepdims=True)
        gate_ref[rows, :] = scores * chosen * norm
        idx_ref[rows, :] = top_idx
        topg_ref[rows, :] = top_s * norm
        cnt_ref[...] += jnp.sum(chosen, axis=0, keepdims=True)


def _post_mixer(x, attn_o, attn_lse, y_tb, w, rows_per_seq, tm):
    n = x.shape[0]
    tiles_per_seq = rows_per_seq // tm
    row = lambda width: pl.BlockSpec((tm, width), lambda i: (i, 0))
    return pl.pallas_call(
        _post_mixer_kernel,
        grid=(n // tm,),
        in_specs=[row(D_MODEL), row(ATTN_WIDTH), row(ATTN_WIDTH),
                  pl.BlockSpec((tm, SSM_WIDTH), lambda i: (i % tiles_per_seq, i // tiles_per_seq)),
                  _full((SSM_WIDTH, SSM_WIDTH)), _full((1, SSM_WIDTH)),
                  _full((D_MODEL, 2 * D_MODEL)), _full((1, 2 * D_MODEL)),
                  _full((GROUP_WIDTH, D_MODEL)), _full((SSM_WIDTH, D_MODEL)), _full((D_MODEL, D_MODEL)),
                  _full((1, D_MODEL)), _full((1, D_MODEL)),
                  _full((D_MODEL, 2 * N_EXPERTS)), _full((1, N_EXPERTS))],
        out_specs=[row(D_MODEL), pl.BlockSpec((tm * ROW_TILE, LANES), lambda i: (i, 0)),
                   row(N_EXPERTS), row(LANES), row(LANES), _full((1, N_EXPERTS))],
        out_shape=[jax.ShapeDtypeStruct((n, D_MODEL), F32), jax.ShapeDtypeStruct((n * ROW_TILE, LANES), jnp.int32),
                   jax.ShapeDtypeStruct((n, N_EXPERTS), F32), jax.ShapeDtypeStruct((n, LANES), F32),
                   jax.ShapeDtypeStruct((n, LANES), F32), jax.ShapeDtypeStruct((1, N_EXPERTS), F32)],
        compiler_params=_params(("arbitrary",)),
        name="post_mixer",
    )(x, attn_o, attn_lse, y_tb, w["w_glu"], w["b_glu"], w["w_gate"], w["b_gate"], w["w_attn_br"], w["w_ssm_br"],
      w["w_out"], w["ln1_g"], w["ln1_b"], w["w_router"], w["router_bias"])


def _moe_ffn_kernel(x_ref, gate_ref, p_ref, w13_ref, w2_ref, ws13_ref, ws2_ref, wpg_ref, wple_ref,
                    lng_ref, lnb_ref, o_ref, acc_sc, xb_sc):
    e = pl.program_id(1)

    def glu_ffn(xb, w13, w2, row_scale):
        h13 = jnp.dot(xb, w13, preferred_element_type=F32)
        h = jax.nn.silu(h13[:, :EXPERT_FF]) * h13[:, EXPERT_FF:]
        if row_scale is not None:
            h = h * row_scale
        return jnp.dot(h.astype(BF16), w2, preferred_element_type=F32)

    @pl.when(e == 0)
    def _():
        xb = x_ref[...].astype(BF16)
        xb_sc[...] = xb
        ple = (jax.nn.sigmoid(jnp.dot(xb, wpg_ref[...], preferred_element_type=F32))
               * jnp.dot(p_ref[...].astype(BF16), wple_ref[...], preferred_element_type=F32))
        acc_sc[...] = glu_ffn(xb, ws13_ref[...], ws2_ref[...], None) + ple

    gates = gate_ref[...]
    lane = lax.broadcasted_iota(jnp.int32, gates.shape, 1)
    g_col = jnp.sum(jnp.where(lane == e, gates, 0.0), axis=-1, keepdims=True)
    acc_sc[...] += glu_ffn(xb_sc[...], w13_ref[0], w2_ref[0], g_col)

    @pl.when(e == N_EXPERTS - 1)
    def _():
        o_ref[...] = _layer_norm(DN_ALPHA * x_ref[...] + acc_sc[...], lng_ref[...], lnb_ref[...])


def _moe_ffn(x1, gates, p, w, tm):
    n = x1.shape[0]
    row = lambda width: pl.BlockSpec((tm, width), lambda i, e: (i, 0))
    return pl.pallas_call(
        _moe_ffn_kernel,
        grid=(n // tm, N_EXPERTS),
        in_specs=[row(D_MODEL), row(N_EXPERTS), row(PLE_DIM),
                  pl.BlockSpec((1, D_MODEL, 2 * EXPERT_FF), lambda i, e: (e, 0, 0)),
                  pl.BlockSpec((1, EXPERT_FF, D_MODEL), lambda i, e: (e, 0, 0)),
                  _full((D_MODEL, 2 * EXPERT_FF)), _full((EXPERT_FF, D_MODEL)),
                  _full((D_MODEL, D_MODEL)), _full((PLE_DIM, D_MODEL)),
                  _full((1, D_MODEL)), _full((1, D_MODEL))],
        out_specs=row(D_MODEL),
        out_shape=jax.ShapeDtypeStruct((n, D_MODEL), F32),
        scratch_shapes=[pltpu.VMEM((tm, D_MODEL), F32), pltpu.VMEM((tm, D_MODEL), BF16)],
        compiler_params=_params(("parallel", "arbitrary")),
        name="moe_ffn",
    )(x1, gates, p, w["w13"], w["w2"], w["ws13"], w["ws2"], w["w_ple_gate"], w["w_ple"],
      w["ln2_g"], w["ln2_b"])


def _route_kernel(idx_ref, pstart_ref, slot_ref, base_sc):
    @pl.when(pl.program_id(0) == 0)
    def _():
        base_sc[...] = jnp.zeros(base_sc.shape, F32)

    idx = idx_ref[...]
    tm = idx.shape[0]
    lane = lax.broadcasted_iota(jnp.int32, (tm, N_EXPERTS), 1).astype(F32)
    hits = [lane == idx[:, k:k + 1] for k in range(TOP_K)]
    member = jnp.zeros((tm, N_EXPERTS), F32)
    for hit in hits:
        member = member + jnp.where(hit, 1.0, 0.0)
    r = lax.broadcasted_iota(jnp.int32, (tm, tm), 0)
    c = lax.broadcasted_iota(jnp.int32, (tm, tm), 1)
    earlier = jnp.where(c < r, 1.0, 0.0).astype(BF16)
    row = (jnp.dot(earlier, member.astype(BF16), preferred_element_type=F32)
           + base_sc[...] + pstart_ref[...])
    slot_lane = lax.broadcasted_iota(jnp.int32, (tm, LANES), 1)
    out = jnp.zeros((tm, LANES), F32)
    for k, hit in enumerate(hits):
        out = jnp.where(slot_lane == k, jnp.sum(jnp.where(hit, row, 0.0), axis=-1, keepdims=True), out)
    slot_ref[...] = out.astype(jnp.int32)
    base_sc[...] += jnp.sum(member, axis=0, keepdims=True)


def _route(top_idx, pstart, tm):
    n = top_idx.shape[0]
    return pl.pallas_call(
        _route_kernel,
        grid=(n // tm,),
        in_specs=[pl.BlockSpec((tm, LANES), lambda i: (i, 0)), _full((1, N_EXPERTS))],
        out_specs=pl.BlockSpec((tm, LANES), lambda i: (i, 0)),
        out_shape=jax.ShapeDtypeStruct((n, LANES), jnp.int32),
        scratch_shapes=[pltpu.VMEM((1, N_EXPERTS), F32)],
        compiler_params=_params(("arbitrary",)),
        name="route",
    )(top_idx, pstart)


def _sc_mesh():
    return plsc.VectorSubcoreMesh(core_axis_name="c", subcore_axis_name="s",
                                  num_cores=SC_CORES, num_subcores=SC_SUBCORES)


def _sc_dispatch(x_tiles, slots, n_rows):
    n = x_tiles.shape[0]
    wins_per_worker = n // SC_WINDOW // (SC_CORES * SC_SUBCORES)

    def body(x_hbm, slot_hbm, xs_hbm, idx_v, rows_v, sem):
        wid = lax.axis_index("s") * SC_CORES + lax.axis_index("c")

        @pl.loop(0, wins_per_worker)
        def _(i):
            win = wid * wins_per_worker + i
            pltpu.sync_copy(slot_hbm.at[win], idx_v)
            pltpu.sync_copy(x_hbm.at[pl.ds(win * SC_WINDOW, SC_WINDOW)], rows_v)
            copies = [pltpu.async_copy(rows_v, xs_hbm.at[idx_v.at[k]], sem) for k in range(TOP_K)]
            for copy in copies:
                copy.wait()

    return pl.kernel(
        body, out_type=jax.ShapeDtypeStruct((n_rows, ROW_TILE, LANES), jnp.int32), mesh=_sc_mesh(),
        scratch_types=[pltpu.VMEM((TOP_K, SC_WINDOW), jnp.int32),
                       pltpu.VMEM((SC_WINDOW, ROW_TILE, LANES), jnp.int32),
                       pltpu.SemaphoreType.DMA],
        name="sc_dispatch",
    )(x_tiles, slots)


def _sc_combine(y_tiles, slots, n):
    wins_per_worker = n // SC_WINDOW // (SC_CORES * SC_SUBCORES)

    def body(ys_hbm, slot_hbm, yg_hbm, idx_v, rows_a, rows_b, gather_sems, write_sems):
        wid = lax.axis_index("s") * SC_CORES + lax.axis_index("c")
        bufs = (rows_a, rows_b)

        @pl.loop(0, wins_per_worker)
        def _(i):
            win = wid * wins_per_worker + i
            pltpu.sync_copy(slot_hbm.at[win], idx_v)

            def gather(k):
                return pltpu.async_copy(ys_hbm.at[idx_v.at[k]], bufs[k % 2], gather_sems.at[k % 2])

            def write(k):
                return pltpu.async_copy(bufs[k % 2], yg_hbm.at[k, pl.ds(win * SC_WINDOW, SC_WINDOW)],
                                        write_sems.at[k % 2])

            gathers = {0: gather(0)}
            writes = {}
            for k in range(TOP_K):
                gathers[k].wait()
                if k + 1 < TOP_K:
                    if k >= 1:
                        writes[k - 1].wait()
                    gathers[k + 1] = gather(k + 1)
                writes[k] = write(k)
            writes[TOP_K - 2].wait()
            writes[TOP_K - 1].wait()

    return pl.kernel(
        body, out_type=jax.ShapeDtypeStruct((TOP_K, n, ROW_TILE, LANES), jnp.int32), mesh=_sc_mesh(),
        scratch_types=[pltpu.VMEM((TOP_K, SC_WINDOW), jnp.int32),
                       pltpu.VMEM((SC_WINDOW, ROW_TILE, LANES), jnp.int32),
                       pltpu.VMEM((SC_WINDOW, ROW_TILE, LANES), jnp.int32),
                       pltpu.SemaphoreType.DMA((2,)), pltpu.SemaphoreType.DMA((2,))],
        name="sc_combine",
    )(y_tiles, slots)


def _expert_ffn_kernel(bexp_ref, nused_ref, xs_ref, w13_ref, w2_ref, ys_ref):
    del bexp_ref

    @pl.when(pl.program_id(0) < nused_ref[0])
    def _():
        sub = MOE_BLOCK // EXPERT_FFN_SUBTILES
        for r0 in range(0, MOE_BLOCK, sub):
            x = jnp.concatenate(_load_packed_chunks(xs_ref, sub, row0=r0), axis=1)
            h13 = jnp.dot(x.astype(BF16), w13_ref[0], preferred_element_type=F32)
            h = jax.nn.silu(h13[:, :EXPERT_FF]) * h13[:, EXPERT_FF:]
            _store_packed_rows(ys_ref, jnp.dot(h.astype(BF16), w2_ref[0], preferred_element_type=F32), r0)


def _expert_ffn(xs_rows, block_expert, n_used, w):
    n_blocks = block_expert.shape[0]
    blk = (MOE_BLOCK * ROW_TILE, LANES)
    return pl.pallas_call(
        _expert_ffn_kernel,
        grid_spec=pltpu.PrefetchScalarGridSpec(
            num_scalar_prefetch=2, grid=(n_blocks,),
            in_specs=[pl.BlockSpec(blk, lambda i, be, nu: (i, 0)),
                      pl.BlockSpec((1, D_MODEL, 2 * EXPERT_FF), lambda i, be, nu: (be[i], 0, 0)),
                      pl.BlockSpec((1, EXPERT_FF, D_MODEL), lambda i, be, nu: (be[i], 0, 0))],
            out_specs=pl.BlockSpec(blk, lambda i, be, nu: (i, 0))),
        out_shape=jax.ShapeDtypeStruct(xs_rows.shape, jnp.int32),
        compiler_params=_params(("parallel",)),
        name="expert_ffn",
    )(block_expert, n_used, xs_rows, w["w13"], w["w2"])


def _moe_out_kernel(x_ref, g_ref, p_ref, yg_ref, ws13_ref, ws2_ref, wpg_ref, wple_ref, lng_ref, lnb_ref, o_ref):
    x = x_ref[...]
    xb = x.astype(BF16)
    tm = x.shape[0]
    g = g_ref[...]
    parts = None
    for k in range(TOP_K):
        chunks = [g[:, k:k + 1] * c for c in _load_packed_chunks(yg_ref, tm, lead=k)]
        parts = chunks if parts is None else [a + c for a, c in zip(parts, chunks)]
    routed = jnp.concatenate(parts, axis=1)
    h13 = jnp.dot(xb, ws13_ref[...], preferred_element_type=F32)
    h = jax.nn.silu(h13[:, :EXPERT_FF]) * h13[:, EXPERT_FF:]
    shared = jnp.dot(h.astype(BF16), ws2_ref[...], preferred_element_type=F32)
    ple = (jax.nn.sigmoid(jnp.dot(xb, wpg_ref[...], preferred_element_type=F32))
           * jnp.dot(p_ref[...].astype(BF16), wple_ref[...], preferred_element_type=F32))
    o_ref[...] = _layer_norm(DN_ALPHA * x + routed + shared + ple, lng_ref[...], lnb_ref[...])


def _moe_out(x1, top_gates, p, yg_rows, w, tm):
    n = x1.shape[0]
    row = lambda width: pl.BlockSpec((tm, width), lambda i: (i, 0))
    return pl.pallas_call(
        _moe_out_kernel,
        grid=(n // tm,),
        in_specs=[row(D_MODEL), row(LANES), row(PLE_DIM),
                  pl.BlockSpec((TOP_K, tm * ROW_TILE, LANES), lambda i: (0, i, 0)),
                  _full((D_MODEL, 2 * EXPERT_FF)), _full((EXPERT_FF, D_MODEL)),
                  _full((D_MODEL, D_MODEL)), _full((PLE_DIM, D_MODEL)),
                  _full((1, D_MODEL)), _full((1, D_MODEL))],
        out_specs=row(D_MODEL),
        out_shape=jax.ShapeDtypeStruct((n, D_MODEL), F32),
        compiler_params=_params(("parallel",)),
        name="moe_out",
    )(x1, top_gates, p, yg_rows, w["ws13"], w["ws2"], w["w_ple_gate"], w["w_ple"], w["ln2_g"], w["ln2_b"])


def _moe_sorted(x1, x1_tiles, top_idx, top_gates, counts, p, w):
    n = x1.shape[0]
    n_blocks = n * TOP_K // MOE_BLOCK + N_EXPERTS
    n_rows = n_blocks * MOE_BLOCK
    cnt = counts.reshape(N_EXPERTS).astype(jnp.int32)
    padded = (cnt + MOE_BLOCK - 1) // MOE_BLOCK * MOE_BLOCK
    pend = jnp.cumsum(padded)
    pstart = (pend - padded).astype(F32).reshape(1, N_EXPERTS)
    block_start = jnp.arange(n_blocks, dtype=jnp.int32) * MOE_BLOCK
    block_expert = jnp.minimum(jnp.sum((pend[None, :] <= block_start[:, None]).astype(jnp.int32), axis=1),
                               N_EXPERTS - 1)
    n_used = (pend[-1:] // MOE_BLOCK).astype(jnp.int32)
    slots = _route(top_idx, pstart, 512)[:, :TOP_K]
    slots = jnp.transpose(slots.reshape(n // SC_WINDOW, SC_WINDOW, TOP_K), (0, 2, 1))
    xs = _sc_dispatch(x1_tiles.reshape(n, ROW_TILE, LANES), slots, n_rows)
    ys = _expert_ffn(xs.reshape(n_rows * ROW_TILE, LANES), block_expert, n_used, w)
    yg = _sc_combine(ys.reshape(n_rows, ROW_TILE, LANES), slots, n)
    return _moe_out(x1, top_gates, p, yg.reshape(TOP_K, n * ROW_TILE, LANES), w, 512)


def _kv_rows(k, v, batch, seq, keep, g):
    cols = slice(g * GROUP_WIDTH, (g + 1) * GROUP_WIDTH)
    shape = (batch, keep, HEADS_PER_GROUP, HEAD_DIM)
    k_g = k.reshape(batch, seq, ATTN_WIDTH)[:, seq - keep:, cols].reshape(shape)
    v_g = v.reshape(batch, seq, ATTN_WIDTH)[:, seq - keep:, cols].reshape(shape)
    return jnp.stack([k_g, v_g], axis=2)


def _layer_prompt(x, p, w, ssm):
    batch, seq, _ = x.shape
    n = batch * seq
    x2 = x.reshape(n, D_MODEL)
    tabs = _rope_tables(jnp.arange(seq, dtype=jnp.int32))
    q, k, v, u = _in_proj(x2, w["w_in"], tabs, seq, 512)
    attn_o, attn_lse = _attn_prompt(q, k, v, batch, seq)
    zeros = jnp.zeros((batch, SSM_LANES), F32)
    y_tb, h_re, h_im = _s5_scan(u.reshape(seq * batch, SSM_WIDTH), ssm, zeros, zeros, batch, 128)
    x1, x1_tiles, _, top_idx, top_gates, counts = _post_mixer(
        x2, attn_o, attn_lse, y_tb.reshape(seq, batch * SSM_WIDTH), w, seq, 512)
    y = _moe_sorted(x1, x1_tiles, top_idx, top_gates, counts, p.reshape(n, PLE_DIM), w)
    kv = [_kv_rows(k, v, batch, seq, min(win, seq), g) for g, (win, _) in enumerate(DILATION_GROUPS)]
    h_last = jnp.stack([h_re, h_im], axis=-1).reshape(batch, SSM_GROUPS, SSM_STATE, 2)
    return y.reshape(batch, seq, D_MODEL), kv, h_last


def _layer_sample(x, p, caches, state, w, ssm):
    batch, seq, _ = x.shape
    assert seq == 1
    x2 = x.reshape(batch, D_MODEL)
    tabs = _rope_tables(jnp.full((batch,), PAST_LEN, dtype=jnp.int32))
    q, k, v, u = _in_proj(x2, w["w_in"], tabs, batch, batch)
    attn_o, attn_lse = _attn_sample(q, k, v, caches, 2)
    h0 = state.reshape(batch, SSM_LANES, 2)
    y_tb, h_re, h_im = _s5_scan(u, ssm, h0[..., 0], h0[..., 1], batch, 1)
    x1, _, gates, _, _, _ = _post_mixer(x2, attn_o, attn_lse, y_tb, w, batch, batch)
    y = _moe_ffn(x1, gates, p.reshape(batch, PLE_DIM), w, batch)
    kv = [_kv_rows(k, v, batch, 1, 1, g) for g in range(len(DILATION_GROUPS))]
    h_last = jnp.stack([h_re, h_im], axis=-1).reshape(batch, SSM_GROUPS, SSM_STATE, 2)
    return y.reshape(batch, 1, D_MODEL), kv, h_last


def _hi_lo(t):
    hi = t.astype(BF16)
    return jnp.concatenate([hi, (t - hi.astype(F32)).astype(BF16)], axis=1)


def kernel(x_prompt, x_sample, cache_kv_w128, cache_kv_w512, cache_kv_w2048, state_ssm, p_prompt, p_sample,
           w_in, a_re, a_im, log_dt, b_re, b_im, c_re, c_im, d_skip, w_glu, b_glu, w_attn_br, w_ssm_br,
           w_gate, b_gate, w_out, ln1_g, ln1_b, w_router, router_bias, w1, w3, w2, ws1, ws3, ws2,
           w_ple_gate, w_ple, ln2_g, ln2_b):
    assert w_in.shape[0] == DEPTH == 1
    l = 0
    row = lambda t: t[l].reshape(1, -1)
    w = {
        "w_in": w_in[l].astype(BF16),
        "w_glu": w_glu[l].astype(BF16), "b_glu": row(b_glu),
        "w_gate": w_gate[l].astype(BF16), "b_gate": row(b_gate),
        "w_attn_br": w_attn_br[l].astype(BF16), "w_ssm_br": w_ssm_br[l].astype(BF16),
        "w_out": w_out[l].astype(BF16), "ln1_g": row(ln1_g), "ln1_b": row(ln1_b),
        "w_router": _hi_lo(w_router[l]), "router_bias": row(router_bias),
        "w13": jnp.concatenate([w1[l], w3[l]], axis=-1).astype(BF16), "w2": w2[l].astype(BF16),
        "ws13": jnp.concatenate([ws1[l], ws3[l]], axis=-1).astype(BF16), "ws2": ws2[l].astype(BF16),
        "w_ple_gate": w_ple_gate[l].astype(BF16), "w_ple": w_ple[l].astype(BF16),
        "ln2_g": row(ln2_g), "ln2_b": row(ln2_b),
    }
    ssm = _s5_params(a_re[l], a_im[l], log_dt[l], b_re[l], b_im[l], c_re[l], c_im[l], d_skip[l])
    yp, kv_p, h_p = _layer_prompt(x_prompt, p_prompt[l], w, ssm)
    caches = (cache_kv_w128[l], cache_kv_w512[l], cache_kv_w2048[l])
    ys, kv_s, h_s = _layer_sample(x_sample, p_sample[l], caches, state_ssm[l], w, ssm)
    return (yp, ys, kv_p[0][None], kv_s[0][None], kv_p[1][None], kv_s[1][None],
            kv_p[2][None], kv_s[2][None], h_p[None], h_s[None])
```

```python
import functools
import math

import jax
import jax.numpy as jnp
from jax import lax
from jax.experimental import pallas as pl
from jax.experimental.pallas import tpu as pltpu
from jax.experimental.pallas import tpu_sc as plsc

F32 = jnp.float32
BF16 = jnp.bfloat16

D_MODEL = 1024
HEAD_DIM = 64
HEADS_PER_GROUP = 4
DILATION_GROUPS = ((128, 1), (512, 4), (2048, 16))
N_BACK = 128
GROUP_WIDTH = HEADS_PER_GROUP * HEAD_DIM
ATTN_WIDTH = 3 * GROUP_WIDTH
ROPE_THETA = 10000.0
SSM_WIDTH = 256
SSM_GROUP = 16
SSM_GROUPS = 16
SSM_STATE = 64
SSM_LANES = SSM_GROUPS * SSM_STATE
IN_WIDTH = 3 * ATTN_WIDTH + SSM_WIDTH
N_EXPERTS = 64
TOP_K = 8
EXPERT_FF = 256
ROUTED_SCALE = 2.5
PLE_DIM = 256
DEPTH = 1
PAST_LEN = 8192
DN_ALPHA = (2.0 * DEPTH) ** 0.25
LN_EPS = 1e-5

LANES = 128
ROW_TILE = D_MODEL // LANES // 2
SC_CORES = 2
SC_SUBCORES = 16
SC_WINDOW = 64
MOE_BLOCK = 1024
POST_MIXER_SUBTILES = 2
EXPERT_FFN_SUBTILES = 1
ATTN_CHUNK = 2048
ATTN_UNROLL = 8
VMEM_LIMIT = 56 * 1024 * 1024


def _params(semantics):
    return pltpu.CompilerParams(dimension_semantics=semantics, vmem_limit_bytes=VMEM_LIMIT)


def _full(shape):
    return pl.BlockSpec(shape, lambda *_: (0,) * len(shape))


def _in_proj_kernel(x_ref, w_ref, cos_ref, sina_ref, sinb_ref, q_ref, k_ref, v_ref, u_ref):
    xb = x_ref[...].astype(BF16)
    cos = cos_ref[...]
    sin_a = sina_ref[...]
    sin_b = sinb_ref[...]

    def rope_store(col0, out_ref, scale):
        t = jnp.dot(xb, w_ref[:, col0:col0 + ATTN_WIDTH], preferred_element_type=F32)
        for c in range(ATTN_WIDTH // LANES):
            xc = t[:, c * LANES:(c + 1) * LANES]
            r = xc * cos + pltpu.roll(xc, LANES - 32, 1) * sin_a + pltpu.roll(xc, 32, 1) * sin_b
            out_ref[:, c * LANES:(c + 1) * LANES] = r * scale if scale != 1.0 else r

    rope_store(0, q_ref, HEAD_DIM ** -0.5)
    rope_store(ATTN_WIDTH, k_ref, 1.0)
    v_ref[...] = jnp.dot(xb, w_ref[:, 2 * ATTN_WIDTH:3 * ATTN_WIDTH], preferred_element_type=F32)
    u_ref[...] = jnp.dot(xb, w_ref[:, 3 * ATTN_WIDTH:], preferred_element_type=F32)


def _in_proj(x, w_in_bf, rope_tabs, rows_per_seq, tm):
    n = x.shape[0]
    tiles_per_seq = rows_per_seq // tm
    n_seq = n // rows_per_seq
    tab_tiles = rope_tabs[0].shape[0] // tm
    tab_spec = pl.BlockSpec((tm, LANES), lambda i: (i % tab_tiles, 0))
    row_spec = pl.BlockSpec((tm, ATTN_WIDTH), lambda i: (i, 0))
    return pl.pallas_call(
        _in_proj_kernel,
        grid=(n // tm,),
        in_specs=[pl.BlockSpec((tm, D_MODEL), lambda i: (i, 0)), _full((D_MODEL, IN_WIDTH)),
                  tab_spec, tab_spec, tab_spec],
        out_specs=[row_spec, row_spec, row_spec,
                   pl.BlockSpec((tm, SSM_WIDTH), lambda i: (i % tiles_per_seq, i // tiles_per_seq))],
        out_shape=[jax.ShapeDtypeStruct((n, ATTN_WIDTH), F32)] * 3
        + [jax.ShapeDtypeStruct((rows_per_seq, n_seq * SSM_WIDTH), F32)],
        compiler_params=_params(("parallel",)),
        name="in_proj",
    )(x, w_in_bf, *rope_tabs)


def _rope_tables(pos):
    half = HEAD_DIM // 2
    inv = ROPE_THETA ** (-jnp.arange(half, dtype=F32) / half)
    ang = pos.astype(F32)[:, None] * inv[None, :]
    cos = jnp.tile(jnp.cos(ang), (1, LANES // half))
    sin = jnp.tile(jnp.sin(ang), (1, LANES // half))
    first_half = (jnp.arange(LANES) % HEAD_DIM) < half
    sin_a = jnp.where(first_half[None, :], -sin, 0.0)
    sin_b = jnp.where(first_half[None, :], 0.0, sin)
    return cos, sin_a, sin_b


def _band_attention(q, k, v, mask):
    head_of_lane = lax.broadcasted_iota(jnp.int32, (N_BACK, LANES), 1) // HEAD_DIM
    kb = k.astype(BF16)
    vb = v.astype(BF16)
    o = lse = None
    for h in range(LANES // HEAD_DIM):
        qh = jnp.where(head_of_lane == h, q, 0.0).astype(BF16)
        logits = lax.dot_general(qh, kb, (((1,), (1,)), ((), ())), preferred_element_type=F32) + mask
        m = jnp.max(logits, axis=1, keepdims=True)
        p = jnp.exp(logits - m)
        l = jnp.sum(p, axis=1, keepdims=True)
        o_h = jnp.dot(p.astype(BF16), vb, preferred_element_type=F32) * (1.0 / l)
        lse_h = jnp.broadcast_to(m + jnp.log(l), (N_BACK, LANES))
        o = o_h if o is None else jnp.where(head_of_lane == h, o_h, o)
        lse = lse_h if lse is None else jnp.where(head_of_lane == h, lse_h, lse)
    return o, lse


def _attn_prompt_kernel(q_ref, kp_ref, kc_ref, vp_ref, vc_ref, o_ref, lse_ref):
    c = pl.program_id(1)
    g = pl.program_id(3)
    ch = ATTN_CHUNK
    qi = lax.broadcasted_iota(jnp.int32, (N_BACK, 2 * N_BACK), 0)
    kj = lax.broadcasted_iota(jnp.int32, (N_BACK, 2 * N_BACK), 1)
    dist = qi + N_BACK - kj
    band = jnp.where(dist >= 0, jnp.where(dist <= N_BACK, 0.0, -jnp.inf), -jnp.inf)
    band_first = jnp.where(kj >= N_BACK, band, -jnp.inf)

    def group_body(d):
        span = N_BACK * d
        n_sub = ch // N_BACK

        def rows(start, size):
            return pl.ds(start, size) if d == 1 else pl.ds(start, size, stride=d)

        def store(q0, o, lse):
            o_ref[rows(q0, N_BACK), :] = o
            lse_ref[rows(q0, N_BACK), :] = lse

        def head_block(r, carry):
            k = jnp.concatenate([kp_ref[rows(ch - span + r, N_BACK), :], kc_ref[rows(r, N_BACK), :]], axis=0)
            v = jnp.concatenate([vp_ref[rows(ch - span + r, N_BACK), :], vc_ref[rows(r, N_BACK), :]], axis=0)
            mask = jnp.where(c == 0, band_first, band)
            store(r, *_band_attention(q_ref[rows(r, N_BACK), :], k, v, mask))
            return carry

        def inner_block(idx, carry):
            s = idx // d
            r = idx % d
            k0 = (s - 1) * span + r
            store(s * span + r, *_band_attention(q_ref[rows(s * span + r, N_BACK), :],
                                                 kc_ref[rows(k0, 2 * N_BACK), :],
                                                 vc_ref[rows(k0, 2 * N_BACK), :], band))
            return carry

        lax.fori_loop(0, d, head_block, 0, unroll=min(d, ATTN_UNROLL))
        if n_sub > d:
            trips = n_sub - d
            lax.fori_loop(d, n_sub, inner_block, 0,
                          unroll=max(u for u in range(1, ATTN_UNROLL + 1) if trips % u == 0))

    for gi, (_, d) in enumerate(DILATION_GROUPS):
        pl.when(g == gi)(functools.partial(group_body, d))


def _attn_prompt(q, k, v, batch, seq):
    ch = ATTN_CHUNK
    cps = seq // ch
    n = batch * seq
    pairs = GROUP_WIDTH // LANES
    cur = lambda b, c, hp, g: (b * cps + c, g * pairs + hp)
    prev = lambda b, c, hp, g: (b * cps + jnp.maximum(c - 1, 0), g * pairs + hp)
    blk = (ch, LANES)
    return pl.pallas_call(
        _attn_prompt_kernel,
        grid=(batch, cps, pairs, len(DILATION_GROUPS)),
        in_specs=[pl.BlockSpec(blk, cur), pl.BlockSpec(blk, prev), pl.BlockSpec(blk, cur),
                  pl.BlockSpec(blk, prev), pl.BlockSpec(blk, cur)],
        out_specs=[pl.BlockSpec(blk, cur), pl.BlockSpec(blk, cur)],
        out_shape=[jax.ShapeDtypeStruct((n, ATTN_WIDTH), F32)] * 2,
        compiler_params=_params(("parallel", "parallel", "parallel", "parallel")),
        name="attn_prompt",
    )(q, k, k, v, v)


def _attn_sample_kernel(q_ref, k_ref, v_ref, c0_ref, c1_ref, c2_ref, o_ref, lse_ref):
    bt = q_ref.shape[0]
    for b in range(bt):
        for g, (c_ref, (win, d)) in enumerate(zip((c0_ref, c1_ref, c2_ref), DILATION_GROUPS)):
            pos = lax.broadcasted_iota(jnp.int32, (1, win), 1)
            off_stride = (pos % d) != 0
            j0 = g * HEADS_PER_GROUP
            heads = range(HEADS_PER_GROUP)
            qs = [q_ref[b, :, j0 + h:j0 + h + 1] for h in heads]
            s_c = jnp.concatenate([jnp.sum(c_ref[b, 0, h] * qs[h], axis=0, keepdims=True) for h in heads],
                                  axis=0)
            s_c = jnp.where(off_stride, -jnp.inf, s_c)
            s_new = jnp.concatenate([jnp.sum(k_ref[b, :, j0 + h:j0 + h + 1] * qs[h], axis=0, keepdims=True)
                                     for h in heads], axis=0)
            m = jnp.maximum(jnp.max(s_c, axis=1, keepdims=True), s_new)
            p_c = jnp.exp(s_c - m)
            p_new = jnp.exp(s_new - m)
            l = jnp.sum(p_c, axis=1, keepdims=True) + p_new
            inv_l = 1.0 / l
            lse_ref[b, j0:j0 + HEADS_PER_GROUP, :] = m + jnp.log(l)
            for h in heads:
                num = (jnp.sum(c_ref[b, 1, h] * p_c[h:h + 1, :], axis=1, keepdims=True)
                       + p_new[h:h + 1, :] * v_ref[b, :, j0 + h:j0 + h + 1])
                o_ref[b, :, j0 + h:j0 + h + 1] = num * inv_l[h:h + 1, :]


def _attn_sample(q, k, v, caches, bt):
    b = q.shape[0]
    n_heads = ATTN_WIDTH // HEAD_DIM
    views, specs = [], []
    for cache, (win, d) in zip(caches, DILATION_GROUPS):
        assert cache.shape[1] == win == N_BACK * d
        views.append(jnp.transpose(cache, (0, 2, 3, 4, 1)))
        specs.append(pl.BlockSpec((bt, 2, HEADS_PER_GROUP, HEAD_DIM, win), lambda i: (i, 0, 0, 0, 0)))
    col_spec = pl.BlockSpec((bt, HEAD_DIM, n_heads), lambda i: (i, 0, 0))
    lse_spec = pl.BlockSpec((bt, n_heads, 1), lambda i: (i, 0, 0))
    cols = lambda t: jnp.transpose(t.reshape(b, n_heads, HEAD_DIM), (0, 2, 1))
    o, lse = pl.pallas_call(
        _attn_sample_kernel,
        grid=(b // bt,),
        in_specs=[col_spec, col_spec, col_spec] + specs,
        out_specs=[col_spec, lse_spec],
        out_shape=[jax.ShapeDtypeStruct((b, HEAD_DIM, n_heads), F32),
                   jax.ShapeDtypeStruct((b, n_heads, 1), F32)],
        compiler_params=_params(("parallel",)),
        name="attn_sample",
    )(cols(q), cols(k), cols(v), *views)
    o = jnp.transpose(o, (0, 2, 1)).reshape(b, ATTN_WIDTH)
    lse = jnp.broadcast_to(lse, (b, n_heads, HEAD_DIM)).reshape(b, ATTN_WIDTH)
    return o, lse


def _s5_scan_kernel(u_ref, bmat_ref, cmat_ref, are_ref, aim_ref, d_ref, h0re_ref, h0im_ref,
                    y_ref, hre_ref, him_ref, hist_sc, *, bg, steps):
    t_chunk = pl.program_id(0)

    @pl.when(t_chunk == 0)
    def _():
        hre_ref[...] = h0re_ref[...]
        him_ref[...] = h0im_ref[...]

    u = u_ref[...]
    hist_sc[...] = jnp.dot(u.astype(BF16), bmat_ref[...], preferred_element_type=F32)
    a_re = jnp.broadcast_to(are_ref[...], (bg, SSM_LANES))
    a_im = jnp.broadcast_to(aim_ref[...], (bg, SSM_LANES))

    def step(t, carry):
        h_re, h_im = carry
        rows = pl.ds(pl.multiple_of(t * bg, bg), bg)
        n_re = a_re * h_re - a_im * h_im + hist_sc[rows, 0:SSM_LANES]
        n_im = a_re * h_im + a_im * h_re + hist_sc[rows, SSM_LANES:2 * SSM_LANES]
        hist_sc[rows, 0:SSM_LANES] = n_re
        hist_sc[rows, SSM_LANES:2 * SSM_LANES] = n_im
        return n_re, n_im

    h_re, h_im = lax.fori_loop(0, steps, step, (hre_ref[...], him_ref[...]))
    hre_ref[...] = h_re
    him_ref[...] = h_im
    y_ref[...] = (jnp.dot(hist_sc[...].astype(BF16), cmat_ref[...], preferred_element_type=F32)
                  + d_ref[...] * u)


def _s5_scan(u_tb, ssm, h0_re, h0_im, bg, steps):
    rows = u_tb.shape[0]
    blk = steps * bg
    kern = functools.partial(_s5_scan_kernel, bg=bg, steps=steps)
    state_spec = _full((bg, SSM_LANES))
    return pl.pallas_call(
        kern,
        grid=(rows // blk,),
        in_specs=[pl.BlockSpec((blk, SSM_WIDTH), lambda i: (i, 0)),
                  _full((SSM_WIDTH, 2 * SSM_LANES)), _full((2 * SSM_LANES, SSM_WIDTH)),
                  _full((1, SSM_LANES)), _full((1, SSM_LANES)), _full((1, SSM_WIDTH)),
                  state_spec, state_spec],
        out_specs=[pl.BlockSpec((blk, SSM_WIDTH), lambda i: (i, 0)), state_spec, state_spec],
        out_shape=[jax.ShapeDtypeStruct((rows, SSM_WIDTH), F32),
                   jax.ShapeDtypeStruct((bg, SSM_LANES), F32), jax.ShapeDtypeStruct((bg, SSM_LANES), F32)],
        scratch_shapes=[pltpu.VMEM((blk, 2 * SSM_LANES), F32)],
        compiler_params=_params(("arbitrary",)),
        name="s5_scan",
    )(u_tb, ssm["bmat"], ssm["cmat"], ssm["a_re"], ssm["a_im"], ssm["d_skip"], h0_re, h0_im)


def _s5_params(a_re, a_im, log_dt, b_re, b_im, c_re, c_im, d_skip):
    dt = jnp.exp(log_dt)[:, None]
    mag = jnp.exp(a_re * dt)
    abar_re = mag * jnp.cos(a_im * dt)
    abar_im = mag * jnp.sin(a_im * dt)
    a2 = a_re * a_re + a_im * a_im
    nr = abar_re - 1.0
    coef_re = (nr * a_re + abar_im * a_im) / a2
    coef_im = (abar_im * a_re - nr * a_im) / a2
    bb_re = coef_re[..., None] * b_re - coef_im[..., None] * b_im
    bb_im = coef_re[..., None] * b_im + coef_im[..., None] * b_re
    eye = jnp.eye(SSM_GROUPS, dtype=F32)
    to_b = lambda t: jnp.einsum("gpc,gh->gchp", t, eye).reshape(SSM_WIDTH, SSM_LANES)
    to_c = lambda t: jnp.einsum("gcp,gh->gphc", t, eye).reshape(SSM_LANES, SSM_WIDTH)
    return {
        "bmat": jnp.concatenate([to_b(bb_re), to_b(bb_im)], axis=1).astype(BF16),
        "cmat": jnp.concatenate([to_c(c_re), -to_c(c_im)], axis=0).astype(BF16),
        "a_re": abar_re.reshape(1, SSM_LANES), "a_im": abar_im.reshape(1, SSM_LANES),
        "d_skip": d_skip.reshape(1, SSM_WIDTH),
    }


def _layer_norm(z, g, b):
    mu = jnp.mean(z, axis=-1, keepdims=True)
    zc = z - mu
    var = jnp.mean(zc * zc, axis=-1, keepdims=True)
    return zc * lax.rsqrt(var + LN_EPS) * g + b


def _merge_groups(o, lse):
    parts = [slice(g * GROUP_WIDTH, (g + 1) * GROUP_WIDTH) for g in range(len(DILATION_GROUPS))]
    top = lse[:, parts[0]]
    for cols in parts[1:]:
        top = jnp.maximum(top, lse[:, cols])
    num = den = None
    for cols in parts:
        w = jnp.exp(lse[:, cols] - top)
        num = w * o[:, cols] if num is None else num + w * o[:, cols]
        den = w if den is None else den + w
    return num / den


def _store_packed_rows(ref, x, row0=0):
    rows = x.shape[0]
    for j in range(ROW_TILE):
        lo = x[:, j * LANES:(j + 1) * LANES].astype(BF16).astype(F32)
        hi = x[:, (j + ROW_TILE) * LANES:(j + ROW_TILE + 1) * LANES].astype(BF16).astype(F32)
        word = (lax.bitcast_convert_type(lo, jnp.uint32) >> 16) | lax.bitcast_convert_type(hi, jnp.uint32)
        ref[pl.ds(row0 * ROW_TILE + j, rows, stride=ROW_TILE), :] = lax.bitcast_convert_type(word, jnp.int32)


def _load_packed_chunks(ref, rows, lead=None, row0=0):
    lows, highs = [], []
    for j in range(ROW_TILE):
        idx = (pl.ds(row0 * ROW_TILE + j, rows, stride=ROW_TILE), slice(None))
        word = lax.bitcast_convert_type(ref[idx] if lead is None else ref[(lead,) + idx], jnp.uint32)
        lows.append(lax.bitcast_convert_type(word << 16, F32))
        highs.append(lax.bitcast_convert_type(word & jnp.uint32(0xFFFF0000), F32))
    return lows + highs


def _post_mixer_kernel(x_ref, ao_ref, lse_ref, y_ref, wglu_ref, bglu_ref, wgate_ref, bgate_ref, wab_ref, wsb_ref,
                       wout_ref, lng_ref, lnb_ref, wr_ref, rb_ref,
                       x1_ref, x1t_ref, gate_ref, idx_ref, topg_ref, cnt_ref):
    @pl.when(pl.program_id(0) == 0)
    def _():
        cnt_ref[...] = jnp.zeros(cnt_ref.shape, F32)

    tm = x_ref.shape[0]
    sub = tm // POST_MIXER_SUBTILES if tm % (8 * POST_MIXER_SUBTILES) == 0 else tm
    for r0 in range(0, tm, sub):
        rows = slice(r0, r0 + sub)
        x = x_ref[rows, :]
        xb = x.astype(BF16)
        s = jax.nn.gelu(y_ref[rows, :])
        s = s * jax.nn.sigmoid(jnp.dot(s.astype(BF16), wglu_ref[...], preferred_element_type=F32) + bglu_ref[...])
        gates = jax.nn.sigmoid(jnp.dot(xb, wgate_ref[...], preferred_element_type=F32) + bgate_ref[...])
        attn_o = _merge_groups(ao_ref[rows, :], lse_ref[rows, :])
        attn_br = jnp.dot(attn_o.astype(BF16), wab_ref[...], preferred_element_type=F32)
        ssm_br = jnp.dot(s.astype(BF16), wsb_ref[...], preferred_element_type=F32)
        merged = gates[:, :D_MODEL] * attn_br + gates[:, D_MODEL:] * ssm_br
        mix = jnp.dot(merged.astype(BF16), wout_ref[...], preferred_element_type=F32)
        x1 = _layer_norm(DN_ALPHA * x + mix, lng_ref[...], lnb_ref[...])
        x1_ref[rows, :] = x1
        _store_packed_rows(x1t_ref, x1, r0)

        x1_hi = x1.astype(BF16)
        x1_lo = (x1 - x1_hi.astype(F32)).astype(BF16)
        prod = lax.dot_general(wr_ref[...], jnp.concatenate([x1_hi, x1_lo], axis=0),
                               (((1,), (1,)), ((), ())), preferred_element_type=F32)
        logits = ((prod[:N_EXPERTS, :sub] + prod[N_EXPERTS:, :sub] + prod[:N_EXPERTS, sub:])
                  + prod[N_EXPERTS:, sub:])
        scores = jax.nn.sigmoid(logits)
        sel = scores + rb_ref[...]
        expert = lax.broadcasted_iota(jnp.int32, sel.shape, 0).astype(F32)
        chosen = jnp.zeros(sel.shape, F32)
        idx_rows, score_rows = [], []
        for _ in range(TOP_K):
            top = jnp.max(sel, axis=0, keepdims=True)
            first = jnp.min(jnp.where(sel == top, expert, float(N_EXPERTS)), axis=0, keepdims=True)
            hit = expert == first
            chosen = jnp.where(hit, 1.0, chosen)
            sel = jnp.where(hit, -jnp.inf, sel)
            idx_rows.append(first)
            score_rows.append(jnp.sum(jnp.where(hit, scores, 0.0), axis=0, keepdims=True))
        norm = ROUTED_SCALE / jnp.sum(scores * chosen, axis=0, keepdims=True)
        gate_ref[:, rows] = scores * chosen * norm
        idx_ref[:, rows] = jnp.concatenate(idx_rows, axis=0)
        topg_ref[:, rows] = jnp.concatenate(score_rows, axis=0) * norm
        cnt_ref[...] += jnp.sum(chosen, axis=1, keepdims=True)


def _post_mixer(x, attn_o, attn_lse, y_tb, w, rows_per_seq, tm):
    n = x.shape[0]
    tiles_per_seq = rows_per_seq // tm
    row = lambda width: pl.BlockSpec((tm, width), lambda i: (i, 0))
    col = lambda height: pl.BlockSpec((height, tm), lambda i: (0, i))
    return pl.pallas_call(
        _post_mixer_kernel,
        grid=(n // tm,),
        in_specs=[row(D_MODEL), row(ATTN_WIDTH), row(ATTN_WIDTH),
                  pl.BlockSpec((tm, SSM_WIDTH), lambda i: (i % tiles_per_seq, i // tiles_per_seq)),
                  _full((SSM_WIDTH, SSM_WIDTH)), _full((1, SSM_WIDTH)),
                  _full((D_MODEL, 2 * D_MODEL)), _full((1, 2 * D_MODEL)),
                  _full((GROUP_WIDTH, D_MODEL)), _full((SSM_WIDTH, D_MODEL)), _full((D_MODEL, D_MODEL)),
                  _full((1, D_MODEL)), _full((1, D_MODEL)),
                  _full((2 * N_EXPERTS, D_MODEL)), _full((N_EXPERTS, 1))],
        out_specs=[row(D_MODEL), pl.BlockSpec((tm * ROW_TILE, LANES), lambda i: (i, 0)),
                   col(N_EXPERTS), col(TOP_K), col(TOP_K), _full((N_EXPERTS, 1))],
        out_shape=[jax.ShapeDtypeStruct((n, D_MODEL), F32), jax.ShapeDtypeStruct((n * ROW_TILE, LANES), jnp.int32),
                   jax.ShapeDtypeStruct((N_EXPERTS, n), F32), jax.ShapeDtypeStruct((TOP_K, n), F32),
                   jax.ShapeDtypeStruct((TOP_K, n), F32), jax.ShapeDtypeStruct((N_EXPERTS, 1), F32)],
        compiler_params=_params(("arbitrary",)),
        name="post_mixer",
    )(x, attn_o, attn_lse, y_tb, w["w_glu"], w["b_glu"], w["w_gate"], w["b_gate"], w["w_attn_br"], w["w_ssm_br"],
      w["w_out"], w["ln1_g"], w["ln1_b"], w["w_router"], w["router_bias"])


def _moe_ffn_kernel(x_ref, gate_ref, p_ref, w13_ref, w2_ref, ws13_ref, ws2_ref, wpg_ref, wple_ref,
                    lng_ref, lnb_ref, o_ref, acc_sc, xb_sc):
    e = pl.program_id(1)

    def glu_ffn(xb, w13, w2, row_scale):
        h13 = jnp.dot(xb, w13, preferred_element_type=F32)
        h = jax.nn.silu(h13[:, :EXPERT_FF]) * h13[:, EXPERT_FF:]
        if row_scale is not None:
            h = h * row_scale
        return jnp.dot(h.astype(BF16), w2, preferred_element_type=F32)

    @pl.when(e == 0)
    def _():
        xb = x_ref[...].astype(BF16)
        xb_sc[...] = xb
        ple = (jax.nn.sigmoid(jnp.dot(xb, wpg_ref[...], preferred_element_type=F32))
               * jnp.dot(p_ref[...].astype(BF16), wple_ref[...], preferred_element_type=F32))
        acc_sc[...] = glu_ffn(xb, ws13_ref[...], ws2_ref[...], None) + ple

    gates = gate_ref[...]
    lane = lax.broadcasted_iota(jnp.int32, gates.shape, 1)
    g_col = jnp.sum(jnp.where(lane == e, gates, 0.0), axis=-1, keepdims=True)
    acc_sc[...] += glu_ffn(xb_sc[...], w13_ref[0], w2_ref[0], g_col)

    @pl.when(e == N_EXPERTS - 1)
    def _():
        o_ref[...] = _layer_norm(DN_ALPHA * x_ref[...] + acc_sc[...], lng_ref[...], lnb_ref[...])


def _moe_ffn(x1, gates, p, w, tm):
    n = x1.shape[0]
    row = lambda width: pl.BlockSpec((tm, width), lambda i, e: (i, 0))
    return pl.pallas_call(
        _moe_ffn_kernel,
        grid=(n // tm, N_EXPERTS),
        in_specs=[row(D_MODEL), row(N_EXPERTS), row(PLE_DIM),
                  pl.BlockSpec((1, D_MODEL, 2 * EXPERT_FF), lambda i, e: (e, 0, 0)),
                  pl.BlockSpec((1, EXPERT_FF, D_MODEL), lambda i, e: (e, 0, 0)),
                  _full((D_MODEL, 2 * EXPERT_FF)), _full((EXPERT_FF, D_MODEL)),
                  _full((D_MODEL, D_MODEL)), _full((PLE_DIM, D_MODEL)),
                  _full((1, D_MODEL)), _full((1, D_MODEL))],
        out_specs=row(D_MODEL),
        out_shape=jax.ShapeDtypeStruct((n, D_MODEL), F32),
        scratch_shapes=[pltpu.VMEM((tm, D_MODEL), F32), pltpu.VMEM((tm, D_MODEL), BF16)],
        compiler_params=_params(("parallel", "arbitrary")),
        name="moe_ffn",
    )(x1, gates, p, w["w13"], w["w2"], w["ws13"], w["ws2"], w["w_ple_gate"], w["w_ple"],
      w["ln2_g"], w["ln2_b"])


def _route_kernel(idx_ref, pstart_ref, earlier_ref, slot_ref, base_sc):
    @pl.when(pl.program_id(0) == 0)
    def _():
        base_sc[...] = jnp.zeros(base_sc.shape, F32)

    idx = idx_ref[...]
    tm = idx.shape[1]
    expert = lax.broadcasted_iota(jnp.int32, (N_EXPERTS, tm), 0).astype(F32)
    hits = [expert == idx[k:k + 1, :] for k in range(TOP_K)]
    member = jnp.zeros((N_EXPERTS, tm), F32)
    for hit in hits:
        member = member + jnp.where(hit, 1.0, 0.0)
    row = (jnp.dot(member.astype(BF16), earlier_ref[...], preferred_element_type=F32)
           + base_sc[...] + pstart_ref[...])
    slots = [jnp.sum(jnp.where(hit, row, 0.0), axis=0, keepdims=True) for hit in hits]
    slot_ref[...] = jnp.concatenate(slots, axis=0).astype(jnp.int32)
    base_sc[...] += jnp.sum(member, axis=1, keepdims=True)


def _route(top_idx, pstart, tm):
    n = top_idx.shape[1]
    earlier = jnp.triu(jnp.ones((tm, tm), F32), k=1).astype(BF16)
    return pl.pallas_call(
        _route_kernel,
        grid=(n // tm,),
        in_specs=[pl.BlockSpec((TOP_K, tm), lambda i: (0, i)), _full((N_EXPERTS, 1)), _full((tm, tm))],
        out_specs=pl.BlockSpec((TOP_K, tm), lambda i: (0, i)),
        out_shape=jax.ShapeDtypeStruct((TOP_K, n), jnp.int32),
        scratch_shapes=[pltpu.VMEM((N_EXPERTS, 1), F32)],
        compiler_params=_params(("arbitrary",)),
        name="route",
    )(top_idx, pstart, earlier)


def _sc_mesh():
    return plsc.VectorSubcoreMesh(core_axis_name="c", subcore_axis_name="s",
                                  num_cores=SC_CORES, num_subcores=SC_SUBCORES)


def _sc_dispatch(x_tiles, slots, n_rows):
    n = x_tiles.shape[0]
    wins_per_worker = n // SC_WINDOW // (SC_CORES * SC_SUBCORES)

    def body(x_hbm, slot_hbm, xs_hbm, idx_v, rows_v, sem):
        wid = lax.axis_index("s") * SC_CORES + lax.axis_index("c")

        @pl.loop(0, wins_per_worker)
        def _(i):
            win = wid * wins_per_worker + i
            pltpu.sync_copy(slot_hbm.at[win], idx_v)
            pltpu.sync_copy(x_hbm.at[pl.ds(win * SC_WINDOW, SC_WINDOW)], rows_v)
            copies = [pltpu.async_copy(rows_v, xs_hbm.at[idx_v.at[k]], sem) for k in range(TOP_K)]
            for copy in copies:
                copy.wait()

    return pl.kernel(
        body, out_type=jax.ShapeDtypeStruct((n_rows, ROW_TILE, LANES), jnp.int32), mesh=_sc_mesh(),
        scratch_types=[pltpu.VMEM((TOP_K, SC_WINDOW), jnp.int32),
                       pltpu.VMEM((SC_WINDOW, ROW_TILE, LANES), jnp.int32),
                       pltpu.SemaphoreType.DMA],
        name="sc_dispatch",
    )(x_tiles, slots)


def _sc_combine(y_tiles, slots, n):
    wins_per_worker = n // SC_WINDOW // (SC_CORES * SC_SUBCORES)

    def body(ys_hbm, slot_hbm, yg_hbm, idx_v, rows_a, rows_b, gather_sems, write_sems):
        wid = lax.axis_index("s") * SC_CORES + lax.axis_index("c")
        bufs = (rows_a, rows_b)

        @pl.loop(0, wins_per_worker)
        def _(i):
            win = wid * wins_per_worker + i
            pltpu.sync_copy(slot_hbm.at[win], idx_v)

            def gather(k):
                return pltpu.async_copy(ys_hbm.at[idx_v.at[k]], bufs[k % 2], gather_sems.at[k % 2])

            def write(k):
                return pltpu.async_copy(bufs[k % 2], yg_hbm.at[k, pl.ds(win * SC_WINDOW, SC_WINDOW)],
                                        write_sems.at[k % 2])

            gathers = {0: gather(0)}
            writes = {}
            for k in range(TOP_K):
                gathers[k].wait()
                if k + 1 < TOP_K:
                    if k >= 1:
                        writes[k - 1].wait()
                    gathers[k + 1] = gather(k + 1)
                writes[k] = write(k)
            writes[TOP_K - 2].wait()
            writes[TOP_K - 1].wait()

    return pl.kernel(
        body, out_type=jax.ShapeDtypeStruct((TOP_K, n, ROW_TILE, LANES), jnp.int32), mesh=_sc_mesh(),
        scratch_types=[pltpu.VMEM((TOP_K, SC_WINDOW), jnp.int32),
                       pltpu.VMEM((SC_WINDOW, ROW_TILE, LANES), jnp.int32),
                       pltpu.VMEM((SC_WINDOW, ROW_TILE, LANES), jnp.int32),
                       pltpu.SemaphoreType.DMA((2,)), pltpu.SemaphoreType.DMA((2,))],
        name="sc_combine",
    )(y_tiles, slots)


def _expert_ffn_kernel(bexp_ref, nused_ref, xs_ref, w13_ref, w2_ref, ys_ref):
    del bexp_ref

    @pl.when(pl.program_id(0) < nused_ref[0])
    def _():
        sub = MOE_BLOCK // EXPERT_FFN_SUBTILES
        for r0 in range(0, MOE_BLOCK, sub):
            x = jnp.concatenate(_load_packed_chunks(xs_ref, sub, row0=r0), axis=1)
            h13 = jnp.dot(x.astype(BF16), w13_ref[0], preferred_element_type=F32)
            h = jax.nn.silu(h13[:, :EXPERT_FF]) * h13[:, EXPERT_FF:]
            _store_packed_rows(ys_ref, jnp.dot(h.astype(BF16), w2_ref[0], preferred_element_type=F32), r0)


def _expert_ffn(xs_rows, block_expert, n_used, w):
    n_blocks = block_expert.shape[0]
    blk = (MOE_BLOCK * ROW_TILE, LANES)
    return pl.pallas_call(
        _expert_ffn_kernel,
        grid_spec=pltpu.PrefetchScalarGridSpec(
            num_scalar_prefetch=2, grid=(n_blocks,),
            in_specs=[pl.BlockSpec(blk, lambda i, be, nu: (i, 0)),
                      pl.BlockSpec((1, D_MODEL, 2 * EXPERT_FF), lambda i, be, nu: (be[i], 0, 0)),
                      pl.BlockSpec((1, EXPERT_FF, D_MODEL), lambda i, be, nu: (be[i], 0, 0))],
            out_specs=pl.BlockSpec(blk, lambda i, be, nu: (i, 0))),
        out_shape=jax.ShapeDtypeStruct(xs_rows.shape, jnp.int32),
        compiler_params=_params(("parallel",)),
        name="expert_ffn",
    )(block_expert, n_used, xs_rows, w["w13"], w["w2"])


def _moe_out_kernel(x_ref, g_ref, p_ref, yg_ref, ws13_ref, ws2_ref, wpg_ref, wple_ref, lng_ref, lnb_ref, o_ref):
    x = x_ref[...]
    xb = x.astype(BF16)
    tm = x.shape[0]
    g = g_ref[...]
    parts = None
    for k in range(TOP_K):
        chunks = [g[:, k:k + 1] * c for c in _load_packed_chunks(yg_ref, tm, lead=k)]
        parts = chunks if parts is None else [a + c for a, c in zip(parts, chunks)]
    routed = jnp.concatenate(parts, axis=1)
    h13 = jnp.dot(xb, ws13_ref[...], preferred_element_type=F32)
    h = jax.nn.silu(h13[:, :EXPERT_FF]) * h13[:, EXPERT_FF:]
    shared = jnp.dot(h.astype(BF16), ws2_ref[...], preferred_element_type=F32)
    ple = (jax.nn.sigmoid(jnp.dot(xb, wpg_ref[...], preferred_element_type=F32))
           * jnp.dot(p_ref[...].astype(BF16), wple_ref[...], preferred_element_type=F32))
    o_ref[...] = _layer_norm(DN_ALPHA * x + routed + shared + ple, lng_ref[...], lnb_ref[...])


def _moe_out(x1, top_gates, p, yg_rows, w, tm):
    n = x1.shape[0]
    row = lambda width: pl.BlockSpec((tm, width), lambda i: (i, 0))
    return pl.pallas_call(
        _moe_out_kernel,
        grid=(n // tm,),
        in_specs=[row(D_MODEL), row(TOP_K), row(PLE_DIM),
                  pl.BlockSpec((TOP_K, tm * ROW_TILE, LANES), lambda i: (0, i, 0)),
                  _full((D_MODEL, 2 * EXPERT_FF)), _full((EXPERT_FF, D_MODEL)),
                  _full((D_MODEL, D_MODEL)), _full((PLE_DIM, D_MODEL)),
                  _full((1, D_MODEL)), _full((1, D_MODEL))],
        out_specs=row(D_MODEL),
        out_shape=jax.ShapeDtypeStruct((n, D_MODEL), F32),
        compiler_params=_params(("parallel",)),
        name="moe_out",
    )(x1, top_gates, p, yg_rows, w["ws13"], w["ws2"], w["w_ple_gate"], w["w_ple"], w["ln2_g"], w["ln2_b"])


def _moe_sorted(x1, x1_tiles, top_idx, top_gates, counts, p, w):
    n = x1.shape[0]
    n_blocks = n * TOP_K // MOE_BLOCK + N_EXPERTS
    n_rows = n_blocks * MOE_BLOCK
    cnt = counts.reshape(N_EXPERTS).astype(jnp.int32)
    padded = (cnt + MOE_BLOCK - 1) // MOE_BLOCK * MOE_BLOCK
    pend = jnp.cumsum(padded)
    pstart = (pend - padded).astype(F32).reshape(N_EXPERTS, 1)
    block_start = jnp.arange(n_blocks, dtype=jnp.int32) * MOE_BLOCK
    block_expert = jnp.minimum(jnp.sum((pend[None, :] <= block_start[:, None]).astype(jnp.int32), axis=1),
                               N_EXPERTS - 1)
    n_used = (pend[-1:] // MOE_BLOCK).astype(jnp.int32)
    slots = _route(top_idx, pstart, 512)
    slots = jnp.transpose(slots.reshape(TOP_K, n // SC_WINDOW, SC_WINDOW), (1, 0, 2))
    xs = _sc_dispatch(x1_tiles.reshape(n, ROW_TILE, LANES), slots, n_rows)
    ys = _expert_ffn(xs.reshape(n_rows * ROW_TILE, LANES), block_expert, n_used, w)
    yg = _sc_combine(ys.reshape(n_rows, ROW_TILE, LANES), slots, n)
    return _moe_out(x1, top_gates.T, p, yg.reshape(TOP_K, n * ROW_TILE, LANES), w, 512)


def _kv_rows(k, v, batch, seq, keep, g):
    cols = slice(g * GROUP_WIDTH, (g + 1) * GROUP_WIDTH)
    shape = (batch, keep, HEADS_PER_GROUP, HEAD_DIM)
    k_g = k.reshape(batch, seq, ATTN_WIDTH)[:, seq - keep:, cols].reshape(shape)
    v_g = v.reshape(batch, seq, ATTN_WIDTH)[:, seq - keep:, cols].reshape(shape)
    return jnp.stack([k_g, v_g], axis=2)


def _layer_prompt(x, p, w, ssm):
    batch, seq, _ = x.shape
    n = batch * seq
    x2 = x.reshape(n, D_MODEL)
    tabs = _rope_tables(jnp.arange(seq, dtype=jnp.int32))
    q, k, v, u = _in_proj(x2, w["w_in"], tabs, seq, 512)
    attn_o, attn_lse = _attn_prompt(q, k, v, batch, seq)
    zeros = jnp.zeros((batch, SSM_LANES), F32)
    y_tb, h_re, h_im = _s5_scan(u.reshape(seq * batch, SSM_WIDTH), ssm, zeros, zeros, batch, 128)
    x1, x1_tiles, _, top_idx, top_gates, counts = _post_mixer(
        x2, attn_o, attn_lse, y_tb.reshape(seq, batch * SSM_WIDTH), w, seq, 512)
    y = _moe_sorted(x1, x1_tiles, top_idx, top_gates, counts, p.reshape(n, PLE_DIM), w)
    kv = [_kv_rows(k, v, batch, seq, min(win, seq), g) for g, (win, _) in enumerate(DILATION_GROUPS)]
    h_last = jnp.stack([h_re, h_im], axis=-1).reshape(batch, SSM_GROUPS, SSM_STATE, 2)
    return y.reshape(batch, seq, D_MODEL), kv, h_last


def _layer_sample(x, p, caches, state, w, ssm):
    batch, seq, _ = x.shape
    assert seq == 1
    x2 = x.reshape(batch, D_MODEL)
    tabs = _rope_tables(jnp.full((batch,), PAST_LEN, dtype=jnp.int32))
    q, k, v, u = _in_proj(x2, w["w_in"], tabs, batch, batch)
    attn_o, attn_lse = _attn_sample(q, k, v, caches, 2)
    h0 = state.reshape(batch, SSM_LANES, 2)
    y_tb, h_re, h_im = _s5_scan(u, ssm, h0[..., 0], h0[..., 1], batch, 1)
    x1, _, gates, _, _, _ = _post_mixer(x2, attn_o, attn_lse, y_tb, w, batch, batch)
    y = _moe_ffn(x1, gates.T, p.reshape(batch, PLE_DIM), w, batch)
    kv = [_kv_rows(k, v, batch, 1, 1, g) for g in range(len(DILATION_GROUPS))]
    h_last = jnp.stack([h_re, h_im], axis=-1).reshape(batch, SSM_GROUPS, SSM_STATE, 2)
    return y.reshape(batch, 1, D_MODEL), kv, h_last


def _hi_lo_rows(t):
    hi = t.astype(BF16)
    return jnp.concatenate([hi, (t - hi.astype(F32)).astype(BF16)], axis=1).T


def kernel(x_prompt, x_sample, cache_kv_w128, cache_kv_w512, cache_kv_w2048, state_ssm, p_prompt, p_sample,
           w_in, a_re, a_im, log_dt, b_re, b_im, c_re, c_im, d_skip, w_glu, b_glu, w_attn_br, w_ssm_br,
           w_gate, b_gate, w_out, ln1_g, ln1_b, w_router, router_bias, w1, w3, w2, ws1, ws3, ws2,
           w_ple_gate, w_ple, ln2_g, ln2_b):
    assert w_in.shape[0] == DEPTH == 1
    l = 0
    row = lambda t: t[l].reshape(1, -1)
    w = {
        "w_in": w_in[l].astype(BF16),
        "w_glu": w_glu[l].astype(BF16), "b_glu": row(b_glu),
        "w_gate": w_gate[l].astype(BF16), "b_gate": row(b_gate),
        "w_attn_br": w_attn_br[l].astype(BF16), "w_ssm_br": w_ssm_br[l].astype(BF16),
        "w_out": w_out[l].astype(BF16), "ln1_g": row(ln1_g), "ln1_b": row(ln1_b),
        "w_router": _hi_lo_rows(w_router[l]), "router_bias": router_bias[l].reshape(N_EXPERTS, 1),
        "w13": jnp.concatenate([w1[l], w3[l]], axis=-1).astype(BF16), "w2": w2[l].astype(BF16),
        "ws13": jnp.concatenate([ws1[l], ws3[l]], axis=-1).astype(BF16), "ws2": ws2[l].astype(BF16),
        "w_ple_gate": w_ple_gate[l].astype(BF16), "w_ple": w_ple[l].astype(BF16),
        "ln2_g": row(ln2_g), "ln2_b": row(ln2_b),
    }
    ssm = _s5_params(a_re[l], a_im[l], log_dt[l], b_re[l], b_im[l], c_re[l], c_im[l], d_skip[l])
    yp, kv_p, h_p = _layer_prompt(x_prompt, p_prompt[l], w, ssm)
    caches = (cache_kv_w128[l], cache_kv_w512[l], cache_kv_w2048[l])
    ys, kv_s, h_s = _layer_sample(x_sample, p_sample[l], caches, state_ssm[l], w, ssm)
    return (yp, ys, kv_p[0][None], kv_s[0][None], kv_p[1][None], kv_s[1][None],
            kv_p[2][None], kv_s[2][None], h_p[None], h_s[None])
```

```python
import functools
import math

import jax
import jax.numpy as jnp
from jax import lax
from jax.experimental import pallas as pl
from jax.experimental.pallas import tpu as pltpu
from jax.experimental.pallas import tpu_sc as plsc

F32 = jnp.float32
BF16 = jnp.bfloat16

D_MODEL = 1024
HEAD_DIM = 64
HEADS_PER_GROUP = 4
DILATION_GROUPS = ((128, 1), (512, 4), (2048, 16))
N_BACK = 128
GROUP_WIDTH = HEADS_PER_GROUP * HEAD_DIM
ATTN_WIDTH = 3 * GROUP_WIDTH
ROPE_THETA = 10000.0
SSM_WIDTH = 256
SSM_GROUP = 16
SSM_GROUPS = 16
SSM_STATE = 64
SSM_LANES = SSM_GROUPS * SSM_STATE
IN_WIDTH = 3 * ATTN_WIDTH + SSM_WIDTH
N_EXPERTS = 64
TOP_K = 8
EXPERT_FF = 256
ROUTED_SCALE = 2.5
PLE_DIM = 256
DEPTH = 1
PAST_LEN = 8192
DN_ALPHA = (2.0 * DEPTH) ** 0.25
LN_EPS = 1e-5

LANES = 128
ROW_TILE = D_MODEL // LANES // 2
SC_CORES = 2
SC_SUBCORES = 16
SC_WINDOW = 64
MOE_BLOCK = 1024
POST_MIXER_SUBTILES = 2
EXPERT_FFN_SUBTILES = 4
ATTN_CHUNK = 2048
ATTN_UNROLL = 8
VMEM_LIMIT = 56 * 1024 * 1024


def _params(semantics):
    return pltpu.CompilerParams(dimension_semantics=semantics, vmem_limit_bytes=VMEM_LIMIT)


def _full(shape):
    return pl.BlockSpec(shape, lambda *_: (0,) * len(shape))


def _in_proj_kernel(x_ref, w_ref, cos_ref, sina_ref, sinb_ref, q_ref, k_ref, v_ref, u_ref):
    xb = x_ref[...].astype(BF16)
    cos = cos_ref[...]
    sin_a = sina_ref[...]
    sin_b = sinb_ref[...]

    def rope_store(col0, out_ref, scale):
        t = jnp.dot(xb, w_ref[:, col0:col0 + ATTN_WIDTH], preferred_element_type=F32)
        for c in range(ATTN_WIDTH // LANES):
            xc = t[:, c * LANES:(c + 1) * LANES]
            r = xc * cos + pltpu.roll(xc, LANES - 32, 1) * sin_a + pltpu.roll(xc, 32, 1) * sin_b
            out_ref[:, c * LANES:(c + 1) * LANES] = r * scale if scale != 1.0 else r

    rope_store(0, q_ref, HEAD_DIM ** -0.5)
    rope_store(ATTN_WIDTH, k_ref, 1.0)
    v_ref[...] = jnp.dot(xb, w_ref[:, 2 * ATTN_WIDTH:3 * ATTN_WIDTH], preferred_element_type=F32)
    u_ref[...] = jnp.dot(xb, w_ref[:, 3 * ATTN_WIDTH:], preferred_element_type=F32)


def _in_proj(x, w_in_bf, rope_tabs, rows_per_seq, tm):
    n = x.shape[0]
    tiles_per_seq = rows_per_seq // tm
    n_seq = n // rows_per_seq
    tab_tiles = rope_tabs[0].shape[0] // tm
    tab_spec = pl.BlockSpec((tm, LANES), lambda i: (i % tab_tiles, 0))
    row_spec = pl.BlockSpec((tm, ATTN_WIDTH), lambda i: (i, 0))
    return pl.pallas_call(
        _in_proj_kernel,
        grid=(n // tm,),
        in_specs=[pl.BlockSpec((tm, D_MODEL), lambda i: (i, 0)), _full((D_MODEL, IN_WIDTH)),
                  tab_spec, tab_spec, tab_spec],
        out_specs=[row_spec, row_spec, row_spec,
                   pl.BlockSpec((tm, SSM_WIDTH), lambda i: (i % tiles_per_seq, i // tiles_per_seq))],
        out_shape=[jax.ShapeDtypeStruct((n, ATTN_WIDTH), F32)] * 3
        + [jax.ShapeDtypeStruct((rows_per_seq, n_seq * SSM_WIDTH), F32)],
        compiler_params=_params(("parallel",)),
        name="in_proj",
    )(x, w_in_bf, *rope_tabs)


def _rope_tables(pos):
    half = HEAD_DIM // 2
    inv = ROPE_THETA ** (-jnp.arange(half, dtype=F32) / half)
    ang = pos.astype(F32)[:, None] * inv[None, :]
    cos = jnp.tile(jnp.cos(ang), (1, LANES // half))
    sin = jnp.tile(jnp.sin(ang), (1, LANES // half))
    first_half = (jnp.arange(LANES) % HEAD_DIM) < half
    sin_a = jnp.where(first_half[None, :], -sin, 0.0)
    sin_b = jnp.where(first_half[None, :], 0.0, sin)
    return cos, sin_a, sin_b


def _band_attention(q, k, v, mask):
    head_of_lane = lax.broadcasted_iota(jnp.int32, (N_BACK, LANES), 1) // HEAD_DIM
    kb = k.astype(BF16)
    vb = v.astype(BF16)
    o = lse = None
    for h in range(LANES // HEAD_DIM):
        qh = jnp.where(head_of_lane == h, q, 0.0).astype(BF16)
        logits = lax.dot_general(qh, kb, (((1,), (1,)), ((), ())), preferred_element_type=F32) + mask
        m = jnp.max(logits, axis=1, keepdims=True)
        p = jnp.exp(logits - m)
        l = jnp.sum(p, axis=1, keepdims=True)
        o_h = jnp.dot(p.astype(BF16), vb, preferred_element_type=F32) * (1.0 / l)
        lse_h = jnp.broadcast_to(m + jnp.log(l), (N_BACK, LANES))
        o = o_h if o is None else jnp.where(head_of_lane == h, o_h, o)
        lse = lse_h if lse is None else jnp.where(head_of_lane == h, lse_h, lse)
    return o, lse


def _attn_prompt_kernel(q_ref, kp_ref, kc_ref, vp_ref, vc_ref, o_ref, lse_ref):
    c = pl.program_id(1)
    g = pl.program_id(3)
    ch = ATTN_CHUNK
    qi = lax.broadcasted_iota(jnp.int32, (N_BACK, 2 * N_BACK), 0)
    kj = lax.broadcasted_iota(jnp.int32, (N_BACK, 2 * N_BACK), 1)
    dist = qi + N_BACK - kj
    band = jnp.where(dist >= 0, jnp.where(dist <= N_BACK, 0.0, -jnp.inf), -jnp.inf)
    band_first = jnp.where(kj >= N_BACK, band, -jnp.inf)

    def group_body(d):
        span = N_BACK * d
        n_sub = ch // N_BACK

        def rows(start, size):
            return pl.ds(start, size) if d == 1 else pl.ds(start, size, stride=d)

        def store(q0, o, lse):
            o_ref[rows(q0, N_BACK), :] = o
            lse_ref[rows(q0, N_BACK), :] = lse

        def head_block(r, carry):
            k = jnp.concatenate([kp_ref[rows(ch - span + r, N_BACK), :], kc_ref[rows(r, N_BACK), :]], axis=0)
            v = jnp.concatenate([vp_ref[rows(ch - span + r, N_BACK), :], vc_ref[rows(r, N_BACK), :]], axis=0)
            mask = jnp.where(c == 0, band_first, band)
            store(r, *_band_attention(q_ref[rows(r, N_BACK), :], k, v, mask))
            return carry

        def inner_block(idx, carry):
            s = idx // d
            r = idx % d
            k0 = (s - 1) * span + r
            store(s * span + r, *_band_attention(q_ref[rows(s * span + r, N_BACK), :],
                                                 kc_ref[rows(k0, 2 * N_BACK), :],
                                                 vc_ref[rows(k0, 2 * N_BACK), :], band))
            return carry

        lax.fori_loop(0, d, head_block, 0, unroll=min(d, ATTN_UNROLL))
        if n_sub > d:
            trips = n_sub - d
            lax.fori_loop(d, n_sub, inner_block, 0,
                          unroll=max(u for u in range(1, ATTN_UNROLL + 1) if trips % u == 0))

    for gi, (_, d) in enumerate(DILATION_GROUPS):
        pl.when(g == gi)(functools.partial(group_body, d))


def _attn_prompt(q, k, v, batch, seq):
    ch = ATTN_CHUNK
    cps = seq // ch
    n = batch * seq
    pairs = GROUP_WIDTH // LANES
    cur = lambda b, c, hp, g: (b * cps + c, g * pairs + hp)
    prev = lambda b, c, hp, g: (b * cps + jnp.maximum(c - 1, 0), g * pairs + hp)
    blk = (ch, LANES)
    return pl.pallas_call(
        _attn_prompt_kernel,
        grid=(batch, cps, pairs, len(DILATION_GROUPS)),
        in_specs=[pl.BlockSpec(blk, cur), pl.BlockSpec(blk, prev), pl.BlockSpec(blk, cur),
                  pl.BlockSpec(blk, prev), pl.BlockSpec(blk, cur)],
        out_specs=[pl.BlockSpec(blk, cur), pl.BlockSpec(blk, cur)],
        out_shape=[jax.ShapeDtypeStruct((n, ATTN_WIDTH), F32)] * 2,
        compiler_params=_params(("parallel", "parallel", "parallel", "parallel")),
        name="attn_prompt",
    )(q, k, k, v, v)


def _attn_sample_kernel(q_ref, k_ref, v_ref, c0_ref, c1_ref, c2_ref, o_ref, lse_ref):
    bt = q_ref.shape[0]
    for b in range(bt):
        for g, (c_ref, (win, d)) in enumerate(zip((c0_ref, c1_ref, c2_ref), DILATION_GROUPS)):
            pos = lax.broadcasted_iota(jnp.int32, (1, win), 1)
            off_stride = (pos % d) != 0
            j0 = g * HEADS_PER_GROUP
            heads = range(HEADS_PER_GROUP)
            qs = [q_ref[b, :, j0 + h:j0 + h + 1] for h in heads]
            s_c = jnp.concatenate([jnp.sum(c_ref[b, 0, h] * qs[h], axis=0, keepdims=True) for h in heads],
                                  axis=0)
            s_c = jnp.where(off_stride, -jnp.inf, s_c)
            s_new = jnp.concatenate([jnp.sum(k_ref[b, :, j0 + h:j0 + h + 1] * qs[h], axis=0, keepdims=True)
                                     for h in heads], axis=0)
            m = jnp.maximum(jnp.max(s_c, axis=1, keepdims=True), s_new)
            p_c = jnp.exp(s_c - m)
            p_new = jnp.exp(s_new - m)
            l = jnp.sum(p_c, axis=1, keepdims=True) + p_new
            inv_l = 1.0 / l
            lse_ref[b, j0:j0 + HEADS_PER_GROUP, :] = m + jnp.log(l)
            for h in heads:
                num = (jnp.sum(c_ref[b, 1, h] * p_c[h:h + 1, :], axis=1, keepdims=True)
                       + p_new[h:h + 1, :] * v_ref[b, :, j0 + h:j0 + h + 1])
                o_ref[b, :, j0 + h:j0 + h + 1] = num * inv_l[h:h + 1, :]


def _attn_sample(q, k, v, caches, bt):
    b = q.shape[0]
    n_heads = ATTN_WIDTH // HEAD_DIM
    views, specs = [], []
    for cache, (win, d) in zip(caches, DILATION_GROUPS):
        assert cache.shape[1] == win == N_BACK * d
        views.append(jnp.transpose(cache, (0, 2, 3, 4, 1)))
        specs.append(pl.BlockSpec((bt, 2, HEADS_PER_GROUP, HEAD_DIM, win), lambda i: (i, 0, 0, 0, 0)))
    col_spec = pl.BlockSpec((bt, HEAD_DIM, n_heads), lambda i: (i, 0, 0))
    lse_spec = pl.BlockSpec((bt, n_heads, 1), lambda i: (i, 0, 0))
    cols = lambda t: jnp.transpose(t.reshape(b, n_heads, HEAD_DIM), (0, 2, 1))
    o, lse = pl.pallas_call(
        _attn_sample_kernel,
        grid=(b // bt,),
        in_specs=[col_spec, col_spec, col_spec] + specs,
        out_specs=[col_spec, lse_spec],
        out_shape=[jax.ShapeDtypeStruct((b, HEAD_DIM, n_heads), F32),
                   jax.ShapeDtypeStruct((b, n_heads, 1), F32)],
        compiler_params=_params(("parallel",)),
        name="attn_sample",
    )(cols(q), cols(k), cols(v), *views)
    o = jnp.transpose(o, (0, 2, 1)).reshape(b, ATTN_WIDTH)
    lse = jnp.broadcast_to(lse, (b, n_heads, HEAD_DIM)).reshape(b, ATTN_WIDTH)
    return o, lse


def _s5_scan_kernel(u_ref, bmat_ref, cmat_ref, are_ref, aim_ref, d_ref, h0re_ref, h0im_ref,
                    y_ref, hre_ref, him_ref, hist_sc, *, bg, steps):
    t_chunk = pl.program_id(0)

    @pl.when(t_chunk == 0)
    def _():
        hre_ref[...] = h0re_ref[...]
        him_ref[...] = h0im_ref[...]

    u = u_ref[...]
    hist_sc[...] = jnp.dot(u.astype(BF16), bmat_ref[...], preferred_element_type=F32)
    a_re = jnp.broadcast_to(are_ref[...], (bg, SSM_LANES))
    a_im = jnp.broadcast_to(aim_ref[...], (bg, SSM_LANES))

    def step(t, carry):
        h_re, h_im = carry
        rows = pl.ds(pl.multiple_of(t * bg, bg), bg)
        n_re = a_re * h_re - a_im * h_im + hist_sc[rows, 0:SSM_LANES]
        n_im = a_re * h_im + a_im * h_re + hist_sc[rows, SSM_LANES:2 * SSM_LANES]
        hist_sc[rows, 0:SSM_LANES] = n_re
        hist_sc[rows, SSM_LANES:2 * SSM_LANES] = n_im
        return n_re, n_im

    h_re, h_im = lax.fori_loop(0, steps, step, (hre_ref[...], him_ref[...]))
    hre_ref[...] = h_re
    him_ref[...] = h_im
    y_ref[...] = (jnp.dot(hist_sc[...].astype(BF16), cmat_ref[...], preferred_element_type=F32)
                  + d_ref[...] * u)


def _s5_scan(u_tb, ssm, h0_re, h0_im, bg, steps):
    rows = u_tb.shape[0]
    blk = steps * bg
    kern = functools.partial(_s5_scan_kernel, bg=bg, steps=steps)
    state_spec = _full((bg, SSM_LANES))
    return pl.pallas_call(
        kern,
        grid=(rows // blk,),
        in_specs=[pl.BlockSpec((blk, SSM_WIDTH), lambda i: (i, 0)),
                  _full((SSM_WIDTH, 2 * SSM_LANES)), _full((2 * SSM_LANES, SSM_WIDTH)),
                  _full((1, SSM_LANES)), _full((1, SSM_LANES)), _full((1, SSM_WIDTH)),
                  state_spec, state_spec],
        out_specs=[pl.BlockSpec((blk, SSM_WIDTH), lambda i: (i, 0)), state_spec, state_spec],
        out_shape=[jax.ShapeDtypeStruct((rows, SSM_WIDTH), F32),
                   jax.ShapeDtypeStruct((bg, SSM_LANES), F32), jax.ShapeDtypeStruct((bg, SSM_LANES), F32)],
        scratch_shapes=[pltpu.VMEM((blk, 2 * SSM_LANES), F32)],
        compiler_params=_params(("arbitrary",)),
        name="s5_scan",
    )(u_tb, ssm["bmat"], ssm["cmat"], ssm["a_re"], ssm["a_im"], ssm["d_skip"], h0_re, h0_im)


def _s5_params(a_re, a_im, log_dt, b_re, b_im, c_re, c_im, d_skip):
    dt = jnp.exp(log_dt)[:, None]
    mag = jnp.exp(a_re * dt)
    abar_re = mag * jnp.cos(a_im * dt)
    abar_im = mag * jnp.sin(a_im * dt)
    a2 = a_re * a_re + a_im * a_im
    nr = abar_re - 1.0
    coef_re = (nr * a_re + abar_im * a_im) / a2
    coef_im = (abar_im * a_re - nr * a_im) / a2
    bb_re = coef_re[..., None] * b_re - coef_im[..., None] * b_im
    bb_im = coef_re[..., None] * b_im + coef_im[..., None] * b_re
    eye = jnp.eye(SSM_GROUPS, dtype=F32)
    to_b = lambda t: jnp.einsum("gpc,gh->gchp", t, eye).reshape(SSM_WIDTH, SSM_LANES)
    to_c = lambda t: jnp.einsum("gcp,gh->gphc", t, eye).reshape(SSM_LANES, SSM_WIDTH)
    return {
        "bmat": jnp.concatenate([to_b(bb_re), to_b(bb_im)], axis=1).astype(BF16),
        "cmat": jnp.concatenate([to_c(c_re), -to_c(c_im)], axis=0).astype(BF16),
        "a_re": abar_re.reshape(1, SSM_LANES), "a_im": abar_im.reshape(1, SSM_LANES),
        "d_skip": d_skip.reshape(1, SSM_WIDTH),
    }


def _layer_norm(z, g, b):
    mu = jnp.mean(z, axis=-1, keepdims=True)
    zc = z - mu
    var = jnp.mean(zc * zc, axis=-1, keepdims=True)
    return zc * lax.rsqrt(var + LN_EPS) * g + b


def _merge_groups(o, lse):
    parts = [slice(g * GROUP_WIDTH, (g + 1) * GROUP_WIDTH) for g in range(len(DILATION_GROUPS))]
    top = lse[:, parts[0]]
    for cols in parts[1:]:
        top = jnp.maximum(top, lse[:, cols])
    num = den = None
    for cols in parts:
        w = jnp.exp(lse[:, cols] - top)
        num = w * o[:, cols] if num is None else num + w * o[:, cols]
        den = w if den is None else den + w
    return num / den


def _store_packed_rows(ref, x, row0=0):
    rows = x.shape[0]
    for j in range(ROW_TILE):
        lo = x[:, j * LANES:(j + 1) * LANES].astype(BF16).astype(F32)
        hi = x[:, (j + ROW_TILE) * LANES:(j + ROW_TILE + 1) * LANES].astype(BF16).astype(F32)
        word = (lax.bitcast_convert_type(lo, jnp.uint32) >> 16) | lax.bitcast_convert_type(hi, jnp.uint32)
        ref[pl.ds(row0 * ROW_TILE + j, rows, stride=ROW_TILE), :] = lax.bitcast_convert_type(word, jnp.int32)


def _load_packed_chunks(ref, rows, lead=None, row0=0):
    lows, highs = [], []
    for j in range(ROW_TILE):
        idx = (pl.ds(row0 * ROW_TILE + j, rows, stride=ROW_TILE), slice(None))
        word = lax.bitcast_convert_type(ref[idx] if lead is None else ref[(lead,) + idx], jnp.uint32)
        lows.append(lax.bitcast_convert_type(word << 16, F32))
        highs.append(lax.bitcast_convert_type(word & jnp.uint32(0xFFFF0000), F32))
    return lows + highs


def _post_mixer_kernel(x_ref, ao_ref, lse_ref, y_ref, wglu_ref, bglu_ref, wgate_ref, bgate_ref, wab_ref, wsb_ref,
                       wout_ref, lng_ref, lnb_ref, wr_ref, rb_ref,
                       x1_ref, x1t_ref, gate_ref, idx_ref, topg_ref, cnt_ref):
    @pl.when(pl.program_id(0) == 0)
    def _():
        cnt_ref[...] = jnp.zeros(cnt_ref.shape, F32)

    tm = x_ref.shape[0]
    sub = tm // POST_MIXER_SUBTILES if tm % (8 * POST_MIXER_SUBTILES) == 0 else tm
    for r0 in range(0, tm, sub):
        rows = slice(r0, r0 + sub)
        x = x_ref[rows, :]
        xb = x.astype(BF16)
        s = jax.nn.gelu(y_ref[rows, :])
        s = s * jax.nn.sigmoid(jnp.dot(s.astype(BF16), wglu_ref[...], preferred_element_type=F32) + bglu_ref[...])
        gates = jax.nn.sigmoid(jnp.dot(xb, wgate_ref[...], preferred_element_type=F32) + bgate_ref[...])
        attn_o = _merge_groups(ao_ref[rows, :], lse_ref[rows, :])
        attn_br = jnp.dot(attn_o.astype(BF16), wab_ref[...], preferred_element_type=F32)
        ssm_br = jnp.dot(s.astype(BF16), wsb_ref[...], preferred_element_type=F32)
        merged = gates[:, :D_MODEL] * attn_br + gates[:, D_MODEL:] * ssm_br
        mix = jnp.dot(merged.astype(BF16), wout_ref[...], preferred_element_type=F32)
        x1 = _layer_norm(DN_ALPHA * x + mix, lng_ref[...], lnb_ref[...])
        x1_ref[rows, :] = x1
        _store_packed_rows(x1t_ref, x1, r0)

        x1_hi = x1.astype(BF16)
        x1_lo = (x1 - x1_hi.astype(F32)).astype(BF16)
        prod = lax.dot_general(wr_ref[...], jnp.concatenate([x1_hi, x1_lo], axis=0),
                               (((1,), (1,)), ((), ())), preferred_element_type=F32)
        logits = ((prod[:N_EXPERTS, :sub] + prod[N_EXPERTS:, :sub] + prod[:N_EXPERTS, sub:])
                  + prod[N_EXPERTS:, sub:])
        scores = jax.nn.sigmoid(logits)
        sel = scores + rb_ref[...]
        expert = lax.broadcasted_iota(jnp.int32, sel.shape, 0).astype(F32)
        chosen = jnp.zeros(sel.shape, F32)
        idx_rows, score_rows = [], []
        for _ in range(TOP_K):
            top = jnp.max(sel, axis=0, keepdims=True)
            first = jnp.min(jnp.where(sel == top, expert, float(N_EXPERTS)), axis=0, keepdims=True)
            hit = expert == first
            chosen = jnp.where(hit, 1.0, chosen)
            sel = jnp.where(hit, -jnp.inf, sel)
            idx_rows.append(first)
            score_rows.append(jnp.sum(jnp.where(hit, scores, 0.0), axis=0, keepdims=True))
        norm = ROUTED_SCALE / jnp.sum(scores * chosen, axis=0, keepdims=True)
        gate_ref[:, rows] = scores * chosen * norm
        idx_ref[:, rows] = jnp.concatenate(idx_rows, axis=0)
        topg_ref[:, rows] = jnp.concatenate(score_rows, axis=0) * norm
        cnt_ref[...] += jnp.sum(chosen, axis=1, keepdims=True)


def _post_mixer(x, attn_o, attn_lse, y_tb, w, rows_per_seq, tm):
    n = x.shape[0]
    tiles_per_seq = rows_per_seq // tm
    row = lambda width: pl.BlockSpec((tm, width), lambda i: (i, 0))
    col = lambda height: pl.BlockSpec((height, tm), lambda i: (0, i))
    return pl.pallas_call(
        _post_mixer_kernel,
        grid=(n // tm,),
        in_specs=[row(D_MODEL), row(ATTN_WIDTH), row(ATTN_WIDTH),
                  pl.BlockSpec((tm, SSM_WIDTH), lambda i: (i % tiles_per_seq, i // tiles_per_seq)),
                  _full((SSM_WIDTH, SSM_WIDTH)), _full((1, SSM_WIDTH)),
                  _full((D_MODEL, 2 * D_MODEL)), _full((1, 2 * D_MODEL)),
                  _full((GROUP_WIDTH, D_MODEL)), _full((SSM_WIDTH, D_MODEL)), _full((D_MODEL, D_MODEL)),
                  _full((1, D_MODEL)), _full((1, D_MODEL)),
                  _full((2 * N_EXPERTS, D_MODEL)), _full((N_EXPERTS, 1))],
        out_specs=[row(D_MODEL), pl.BlockSpec((tm * ROW_TILE, LANES), lambda i: (i, 0)),
                   col(N_EXPERTS), col(TOP_K), col(TOP_K), _full((N_EXPERTS, 1))],
        out_shape=[jax.ShapeDtypeStruct((n, D_MODEL), F32), jax.ShapeDtypeStruct((n * ROW_TILE, LANES), jnp.int32),
                   jax.ShapeDtypeStruct((N_EXPERTS, n), F32), jax.ShapeDtypeStruct((TOP_K, n), F32),
                   jax.ShapeDtypeStruct((TOP_K, n), F32), jax.ShapeDtypeStruct((N_EXPERTS, 1), F32)],
        compiler_params=_params(("arbitrary",)),
        name="post_mixer",
    )(x, attn_o, attn_lse, y_tb, w["w_glu"], w["b_glu"], w["w_gate"], w["b_gate"], w["w_attn_br"], w["w_ssm_br"],
      w["w_out"], w["ln1_g"], w["ln1_b"], w["w_router"], w["router_bias"])


def _moe_ffn_kernel(x_ref, gate_ref, p_ref, w13_ref, w2_ref, ws13_ref, ws2_ref, wpg_ref, wple_ref,
                    lng_ref, lnb_ref, o_ref, acc_sc, xb_sc):
    e = pl.program_id(1)

    def glu_ffn(xb, w13, w2, row_scale):
        h13 = jnp.dot(xb, w13, preferred_element_type=F32)
        h = jax.nn.silu(h13[:, :EXPERT_FF]) * h13[:, EXPERT_FF:]
        if row_scale is not None:
            h = h * row_scale
        return jnp.dot(h.astype(BF16), w2, preferred_element_type=F32)

    @pl.when(e == 0)
    def _():
        xb = x_ref[...].astype(BF16)
        xb_sc[...] = xb
        ple = (jax.nn.sigmoid(jnp.dot(xb, wpg_ref[...], preferred_element_type=F32))
               * jnp.dot(p_ref[...].astype(BF16), wple_ref[...], preferred_element_type=F32))
        acc_sc[...] = glu_ffn(xb, ws13_ref[...], ws2_ref[...], None) + ple

    gates = gate_ref[...]
    lane = lax.broadcasted_iota(jnp.int32, gates.shape, 1)
    g_col = jnp.sum(jnp.where(lane == e, gates, 0.0), axis=-1, keepdims=True)
    acc_sc[...] += glu_ffn(xb_sc[...], w13_ref[0], w2_ref[0], g_col)

    @pl.when(e == N_EXPERTS - 1)
    def _():
        o_ref[...] = _layer_norm(DN_ALPHA * x_ref[...] + acc_sc[...], lng_ref[...], lnb_ref[...])


def _moe_ffn(x1, gates, p, w, tm):
    n = x1.shape[0]
    row = lambda width: pl.BlockSpec((tm, width), lambda i, e: (i, 0))
    return pl.pallas_call(
        _moe_ffn_kernel,
        grid=(n // tm, N_EXPERTS),
        in_specs=[row(D_MODEL), row(N_EXPERTS), row(PLE_DIM),
                  pl.BlockSpec((1, D_MODEL, 2 * EXPERT_FF), lambda i, e: (e, 0, 0)),
                  pl.BlockSpec((1, EXPERT_FF, D_MODEL), lambda i, e: (e, 0, 0)),
                  _full((D_MODEL, 2 * EXPERT_FF)), _full((EXPERT_FF, D_MODEL)),
                  _full((D_MODEL, D_MODEL)), _full((PLE_DIM, D_MODEL)),
                  _full((1, D_MODEL)), _full((1, D_MODEL))],
        out_specs=row(D_MODEL),
        out_shape=jax.ShapeDtypeStruct((n, D_MODEL), F32),
        scratch_shapes=[pltpu.VMEM((tm, D_MODEL), F32), pltpu.VMEM((tm, D_MODEL), BF16)],
        compiler_params=_params(("parallel", "arbitrary")),
        name="moe_ffn",
    )(x1, gates, p, w["w13"], w["w2"], w["ws13"], w["ws2"], w["w_ple_gate"], w["w_ple"],
      w["ln2_g"], w["ln2_b"])


def _route_kernel(idx_ref, pstart_ref, earlier_ref, slot_ref, base_sc):
    @pl.when(pl.program_id(0) == 0)
    def _():
        base_sc[...] = jnp.zeros(base_sc.shape, F32)

    idx = idx_ref[...]
    tm = idx.shape[1]
    expert = lax.broadcasted_iota(jnp.int32, (N_EXPERTS, tm), 0).astype(F32)
    hits = [expert == idx[k:k + 1, :] for k in range(TOP_K)]
    member = jnp.zeros((N_EXPERTS, tm), F32)
    for hit in hits:
        member = member + jnp.where(hit, 1.0, 0.0)
    row = (jnp.dot(member.astype(BF16), earlier_ref[...], preferred_element_type=F32)
           + base_sc[...] + pstart_ref[...])
    slots = [jnp.sum(jnp.where(hit, row, 0.0), axis=0, keepdims=True) for hit in hits]
    slot_ref[...] = jnp.concatenate(slots, axis=0).astype(jnp.int32)
    base_sc[...] += jnp.sum(member, axis=1, keepdims=True)


def _route(top_idx, pstart, tm):
    n = top_idx.shape[1]
    earlier = jnp.triu(jnp.ones((tm, tm), F32), k=1).astype(BF16)
    return pl.pallas_call(
        _route_kernel,
        grid=(n // tm,),
        in_specs=[pl.BlockSpec((TOP_K, tm), lambda i: (0, i)), _full((N_EXPERTS, 1)), _full((tm, tm))],
        out_specs=pl.BlockSpec((TOP_K, tm), lambda i: (0, i)),
        out_shape=jax.ShapeDtypeStruct((TOP_K, n), jnp.int32),
        scratch_shapes=[pltpu.VMEM((N_EXPERTS, 1), F32)],
        compiler_params=_params(("arbitrary",)),
        name="route",
    )(top_idx, pstart, earlier)


def _sc_mesh():
    return plsc.VectorSubcoreMesh(core_axis_name="c", subcore_axis_name="s",
                                  num_cores=SC_CORES, num_subcores=SC_SUBCORES)


def _sc_dispatch(x_tiles, slots, n_rows):
    n = x_tiles.shape[0]
    wins_per_worker = n // SC_WINDOW // (SC_CORES * SC_SUBCORES)

    def body(x_hbm, slot_hbm, xs_hbm, idx_v, rows_v, sem):
        wid = lax.axis_index("s") * SC_CORES + lax.axis_index("c")

        @pl.loop(0, wins_per_worker)
        def _(i):
            win = wid * wins_per_worker + i
            pltpu.sync_copy(slot_hbm.at[win], idx_v)
            pltpu.sync_copy(x_hbm.at[pl.ds(win * SC_WINDOW, SC_WINDOW)], rows_v)
            copies = [pltpu.async_copy(rows_v, xs_hbm.at[idx_v.at[k]], sem) for k in range(TOP_K)]
            for copy in copies:
                copy.wait()

    return pl.kernel(
        body, out_type=jax.ShapeDtypeStruct((n_rows, ROW_TILE, LANES), jnp.int32), mesh=_sc_mesh(),
        scratch_types=[pltpu.VMEM((TOP_K, SC_WINDOW), jnp.int32),
                       pltpu.VMEM((SC_WINDOW, ROW_TILE, LANES), jnp.int32),
                       pltpu.SemaphoreType.DMA],
        name="sc_dispatch",
    )(x_tiles, slots)


def _sc_combine(y_tiles, slots, n):
    wins_per_worker = n // SC_WINDOW // (SC_CORES * SC_SUBCORES)

    def body(ys_hbm, slot_hbm, yg_hbm, idx_v, rows_a, rows_b, gather_sems, write_sems):
        wid = lax.axis_index("s") * SC_CORES + lax.axis_index("c")
        bufs = (rows_a, rows_b)

        @pl.loop(0, wins_per_worker)
        def _(i):
            win = wid * wins_per_worker + i
            pltpu.sync_copy(slot_hbm.at[win], idx_v)

            def gather(k):
                return pltpu.async_copy(ys_hbm.at[idx_v.at[k]], bufs[k % 2], gather_sems.at[k % 2])

            def write(k):
                return pltpu.async_copy(bufs[k % 2], yg_hbm.at[k, pl.ds(win * SC_WINDOW, SC_WINDOW)],
                                        write_sems.at[k % 2])

            gathers = {0: gather(0)}
            writes = {}
            for k in range(TOP_K):
                gathers[k].wait()
                if k + 1 < TOP_K:
                    if k >= 1:
                        writes[k - 1].wait()
                    gathers[k + 1] = gather(k + 1)
                writes[k] = write(k)
            writes[TOP_K - 2].wait()
            writes[TOP_K - 1].wait()

    return pl.kernel(
        body, out_type=jax.ShapeDtypeStruct((TOP_K, n, ROW_TILE, LANES), jnp.int32), mesh=_sc_mesh(),
        scratch_types=[pltpu.VMEM((TOP_K, SC_WINDOW), jnp.int32),
                       pltpu.VMEM((SC_WINDOW, ROW_TILE, LANES), jnp.int32),
                       pltpu.VMEM((SC_WINDOW, ROW_TILE, LANES), jnp.int32),
                       pltpu.SemaphoreType.DMA((2,)), pltpu.SemaphoreType.DMA((2,))],
        name="sc_combine",
    )(y_tiles, slots)


def _expert_ffn_kernel(bexp_ref, valid_ref, xs_ref, w13_ref, w2_ref, ys_ref):
    del bexp_ref
    valid = valid_ref[pl.program_id(0)]

    def ffn(r0, rows):
        x = jnp.concatenate(_load_packed_chunks(xs_ref, rows, row0=r0), axis=1)
        h13 = jnp.dot(x.astype(BF16), w13_ref[0], preferred_element_type=F32)
        h = jax.nn.silu(h13[:, :EXPERT_FF]) * h13[:, EXPERT_FF:]
        _store_packed_rows(ys_ref, jnp.dot(h.astype(BF16), w2_ref[0], preferred_element_type=F32), r0)

    @pl.when(valid == MOE_BLOCK)
    def _():
        ffn(0, MOE_BLOCK)

    @pl.when(jnp.logical_and(valid > 0, valid < MOE_BLOCK))
    def _():
        sub = MOE_BLOCK // EXPERT_FFN_SUBTILES
        for r0 in range(0, MOE_BLOCK, sub):
            pl.when(valid > r0)(functools.partial(ffn, r0, sub))


def _expert_ffn(xs_rows, block_expert, block_valid, w):
    n_blocks = block_expert.shape[0]
    blk = (MOE_BLOCK * ROW_TILE, LANES)
    return pl.pallas_call(
        _expert_ffn_kernel,
        grid_spec=pltpu.PrefetchScalarGridSpec(
            num_scalar_prefetch=2, grid=(n_blocks,),
            in_specs=[pl.BlockSpec(blk, lambda i, be, nu: (i, 0)),
                      pl.BlockSpec((1, D_MODEL, 2 * EXPERT_FF), lambda i, be, nu: (be[i], 0, 0)),
                      pl.BlockSpec((1, EXPERT_FF, D_MODEL), lambda i, be, nu: (be[i], 0, 0))],
            out_specs=pl.BlockSpec(blk, lambda i, be, nu: (i, 0))),
        out_shape=jax.ShapeDtypeStruct(xs_rows.shape, jnp.int32),
        compiler_params=_params(("parallel",)),
        name="expert_ffn",
    )(block_expert, block_valid, xs_rows, w["w13"], w["w2"])


def _moe_out_kernel(x_ref, g_ref, p_ref, yg_ref, ws13_ref, ws2_ref, wpg_ref, wple_ref, lng_ref, lnb_ref, o_ref):
    x = x_ref[...]
    xb = x.astype(BF16)
    tm = x.shape[0]
    g = g_ref[...]
    parts = None
    for k in range(TOP_K):
        chunks = [g[:, k:k + 1] * c for c in _load_packed_chunks(yg_ref, tm, lead=k)]
        parts = chunks if parts is None else [a + c for a, c in zip(parts, chunks)]
    routed = jnp.concatenate(parts, axis=1)
    h13 = jnp.dot(xb, ws13_ref[...], preferred_element_type=F32)
    h = jax.nn.silu(h13[:, :EXPERT_FF]) * h13[:, EXPERT_FF:]
    shared = jnp.dot(h.astype(BF16), ws2_ref[...], preferred_element_type=F32)
    ple = (jax.nn.sigmoid(jnp.dot(xb, wpg_ref[...], preferred_element_type=F32))
           * jnp.dot(p_ref[...].astype(BF16), wple_ref[...], preferred_element_type=F32))
    o_ref[...] = _layer_norm(DN_ALPHA * x + routed + shared + ple, lng_ref[...], lnb_ref[...])


def _moe_out(x1, top_gates, p, yg_rows, w, tm):
    n = x1.shape[0]
    row = lambda width: pl.BlockSpec((tm, width), lambda i: (i, 0))
    return pl.pallas_call(
        _moe_out_kernel,
        grid=(n // tm,),
        in_specs=[row(D_MODEL), row(TOP_K), row(PLE_DIM),
                  pl.BlockSpec((TOP_K, tm * ROW_TILE, LANES), lambda i: (0, i, 0)),
                  _full((D_MODEL, 2 * EXPERT_FF)), _full((EXPERT_FF, D_MODEL)),
                  _full((D_MODEL, D_MODEL)), _full((PLE_DIM, D_MODEL)),
                  _full((1, D_MODEL)), _full((1, D_MODEL))],
        out_specs=row(D_MODEL),
        out_shape=jax.ShapeDtypeStruct((n, D_MODEL), F32),
        compiler_params=_params(("parallel",)),
        name="moe_out",
    )(x1, top_gates, p, yg_rows, w["ws13"], w["ws2"], w["w_ple_gate"], w["w_ple"], w["ln2_g"], w["ln2_b"])


def _moe_sorted(x1, x1_tiles, top_idx, top_gates, counts, p, w):
    n = x1.shape[0]
    n_blocks = n * TOP_K // MOE_BLOCK + N_EXPERTS
    n_rows = n_blocks * MOE_BLOCK
    cnt = counts.reshape(N_EXPERTS).astype(jnp.int32)
    padded = (cnt + MOE_BLOCK - 1) // MOE_BLOCK * MOE_BLOCK
    pend = jnp.cumsum(padded)
    pstart = (pend - padded).astype(F32).reshape(N_EXPERTS, 1)
    block_start = jnp.arange(n_blocks, dtype=jnp.int32) * MOE_BLOCK
    block_expert = jnp.minimum(jnp.sum((pend[None, :] <= block_start[:, None]).astype(jnp.int32), axis=1),
                               N_EXPERTS - 1)
    real_end = pend - padded + cnt
    block_valid = jnp.clip(jnp.take(real_end, block_expert) - block_start, 0, MOE_BLOCK).astype(jnp.int32)
    slots = _route(top_idx, pstart, 512)
    slots = jnp.transpose(slots.reshape(TOP_K, n // SC_WINDOW, SC_WINDOW), (1, 0, 2))
    xs = _sc_dispatch(x1_tiles.reshape(n, ROW_TILE, LANES), slots, n_rows)
    ys = _expert_ffn(xs.reshape(n_rows * ROW_TILE, LANES), block_expert, block_valid, w)
    yg = _sc_combine(ys.reshape(n_rows, ROW_TILE, LANES), slots, n)
    return _moe_out(x1, top_gates.T, p, yg.reshape(TOP_K, n * ROW_TILE, LANES), w, 512)


def _kv_rows(k, v, batch, seq, keep, g):
    cols = slice(g * GROUP_WIDTH, (g + 1) * GROUP_WIDTH)
    shape = (batch, keep, HEADS_PER_GROUP, HEAD_DIM)
    k_g = k.reshape(batch, seq, ATTN_WIDTH)[:, seq - keep:, cols].reshape(shape)
    v_g = v.reshape(batch, seq, ATTN_WIDTH)[:, seq - keep:, cols].reshape(shape)
    return jnp.stack([k_g, v_g], axis=2)


def _layer_prompt(x, p, w, ssm):
    batch, seq, _ = x.shape
    n = batch * seq
    x2 = x.reshape(n, D_MODEL)
    tabs = _rope_tables(jnp.arange(seq, dtype=jnp.int32))
    q, k, v, u = _in_proj(x2, w["w_in"], tabs, seq, 512)
    attn_o, attn_lse = _attn_prompt(q, k, v, batch, seq)
    zeros = jnp.zeros((batch, SSM_LANES), F32)
    y_tb, h_re, h_im = _s5_scan(u.reshape(seq * batch, SSM_WIDTH), ssm, zeros, zeros, batch, 128)
    x1, x1_tiles, _, top_idx, top_gates, counts = _post_mixer(
        x2, attn_o, attn_lse, y_tb.reshape(seq, batch * SSM_WIDTH), w, seq, 512)
    y = _moe_sorted(x1, x1_tiles, top_idx, top_gates, counts, p.reshape(n, PLE_DIM), w)
    kv = [_kv_rows(k, v, batch, seq, min(win, seq), g) for g, (win, _) in enumerate(DILATION_GROUPS)]
    h_last = jnp.stack([h_re, h_im], axis=-1).reshape(batch, SSM_GROUPS, SSM_STATE, 2)
    return y.reshape(batch, seq, D_MODEL), kv, h_last


def _layer_sample(x, p, caches, state, w, ssm):
    batch, seq, _ = x.shape
    assert seq == 1
    x2 = x.reshape(batch, D_MODEL)
    tabs = _rope_tables(jnp.full((batch,), PAST_LEN, dtype=jnp.int32))
    q, k, v, u = _in_proj(x2, w["w_in"], tabs, batch, batch)
    attn_o, attn_lse = _attn_sample(q, k, v, caches, 2)
    h0 = state.reshape(batch, SSM_LANES, 2)
    y_tb, h_re, h_im = _s5_scan(u, ssm, h0[..., 0], h0[..., 1], batch, 1)
    x1, _, gates, _, _, _ = _post_mixer(x2, attn_o, attn_lse, y_tb, w, batch, batch)
    y = _moe_ffn(x1, gates.T, p.reshape(batch, PLE_DIM), w, batch)
    kv = [_kv_rows(k, v, batch, 1, 1, g) for g in range(len(DILATION_GROUPS))]
    h_last = jnp.stack([h_re, h_im], axis=-1).reshape(batch, SSM_GROUPS, SSM_STATE, 2)
    return y.reshape(batch, 1, D_MODEL), kv, h_last


def _hi_lo_rows(t):
    hi = t.astype(BF16)
    return jnp.concatenate([hi, (t - hi.astype(F32)).astype(BF16)], axis=1).T


def kernel(x_prompt, x_sample, cache_kv_w128, cache_kv_w512, cache_kv_w2048, state_ssm, p_prompt, p_sample,
           w_in, a_re, a_im, log_dt, b_re, b_im, c_re, c_im, d_skip, w_glu, b_glu, w_attn_br, w_ssm_br,
           w_gate, b_gate, w_out, ln1_g, ln1_b, w_router, router_bias, w1, w3, w2, ws1, ws3, ws2,
           w_ple_gate, w_ple, ln2_g, ln2_b):
    assert w_in.shape[0] == DEPTH == 1
    l = 0
    row = lambda t: t[l].reshape(1, -1)
    w = {
        "w_in": w_in[l].astype(BF16),
        "w_glu": w_glu[l].astype(BF16), "b_glu": row(b_glu),
        "w_gate": w_gate[l].astype(BF16), "b_gate": row(b_gate),
        "w_attn_br": w_attn_br[l].astype(BF16), "w_ssm_br": w_ssm_br[l].astype(BF16),
        "w_out": w_out[l].astype(BF16), "ln1_g": row(ln1_g), "ln1_b": row(ln1_b),
        "w_router": _hi_lo_rows(w_router[l]), "router_bias": router_bias[l].reshape(N_EXPERTS, 1),
        "w13": jnp.concatenate([w1[l], w3[l]], axis=-1).astype(BF16), "w2": w2[l].astype(BF16),
        "ws13": jnp.concatenate([ws1[l], ws3[l]], axis=-1).astype(BF16), "ws2": ws2[l].astype(BF16),
        "w_ple_gate": w_ple_gate[l].astype(BF16), "w_ple": w_ple[l].astype(BF16),
        "ln2_g": row(ln2_g), "ln2_b": row(ln2_b),
    }
    ssm = _s5_params(a_re[l], a_im[l], log_dt[l], b_re[l], b_im[l], c_re[l], c_im[l], d_skip[l])
    yp, kv_p, h_p = _layer_prompt(x_prompt, p_prompt[l], w, ssm)
    caches = (cache_kv_w128[l], cache_kv_w512[l], cache_kv_w2048[l])
    ys, kv_s, h_s = _layer_sample(x_sample, p_sample[l], caches, state_ssm[l], w, ssm)
    return (yp, ys, kv_p[0][None], kv_s[0][None], kv_p[1][None], kv_s[1][None],
            kv_p[2][None], kv_s[2][None], h_p[None], h_s[None])
```

```python
import functools
import math

import jax
import jax.numpy as jnp
from jax import lax
from jax.experimental import pallas as pl
from jax.experimental.pallas import tpu as pltpu
from jax.experimental.pallas import tpu_sc as plsc

F32 = jnp.float32
BF16 = jnp.bfloat16

D_MODEL = 1024
HEAD_DIM = 64
HEADS_PER_GROUP = 4
DILATION_GROUPS = ((128, 1), (512, 4), (2048, 16))
N_BACK = 128
GROUP_WIDTH = HEADS_PER_GROUP * HEAD_DIM
ATTN_WIDTH = 3 * GROUP_WIDTH
ROPE_THETA = 10000.0
SSM_WIDTH = 256
SSM_GROUP = 16
SSM_GROUPS = 16
SSM_STATE = 64
SSM_LANES = SSM_GROUPS * SSM_STATE
IN_WIDTH = 3 * ATTN_WIDTH + SSM_WIDTH
N_EXPERTS = 64
TOP_K = 8
EXPERT_FF = 256
ROUTED_SCALE = 2.5
PLE_DIM = 256
DEPTH = 1
PAST_LEN = 8192
DN_ALPHA = (2.0 * DEPTH) ** 0.25
LN_EPS = 1e-5

LANES = 128
ROW_TILE = D_MODEL // LANES // 2
SC_CORES = 2
SC_SUBCORES = 16
SC_WINDOW = 64
MOE_BLOCK = 1024
POST_MIXER_SUBTILES = 2
EXPERT_FFN_SUBTILES = 4
ATTN_CHUNK = 2048
ATTN_UNROLL = 8
VMEM_LIMIT = 56 * 1024 * 1024


def _params(semantics):
    return pltpu.CompilerParams(dimension_semantics=semantics, vmem_limit_bytes=VMEM_LIMIT)


def _full(shape):
    return pl.BlockSpec(shape, lambda *_: (0,) * len(shape))


def _in_proj_kernel(x_ref, w_ref, cos_ref, sina_ref, sinb_ref, q_ref, k_ref, v_ref, u_ref):
    xb = x_ref[...].astype(BF16)
    cos = cos_ref[...]
    sin_a = sina_ref[...]
    sin_b = sinb_ref[...]

    def rope_store(col0, out_ref, scale):
        t = jnp.dot(xb, w_ref[:, col0:col0 + ATTN_WIDTH], preferred_element_type=F32)
        for c in range(ATTN_WIDTH // LANES):
            xc = t[:, c * LANES:(c + 1) * LANES]
            r = xc * cos + pltpu.roll(xc, LANES - 32, 1) * sin_a + pltpu.roll(xc, 32, 1) * sin_b
            out_ref[:, c * LANES:(c + 1) * LANES] = r * scale if scale != 1.0 else r

    rope_store(0, q_ref, HEAD_DIM ** -0.5)
    rope_store(ATTN_WIDTH, k_ref, 1.0)
    v_ref[...] = jnp.dot(xb, w_ref[:, 2 * ATTN_WIDTH:3 * ATTN_WIDTH], preferred_element_type=F32)
    u_ref[...] = jnp.dot(xb, w_ref[:, 3 * ATTN_WIDTH:], preferred_element_type=F32)


def _in_proj(x, w_in_bf, rope_tabs, rows_per_seq, tm):
    n = x.shape[0]
    tiles_per_seq = rows_per_seq // tm
    n_seq = n // rows_per_seq
    tab_tiles = rope_tabs[0].shape[0] // tm
    tab_spec = pl.BlockSpec((tm, LANES), lambda i: (i % tab_tiles, 0))
    row_spec = pl.BlockSpec((tm, ATTN_WIDTH), lambda i: (i, 0))
    return pl.pallas_call(
        _in_proj_kernel,
        grid=(n // tm,),
        in_specs=[pl.BlockSpec((tm, D_MODEL), lambda i: (i, 0)), _full((D_MODEL, IN_WIDTH)),
                  tab_spec, tab_spec, tab_spec],
        out_specs=[row_spec, row_spec, row_spec,
                   pl.BlockSpec((tm, SSM_WIDTH), lambda i: (i % tiles_per_seq, i // tiles_per_seq))],
        out_shape=[jax.ShapeDtypeStruct((n, ATTN_WIDTH), F32)] * 3
        + [jax.ShapeDtypeStruct((rows_per_seq, n_seq * SSM_WIDTH), F32)],
        compiler_params=_params(("parallel",)),
        name="in_proj",
    )(x, w_in_bf, *rope_tabs)


def _rope_tables(pos):
    half = HEAD_DIM // 2
    inv = ROPE_THETA ** (-jnp.arange(half, dtype=F32) / half)
    ang = pos.astype(F32)[:, None] * inv[None, :]
    cos = jnp.tile(jnp.cos(ang), (1, LANES // half))
    sin = jnp.tile(jnp.sin(ang), (1, LANES // half))
    first_half = (jnp.arange(LANES) % HEAD_DIM) < half
    sin_a = jnp.where(first_half[None, :], -sin, 0.0)
    sin_b = jnp.where(first_half[None, :], 0.0, sin)
    return cos, sin_a, sin_b


def _band_attention(q, k, v, mask):
    head_of_lane = lax.broadcasted_iota(jnp.int32, (N_BACK, LANES), 1) // HEAD_DIM
    kb = k.astype(BF16)
    vb = v.astype(BF16)
    o = lse = None
    for h in range(LANES // HEAD_DIM):
        qh = jnp.where(head_of_lane == h, q, 0.0).astype(BF16)
        logits = lax.dot_general(qh, kb, (((1,), (1,)), ((), ())), preferred_element_type=F32) + mask
        m = jnp.max(logits, axis=1, keepdims=True)
        p = jnp.exp(logits - m)
        l = jnp.sum(p, axis=1, keepdims=True)
        o_h = jnp.dot(p.astype(BF16), vb, preferred_element_type=F32) * (1.0 / l)
        lse_h = jnp.broadcast_to(m + jnp.log(l), (N_BACK, LANES))
        o = o_h if o is None else jnp.where(head_of_lane == h, o_h, o)
        lse = lse_h if lse is None else jnp.where(head_of_lane == h, lse_h, lse)
    return o, lse


def _attn_prompt_kernel(q_ref, kp_ref, kc_ref, vp_ref, vc_ref, o_ref, lse_ref):
    c = pl.program_id(1)
    g = pl.program_id(3)
    ch = ATTN_CHUNK
    qi = lax.broadcasted_iota(jnp.int32, (N_BACK, 2 * N_BACK), 0)
    kj = lax.broadcasted_iota(jnp.int32, (N_BACK, 2 * N_BACK), 1)
    dist = qi + N_BACK - kj
    band = jnp.where(dist >= 0, jnp.where(dist <= N_BACK, 0.0, -jnp.inf), -jnp.inf)
    band_first = jnp.where(kj >= N_BACK, band, -jnp.inf)

    def group_body(d):
        span = N_BACK * d
        n_sub = ch // N_BACK

        def rows(start, size):
            return pl.ds(start, size) if d == 1 else pl.ds(start, size, stride=d)

        def store(q0, o, lse):
            o_ref[rows(q0, N_BACK), :] = o
            lse_ref[rows(q0, N_BACK), :] = lse

        def head_block(r, carry):
            k = jnp.concatenate([kp_ref[rows(ch - span + r, N_BACK), :], kc_ref[rows(r, N_BACK), :]], axis=0)
            v = jnp.concatenate([vp_ref[rows(ch - span + r, N_BACK), :], vc_ref[rows(r, N_BACK), :]], axis=0)
            mask = jnp.where(c == 0, band_first, band)
            store(r, *_band_attention(q_ref[rows(r, N_BACK), :], k, v, mask))
            return carry

        def inner_block(idx, carry):
            s = idx // d
            r = idx % d
            k0 = (s - 1) * span + r
            store(s * span + r, *_band_attention(q_ref[rows(s * span + r, N_BACK), :],
                                                 kc_ref[rows(k0, 2 * N_BACK), :],
                                                 vc_ref[rows(k0, 2 * N_BACK), :], band))
            return carry

        lax.fori_loop(0, d, head_block, 0, unroll=min(d, ATTN_UNROLL))
        if n_sub > d:
            trips = n_sub - d
            lax.fori_loop(d, n_sub, inner_block, 0,
                          unroll=max(u for u in range(1, ATTN_UNROLL + 1) if trips % u == 0))

    for gi, (_, d) in enumerate(DILATION_GROUPS):
        pl.when(g == gi)(functools.partial(group_body, d))


def _attn_prompt(q, k, v, batch, seq):
    ch = ATTN_CHUNK
    cps = seq // ch
    n = batch * seq
    pairs = GROUP_WIDTH // LANES
    cur = lambda b, c, hp, g: (b * cps + c, g * pairs + hp)
    prev = lambda b, c, hp, g: (b * cps + jnp.maximum(c - 1, 0), g * pairs + hp)
    blk = (ch, LANES)
    return pl.pallas_call(
        _attn_prompt_kernel,
        grid=(batch, cps, pairs, len(DILATION_GROUPS)),
        in_specs=[pl.BlockSpec(blk, cur), pl.BlockSpec(blk, prev), pl.BlockSpec(blk, cur),
                  pl.BlockSpec(blk, prev), pl.BlockSpec(blk, cur)],
        out_specs=[pl.BlockSpec(blk, cur), pl.BlockSpec(blk, cur)],
        out_shape=[jax.ShapeDtypeStruct((n, ATTN_WIDTH), F32)] * 2,
        compiler_params=_params(("parallel", "parallel", "parallel", "parallel")),
        name="attn_prompt",
    )(q, k, k, v, v)


def _attn_sample_kernel(q_ref, k_ref, v_ref, c0_ref, c1_ref, c2_ref, o_ref, lse_ref):
    bt = q_ref.shape[0]
    for b in range(bt):
        for g, (c_ref, (win, d)) in enumerate(zip((c0_ref, c1_ref, c2_ref), DILATION_GROUPS)):
            pos = lax.broadcasted_iota(jnp.int32, (1, win), 1)
            off_stride = (pos % d) != 0
            j0 = g * HEADS_PER_GROUP
            heads = range(HEADS_PER_GROUP)
            qs = [q_ref[b, :, j0 + h:j0 + h + 1] for h in heads]
            s_c = jnp.concatenate([jnp.sum(c_ref[b, 0, h] * qs[h], axis=0, keepdims=True) for h in heads],
                                  axis=0)
            s_c = jnp.where(off_stride, -jnp.inf, s_c)
            s_new = jnp.concatenate([jnp.sum(k_ref[b, :, j0 + h:j0 + h + 1] * qs[h], axis=0, keepdims=True)
                                     for h in heads], axis=0)
            m = jnp.maximum(jnp.max(s_c, axis=1, keepdims=True), s_new)
            p_c = jnp.exp(s_c - m)
            p_new = jnp.exp(s_new - m)
            l = jnp.sum(p_c, axis=1, keepdims=True) + p_new
            inv_l = 1.0 / l
            lse_ref[b, j0:j0 + HEADS_PER_GROUP, :] = m + jnp.log(l)
            for h in heads:
                num = (jnp.sum(c_ref[b, 1, h] * p_c[h:h + 1, :], axis=1, keepdims=True)
                       + p_new[h:h + 1, :] * v_ref[b, :, j0 + h:j0 + h + 1])
                o_ref[b, :, j0 + h:j0 + h + 1] = num * inv_l[h:h + 1, :]


def _attn_sample(q, k, v, caches, bt):
    b = q.shape[0]
    n_heads = ATTN_WIDTH // HEAD_DIM
    views, specs = [], []
    for cache, (win, d) in zip(caches, DILATION_GROUPS):
        assert cache.shape[1] == win == N_BACK * d
        views.append(jnp.transpose(cache, (0, 2, 3, 4, 1)))
        specs.append(pl.BlockSpec((bt, 2, HEADS_PER_GROUP, HEAD_DIM, win), lambda i: (i, 0, 0, 0, 0)))
    col_spec = pl.BlockSpec((bt, HEAD_DIM, n_heads), lambda i: (i, 0, 0))
    lse_spec = pl.BlockSpec((bt, n_heads, 1), lambda i: (i, 0, 0))
    cols = lambda t: jnp.transpose(t.reshape(b, n_heads, HEAD_DIM), (0, 2, 1))
    o, lse = pl.pallas_call(
        _attn_sample_kernel,
        grid=(b // bt,),
        in_specs=[col_spec, col_spec, col_spec] + specs,
        out_specs=[col_spec, lse_spec],
        out_shape=[jax.ShapeDtypeStruct((b, HEAD_DIM, n_heads), F32),
                   jax.ShapeDtypeStruct((b, n_heads, 1), F32)],
        compiler_params=_params(("parallel",)),
        name="attn_sample",
    )(cols(q), cols(k), cols(v), *views)
    o = jnp.transpose(o, (0, 2, 1)).reshape(b, ATTN_WIDTH)
    lse = jnp.broadcast_to(lse, (b, n_heads, HEAD_DIM)).reshape(b, ATTN_WIDTH)
    return o, lse


def _s5_scan_kernel(u_ref, bmat_ref, cmat_ref, are_ref, aim_ref, d_ref, h0re_ref, h0im_ref,
                    y_ref, hre_ref, him_ref, hist_sc, *, bg, steps):
    t_chunk = pl.program_id(0)

    @pl.when(t_chunk == 0)
    def _():
        hre_ref[...] = h0re_ref[...]
        him_ref[...] = h0im_ref[...]

    u = u_ref[...]
    hist_sc[...] = jnp.dot(u.astype(BF16), bmat_ref[...], preferred_element_type=F32)
    a_re = jnp.broadcast_to(are_ref[...], (bg, SSM_LANES))
    a_im = jnp.broadcast_to(aim_ref[...], (bg, SSM_LANES))

    def step(t, carry):
        h_re, h_im = carry
        rows = pl.ds(pl.multiple_of(t * bg, bg), bg)
        n_re = a_re * h_re - a_im * h_im + hist_sc[rows, 0:SSM_LANES]
        n_im = a_re * h_im + a_im * h_re + hist_sc[rows, SSM_LANES:2 * SSM_LANES]
        hist_sc[rows, 0:SSM_LANES] = n_re
        hist_sc[rows, SSM_LANES:2 * SSM_LANES] = n_im
        return n_re, n_im

    h_re, h_im = lax.fori_loop(0, steps, step, (hre_ref[...], him_ref[...]))
    hre_ref[...] = h_re
    him_ref[...] = h_im
    y_ref[...] = (jnp.dot(hist_sc[...].astype(BF16), cmat_ref[...], preferred_element_type=F32)
                  + d_ref[...] * u)


def _s5_scan(u_tb, ssm, h0_re, h0_im, bg, steps):
    rows = u_tb.shape[0]
    blk = steps * bg
    kern = functools.partial(_s5_scan_kernel, bg=bg, steps=steps)
    state_spec = _full((bg, SSM_LANES))
    return pl.pallas_call(
        kern,
        grid=(rows // blk,),
        in_specs=[pl.BlockSpec((blk, SSM_WIDTH), lambda i: (i, 0)),
                  _full((SSM_WIDTH, 2 * SSM_LANES)), _full((2 * SSM_LANES, SSM_WIDTH)),
                  _full((1, SSM_LANES)), _full((1, SSM_LANES)), _full((1, SSM_WIDTH)),
                  state_spec, state_spec],
        out_specs=[pl.BlockSpec((blk, SSM_WIDTH), lambda i: (i, 0)), state_spec, state_spec],
        out_shape=[jax.ShapeDtypeStruct((rows, SSM_WIDTH), F32),
                   jax.ShapeDtypeStruct((bg, SSM_LANES), F32), jax.ShapeDtypeStruct((bg, SSM_LANES), F32)],
        scratch_shapes=[pltpu.VMEM((blk, 2 * SSM_LANES), F32)],
        compiler_params=_params(("arbitrary",)),
        name="s5_scan",
    )(u_tb, ssm["bmat"], ssm["cmat"], ssm["a_re"], ssm["a_im"], ssm["d_skip"], h0_re, h0_im)


def _s5_params(a_re, a_im, log_dt, b_re, b_im, c_re, c_im, d_skip):
    dt = jnp.exp(log_dt)[:, None]
    mag = jnp.exp(a_re * dt)
    abar_re = mag * jnp.cos(a_im * dt)
    abar_im = mag * jnp.sin(a_im * dt)
    a2 = a_re * a_re + a_im * a_im
    nr = abar_re - 1.0
    coef_re = (nr * a_re + abar_im * a_im) / a2
    coef_im = (abar_im * a_re - nr * a_im) / a2
    bb_re = coef_re[..., None] * b_re - coef_im[..., None] * b_im
    bb_im = coef_re[..., None] * b_im + coef_im[..., None] * b_re
    eye = jnp.eye(SSM_GROUPS, dtype=F32)
    to_b = lambda t: jnp.einsum("gpc,gh->gchp", t, eye).reshape(SSM_WIDTH, SSM_LANES)
    to_c = lambda t: jnp.einsum("gcp,gh->gphc", t, eye).reshape(SSM_LANES, SSM_WIDTH)
    return {
        "bmat": jnp.concatenate([to_b(bb_re), to_b(bb_im)], axis=1).astype(BF16),
        "cmat": jnp.concatenate([to_c(c_re), -to_c(c_im)], axis=0).astype(BF16),
        "a_re": abar_re.reshape(1, SSM_LANES), "a_im": abar_im.reshape(1, SSM_LANES),
        "d_skip": d_skip.reshape(1, SSM_WIDTH),
    }


def _layer_norm(z, g, b):
    mu = jnp.mean(z, axis=-1, keepdims=True)
    zc = z - mu
    var = jnp.mean(zc * zc, axis=-1, keepdims=True)
    return zc * lax.rsqrt(var + LN_EPS) * g + b


def _merge_groups(o, lse):
    parts = [slice(g * GROUP_WIDTH, (g + 1) * GROUP_WIDTH) for g in range(len(DILATION_GROUPS))]
    top = lse[:, parts[0]]
    for cols in parts[1:]:
        top = jnp.maximum(top, lse[:, cols])
    num = den = None
    for cols in parts:
        w = jnp.exp(lse[:, cols] - top)
        num = w * o[:, cols] if num is None else num + w * o[:, cols]
        den = w if den is None else den + w
    return num / den


def _store_packed_rows(ref, x, row0=0):
    rows = x.shape[0]
    for j in range(ROW_TILE):
        lo = x[:, j * LANES:(j + 1) * LANES].astype(BF16).astype(F32)
        hi = x[:, (j + ROW_TILE) * LANES:(j + ROW_TILE + 1) * LANES].astype(BF16).astype(F32)
        word = (lax.bitcast_convert_type(lo, jnp.uint32) >> 16) | lax.bitcast_convert_type(hi, jnp.uint32)
        ref[pl.ds(row0 * ROW_TILE + j, rows, stride=ROW_TILE), :] = lax.bitcast_convert_type(word, jnp.int32)


def _load_packed_chunks(ref, rows, lead=None, row0=0):
    lows, highs = [], []
    for j in range(ROW_TILE):
        idx = (pl.ds(row0 * ROW_TILE + j, rows, stride=ROW_TILE), slice(None))
        word = lax.bitcast_convert_type(ref[idx] if lead is None else ref[(lead,) + idx], jnp.uint32)
        lows.append(lax.bitcast_convert_type(word << 16, F32))
        highs.append(lax.bitcast_convert_type(word & jnp.uint32(0xFFFF0000), F32))
    return lows + highs


def _post_mixer_kernel(x_ref, ao_ref, lse_ref, y_ref, wglu_ref, bglu_ref, wgate_ref, bgate_ref, wab_ref, wsb_ref,
                       wout_ref, lng_ref, lnb_ref, wr_ref, rb_ref,
                       x1_ref, x1t_ref, gate_ref, idx_ref, topg_ref, cnt_ref):
    @pl.when(pl.program_id(0) == 0)
    def _():
        cnt_ref[...] = jnp.zeros(cnt_ref.shape, F32)

    tm = x_ref.shape[0]
    sub = tm // POST_MIXER_SUBTILES if tm % (8 * POST_MIXER_SUBTILES) == 0 else tm
    for r0 in range(0, tm, sub):
        rows = slice(r0, r0 + sub)
        x = x_ref[rows, :]
        xb = x.astype(BF16)
        s = jax.nn.gelu(y_ref[rows, :])
        s = s * jax.nn.sigmoid(jnp.dot(s.astype(BF16), wglu_ref[...], preferred_element_type=F32) + bglu_ref[...])
        gates = jax.nn.sigmoid(jnp.dot(xb, wgate_ref[...], preferred_element_type=F32) + bgate_ref[...])
        attn_o = _merge_groups(ao_ref[rows, :], lse_ref[rows, :])
        attn_br = jnp.dot(attn_o.astype(BF16), wab_ref[...], preferred_element_type=F32)
        ssm_br = jnp.dot(s.astype(BF16), wsb_ref[...], preferred_element_type=F32)
        merged = gates[:, :D_MODEL] * attn_br + gates[:, D_MODEL:] * ssm_br
        mix = jnp.dot(merged.astype(BF16), wout_ref[...], preferred_element_type=F32)
        x1 = _layer_norm(DN_ALPHA * x + mix, lng_ref[...], lnb_ref[...])
        x1_ref[rows, :] = x1
        _store_packed_rows(x1t_ref, x1, r0)

        x1_hi = x1.astype(BF16)
        x1_lo = (x1 - x1_hi.astype(F32)).astype(BF16)
        prod = lax.dot_general(wr_ref[...], jnp.concatenate([x1_hi, x1_lo], axis=0),
                               (((1,), (1,)), ((), ())), preferred_element_type=F32)
        logits = ((prod[:N_EXPERTS, :sub] + prod[N_EXPERTS:, :sub] + prod[:N_EXPERTS, sub:])
                  + prod[N_EXPERTS:, sub:])
        scores = jax.nn.sigmoid(logits)
        sel = scores + rb_ref[...]
        expert = lax.broadcasted_iota(jnp.int32, sel.shape, 0).astype(F32)
        chosen = jnp.zeros(sel.shape, F32)
        idx_rows, score_rows = [], []
        for _ in range(TOP_K):
            top = jnp.max(sel, axis=0, keepdims=True)
            first = jnp.min(jnp.where(sel == top, expert, float(N_EXPERTS)), axis=0, keepdims=True)
            hit = expert == first
            chosen = jnp.where(hit, 1.0, chosen)
            sel = jnp.where(hit, -jnp.inf, sel)
            idx_rows.append(first)
            score_rows.append(jnp.sum(jnp.where(hit, scores, 0.0), axis=0, keepdims=True))
        norm = ROUTED_SCALE / jnp.sum(scores * chosen, axis=0, keepdims=True)
        gate_ref[:, rows] = scores * chosen * norm
        idx_ref[:, rows] = jnp.concatenate(idx_rows, axis=0)
        topg_ref[:, rows] = jnp.concatenate(score_rows, axis=0) * norm
        cnt_ref[...] += jnp.sum(chosen, axis=1, keepdims=True)


def _post_mixer(x, attn_o, attn_lse, y_tb, w, rows_per_seq, tm):
    n = x.shape[0]
    tiles_per_seq = rows_per_seq // tm
    row = lambda width: pl.BlockSpec((tm, width), lambda i: (i, 0))
    col = lambda height: pl.BlockSpec((height, tm), lambda i: (0, i))
    return pl.pallas_call(
        _post_mixer_kernel,
        grid=(n // tm,),
        in_specs=[row(D_MODEL), row(ATTN_WIDTH), row(ATTN_WIDTH),
                  pl.BlockSpec((tm, SSM_WIDTH), lambda i: (i % tiles_per_seq, i // tiles_per_seq)),
                  _full((SSM_WIDTH, SSM_WIDTH)), _full((1, SSM_WIDTH)),
                  _full((D_MODEL, 2 * D_MODEL)), _full((1, 2 * D_MODEL)),
                  _full((GROUP_WIDTH, D_MODEL)), _full((SSM_WIDTH, D_MODEL)), _full((D_MODEL, D_MODEL)),
                  _full((1, D_MODEL)), _full((1, D_MODEL)),
                  _full((2 * N_EXPERTS, D_MODEL)), _full((N_EXPERTS, 1))],
        out_specs=[row(D_MODEL), pl.BlockSpec((tm * ROW_TILE, LANES), lambda i: (i, 0)),
                   col(N_EXPERTS), col(TOP_K), col(TOP_K), _full((N_EXPERTS, 1))],
        out_shape=[jax.ShapeDtypeStruct((n, D_MODEL), F32), jax.ShapeDtypeStruct((n * ROW_TILE, LANES), jnp.int32),
                   jax.ShapeDtypeStruct((N_EXPERTS, n), F32), jax.ShapeDtypeStruct((TOP_K, n), F32),
                   jax.ShapeDtypeStruct((TOP_K, n), F32), jax.ShapeDtypeStruct((N_EXPERTS, 1), F32)],
        compiler_params=_params(("arbitrary",)),
        name="post_mixer",
    )(x, attn_o, attn_lse, y_tb, w["w_glu"], w["b_glu"], w["w_gate"], w["b_gate"], w["w_attn_br"], w["w_ssm_br"],
      w["w_out"], w["ln1_g"], w["ln1_b"], w["w_router"], w["router_bias"])


def _moe_ffn_kernel(x_ref, gate_ref, p_ref, w13_ref, w2_ref, ws13_ref, ws2_ref, wpg_ref, wple_ref,
                    lng_ref, lnb_ref, o_ref, acc_sc, xb_sc):
    e = pl.program_id(1)

    def glu_ffn(xb, w13, w2, row_scale):
        h13 = jnp.dot(xb, w13, preferred_element_type=F32)
        h = jax.nn.silu(h13[:, :EXPERT_FF]) * h13[:, EXPERT_FF:]
        if row_scale is not None:
            h = h * row_scale
        return jnp.dot(h.astype(BF16), w2, preferred_element_type=F32)

    @pl.when(e == 0)
    def _():
        xb = x_ref[...].astype(BF16)
        xb_sc[...] = xb
        ple = (jax.nn.sigmoid(jnp.dot(xb, wpg_ref[...], preferred_element_type=F32))
               * jnp.dot(p_ref[...].astype(BF16), wple_ref[...], preferred_element_type=F32))
        acc_sc[...] = glu_ffn(xb, ws13_ref[...], ws2_ref[...], None) + ple

    gates = gate_ref[...]
    lane = lax.broadcasted_iota(jnp.int32, gates.shape, 1)
    g_col = jnp.sum(jnp.where(lane == e, gates, 0.0), axis=-1, keepdims=True)
    acc_sc[...] += glu_ffn(xb_sc[...], w13_ref[0], w2_ref[0], g_col)

    @pl.when(e == N_EXPERTS - 1)
    def _():
        o_ref[...] = _layer_norm(DN_ALPHA * x_ref[...] + acc_sc[...], lng_ref[...], lnb_ref[...])


def _moe_ffn(x1, gates, p, w, tm):
    n = x1.shape[0]
    row = lambda width: pl.BlockSpec((tm, width), lambda i, e: (i, 0))
    return pl.pallas_call(
        _moe_ffn_kernel,
        grid=(n // tm, N_EXPERTS),
        in_specs=[row(D_MODEL), row(N_EXPERTS), row(PLE_DIM),
                  pl.BlockSpec((1, D_MODEL, 2 * EXPERT_FF), lambda i, e: (e, 0, 0)),
                  pl.BlockSpec((1, EXPERT_FF, D_MODEL), lambda i, e: (e, 0, 0)),
                  _full((D_MODEL, 2 * EXPERT_FF)), _full((EXPERT_FF, D_MODEL)),
                  _full((D_MODEL, D_MODEL)), _full((PLE_DIM, D_MODEL)),
                  _full((1, D_MODEL)), _full((1, D_MODEL))],
        out_specs=row(D_MODEL),
        out_shape=jax.ShapeDtypeStruct((n, D_MODEL), F32),
        scratch_shapes=[pltpu.VMEM((tm, D_MODEL), F32), pltpu.VMEM((tm, D_MODEL), BF16)],
        compiler_params=_params(("parallel", "arbitrary")),
        name="moe_ffn",
    )(x1, gates, p, w["w13"], w["w2"], w["ws13"], w["ws2"], w["w_ple_gate"], w["w_ple"],
      w["ln2_g"], w["ln2_b"])


def _route_kernel(idx_ref, pstart_ref, earlier_ref, slot_ref, base_sc):
    @pl.when(pl.program_id(0) == 0)
    def _():
        base_sc[...] = jnp.zeros(base_sc.shape, F32)

    idx = idx_ref[...]
    tm = idx.shape[1]
    expert = lax.broadcasted_iota(jnp.int32, (N_EXPERTS, tm), 0).astype(F32)
    hits = [expert == idx[k:k + 1, :] for k in range(TOP_K)]
    member = jnp.zeros((N_EXPERTS, tm), F32)
    for hit in hits:
        member = member + jnp.where(hit, 1.0, 0.0)
    row = (jnp.dot(member.astype(BF16), earlier_ref[...], preferred_element_type=F32)
           + base_sc[...] + pstart_ref[...])
    slots = [jnp.sum(jnp.where(hit, row, 0.0), axis=0, keepdims=True) for hit in hits]
    slot_ref[...] = jnp.concatenate(slots, axis=0).astype(jnp.int32)
    base_sc[...] += jnp.sum(member, axis=1, keepdims=True)


def _route(top_idx, pstart, tm):
    n = top_idx.shape[1]
    earlier = jnp.triu(jnp.ones((tm, tm), F32), k=1).astype(BF16)
    return pl.pallas_call(
        _route_kernel,
        grid=(n // tm,),
        in_specs=[pl.BlockSpec((TOP_K, tm), lambda i: (0, i)), _full((N_EXPERTS, 1)), _full((tm, tm))],
        out_specs=pl.BlockSpec((TOP_K, tm), lambda i: (0, i)),
        out_shape=jax.ShapeDtypeStruct((TOP_K, n), jnp.int32),
        scratch_shapes=[pltpu.VMEM((N_EXPERTS, 1), F32)],
        compiler_params=_params(("arbitrary",)),
        name="route",
    )(top_idx, pstart, earlier)


def _sc_mesh():
    return plsc.VectorSubcoreMesh(core_axis_name="c", subcore_axis_name="s",
                                  num_cores=SC_CORES, num_subcores=SC_SUBCORES)


def _sc_dispatch(x_tiles, slots, n_rows):
    n = x_tiles.shape[0]
    wins_per_worker = n // SC_WINDOW // (SC_CORES * SC_SUBCORES)

    def body(x_hbm, slot_hbm, xs_hbm, idx_v, rows_v, sem):
        wid = lax.axis_index("s") * SC_CORES + lax.axis_index("c")

        @pl.loop(0, wins_per_worker)
        def _(i):
            win = wid * wins_per_worker + i
            pltpu.sync_copy(slot_hbm.at[win], idx_v)
            pltpu.sync_copy(x_hbm.at[pl.ds(win * SC_WINDOW, SC_WINDOW)], rows_v)
            copies = [pltpu.async_copy(rows_v, xs_hbm.at[idx_v.at[k]], sem) for k in range(TOP_K)]
            for copy in copies:
                copy.wait()

    return pl.kernel(
        body, out_type=jax.ShapeDtypeStruct((n_rows, ROW_TILE, LANES), jnp.int32), mesh=_sc_mesh(),
        scratch_types=[pltpu.VMEM((TOP_K, SC_WINDOW), jnp.int32),
                       pltpu.VMEM((SC_WINDOW, ROW_TILE, LANES), jnp.int32),
                       pltpu.SemaphoreType.DMA],
        name="sc_dispatch",
    )(x_tiles, slots)


def _sc_combine(y_tiles, slots, n):
    wins_per_worker = n // SC_WINDOW // (SC_CORES * SC_SUBCORES)

    def body(ys_hbm, slot_hbm, yg_hbm, idx_v, rows_a, rows_b, gather_sems, write_sems):
        wid = lax.axis_index("s") * SC_CORES + lax.axis_index("c")
        bufs = (rows_a, rows_b)

        @pl.loop(0, wins_per_worker)
        def _(i):
            win = wid * wins_per_worker + i
            pltpu.sync_copy(slot_hbm.at[win], idx_v)

            def gather(k):
                return pltpu.async_copy(ys_hbm.at[idx_v.at[k]], bufs[k % 2], gather_sems.at[k % 2])

            def write(k):
                return pltpu.async_copy(bufs[k % 2], yg_hbm.at[k, pl.ds(win * SC_WINDOW, SC_WINDOW)],
                                        write_sems.at[k % 2])

            gathers = {0: gather(0)}
            writes = {}
            for k in range(TOP_K):
                gathers[k].wait()
                if k + 1 < TOP_K:
                    if k >= 1:
                        writes[k - 1].wait()
                    gathers[k + 1] = gather(k + 1)
                writes[k] = write(k)
            writes[TOP_K - 2].wait()
            writes[TOP_K - 1].wait()

    return pl.kernel(
        body, out_type=jax.ShapeDtypeStruct((TOP_K, n, ROW_TILE, LANES), jnp.int32), mesh=_sc_mesh(),
        scratch_types=[pltpu.VMEM((TOP_K, SC_WINDOW), jnp.int32),
                       pltpu.VMEM((SC_WINDOW, ROW_TILE, LANES), jnp.int32),
                       pltpu.VMEM((SC_WINDOW, ROW_TILE, LANES), jnp.int32),
                       pltpu.SemaphoreType.DMA((2,)), pltpu.SemaphoreType.DMA((2,))],
        name="sc_combine",
    )(y_tiles, slots)


def _expert_ffn_kernel(bexp_ref, valid_ref, xs_ref, w1_ref, w3_ref, w2_ref, anchor_ref, ys_ref, w13_sc, w2_sc):
    del anchor_ref
    i = pl.program_id(0)
    valid = valid_ref[i]

    @pl.when(jnp.logical_or(i == 0, bexp_ref[i] != bexp_ref[jnp.maximum(i - 1, 0)]))
    def _():
        w13_sc[:, :EXPERT_FF] = w1_ref[0].astype(BF16)
        w13_sc[:, EXPERT_FF:] = w3_ref[0].astype(BF16)
        w2_sc[...] = w2_ref[0].astype(BF16)

    def ffn(r0, rows):
        x = jnp.concatenate(_load_packed_chunks(xs_ref, rows, row0=r0), axis=1)
        h13 = jnp.dot(x.astype(BF16), w13_sc[...], preferred_element_type=F32)
        h = jax.nn.silu(h13[:, :EXPERT_FF]) * h13[:, EXPERT_FF:]
        _store_packed_rows(ys_ref, jnp.dot(h.astype(BF16), w2_sc[...], preferred_element_type=F32), r0)

    @pl.when(valid == MOE_BLOCK)
    def _():
        ffn(0, MOE_BLOCK)

    @pl.when(jnp.logical_and(valid > 0, valid < MOE_BLOCK))
    def _():
        sub = MOE_BLOCK // EXPERT_FFN_SUBTILES
        for r0 in range(0, MOE_BLOCK, sub):
            pl.when(valid > r0)(functools.partial(ffn, r0, sub))


def _expert_ffn(xs_rows, block_expert, block_valid, w, anchor):
    n_blocks = block_expert.shape[0]
    blk = (MOE_BLOCK * ROW_TILE, LANES)
    return pl.pallas_call(
        _expert_ffn_kernel,
        grid_spec=pltpu.PrefetchScalarGridSpec(
            num_scalar_prefetch=2, grid=(n_blocks,),
            in_specs=[pl.BlockSpec(blk, lambda i, be, nu: (i, 0)),
                      pl.BlockSpec((1, D_MODEL, EXPERT_FF), lambda i, be, nu: (be[i], 0, 0)),
                      pl.BlockSpec((1, D_MODEL, EXPERT_FF), lambda i, be, nu: (be[i], 0, 0)),
                      pl.BlockSpec((1, EXPERT_FF, D_MODEL), lambda i, be, nu: (be[i], 0, 0)),
                      pl.BlockSpec((8, LANES), lambda i, be, nu: (0, 0))],
            out_specs=pl.BlockSpec(blk, lambda i, be, nu: (i, 0)),
            scratch_shapes=[pltpu.VMEM((D_MODEL, 2 * EXPERT_FF), BF16), pltpu.VMEM((EXPERT_FF, D_MODEL), BF16)]),
        out_shape=jax.ShapeDtypeStruct(xs_rows.shape, jnp.int32),
        compiler_params=_params(("arbitrary",)),
        name="expert_ffn",
    )(block_expert, block_valid, xs_rows, w["w1_f32"], w["w3_f32"], w["w2_f32"], anchor)


def _moe_out_kernel(x_ref, g_ref, p_ref, yg_ref, ws13_ref, ws2_ref, wpg_ref, wple_ref, lng_ref, lnb_ref, o_ref):
    x = x_ref[...]
    xb = x.astype(BF16)
    tm = x.shape[0]
    g = g_ref[...]
    parts = None
    for k in range(TOP_K):
        chunks = [g[:, k:k + 1] * c for c in _load_packed_chunks(yg_ref, tm, lead=k)]
        parts = chunks if parts is None else [a + c for a, c in zip(parts, chunks)]
    routed = jnp.concatenate(parts, axis=1)
    h13 = jnp.dot(xb, ws13_ref[...], preferred_element_type=F32)
    h = jax.nn.silu(h13[:, :EXPERT_FF]) * h13[:, EXPERT_FF:]
    shared = jnp.dot(h.astype(BF16), ws2_ref[...], preferred_element_type=F32)
    ple = (jax.nn.sigmoid(jnp.dot(xb, wpg_ref[...], preferred_element_type=F32))
           * jnp.dot(p_ref[...].astype(BF16), wple_ref[...], preferred_element_type=F32))
    o_ref[...] = _layer_norm(DN_ALPHA * x + routed + shared + ple, lng_ref[...], lnb_ref[...])


def _moe_out(x1, top_gates, p, yg_rows, w, tm):
    n = x1.shape[0]
    row = lambda width: pl.BlockSpec((tm, width), lambda i: (i, 0))
    return pl.pallas_call(
        _moe_out_kernel,
        grid=(n // tm,),
        in_specs=[row(D_MODEL), row(TOP_K), row(PLE_DIM),
                  pl.BlockSpec((TOP_K, tm * ROW_TILE, LANES), lambda i: (0, i, 0)),
                  _full((D_MODEL, 2 * EXPERT_FF)), _full((EXPERT_FF, D_MODEL)),
                  _full((D_MODEL, D_MODEL)), _full((PLE_DIM, D_MODEL)),
                  _full((1, D_MODEL)), _full((1, D_MODEL))],
        out_specs=row(D_MODEL),
        out_shape=jax.ShapeDtypeStruct((n, D_MODEL), F32),
        compiler_params=_params(("parallel",)),
        name="moe_out",
    )(x1, top_gates, p, yg_rows, w["ws13"], w["ws2"], w["w_ple_gate"], w["w_ple"], w["ln2_g"], w["ln2_b"])


def _moe_sorted(x1, x1_tiles, top_idx, top_gates, counts, p, w, anchor):
    n = x1.shape[0]
    n_blocks = n * TOP_K // MOE_BLOCK + N_EXPERTS
    n_rows = n_blocks * MOE_BLOCK
    cnt = counts.reshape(N_EXPERTS).astype(jnp.int32)
    padded = (cnt + MOE_BLOCK - 1) // MOE_BLOCK * MOE_BLOCK
    pend = jnp.cumsum(padded)
    pstart = (pend - padded).astype(F32).reshape(N_EXPERTS, 1)
    block_start = jnp.arange(n_blocks, dtype=jnp.int32) * MOE_BLOCK
    block_expert = jnp.minimum(jnp.sum((pend[None, :] <= block_start[:, None]).astype(jnp.int32), axis=1),
                               N_EXPERTS - 1)
    real_end = pend - padded + cnt
    block_valid = jnp.clip(jnp.take(real_end, block_expert) - block_start, 0, MOE_BLOCK).astype(jnp.int32)
    slots = _route(top_idx, pstart, 512)
    slots = jnp.transpose(slots.reshape(TOP_K, n // SC_WINDOW, SC_WINDOW), (1, 0, 2))
    xs = _sc_dispatch(x1_tiles.reshape(n, ROW_TILE, LANES), slots, n_rows)
    ys = _expert_ffn(xs.reshape(n_rows * ROW_TILE, LANES), block_expert, block_valid, w, anchor)
    yg = _sc_combine(ys.reshape(n_rows, ROW_TILE, LANES), slots, n)
    return _moe_out(x1, top_gates.T, p, yg.reshape(TOP_K, n * ROW_TILE, LANES), w, 512)


def _kv_rows(k, v, batch, seq, keep, g):
    cols = slice(g * GROUP_WIDTH, (g + 1) * GROUP_WIDTH)
    shape = (batch, keep, HEADS_PER_GROUP, HEAD_DIM)
    k_g = k.reshape(batch, seq, ATTN_WIDTH)[:, seq - keep:, cols].reshape(shape)
    v_g = v.reshape(batch, seq, ATTN_WIDTH)[:, seq - keep:, cols].reshape(shape)
    return jnp.stack([k_g, v_g], axis=2)


def _layer_prompt(x, p, w, ssm, anchor):
    batch, seq, _ = x.shape
    n = batch * seq
    x2 = x.reshape(n, D_MODEL)
    tabs = _rope_tables(jnp.arange(seq, dtype=jnp.int32))
    q, k, v, u = _in_proj(x2, w["w_in"], tabs, seq, 512)
    attn_o, attn_lse = _attn_prompt(q, k, v, batch, seq)
    zeros = jnp.zeros((batch, SSM_LANES), F32)
    y_tb, h_re, h_im = _s5_scan(u.reshape(seq * batch, SSM_WIDTH), ssm, zeros, zeros, batch, 128)
    x1, x1_tiles, _, top_idx, top_gates, counts = _post_mixer(
        x2, attn_o, attn_lse, y_tb.reshape(seq, batch * SSM_WIDTH), w, seq, 512)
    y = _moe_sorted(x1, x1_tiles, top_idx, top_gates, counts, p.reshape(n, PLE_DIM), w, anchor)
    kv = [_kv_rows(k, v, batch, seq, min(win, seq), g) for g, (win, _) in enumerate(DILATION_GROUPS)]
    h_last = jnp.stack([h_re, h_im], axis=-1).reshape(batch, SSM_GROUPS, SSM_STATE, 2)
    return y.reshape(batch, seq, D_MODEL), kv, h_last


def _sample_attention(x, caches, w):
    batch, seq, _ = x.shape
    assert seq == 1
    x2 = x.reshape(batch, D_MODEL)
    tabs = _rope_tables(jnp.full((batch,), PAST_LEN, dtype=jnp.int32))
    q, k, v, u = _in_proj(x2, w["w_in"], tabs, batch, batch)
    attn_o, attn_lse = _attn_sample(q, k, v, caches, 2)
    return x2, k, v, u, attn_o, attn_lse


def _layer_sample(p, state, w, ssm, x2, k, v, u, attn_o, attn_lse):
    batch = x2.shape[0]
    h0 = state.reshape(batch, SSM_LANES, 2)
    y_tb, h_re, h_im = _s5_scan(u, ssm, h0[..., 0], h0[..., 1], batch, 1)
    x1, _, gates, _, _, _ = _post_mixer(x2, attn_o, attn_lse, y_tb, w, batch, batch)
    y = _moe_ffn(x1, gates.T, p.reshape(batch, PLE_DIM), w, batch)
    kv = [_kv_rows(k, v, batch, 1, 1, g) for g in range(len(DILATION_GROUPS))]
    h_last = jnp.stack([h_re, h_im], axis=-1).reshape(batch, SSM_GROUPS, SSM_STATE, 2)
    return y.reshape(batch, 1, D_MODEL), kv, h_last


def _hi_lo_rows(t):
    hi = t.astype(BF16)
    return jnp.concatenate([hi, (t - hi.astype(F32)).astype(BF16)], axis=1).T


def kernel(x_prompt, x_sample, cache_kv_w128, cache_kv_w512, cache_kv_w2048, state_ssm, p_prompt, p_sample,
           w_in, a_re, a_im, log_dt, b_re, b_im, c_re, c_im, d_skip, w_glu, b_glu, w_attn_br, w_ssm_br,
           w_gate, b_gate, w_out, ln1_g, ln1_b, w_router, router_bias, w1, w3, w2, ws1, ws3, ws2,
           w_ple_gate, w_ple, ln2_g, ln2_b):
    assert w_in.shape[0] == DEPTH == 1
    l = 0
    row = lambda t: t[l].reshape(1, -1)
    w = {
        "w_in": w_in[l].astype(BF16),
        "w_glu": w_glu[l].astype(BF16), "b_glu": row(b_glu),
        "w_gate": w_gate[l].astype(BF16), "b_gate": row(b_gate),
        "w_attn_br": w_attn_br[l].astype(BF16), "w_ssm_br": w_ssm_br[l].astype(BF16),
        "w_out": w_out[l].astype(BF16), "ln1_g": row(ln1_g), "ln1_b": row(ln1_b),
        "w_router": _hi_lo_rows(w_router[l]), "router_bias": router_bias[l].reshape(N_EXPERTS, 1),
        "w13": jnp.concatenate([w1[l], w3[l]], axis=-1).astype(BF16), "w2": w2[l].astype(BF16),
        "w1_f32": w1[l], "w3_f32": w3[l], "w2_f32": w2[l],
        "ws13": jnp.concatenate([ws1[l], ws3[l]], axis=-1).astype(BF16), "ws2": ws2[l].astype(BF16),
        "w_ple_gate": w_ple_gate[l].astype(BF16), "w_ple": w_ple[l].astype(BF16),
        "ln2_g": row(ln2_g), "ln2_b": row(ln2_b),
    }
    ssm = _s5_params(a_re[l], a_im[l], log_dt[l], b_re[l], b_im[l], c_re[l], c_im[l], d_skip[l])
    caches = (cache_kv_w128[l], cache_kv_w512[l], cache_kv_w2048[l])
    sample = _sample_attention(x_sample, caches, w)
    yp, kv_p, h_p = _layer_prompt(x_prompt, p_prompt[l], w, ssm, anchor=sample[-1][:8, :LANES])
    ys, kv_s, h_s = _layer_sample(p_sample[l], state_ssm[l], w, ssm, *sample)
    return (yp, ys, kv_p[0][None], kv_s[0][None], kv_p[1][None], kv_s[1][None],
            kv_p[2][None], kv_s[2][None], h_p[None], h_s[None])
```

```python
import functools
import math

import jax
import jax.numpy as jnp
from jax import lax
from jax.experimental import pallas as pl
from jax.experimental.pallas import tpu as pltpu
from jax.experimental.pallas import tpu_sc as plsc

F32 = jnp.float32
BF16 = jnp.bfloat16

D_MODEL = 1024
HEAD_DIM = 64
HEADS_PER_GROUP = 4
DILATION_GROUPS = ((128, 1), (512, 4), (2048, 16))
N_BACK = 128
GROUP_WIDTH = HEADS_PER_GROUP * HEAD_DIM
ATTN_WIDTH = 3 * GROUP_WIDTH
ROPE_THETA = 10000.0
SSM_WIDTH = 256
SSM_GROUP = 16
SSM_GROUPS = 16
SSM_STATE = 64
SSM_LANES = SSM_GROUPS * SSM_STATE
IN_WIDTH = 3 * ATTN_WIDTH + SSM_WIDTH
N_EXPERTS = 64
TOP_K = 8
EXPERT_FF = 256
ROUTED_SCALE = 2.5
PLE_DIM = 256
DEPTH = 1
PAST_LEN = 8192
DN_ALPHA = (2.0 * DEPTH) ** 0.25
LN_EPS = 1e-5

LANES = 128
ROW_TILE = D_MODEL // LANES // 2
SC_CORES = 2
SC_SUBCORES = 16
SC_WINDOW = 64
MOE_BLOCK = 1024
POST_MIXER_SUBTILES = 2
EXPERT_FFN_SUBTILES = 4
ATTN_CHUNK = 2048
ATTN_UNROLL = 8
VMEM_LIMIT = 56 * 1024 * 1024


def _params(semantics):
    return pltpu.CompilerParams(dimension_semantics=semantics, vmem_limit_bytes=VMEM_LIMIT)


def _full(shape):
    return pl.BlockSpec(shape, lambda *_: (0,) * len(shape))


def _in_proj_kernel(x_ref, w_ref, cos_ref, sina_ref, sinb_ref, q_ref, k_ref, v_ref, u_ref):
    xb = x_ref[...].astype(BF16)
    cos = cos_ref[...]
    sin_a = sina_ref[...]
    sin_b = sinb_ref[...]

    def rope_store(col0, out_ref, scale):
        t = jnp.dot(xb, w_ref[:, col0:col0 + ATTN_WIDTH], preferred_element_type=F32)
        for c in range(ATTN_WIDTH // LANES):
            xc = t[:, c * LANES:(c + 1) * LANES]
            r = xc * cos + pltpu.roll(xc, LANES - 32, 1) * sin_a + pltpu.roll(xc, 32, 1) * sin_b
            out_ref[:, c * LANES:(c + 1) * LANES] = r * scale if scale != 1.0 else r

    rope_store(0, q_ref, HEAD_DIM ** -0.5)
    rope_store(ATTN_WIDTH, k_ref, 1.0)
    v_ref[...] = jnp.dot(xb, w_ref[:, 2 * ATTN_WIDTH:3 * ATTN_WIDTH], preferred_element_type=F32)
    u_ref[...] = jnp.dot(xb, w_ref[:, 3 * ATTN_WIDTH:], preferred_element_type=F32)


def _in_proj(x, w_in_bf, rope_tabs, rows_per_seq, tm):
    n = x.shape[0]
    tiles_per_seq = rows_per_seq // tm
    n_seq = n // rows_per_seq
    tab_tiles = rope_tabs[0].shape[0] // tm
    tab_spec = pl.BlockSpec((tm, LANES), lambda i: (i % tab_tiles, 0))
    row_spec = pl.BlockSpec((tm, ATTN_WIDTH), lambda i: (i, 0))
    return pl.pallas_call(
        _in_proj_kernel,
        grid=(n // tm,),
        in_specs=[pl.BlockSpec((tm, D_MODEL), lambda i: (i, 0)), _full((D_MODEL, IN_WIDTH)),
                  tab_spec, tab_spec, tab_spec],
        out_specs=[row_spec, row_spec, row_spec,
                   pl.BlockSpec((tm, SSM_WIDTH), lambda i: (i % tiles_per_seq, i // tiles_per_seq))],
        out_shape=[jax.ShapeDtypeStruct((n, ATTN_WIDTH), F32)] * 3
        + [jax.ShapeDtypeStruct((rows_per_seq, n_seq * SSM_WIDTH), F32)],
        compiler_params=_params(("parallel",)),
        name="in_proj",
    )(x, w_in_bf, *rope_tabs)


def _rope_tables(pos):
    half = HEAD_DIM // 2
    inv = ROPE_THETA ** (-jnp.arange(half, dtype=F32) / half)
    ang = pos.astype(F32)[:, None] * inv[None, :]
    cos = jnp.tile(jnp.cos(ang), (1, LANES // half))
    sin = jnp.tile(jnp.sin(ang), (1, LANES // half))
    first_half = (jnp.arange(LANES) % HEAD_DIM) < half
    sin_a = jnp.where(first_half[None, :], -sin, 0.0)
    sin_b = jnp.where(first_half[None, :], 0.0, sin)
    return cos, sin_a, sin_b


def _band_attention(q, k, v, mask):
    head_of_lane = lax.broadcasted_iota(jnp.int32, (N_BACK, LANES), 1) // HEAD_DIM
    kb = k.astype(BF16)
    vb = v.astype(BF16)
    o = lse = None
    for h in range(LANES // HEAD_DIM):
        qh = jnp.where(head_of_lane == h, q, 0.0).astype(BF16)
        logits = lax.dot_general(qh, kb, (((1,), (1,)), ((), ())), preferred_element_type=F32) + mask
        m = jnp.max(logits, axis=1, keepdims=True)
        p = jnp.exp(logits - m)
        l = jnp.sum(p, axis=1, keepdims=True)
        o_h = jnp.dot(p.astype(BF16), vb, preferred_element_type=F32) * (1.0 / l)
        lse_h = jnp.broadcast_to(m + jnp.log(l), (N_BACK, LANES))
        o = o_h if o is None else jnp.where(head_of_lane == h, o_h, o)
        lse = lse_h if lse is None else jnp.where(head_of_lane == h, lse_h, lse)
    return o, lse


def _attn_prompt_kernel(q_ref, kp_ref, kc_ref, vp_ref, vc_ref, o_ref, lse_ref):
    c = pl.program_id(1)
    g = pl.program_id(3)
    ch = ATTN_CHUNK
    qi = lax.broadcasted_iota(jnp.int32, (N_BACK, 2 * N_BACK), 0)
    kj = lax.broadcasted_iota(jnp.int32, (N_BACK, 2 * N_BACK), 1)
    dist = qi + N_BACK - kj
    band = jnp.where(dist >= 0, jnp.where(dist <= N_BACK, 0.0, -jnp.inf), -jnp.inf)
    band_first = jnp.where(kj >= N_BACK, band, -jnp.inf)

    def group_body(d):
        span = N_BACK * d
        n_sub = ch // N_BACK

        def rows(start, size):
            return pl.ds(start, size) if d == 1 else pl.ds(start, size, stride=d)

        def store(q0, o, lse):
            o_ref[rows(q0, N_BACK), :] = o
            lse_ref[rows(q0, N_BACK), :] = lse

        def head_block(r, carry):
            k = jnp.concatenate([kp_ref[rows(ch - span + r, N_BACK), :], kc_ref[rows(r, N_BACK), :]], axis=0)
            v = jnp.concatenate([vp_ref[rows(ch - span + r, N_BACK), :], vc_ref[rows(r, N_BACK), :]], axis=0)
            mask = jnp.where(c == 0, band_first, band)
            store(r, *_band_attention(q_ref[rows(r, N_BACK), :], k, v, mask))
            return carry

        def inner_block(idx, carry):
            s = idx // d
            r = idx % d
            k0 = (s - 1) * span + r
            store(s * span + r, *_band_attention(q_ref[rows(s * span + r, N_BACK), :],
                                                 kc_ref[rows(k0, 2 * N_BACK), :],
                                                 vc_ref[rows(k0, 2 * N_BACK), :], band))
            return carry

        lax.fori_loop(0, d, head_block, 0, unroll=min(d, ATTN_UNROLL))
        if n_sub > d:
            trips = n_sub - d
            lax.fori_loop(d, n_sub, inner_block, 0,
                          unroll=max(u for u in range(1, ATTN_UNROLL + 1) if trips % u == 0))

    for gi, (_, d) in enumerate(DILATION_GROUPS):
        pl.when(g == gi)(functools.partial(group_body, d))


def _attn_prompt(q, k, v, batch, seq):
    ch = ATTN_CHUNK
    cps = seq // ch
    n = batch * seq
    pairs = GROUP_WIDTH // LANES
    cur = lambda b, c, hp, g: (b * cps + c, g * pairs + hp)
    prev = lambda b, c, hp, g: (b * cps + jnp.maximum(c - 1, 0), g * pairs + hp)
    blk = (ch, LANES)
    return pl.pallas_call(
        _attn_prompt_kernel,
        grid=(batch, cps, pairs, len(DILATION_GROUPS)),
        in_specs=[pl.BlockSpec(blk, cur), pl.BlockSpec(blk, prev), pl.BlockSpec(blk, cur),
                  pl.BlockSpec(blk, prev), pl.BlockSpec(blk, cur)],
        out_specs=[pl.BlockSpec(blk, cur), pl.BlockSpec(blk, cur)],
        out_shape=[jax.ShapeDtypeStruct((n, ATTN_WIDTH), F32)] * 2,
        compiler_params=_params(("parallel", "parallel", "parallel", "parallel")),
        name="attn_prompt",
    )(q, k, k, v, v)


def _attn_sample_kernel(q_ref, k_ref, v_ref, c0_ref, c1_ref, c2_ref, o_ref, lse_ref):
    bt = q_ref.shape[0]
    for b in range(bt):
        for g, (c_ref, (win, d)) in enumerate(zip((c0_ref, c1_ref, c2_ref), DILATION_GROUPS)):
            pos = lax.broadcasted_iota(jnp.int32, (1, win), 1)
            off_stride = (pos % d) != 0
            j0 = g * HEADS_PER_GROUP
            heads = range(HEADS_PER_GROUP)
            qs = [q_ref[b, :, j0 + h:j0 + h + 1] for h in heads]
            s_c = jnp.concatenate([jnp.sum(c_ref[b, 0, h] * qs[h], axis=0, keepdims=True) for h in heads],
                                  axis=0)
            s_c = jnp.where(off_stride, -jnp.inf, s_c)
            s_new = jnp.concatenate([jnp.sum(k_ref[b, :, j0 + h:j0 + h + 1] * qs[h], axis=0, keepdims=True)
                                     for h in heads], axis=0)
            m = jnp.maximum(jnp.max(s_c, axis=1, keepdims=True), s_new)
            p_c = jnp.exp(s_c - m)
            p_new = jnp.exp(s_new - m)
            l = jnp.sum(p_c, axis=1, keepdims=True) + p_new
            inv_l = 1.0 / l
            lse_ref[b, j0:j0 + HEADS_PER_GROUP, :] = m + jnp.log(l)
            for h in heads:
                num = (jnp.sum(c_ref[b, 1, h] * p_c[h:h + 1, :], axis=1, keepdims=True)
                       + p_new[h:h + 1, :] * v_ref[b, :, j0 + h:j0 + h + 1])
                o_ref[b, :, j0 + h:j0 + h + 1] = num * inv_l[h:h + 1, :]


def _attn_sample(q, k, v, caches, bt):
    b = q.shape[0]
    n_heads = ATTN_WIDTH // HEAD_DIM
    views, specs = [], []
    for cache, (win, d) in zip(caches, DILATION_GROUPS):
        assert cache.shape[1] == win == N_BACK * d
        views.append(jnp.transpose(cache, (0, 2, 3, 4, 1)))
        specs.append(pl.BlockSpec((bt, 2, HEADS_PER_GROUP, HEAD_DIM, win), lambda i: (i, 0, 0, 0, 0)))
    col_spec = pl.BlockSpec((bt, HEAD_DIM, n_heads), lambda i: (i, 0, 0))
    lse_spec = pl.BlockSpec((bt, n_heads, 1), lambda i: (i, 0, 0))
    cols = lambda t: jnp.transpose(t.reshape(b, n_heads, HEAD_DIM), (0, 2, 1))
    o, lse = pl.pallas_call(
        _attn_sample_kernel,
        grid=(b // bt,),
        in_specs=[col_spec, col_spec, col_spec] + specs,
        out_specs=[col_spec, lse_spec],
        out_shape=[jax.ShapeDtypeStruct((b, HEAD_DIM, n_heads), F32),
                   jax.ShapeDtypeStruct((b, n_heads, 1), F32)],
        compiler_params=_params(("parallel",)),
        name="attn_sample",
    )(cols(q), cols(k), cols(v), *views)
    o = jnp.transpose(o, (0, 2, 1)).reshape(b, ATTN_WIDTH)
    lse = jnp.broadcast_to(lse, (b, n_heads, HEAD_DIM)).reshape(b, ATTN_WIDTH)
    return o, lse


def _s5_scan_kernel(u_ref, bmat_ref, cmat_ref, are_ref, aim_ref, d_ref, h0re_ref, h0im_ref,
                    y_ref, hre_ref, him_ref, hist_sc, *, bg, steps):
    t_chunk = pl.program_id(0)

    @pl.when(t_chunk == 0)
    def _():
        hre_ref[...] = h0re_ref[...]
        him_ref[...] = h0im_ref[...]

    u = u_ref[...]
    hist_sc[...] = jnp.dot(u.astype(BF16), bmat_ref[...], preferred_element_type=F32)
    a_re = jnp.broadcast_to(are_ref[...], (bg, SSM_LANES))
    a_im = jnp.broadcast_to(aim_ref[...], (bg, SSM_LANES))

    def step(t, carry):
        h_re, h_im = carry
        rows = pl.ds(pl.multiple_of(t * bg, bg), bg)
        n_re = a_re * h_re - a_im * h_im + hist_sc[rows, 0:SSM_LANES]
        n_im = a_re * h_im + a_im * h_re + hist_sc[rows, SSM_LANES:2 * SSM_LANES]
        hist_sc[rows, 0:SSM_LANES] = n_re
        hist_sc[rows, SSM_LANES:2 * SSM_LANES] = n_im
        return n_re, n_im

    h_re, h_im = lax.fori_loop(0, steps, step, (hre_ref[...], him_ref[...]))
    hre_ref[...] = h_re
    him_ref[...] = h_im
    y_ref[...] = (jnp.dot(hist_sc[...].astype(BF16), cmat_ref[...], preferred_element_type=F32)
                  + d_ref[...] * u)


def _s5_scan(u_tb, ssm, h0_re, h0_im, bg, steps):
    rows = u_tb.shape[0]
    blk = steps * bg
    kern = functools.partial(_s5_scan_kernel, bg=bg, steps=steps)
    state_spec = _full((bg, SSM_LANES))
    return pl.pallas_call(
        kern,
        grid=(rows // blk,),
        in_specs=[pl.BlockSpec((blk, SSM_WIDTH), lambda i: (i, 0)),
                  _full((SSM_WIDTH, 2 * SSM_LANES)), _full((2 * SSM_LANES, SSM_WIDTH)),
                  _full((1, SSM_LANES)), _full((1, SSM_LANES)), _full((1, SSM_WIDTH)),
                  state_spec, state_spec],
        out_specs=[pl.BlockSpec((blk, SSM_WIDTH), lambda i: (i, 0)), state_spec, state_spec],
        out_shape=[jax.ShapeDtypeStruct((rows, SSM_WIDTH), F32),
                   jax.ShapeDtypeStruct((bg, SSM_LANES), F32), jax.ShapeDtypeStruct((bg, SSM_LANES), F32)],
        scratch_shapes=[pltpu.VMEM((blk, 2 * SSM_LANES), F32)],
        compiler_params=_params(("arbitrary",)),
        name="s5_scan",
    )(u_tb, ssm["bmat"], ssm["cmat"], ssm["a_re"], ssm["a_im"], ssm["d_skip"], h0_re, h0_im)


def _s5_params(a_re, a_im, log_dt, b_re, b_im, c_re, c_im, d_skip):
    dt = jnp.exp(log_dt)[:, None]
    mag = jnp.exp(a_re * dt)
    abar_re = mag * jnp.cos(a_im * dt)
    abar_im = mag * jnp.sin(a_im * dt)
    a2 = a_re * a_re + a_im * a_im
    nr = abar_re - 1.0
    coef_re = (nr * a_re + abar_im * a_im) / a2
    coef_im = (abar_im * a_re - nr * a_im) / a2
    bb_re = coef_re[..., None] * b_re - coef_im[..., None] * b_im
    bb_im = coef_re[..., None] * b_im + coef_im[..., None] * b_re
    eye = jnp.eye(SSM_GROUPS, dtype=F32)
    to_b = lambda t: jnp.einsum("gpc,gh->gchp", t, eye).reshape(SSM_WIDTH, SSM_LANES)
    to_c = lambda t: jnp.einsum("gcp,gh->gphc", t, eye).reshape(SSM_LANES, SSM_WIDTH)
    return {
        "bmat": jnp.concatenate([to_b(bb_re), to_b(bb_im)], axis=1).astype(BF16),
        "cmat": jnp.concatenate([to_c(c_re), -to_c(c_im)], axis=0).astype(BF16),
        "a_re": abar_re.reshape(1, SSM_LANES), "a_im": abar_im.reshape(1, SSM_LANES),
        "d_skip": d_skip.reshape(1, SSM_WIDTH),
    }


def _layer_norm(z, g, b):
    mu = jnp.mean(z, axis=-1, keepdims=True)
    zc = z - mu
    var = jnp.mean(zc * zc, axis=-1, keepdims=True)
    return zc * lax.rsqrt(var + LN_EPS) * g + b


def _merge_groups(o, lse):
    parts = [slice(g * GROUP_WIDTH, (g + 1) * GROUP_WIDTH) for g in range(len(DILATION_GROUPS))]
    top = lse[:, parts[0]]
    for cols in parts[1:]:
        top = jnp.maximum(top, lse[:, cols])
    num = den = None
    for cols in parts:
        w = jnp.exp(lse[:, cols] - top)
        num = w * o[:, cols] if num is None else num + w * o[:, cols]
        den = w if den is None else den + w
    return num / den


def _store_packed_rows(ref, x, row0=0):
    rows = x.shape[0]
    for j in range(ROW_TILE):
        lo = x[:, j * LANES:(j + 1) * LANES].astype(BF16).astype(F32)
        hi = x[:, (j + ROW_TILE) * LANES:(j + ROW_TILE + 1) * LANES].astype(BF16).astype(F32)
        word = (lax.bitcast_convert_type(lo, jnp.uint32) >> 16) | lax.bitcast_convert_type(hi, jnp.uint32)
        ref[pl.ds(row0 * ROW_TILE + j, rows, stride=ROW_TILE), :] = lax.bitcast_convert_type(word, jnp.int32)


def _load_packed_chunks(ref, rows, lead=None, row0=0):
    lows, highs = [], []
    for j in range(ROW_TILE):
        idx = (pl.ds(row0 * ROW_TILE + j, rows, stride=ROW_TILE), slice(None))
        word = lax.bitcast_convert_type(ref[idx] if lead is None else ref[(lead,) + idx], jnp.uint32)
        lows.append(lax.bitcast_convert_type(word << 16, F32))
        highs.append(lax.bitcast_convert_type(word & jnp.uint32(0xFFFF0000), F32))
    return lows + highs


def _post_mixer_kernel(x_ref, ao_ref, lse_ref, y_ref, wglu_ref, bglu_ref, wgate_ref, bgate_ref, wab_ref, wsb_ref,
                       wout_ref, lng_ref, lnb_ref, wr_ref, rb_ref,
                       x1_ref, x1t_ref, gate_ref, idx_ref, topg_ref, cnt_ref):
    @pl.when(pl.program_id(0) == 0)
    def _():
        cnt_ref[...] = jnp.zeros(cnt_ref.shape, F32)

    tm = x_ref.shape[0]
    sub = tm // POST_MIXER_SUBTILES if tm % (8 * POST_MIXER_SUBTILES) == 0 else tm
    for r0 in range(0, tm, sub):
        rows = slice(r0, r0 + sub)
        x = x_ref[rows, :]
        xb = x.astype(BF16)
        s = jax.nn.gelu(y_ref[rows, :])
        s = s * jax.nn.sigmoid(jnp.dot(s.astype(BF16), wglu_ref[...], preferred_element_type=F32) + bglu_ref[...])
        gates = jax.nn.sigmoid(jnp.dot(xb, wgate_ref[...], preferred_element_type=F32) + bgate_ref[...])
        attn_o = _merge_groups(ao_ref[rows, :], lse_ref[rows, :])
        attn_br = jnp.dot(attn_o.astype(BF16), wab_ref[...], preferred_element_type=F32)
        ssm_br = jnp.dot(s.astype(BF16), wsb_ref[...], preferred_element_type=F32)
        merged = gates[:, :D_MODEL] * attn_br + gates[:, D_MODEL:] * ssm_br
        mix = jnp.dot(merged.astype(BF16), wout_ref[...], preferred_element_type=F32)
        x1 = _layer_norm(DN_ALPHA * x + mix, lng_ref[...], lnb_ref[...])
        x1_ref[rows, :] = x1
        _store_packed_rows(x1t_ref, x1, r0)

        x1_hi = x1.astype(BF16)
        x1_lo = (x1 - x1_hi.astype(F32)).astype(BF16)
        prod = lax.dot_general(wr_ref[...], jnp.concatenate([x1_hi, x1_lo], axis=0),
                               (((1,), (1,)), ((), ())), preferred_element_type=F32)
        logits = ((prod[:N_EXPERTS, :sub] + prod[N_EXPERTS:, :sub] + prod[:N_EXPERTS, sub:])
                  + prod[N_EXPERTS:, sub:])
        scores = jax.nn.sigmoid(logits)
        sel = scores + rb_ref[...]
        expert = lax.broadcasted_iota(jnp.int32, sel.shape, 0).astype(F32)
        chosen = jnp.zeros(sel.shape, F32)
        idx_rows, score_rows = [], []
        for _ in range(TOP_K):
            top = jnp.max(sel, axis=0, keepdims=True)
            first = jnp.min(jnp.where(sel == top, expert, float(N_EXPERTS)), axis=0, keepdims=True)
            hit = expert == first
            chosen = jnp.where(hit, 1.0, chosen)
            sel = jnp.where(hit, -jnp.inf, sel)
            idx_rows.append(first)
            score_rows.append(jnp.sum(jnp.where(hit, scores, 0.0), axis=0, keepdims=True))
        norm = ROUTED_SCALE / jnp.sum(scores * chosen, axis=0, keepdims=True)
        gate_ref[:, rows] = scores * chosen * norm
        idx_ref[:, rows] = jnp.concatenate(idx_rows, axis=0)
        topg_ref[:, rows] = jnp.concatenate(score_rows, axis=0) * norm
        cnt_ref[...] += jnp.sum(chosen, axis=1, keepdims=True)


def _post_mixer(x, attn_o, attn_lse, y_tb, w, rows_per_seq, tm):
    n = x.shape[0]
    tiles_per_seq = rows_per_seq // tm
    row = lambda width: pl.BlockSpec((tm, width), lambda i: (i, 0))
    col = lambda height: pl.BlockSpec((height, tm), lambda i: (0, i))
    return pl.pallas_call(
        _post_mixer_kernel,
        grid=(n // tm,),
        in_specs=[row(D_MODEL), row(ATTN_WIDTH), row(ATTN_WIDTH),
                  pl.BlockSpec((tm, SSM_WIDTH), lambda i: (i % tiles_per_seq, i // tiles_per_seq)),
                  _full((SSM_WIDTH, SSM_WIDTH)), _full((1, SSM_WIDTH)),
                  _full((D_MODEL, 2 * D_MODEL)), _full((1, 2 * D_MODEL)),
                  _full((GROUP_WIDTH, D_MODEL)), _full((SSM_WIDTH, D_MODEL)), _full((D_MODEL, D_MODEL)),
                  _full((1, D_MODEL)), _full((1, D_MODEL)),
                  _full((2 * N_EXPERTS, D_MODEL)), _full((N_EXPERTS, 1))],
        out_specs=[row(D_MODEL), pl.BlockSpec((tm * ROW_TILE, LANES), lambda i: (i, 0)),
                   col(N_EXPERTS), col(TOP_K), col(TOP_K), _full((N_EXPERTS, 1))],
        out_shape=[jax.ShapeDtypeStruct((n, D_MODEL), F32), jax.ShapeDtypeStruct((n * ROW_TILE, LANES), jnp.int32),
                   jax.ShapeDtypeStruct((N_EXPERTS, n), F32), jax.ShapeDtypeStruct((TOP_K, n), F32),
                   jax.ShapeDtypeStruct((TOP_K, n), F32), jax.ShapeDtypeStruct((N_EXPERTS, 1), F32)],
        compiler_params=_params(("arbitrary",)),
        name="post_mixer",
    )(x, attn_o, attn_lse, y_tb, w["w_glu"], w["b_glu"], w["w_gate"], w["b_gate"], w["w_attn_br"], w["w_ssm_br"],
      w["w_out"], w["ln1_g"], w["ln1_b"], w["w_router"], w["router_bias"])


def _moe_ffn_kernel(x_ref, gate_ref, p_ref, w1_ref, w3_ref, w2_ref, ws13_ref, ws2_ref, wpg_ref, wple_ref,
                    lng_ref, lnb_ref, o_ref, acc_sc, xb_sc):
    e = pl.program_id(1)

    def glu_ffn(xb, w13, w2, row_scale):
        if isinstance(w13, tuple):
            h1 = jnp.dot(xb, w13[0].astype(BF16), preferred_element_type=F32)
            h3 = jnp.dot(xb, w13[1].astype(BF16), preferred_element_type=F32)
        else:
            h13 = jnp.dot(xb, w13, preferred_element_type=F32)
            h1, h3 = h13[:, :EXPERT_FF], h13[:, EXPERT_FF:]
        h = jax.nn.silu(h1) * h3
        if row_scale is not None:
            h = h * row_scale
        return jnp.dot(h.astype(BF16), w2.astype(BF16), preferred_element_type=F32)

    @pl.when(e == 0)
    def _():
        xb = x_ref[...].astype(BF16)
        xb_sc[...] = xb
        ple = (jax.nn.sigmoid(jnp.dot(xb, wpg_ref[...], preferred_element_type=F32))
               * jnp.dot(p_ref[...].astype(BF16), wple_ref[...], preferred_element_type=F32))
        acc_sc[...] = glu_ffn(xb, ws13_ref[...], ws2_ref[...], None) + ple

    gates = gate_ref[...]
    lane = lax.broadcasted_iota(jnp.int32, gates.shape, 1)
    g_col = jnp.sum(jnp.where(lane == e, gates, 0.0), axis=-1, keepdims=True)
    acc_sc[...] += glu_ffn(xb_sc[...], (w1_ref[0], w3_ref[0]), w2_ref[0], g_col)

    @pl.when(e == N_EXPERTS - 1)
    def _():
        o_ref[...] = _layer_norm(DN_ALPHA * x_ref[...] + acc_sc[...], lng_ref[...], lnb_ref[...])


def _moe_ffn(x1, gates, p, w, tm):
    n = x1.shape[0]
    row = lambda width: pl.BlockSpec((tm, width), lambda i, e: (i, 0))
    return pl.pallas_call(
        _moe_ffn_kernel,
        grid=(n // tm, N_EXPERTS),
        in_specs=[row(D_MODEL), row(N_EXPERTS), row(PLE_DIM),
                  pl.BlockSpec((1, D_MODEL, EXPERT_FF), lambda i, e: (e, 0, 0)),
                  pl.BlockSpec((1, D_MODEL, EXPERT_FF), lambda i, e: (e, 0, 0)),
                  pl.BlockSpec((1, EXPERT_FF, D_MODEL), lambda i, e: (e, 0, 0)),
                  _full((D_MODEL, 2 * EXPERT_FF)), _full((EXPERT_FF, D_MODEL)),
                  _full((D_MODEL, D_MODEL)), _full((PLE_DIM, D_MODEL)),
                  _full((1, D_MODEL)), _full((1, D_MODEL))],
        out_specs=row(D_MODEL),
        out_shape=jax.ShapeDtypeStruct((n, D_MODEL), F32),
        scratch_shapes=[pltpu.VMEM((tm, D_MODEL), F32), pltpu.VMEM((tm, D_MODEL), BF16)],
        compiler_params=_params(("parallel", "arbitrary")),
        name="moe_ffn",
    )(x1, gates, p, w["w1"], w["w3"], w["w2"], w["ws13"], w["ws2"], w["w_ple_gate"], w["w_ple"],
      w["ln2_g"], w["ln2_b"])


def _route_kernel(idx_ref, pstart_ref, earlier_ref, slot_ref, base_sc):
    @pl.when(pl.program_id(0) == 0)
    def _():
        base_sc[...] = jnp.zeros(base_sc.shape, F32)

    idx = idx_ref[...]
    tm = idx.shape[1]
    expert = lax.broadcasted_iota(jnp.int32, (N_EXPERTS, tm), 0).astype(F32)
    hits = [expert == idx[k:k + 1, :] for k in range(TOP_K)]
    member = jnp.zeros((N_EXPERTS, tm), F32)
    for hit in hits:
        member = member + jnp.where(hit, 1.0, 0.0)
    row = (jnp.dot(member.astype(BF16), earlier_ref[...], preferred_element_type=F32)
           + base_sc[...] + pstart_ref[...])
    slots = [jnp.sum(jnp.where(hit, row, 0.0), axis=0, keepdims=True) for hit in hits]
    slot_ref[...] = jnp.concatenate(slots, axis=0).astype(jnp.int32)
    base_sc[...] += jnp.sum(member, axis=1, keepdims=True)


def _route(top_idx, pstart, tm):
    n = top_idx.shape[1]
    earlier = jnp.triu(jnp.ones((tm, tm), F32), k=1).astype(BF16)
    return pl.pallas_call(
        _route_kernel,
        grid=(n // tm,),
        in_specs=[pl.BlockSpec((TOP_K, tm), lambda i: (0, i)), _full((N_EXPERTS, 1)), _full((tm, tm))],
        out_specs=pl.BlockSpec((TOP_K, tm), lambda i: (0, i)),
        out_shape=jax.ShapeDtypeStruct((TOP_K, n), jnp.int32),
        scratch_shapes=[pltpu.VMEM((N_EXPERTS, 1), F32)],
        compiler_params=_params(("arbitrary",)),
        name="route",
    )(top_idx, pstart, earlier)


def _sc_mesh():
    return plsc.VectorSubcoreMesh(core_axis_name="c", subcore_axis_name="s",
                                  num_cores=SC_CORES, num_subcores=SC_SUBCORES)


def _sc_dispatch(x_tiles, slots, n_rows):
    n = x_tiles.shape[0]
    wins_per_worker = n // SC_WINDOW // (SC_CORES * SC_SUBCORES)

    def body(x_hbm, slot_hbm, xs_hbm, idx_v, rows_v, sem):
        wid = lax.axis_index("s") * SC_CORES + lax.axis_index("c")

        @pl.loop(0, wins_per_worker)
        def _(i):
            win = wid * wins_per_worker + i
            pltpu.sync_copy(slot_hbm.at[win], idx_v)
            pltpu.sync_copy(x_hbm.at[pl.ds(win * SC_WINDOW, SC_WINDOW)], rows_v)
            copies = [pltpu.async_copy(rows_v, xs_hbm.at[idx_v.at[k]], sem) for k in range(TOP_K)]
            for copy in copies:
                copy.wait()

    return pl.kernel(
        body, out_type=jax.ShapeDtypeStruct((n_rows, ROW_TILE, LANES), jnp.int32), mesh=_sc_mesh(),
        scratch_types=[pltpu.VMEM((TOP_K, SC_WINDOW), jnp.int32),
                       pltpu.VMEM((SC_WINDOW, ROW_TILE, LANES), jnp.int32),
                       pltpu.SemaphoreType.DMA],
        name="sc_dispatch",
    )(x_tiles, slots)


def _sc_combine(y_tiles, slots, n):
    wins_per_worker = n // SC_WINDOW // (SC_CORES * SC_SUBCORES)

    def body(ys_hbm, slot_hbm, yg_hbm, idx_v, rows_a, rows_b, gather_sems, write_sems):
        wid = lax.axis_index("s") * SC_CORES + lax.axis_index("c")
        bufs = (rows_a, rows_b)

        @pl.loop(0, wins_per_worker)
        def _(i):
            win = wid * wins_per_worker + i
            pltpu.sync_copy(slot_hbm.at[win], idx_v)

            def gather(k):
                return pltpu.async_copy(ys_hbm.at[idx_v.at[k]], bufs[k % 2], gather_sems.at[k % 2])

            def write(k):
                return pltpu.async_copy(bufs[k % 2], yg_hbm.at[k, pl.ds(win * SC_WINDOW, SC_WINDOW)],
                                        write_sems.at[k % 2])

            gathers = {0: gather(0)}
            writes = {}
            for k in range(TOP_K):
                gathers[k].wait()
                if k + 1 < TOP_K:
                    if k >= 1:
                        writes[k - 1].wait()
                    gathers[k + 1] = gather(k + 1)
                writes[k] = write(k)
            writes[TOP_K - 2].wait()
            writes[TOP_K - 1].wait()

    return pl.kernel(
        body, out_type=jax.ShapeDtypeStruct((TOP_K, n, ROW_TILE, LANES), jnp.int32), mesh=_sc_mesh(),
        scratch_types=[pltpu.VMEM((TOP_K, SC_WINDOW), jnp.int32),
                       pltpu.VMEM((SC_WINDOW, ROW_TILE, LANES), jnp.int32),
                       pltpu.VMEM((SC_WINDOW, ROW_TILE, LANES), jnp.int32),
                       pltpu.SemaphoreType.DMA((2,)), pltpu.SemaphoreType.DMA((2,))],
        name="sc_combine",
    )(y_tiles, slots)


def _expert_ffn_kernel(bexp_ref, valid_ref, xs_ref, w1_ref, w3_ref, w2_ref, anchor_ref, ys_ref, w13_sc, w2_sc):
    del anchor_ref
    i = pl.program_id(0)
    valid = valid_ref[i]

    @pl.when(jnp.logical_or(i == 0, bexp_ref[i] != bexp_ref[jnp.maximum(i - 1, 0)]))
    def _():
        w13_sc[:, :EXPERT_FF] = w1_ref[0].astype(BF16)
        w13_sc[:, EXPERT_FF:] = w3_ref[0].astype(BF16)
        w2_sc[...] = w2_ref[0].astype(BF16)

    def ffn(r0, rows):
        x = jnp.concatenate(_load_packed_chunks(xs_ref, rows, row0=r0), axis=1)
        h13 = jnp.dot(x.astype(BF16), w13_sc[...], preferred_element_type=F32)
        h = jax.nn.silu(h13[:, :EXPERT_FF]) * h13[:, EXPERT_FF:]
        _store_packed_rows(ys_ref, jnp.dot(h.astype(BF16), w2_sc[...], preferred_element_type=F32), r0)

    @pl.when(valid == MOE_BLOCK)
    def _():
        ffn(0, MOE_BLOCK)

    @pl.when(jnp.logical_and(valid > 0, valid < MOE_BLOCK))
    def _():
        sub = MOE_BLOCK // EXPERT_FFN_SUBTILES
        for r0 in range(0, MOE_BLOCK, sub):
            pl.when(valid > r0)(functools.partial(ffn, r0, sub))


def _expert_ffn(xs_rows, block_expert, block_valid, w, anchor):
    n_blocks = block_expert.shape[0]
    blk = (MOE_BLOCK * ROW_TILE, LANES)
    return pl.pallas_call(
        _expert_ffn_kernel,
        grid_spec=pltpu.PrefetchScalarGridSpec(
            num_scalar_prefetch=2, grid=(n_blocks,),
            in_specs=[pl.BlockSpec(blk, lambda i, be, nu: (i, 0)),
                      pl.BlockSpec((1, D_MODEL, EXPERT_FF), lambda i, be, nu: (be[i], 0, 0)),
                      pl.BlockSpec((1, D_MODEL, EXPERT_FF), lambda i, be, nu: (be[i], 0, 0)),
                      pl.BlockSpec((1, EXPERT_FF, D_MODEL), lambda i, be, nu: (be[i], 0, 0)),
                      pl.BlockSpec((8, LANES), lambda i, be, nu: (0, 0))],
            out_specs=pl.BlockSpec(blk, lambda i, be, nu: (i, 0)),
            scratch_shapes=[pltpu.VMEM((D_MODEL, 2 * EXPERT_FF), BF16), pltpu.VMEM((EXPERT_FF, D_MODEL), BF16)]),
        out_shape=jax.ShapeDtypeStruct(xs_rows.shape, jnp.int32),
        compiler_params=_params(("arbitrary",)),
        name="expert_ffn",
    )(block_expert, block_valid, xs_rows, w["w1"], w["w3"], w["w2"], anchor)


def _moe_out_kernel(x_ref, g_ref, p_ref, yg_ref, ws13_ref, ws2_ref, wpg_ref, wple_ref, lng_ref, lnb_ref, o_ref):
    x = x_ref[...]
    xb = x.astype(BF16)
    tm = x.shape[0]
    g = g_ref[...]
    parts = None
    for k in range(TOP_K):
        chunks = [g[:, k:k + 1] * c for c in _load_packed_chunks(yg_ref, tm, lead=k)]
        parts = chunks if parts is None else [a + c for a, c in zip(parts, chunks)]
    routed = jnp.concatenate(parts, axis=1)
    h13 = jnp.dot(xb, ws13_ref[...], preferred_element_type=F32)
    h = jax.nn.silu(h13[:, :EXPERT_FF]) * h13[:, EXPERT_FF:]
    shared = jnp.dot(h.astype(BF16), ws2_ref[...], preferred_element_type=F32)
    ple = (jax.nn.sigmoid(jnp.dot(xb, wpg_ref[...], preferred_element_type=F32))
           * jnp.dot(p_ref[...].astype(BF16), wple_ref[...], preferred_element_type=F32))
    o_ref[...] = _layer_norm(DN_ALPHA * x + routed + shared + ple, lng_ref[...], lnb_ref[...])


def _moe_out(x1, top_gates, p, yg_rows, w, tm):
    n = x1.shape[0]
    row = lambda width: pl.BlockSpec((tm, width), lambda i: (i, 0))
    return pl.pallas_call(
        _moe_out_kernel,
        grid=(n // tm,),
        in_specs=[row(D_MODEL), row(TOP_K), row(PLE_DIM),
                  pl.BlockSpec((TOP_K, tm * ROW_TILE, LANES), lambda i: (0, i, 0)),
                  _full((D_MODEL, 2 * EXPERT_FF)), _full((EXPERT_FF, D_MODEL)),
                  _full((D_MODEL, D_MODEL)), _full((PLE_DIM, D_MODEL)),
                  _full((1, D_MODEL)), _full((1, D_MODEL))],
        out_specs=row(D_MODEL),
        out_shape=jax.ShapeDtypeStruct((n, D_MODEL), F32),
        compiler_params=_params(("parallel",)),
        name="moe_out",
    )(x1, top_gates, p, yg_rows, w["ws13"], w["ws2"], w["w_ple_gate"], w["w_ple"], w["ln2_g"], w["ln2_b"])


def _moe_sorted(x1, x1_tiles, top_idx, top_gates, counts, p, w, anchor):
    n = x1.shape[0]
    n_blocks = n * TOP_K // MOE_BLOCK + N_EXPERTS
    n_rows = n_blocks * MOE_BLOCK
    cnt = counts.reshape(N_EXPERTS).astype(jnp.int32)
    padded = (cnt + MOE_BLOCK - 1) // MOE_BLOCK * MOE_BLOCK
    pend = jnp.cumsum(padded)
    pstart = (pend - padded).astype(F32).reshape(N_EXPERTS, 1)
    block_start = jnp.arange(n_blocks, dtype=jnp.int32) * MOE_BLOCK
    block_expert = jnp.minimum(jnp.sum((pend[None, :] <= block_start[:, None]).astype(jnp.int32), axis=1),
                               N_EXPERTS - 1)
    real_end = pend - padded + cnt
    block_valid = jnp.clip(jnp.take(real_end, block_expert) - block_start, 0, MOE_BLOCK).astype(jnp.int32)
    slots = _route(top_idx, pstart, 512)
    slots = jnp.transpose(slots.reshape(TOP_K, n // SC_WINDOW, SC_WINDOW), (1, 0, 2))
    xs = _sc_dispatch(x1_tiles.reshape(n, ROW_TILE, LANES), slots, n_rows)
    ys = _expert_ffn(xs.reshape(n_rows * ROW_TILE, LANES), block_expert, block_valid, w, anchor)
    yg = _sc_combine(ys.reshape(n_rows, ROW_TILE, LANES), slots, n)
    return _moe_out(x1, top_gates.T, p, yg.reshape(TOP_K, n * ROW_TILE, LANES), w, 512)


def _kv_rows(k, v, batch, seq, keep, g):
    cols = slice(g * GROUP_WIDTH, (g + 1) * GROUP_WIDTH)
    shape = (batch, keep, HEADS_PER_GROUP, HEAD_DIM)
    k_g = k.reshape(batch, seq, ATTN_WIDTH)[:, seq - keep:, cols].reshape(shape)
    v_g = v.reshape(batch, seq, ATTN_WIDTH)[:, seq - keep:, cols].reshape(shape)
    return jnp.stack([k_g, v_g], axis=2)


def _layer_prompt(x, p, w, ssm, anchor):
    batch, seq, _ = x.shape
    n = batch * seq
    x2 = x.reshape(n, D_MODEL)
    tabs = _rope_tables(jnp.arange(seq, dtype=jnp.int32))
    q, k, v, u = _in_proj(x2, w["w_in"], tabs, seq, 512)
    attn_o, attn_lse = _attn_prompt(q, k, v, batch, seq)
    zeros = jnp.zeros((batch, SSM_LANES), F32)
    y_tb, h_re, h_im = _s5_scan(u.reshape(seq * batch, SSM_WIDTH), ssm, zeros, zeros, batch, 128)
    x1, x1_tiles, _, top_idx, top_gates, counts = _post_mixer(
        x2, attn_o, attn_lse, y_tb.reshape(seq, batch * SSM_WIDTH), w, seq, 512)
    y = _moe_sorted(x1, x1_tiles, top_idx, top_gates, counts, p.reshape(n, PLE_DIM), w, anchor)
    kv = [_kv_rows(k, v, batch, seq, min(win, seq), g) for g, (win, _) in enumerate(DILATION_GROUPS)]
    h_last = jnp.stack([h_re, h_im], axis=-1).reshape(batch, SSM_GROUPS, SSM_STATE, 2)
    return y.reshape(batch, seq, D_MODEL), kv, h_last


def _sample_attention(x, caches, w):
    batch, seq, _ = x.shape
    assert seq == 1
    x2 = x.reshape(batch, D_MODEL)
    tabs = _rope_tables(jnp.full((batch,), PAST_LEN, dtype=jnp.int32))
    q, k, v, u = _in_proj(x2, w["w_in"], tabs, batch, batch)
    attn_o, attn_lse = _attn_sample(q, k, v, caches, 2)
    return x2, k, v, u, attn_o, attn_lse


def _layer_sample(p, state, w, ssm, x2, k, v, u, attn_o, attn_lse):
    batch = x2.shape[0]
    h0 = state.reshape(batch, SSM_LANES, 2)
    y_tb, h_re, h_im = _s5_scan(u, ssm, h0[..., 0], h0[..., 1], batch, 1)
    x1, _, gates, _, _, _ = _post_mixer(x2, attn_o, attn_lse, y_tb, w, batch, batch)
    y = _moe_ffn(x1, gates.T, p.reshape(batch, PLE_DIM), w, batch)
    kv = [_kv_rows(k, v, batch, 1, 1, g) for g in range(len(DILATION_GROUPS))]
    h_last = jnp.stack([h_re, h_im], axis=-1).reshape(batch, SSM_GROUPS, SSM_STATE, 2)
    return y.reshape(batch, 1, D_MODEL), kv, h_last


def _hi_lo_rows(t):
    hi = t.astype(BF16)
    return jnp.concatenate([hi, (t - hi.astype(F32)).astype(BF16)], axis=1).T


def kernel(x_prompt, x_sample, cache_kv_w128, cache_kv_w512, cache_kv_w2048, state_ssm, p_prompt, p_sample,
           w_in, a_re, a_im, log_dt, b_re, b_im, c_re, c_im, d_skip, w_glu, b_glu, w_attn_br, w_ssm_br,
           w_gate, b_gate, w_out, ln1_g, ln1_b, w_router, router_bias, w1, w3, w2, ws1, ws3, ws2,
           w_ple_gate, w_ple, ln2_g, ln2_b):
    assert w_in.shape[0] == DEPTH == 1
    l = 0
    row = lambda t: t[l].reshape(1, -1)
    w = {
        "w_in": w_in[l].astype(BF16),
        "w_glu": w_glu[l].astype(BF16), "b_glu": row(b_glu),
        "w_gate": w_gate[l].astype(BF16), "b_gate": row(b_gate),
        "w_attn_br": w_attn_br[l].astype(BF16), "w_ssm_br": w_ssm_br[l].astype(BF16),
        "w_out": w_out[l].astype(BF16), "ln1_g": row(ln1_g), "ln1_b": row(ln1_b),
        "w_router": _hi_lo_rows(w_router[l]), "router_bias": router_bias[l].reshape(N_EXPERTS, 1),
        "w1": w1[l], "w3": w3[l], "w2": w2[l],
        "ws13": jnp.concatenate([ws1[l], ws3[l]], axis=-1).astype(BF16), "ws2": ws2[l].astype(BF16),
        "w_ple_gate": w_ple_gate[l].astype(BF16), "w_ple": w_ple[l].astype(BF16),
        "ln2_g": row(ln2_g), "ln2_b": row(ln2_b),
    }
    ssm = _s5_params(a_re[l], a_im[l], log_dt[l], b_re[l], b_im[l], c_re[l], c_im[l], d_skip[l])
    caches = (cache_kv_w128[l], cache_kv_w512[l], cache_kv_w2048[l])
    sample = _sample_attention(x_sample, caches, w)
    yp, kv_p, h_p = _layer_prompt(x_prompt, p_prompt[l], w, ssm, anchor=sample[-1][:8, :LANES])
    ys, kv_s, h_s = _layer_sample(p_sample[l], state_ssm[l], w, ssm, *sample)
    return (yp, ys, kv_p[0][None], kv_s[0][None], kv_p[1][None], kv_s[1][None],
            kv_p[2][None], kv_s[2][None], h_p[None], h_s[None])
```

```python
import functools
import math

import jax
import jax.numpy as jnp
from jax import lax
from jax.experimental import pallas as pl
from jax.experimental.pallas import tpu as pltpu
from jax.experimental.pallas import tpu_sc as plsc

F32 = jnp.float32
BF16 = jnp.bfloat16

D_MODEL = 1024
HEAD_DIM = 64
HEADS_PER_GROUP = 4
DILATION_GROUPS = ((128, 1), (512, 4), (2048, 16))
N_BACK = 128
GROUP_WIDTH = HEADS_PER_GROUP * HEAD_DIM
ATTN_WIDTH = 3 * GROUP_WIDTH
ROPE_THETA = 10000.0
SSM_WIDTH = 256
SSM_GROUP = 16
SSM_GROUPS = 16
SSM_STATE = 64
SSM_LANES = SSM_GROUPS * SSM_STATE
IN_WIDTH = 3 * ATTN_WIDTH + SSM_WIDTH
N_EXPERTS = 64
TOP_K = 8
EXPERT_FF = 256
ROUTED_SCALE = 2.5
PLE_DIM = 256
DEPTH = 1
PAST_LEN = 8192
DN_ALPHA = (2.0 * DEPTH) ** 0.25
LN_EPS = 1e-5

LANES = 128
ROW_TILE = D_MODEL // LANES // 2
SC_CORES = 2
SC_SUBCORES = 16
SC_WINDOW = 64
MOE_BLOCK = 1024
POST_MIXER_SUBTILES = 2
EXPERT_FFN_SUBTILES = 4
ATTN_CHUNK = 2048
ATTN_UNROLL = 8
VMEM_LIMIT = 56 * 1024 * 1024


def _params(semantics):
    return pltpu.CompilerParams(dimension_semantics=semantics, vmem_limit_bytes=VMEM_LIMIT)


def _full(shape):
    return pl.BlockSpec(shape, lambda *_: (0,) * len(shape))


def _interleaved_rows(seq, rows, n_seq, row0=0):
    start = row0 * n_seq + seq
    return pl.ds(start, rows) if n_seq == 1 else pl.ds(start, rows, stride=n_seq)


def _in_proj_kernel(x_ref, w_ref, cos_ref, sina_ref, sinb_ref, q_ref, k_ref, v_ref, ulo_ref, uhi_ref, *, n_seq):
    xb = x_ref[...].astype(BF16)
    cos = cos_ref[...]
    sin_a = sina_ref[...]
    sin_b = sinb_ref[...]

    def rope_store(col0, out_ref, scale):
        t = jnp.dot(xb, w_ref[:, col0:col0 + ATTN_WIDTH], preferred_element_type=F32)
        for c in range(ATTN_WIDTH // LANES):
            xc = t[:, c * LANES:(c + 1) * LANES]
            r = xc * cos + pltpu.roll(xc, LANES - 32, 1) * sin_a + pltpu.roll(xc, 32, 1) * sin_b
            out_ref[:, c * LANES:(c + 1) * LANES] = r * scale if scale != 1.0 else r

    rope_store(0, q_ref, HEAD_DIM ** -0.5)
    rope_store(ATTN_WIDTH, k_ref, 1.0)
    v_ref[...] = jnp.dot(xb, w_ref[:, 2 * ATTN_WIDTH:3 * ATTN_WIDTH], preferred_element_type=F32)
    u = jnp.dot(xb, w_ref[:, 3 * ATTN_WIDTH:], preferred_element_type=F32)
    rows = _interleaved_rows(pl.program_id(1), u.shape[0], n_seq)
    ulo_ref[rows, :] = u[:, :LANES]
    uhi_ref[rows, :] = u[:, LANES:]


def _in_proj(x, w_in_bf, rope_tabs, rows_per_seq, tm):
    n = x.shape[0]
    tiles_per_seq = rows_per_seq // tm
    n_seq = n // rows_per_seq
    tab_tiles = rope_tabs[0].shape[0] // tm
    tab_spec = pl.BlockSpec((tm, LANES), lambda t, s: (t % tab_tiles, 0))
    row_spec = pl.BlockSpec((tm, ATTN_WIDTH), lambda t, s: (s * tiles_per_seq + t, 0))
    u_spec = pl.BlockSpec((tm * n_seq, LANES), lambda t, s: (t, 0))
    return pl.pallas_call(
        functools.partial(_in_proj_kernel, n_seq=n_seq),
        grid=(tiles_per_seq, n_seq),
        in_specs=[pl.BlockSpec((tm, D_MODEL), lambda t, s: (s * tiles_per_seq + t, 0)),
                  _full((D_MODEL, IN_WIDTH)), tab_spec, tab_spec, tab_spec],
        out_specs=[row_spec, row_spec, row_spec, u_spec, u_spec],
        out_shape=[jax.ShapeDtypeStruct((n, ATTN_WIDTH), F32)] * 3 + [jax.ShapeDtypeStruct((n, LANES), F32)] * 2,
        compiler_params=_params(("parallel", "arbitrary")),
        name="in_proj",
    )(x, w_in_bf, *rope_tabs)


def _rope_tables(pos):
    half = HEAD_DIM // 2
    inv = ROPE_THETA ** (-jnp.arange(half, dtype=F32) / half)
    ang = pos.astype(F32)[:, None] * inv[None, :]
    cos = jnp.tile(jnp.cos(ang), (1, LANES // half))
    sin = jnp.tile(jnp.sin(ang), (1, LANES // half))
    first_half = (jnp.arange(LANES) % HEAD_DIM) < half
    sin_a = jnp.where(first_half[None, :], -sin, 0.0)
    sin_b = jnp.where(first_half[None, :], 0.0, sin)
    return cos, sin_a, sin_b


def _band_attention(q, k, v, mask):
    head_of_lane = lax.broadcasted_iota(jnp.int32, (N_BACK, LANES), 1) // HEAD_DIM
    kb = k.astype(BF16)
    vb = v.astype(BF16)
    o = lse = None
    for h in range(LANES // HEAD_DIM):
        qh = jnp.where(head_of_lane == h, q, 0.0).astype(BF16)
        logits = lax.dot_general(qh, kb, (((1,), (1,)), ((), ())), preferred_element_type=F32) + mask
        m = jnp.max(logits, axis=1, keepdims=True)
        p = jnp.exp(logits - m)
        l = jnp.sum(p, axis=1, keepdims=True)
        o_h = jnp.dot(p.astype(BF16), vb, preferred_element_type=F32) * (1.0 / l)
        lse_h = jnp.broadcast_to(m + jnp.log(l), (N_BACK, LANES))
        o = o_h if o is None else jnp.where(head_of_lane == h, o_h, o)
        lse = lse_h if lse is None else jnp.where(head_of_lane == h, lse_h, lse)
    return o, lse


def _attn_prompt_kernel(q_ref, kp_ref, kc_ref, vp_ref, vc_ref, o_ref, lse_ref):
    c = pl.program_id(1)
    g = pl.program_id(3)
    ch = ATTN_CHUNK
    qi = lax.broadcasted_iota(jnp.int32, (N_BACK, 2 * N_BACK), 0)
    kj = lax.broadcasted_iota(jnp.int32, (N_BACK, 2 * N_BACK), 1)
    dist = qi + N_BACK - kj
    band = jnp.where(dist >= 0, jnp.where(dist <= N_BACK, 0.0, -jnp.inf), -jnp.inf)
    band_first = jnp.where(kj >= N_BACK, band, -jnp.inf)

    def group_body(d):
        span = N_BACK * d
        n_sub = ch // N_BACK

        def rows(start, size):
            return pl.ds(start, size) if d == 1 else pl.ds(start, size, stride=d)

        def store(q0, o, lse):
            o_ref[rows(q0, N_BACK), :] = o
            lse_ref[rows(q0, N_BACK), :] = lse

        def head_block(r, carry):
            k = jnp.concatenate([kp_ref[rows(ch - span + r, N_BACK), :], kc_ref[rows(r, N_BACK), :]], axis=0)
            v = jnp.concatenate([vp_ref[rows(ch - span + r, N_BACK), :], vc_ref[rows(r, N_BACK), :]], axis=0)
            mask = jnp.where(c == 0, band_first, band)
            store(r, *_band_attention(q_ref[rows(r, N_BACK), :], k, v, mask))
            return carry

        def inner_block(idx, carry):
            s = idx // d
            r = idx % d
            k0 = (s - 1) * span + r
            store(s * span + r, *_band_attention(q_ref[rows(s * span + r, N_BACK), :],
                                                 kc_ref[rows(k0, 2 * N_BACK), :],
                                                 vc_ref[rows(k0, 2 * N_BACK), :], band))
            return carry

        lax.fori_loop(0, d, head_block, 0, unroll=min(d, ATTN_UNROLL))
        if n_sub > d:
            trips = n_sub - d
            lax.fori_loop(d, n_sub, inner_block, 0,
                          unroll=max(u for u in range(1, ATTN_UNROLL + 1) if trips % u == 0))

    for gi, (_, d) in enumerate(DILATION_GROUPS):
        pl.when(g == gi)(functools.partial(group_body, d))


def _attn_prompt(q, k, v, batch, seq):
    ch = ATTN_CHUNK
    cps = seq // ch
    n = batch * seq
    pairs = GROUP_WIDTH // LANES
    cur = lambda b, c, hp, g: (b * cps + c, g * pairs + hp)
    prev = lambda b, c, hp, g: (b * cps + jnp.maximum(c - 1, 0), g * pairs + hp)
    blk = (ch, LANES)
    return pl.pallas_call(
        _attn_prompt_kernel,
        grid=(batch, cps, pairs, len(DILATION_GROUPS)),
        in_specs=[pl.BlockSpec(blk, cur), pl.BlockSpec(blk, prev), pl.BlockSpec(blk, cur),
                  pl.BlockSpec(blk, prev), pl.BlockSpec(blk, cur)],
        out_specs=[pl.BlockSpec(blk, cur), pl.BlockSpec(blk, cur)],
        out_shape=[jax.ShapeDtypeStruct((n, ATTN_WIDTH), F32)] * 2,
        compiler_params=_params(("parallel", "parallel", "parallel", "parallel")),
        name="attn_prompt",
    )(q, k, k, v, v)


def _attn_sample_kernel(q_ref, k_ref, v_ref, c0_ref, c1_ref, c2_ref, o_ref, lse_ref):
    bt = q_ref.shape[0]
    for b in range(bt):
        for g, (c_ref, (win, d)) in enumerate(zip((c0_ref, c1_ref, c2_ref), DILATION_GROUPS)):
            pos = lax.broadcasted_iota(jnp.int32, (1, win), 1)
            off_stride = (pos % d) != 0
            j0 = g * HEADS_PER_GROUP
            heads = range(HEADS_PER_GROUP)
            qs = [q_ref[b, :, j0 + h:j0 + h + 1] for h in heads]
            s_c = jnp.concatenate([jnp.sum(c_ref[b, 0, h] * qs[h], axis=0, keepdims=True) for h in heads],
                                  axis=0)
            s_c = jnp.where(off_stride, -jnp.inf, s_c)
            s_new = jnp.concatenate([jnp.sum(k_ref[b, :, j0 + h:j0 + h + 1] * qs[h], axis=0, keepdims=True)
                                     for h in heads], axis=0)
            m = jnp.maximum(jnp.max(s_c, axis=1, keepdims=True), s_new)
            p_c = jnp.exp(s_c - m)
            p_new = jnp.exp(s_new - m)
            l = jnp.sum(p_c, axis=1, keepdims=True) + p_new
            inv_l = 1.0 / l
            lse_ref[b, j0:j0 + HEADS_PER_GROUP, :] = m + jnp.log(l)
            for h in heads:
                num = (jnp.sum(c_ref[b, 1, h] * p_c[h:h + 1, :], axis=1, keepdims=True)
                       + p_new[h:h + 1, :] * v_ref[b, :, j0 + h:j0 + h + 1])
                o_ref[b, :, j0 + h:j0 + h + 1] = num * inv_l[h:h + 1, :]


def _attn_sample(q, k, v, caches, bt):
    b = q.shape[0]
    n_heads = ATTN_WIDTH // HEAD_DIM
    views, specs = [], []
    for cache, (win, d) in zip(caches, DILATION_GROUPS):
        assert cache.shape[1] == win == N_BACK * d
        views.append(jnp.transpose(cache, (0, 2, 3, 4, 1)))
        specs.append(pl.BlockSpec((bt, 2, HEADS_PER_GROUP, HEAD_DIM, win), lambda i: (i, 0, 0, 0, 0)))
    col_spec = pl.BlockSpec((bt, HEAD_DIM, n_heads), lambda i: (i, 0, 0))
    lse_spec = pl.BlockSpec((bt, n_heads, 1), lambda i: (i, 0, 0))
    cols = lambda t: jnp.transpose(t.reshape(b, n_heads, HEAD_DIM), (0, 2, 1))
    o, lse = pl.pallas_call(
        _attn_sample_kernel,
        grid=(b // bt,),
        in_specs=[col_spec, col_spec, col_spec] + specs,
        out_specs=[col_spec, lse_spec],
        out_shape=[jax.ShapeDtypeStruct((b, HEAD_DIM, n_heads), F32),
                   jax.ShapeDtypeStruct((b, n_heads, 1), F32)],
        compiler_params=_params(("parallel",)),
        name="attn_sample",
    )(cols(q), cols(k), cols(v), *views)
    o = jnp.transpose(o, (0, 2, 1)).reshape(b, ATTN_WIDTH)
    lse = jnp.broadcast_to(lse, (b, n_heads, HEAD_DIM)).reshape(b, ATTN_WIDTH)
    return o, lse


def _s5_scan_kernel(ulo_ref, uhi_ref, bmat_ref, cmat_ref, are_ref, aim_ref, d_ref, h0re_ref, h0im_ref,
                    ylo_ref, yhi_ref, hre_ref, him_ref, hist_sc, *, bg, steps):
    t_chunk = pl.program_id(0)

    @pl.when(t_chunk == 0)
    def _():
        hre_ref[...] = h0re_ref[...]
        him_ref[...] = h0im_ref[...]

    u = jnp.concatenate([ulo_ref[...], uhi_ref[...]], axis=1)
    hist_sc[...] = jnp.dot(u.astype(BF16), bmat_ref[...], preferred_element_type=F32)
    a_re = jnp.broadcast_to(are_ref[...], (bg, SSM_LANES))
    a_im = jnp.broadcast_to(aim_ref[...], (bg, SSM_LANES))

    def step(t, carry):
        h_re, h_im = carry
        rows = pl.ds(pl.multiple_of(t * bg, bg), bg)
        n_re = a_re * h_re - a_im * h_im + hist_sc[rows, 0:SSM_LANES]
        n_im = a_re * h_im + a_im * h_re + hist_sc[rows, SSM_LANES:2 * SSM_LANES]
        hist_sc[rows, 0:SSM_LANES] = n_re
        hist_sc[rows, SSM_LANES:2 * SSM_LANES] = n_im
        return n_re, n_im

    h_re, h_im = lax.fori_loop(0, steps, step, (hre_ref[...], him_ref[...]))
    hre_ref[...] = h_re
    him_ref[...] = h_im
    y = jnp.dot(hist_sc[...].astype(BF16), cmat_ref[...], preferred_element_type=F32) + d_ref[...] * u
    ylo_ref[...] = y[:, :LANES]
    yhi_ref[...] = y[:, LANES:]


def _s5_scan(u_lo, u_hi, ssm, h0_re, h0_im, bg, steps):
    rows = u_lo.shape[0]
    blk = steps * bg
    kern = functools.partial(_s5_scan_kernel, bg=bg, steps=steps)
    state_spec = _full((bg, SSM_LANES))
    half_spec = pl.BlockSpec((blk, LANES), lambda i: (i, 0))
    return pl.pallas_call(
        kern,
        grid=(rows // blk,),
        in_specs=[half_spec, half_spec,
                  _full((SSM_WIDTH, 2 * SSM_LANES)), _full((2 * SSM_LANES, SSM_WIDTH)),
                  _full((1, SSM_LANES)), _full((1, SSM_LANES)), _full((1, SSM_WIDTH)),
                  state_spec, state_spec],
        out_specs=[half_spec, half_spec, state_spec, state_spec],
        out_shape=[jax.ShapeDtypeStruct((rows, LANES), F32), jax.ShapeDtypeStruct((rows, LANES), F32),
                   jax.ShapeDtypeStruct((bg, SSM_LANES), F32), jax.ShapeDtypeStruct((bg, SSM_LANES), F32)],
        scratch_shapes=[pltpu.VMEM((blk, 2 * SSM_LANES), F32)],
        compiler_params=_params(("arbitrary",)),
        name="s5_scan",
    )(u_lo, u_hi, ssm["bmat"], ssm["cmat"], ssm["a_re"], ssm["a_im"], ssm["d_skip"], h0_re, h0_im)


def _s5_params(a_re, a_im, log_dt, b_re, b_im, c_re, c_im, d_skip):
    dt = jnp.exp(log_dt)[:, None]
    mag = jnp.exp(a_re * dt)
    abar_re = mag * jnp.cos(a_im * dt)
    abar_im = mag * jnp.sin(a_im * dt)
    a2 = a_re * a_re + a_im * a_im
    nr = abar_re - 1.0
    coef_re = (nr * a_re + abar_im * a_im) / a2
    coef_im = (abar_im * a_re - nr * a_im) / a2
    bb_re = coef_re[..., None] * b_re - coef_im[..., None] * b_im
    bb_im = coef_re[..., None] * b_im + coef_im[..., None] * b_re
    eye = jnp.eye(SSM_GROUPS, dtype=F32)
    to_b = lambda t: jnp.einsum("gpc,gh->gchp", t, eye).reshape(SSM_WIDTH, SSM_LANES)
    to_c = lambda t: jnp.einsum("gcp,gh->gphc", t, eye).reshape(SSM_LANES, SSM_WIDTH)
    return {
        "bmat": jnp.concatenate([to_b(bb_re), to_b(bb_im)], axis=1).astype(BF16),
        "cmat": jnp.concatenate([to_c(c_re), -to_c(c_im)], axis=0).astype(BF16),
        "a_re": abar_re.reshape(1, SSM_LANES), "a_im": abar_im.reshape(1, SSM_LANES),
        "d_skip": d_skip.reshape(1, SSM_WIDTH),
    }


def _layer_norm(z, g, b):
    mu = jnp.mean(z, axis=-1, keepdims=True)
    zc = z - mu
    var = jnp.mean(zc * zc, axis=-1, keepdims=True)
    return zc * lax.rsqrt(var + LN_EPS) * g + b


def _merge_groups(o, lse):
    parts = [slice(g * GROUP_WIDTH, (g + 1) * GROUP_WIDTH) for g in range(len(DILATION_GROUPS))]
    top = lse[:, parts[0]]
    for cols in parts[1:]:
        top = jnp.maximum(top, lse[:, cols])
    num = den = None
    for cols in parts:
        w = jnp.exp(lse[:, cols] - top)
        num = w * o[:, cols] if num is None else num + w * o[:, cols]
        den = w if den is None else den + w
    return num / den


def _store_packed_rows(ref, x, row0=0):
    rows = x.shape[0]
    for j in range(ROW_TILE):
        lo = x[:, j * LANES:(j + 1) * LANES].astype(BF16).astype(F32)
        hi = x[:, (j + ROW_TILE) * LANES:(j + ROW_TILE + 1) * LANES].astype(BF16).astype(F32)
        word = (lax.bitcast_convert_type(lo, jnp.uint32) >> 16) | lax.bitcast_convert_type(hi, jnp.uint32)
        ref[pl.ds(row0 * ROW_TILE + j, rows, stride=ROW_TILE), :] = lax.bitcast_convert_type(word, jnp.int32)


def _load_packed_chunks(ref, rows, lead=None, row0=0):
    lows, highs = [], []
    for j in range(ROW_TILE):
        idx = (pl.ds(row0 * ROW_TILE + j, rows, stride=ROW_TILE), slice(None))
        word = lax.bitcast_convert_type(ref[idx] if lead is None else ref[(lead,) + idx], jnp.uint32)
        lows.append(lax.bitcast_convert_type(word << 16, F32))
        highs.append(lax.bitcast_convert_type(word & jnp.uint32(0xFFFF0000), F32))
    return lows + highs


def _post_mixer_kernel(x_ref, ao_ref, lse_ref, ylo_ref, yhi_ref, wglu_ref, bglu_ref, wgate_ref, bgate_ref,
                       wab_ref, wsb_ref, wout_ref, lng_ref, lnb_ref, wr_ref, rb_ref,
                       x1_ref, x1t_ref, gate_ref, idx_ref, topg_ref, cnt_ref, *, n_seq):
    seq = pl.program_id(1)

    @pl.when(jnp.logical_and(pl.program_id(0) == 0, seq == 0))
    def _():
        cnt_ref[...] = jnp.zeros(cnt_ref.shape, F32)

    tm = x_ref.shape[0]
    sub = tm // POST_MIXER_SUBTILES if tm % (8 * POST_MIXER_SUBTILES) == 0 else tm
    for r0 in range(0, tm, sub):
        rows = slice(r0, r0 + sub)
        x = x_ref[rows, :]
        xb = x.astype(BF16)
        y_rows = _interleaved_rows(seq, sub, n_seq, r0)
        s = jax.nn.gelu(jnp.concatenate([ylo_ref[y_rows, :], yhi_ref[y_rows, :]], axis=1))
        s = s * jax.nn.sigmoid(jnp.dot(s.astype(BF16), wglu_ref[...], preferred_element_type=F32) + bglu_ref[...])
        gates = jax.nn.sigmoid(jnp.dot(xb, wgate_ref[...], preferred_element_type=F32) + bgate_ref[...])
        attn_o = _merge_groups(ao_ref[rows, :], lse_ref[rows, :])
        attn_br = jnp.dot(attn_o.astype(BF16), wab_ref[...], preferred_element_type=F32)
        ssm_br = jnp.dot(s.astype(BF16), wsb_ref[...], preferred_element_type=F32)
        merged = gates[:, :D_MODEL] * attn_br + gates[:, D_MODEL:] * ssm_br
        mix = jnp.dot(merged.astype(BF16), wout_ref[...], preferred_element_type=F32)
        x1 = _layer_norm(DN_ALPHA * x + mix, lng_ref[...], lnb_ref[...])
        x1_ref[rows, :] = x1
        _store_packed_rows(x1t_ref, x1, r0)

        x1_hi = x1.astype(BF16)
        x1_lo = (x1 - x1_hi.astype(F32)).astype(BF16)
        prod = lax.dot_general(wr_ref[...], jnp.concatenate([x1_hi, x1_lo], axis=0),
                               (((1,), (1,)), ((), ())), preferred_element_type=F32)
        logits = ((prod[:N_EXPERTS, :sub] + prod[N_EXPERTS:, :sub] + prod[:N_EXPERTS, sub:])
                  + prod[N_EXPERTS:, sub:])
        scores = jax.nn.sigmoid(logits)
        sel = scores + rb_ref[...]
        expert = lax.broadcasted_iota(jnp.int32, sel.shape, 0).astype(F32)
        chosen = jnp.zeros(sel.shape, F32)
        idx_rows, score_rows = [], []
        for _ in range(TOP_K):
            top = jnp.max(sel, axis=0, keepdims=True)
            first = jnp.min(jnp.where(sel == top, expert, float(N_EXPERTS)), axis=0, keepdims=True)
            hit = expert == first
            chosen = jnp.where(hit, 1.0, chosen)
            sel = jnp.where(hit, -jnp.inf, sel)
            idx_rows.append(first)
            score_rows.append(jnp.sum(jnp.where(hit, scores, 0.0), axis=0, keepdims=True))
        norm = ROUTED_SCALE / jnp.sum(scores * chosen, axis=0, keepdims=True)
        gate_ref[:, rows] = scores * chosen * norm
        idx_ref[:, rows] = jnp.concatenate(idx_rows, axis=0)
        topg_ref[:, rows] = jnp.concatenate(score_rows, axis=0) * norm
        cnt_ref[...] += jnp.sum(chosen, axis=1, keepdims=True)


def _post_mixer(x, attn_o, attn_lse, y_lo, y_hi, w, rows_per_seq, tm):
    n = x.shape[0]
    tiles_per_seq = rows_per_seq // tm
    n_seq = n // rows_per_seq
    row = lambda width: pl.BlockSpec((tm, width), lambda t, s: (s * tiles_per_seq + t, 0))
    col = lambda height: pl.BlockSpec((height, tm), lambda t, s: (0, s * tiles_per_seq + t))
    y_spec = pl.BlockSpec((tm * n_seq, LANES), lambda t, s: (t, 0))
    return pl.pallas_call(
        functools.partial(_post_mixer_kernel, n_seq=n_seq),
        grid=(tiles_per_seq, n_seq),
        in_specs=[row(D_MODEL), row(ATTN_WIDTH), row(ATTN_WIDTH), y_spec, y_spec,
                  _full((SSM_WIDTH, SSM_WIDTH)), _full((1, SSM_WIDTH)),
                  _full((D_MODEL, 2 * D_MODEL)), _full((1, 2 * D_MODEL)),
                  _full((GROUP_WIDTH, D_MODEL)), _full((SSM_WIDTH, D_MODEL)), _full((D_MODEL, D_MODEL)),
                  _full((1, D_MODEL)), _full((1, D_MODEL)),
                  _full((2 * N_EXPERTS, D_MODEL)), _full((N_EXPERTS, 1))],
        out_specs=[row(D_MODEL),
                   pl.BlockSpec((tm * ROW_TILE, LANES), lambda t, s: (s * tiles_per_seq + t, 0)),
                   col(N_EXPERTS), col(TOP_K), col(TOP_K), _full((N_EXPERTS, 1))],
        out_shape=[jax.ShapeDtypeStruct((n, D_MODEL), F32), jax.ShapeDtypeStruct((n * ROW_TILE, LANES), jnp.int32),
                   jax.ShapeDtypeStruct((N_EXPERTS, n), F32), jax.ShapeDtypeStruct((TOP_K, n), F32),
                   jax.ShapeDtypeStruct((TOP_K, n), F32), jax.ShapeDtypeStruct((N_EXPERTS, 1), F32)],
        compiler_params=_params(("arbitrary", "arbitrary")),
        name="post_mixer",
    )(x, attn_o, attn_lse, y_lo, y_hi, w["w_glu"], w["b_glu"], w["w_gate"], w["b_gate"], w["w_attn_br"], w["w_ssm_br"],
      w["w_out"], w["ln1_g"], w["ln1_b"], w["w_router"], w["router_bias"])


def _moe_ffn_kernel(x_ref, gate_ref, p_ref, w1_ref, w3_ref, w2_ref, ws13_ref, ws2_ref, wpg_ref, wple_ref,
                    lng_ref, lnb_ref, o_ref, acc_sc, xb_sc):
    e = pl.program_id(1)

    def glu_ffn(xb, w13, w2, row_scale):
        if isinstance(w13, tuple):
            h1 = jnp.dot(xb, w13[0].astype(BF16), preferred_element_type=F32)
            h3 = jnp.dot(xb, w13[1].astype(BF16), preferred_element_type=F32)
        else:
            h13 = jnp.dot(xb, w13, preferred_element_type=F32)
            h1, h3 = h13[:, :EXPERT_FF], h13[:, EXPERT_FF:]
        h = jax.nn.silu(h1) * h3
        if row_scale is not None:
            h = h * row_scale
        return jnp.dot(h.astype(BF16), w2.astype(BF16), preferred_element_type=F32)

    @pl.when(e == 0)
    def _():
        xb = x_ref[...].astype(BF16)
        xb_sc[...] = xb
        ple = (jax.nn.sigmoid(jnp.dot(xb, wpg_ref[...], preferred_element_type=F32))
               * jnp.dot(p_ref[...].astype(BF16), wple_ref[...], preferred_element_type=F32))
        acc_sc[...] = glu_ffn(xb, ws13_ref[...], ws2_ref[...], None) + ple

    gates = gate_ref[...]
    lane = lax.broadcasted_iota(jnp.int32, gates.shape, 1)
    g_col = jnp.sum(jnp.where(lane == e, gates, 0.0), axis=-1, keepdims=True)
    acc_sc[...] += glu_ffn(xb_sc[...], (w1_ref[0], w3_ref[0]), w2_ref[0], g_col)

    @pl.when(e == N_EXPERTS - 1)
    def _():
        o_ref[...] = _layer_norm(DN_ALPHA * x_ref[...] + acc_sc[...], lng_ref[...], lnb_ref[...])


def _moe_ffn(x1, gates, p, w, tm):
    n = x1.shape[0]
    row = lambda width: pl.BlockSpec((tm, width), lambda i, e: (i, 0))
    return pl.pallas_call(
        _moe_ffn_kernel,
        grid=(n // tm, N_EXPERTS),
        in_specs=[row(D_MODEL), row(N_EXPERTS), row(PLE_DIM),
                  pl.BlockSpec((1, D_MODEL, EXPERT_FF), lambda i, e: (e, 0, 0)),
                  pl.BlockSpec((1, D_MODEL, EXPERT_FF), lambda i, e: (e, 0, 0)),
                  pl.BlockSpec((1, EXPERT_FF, D_MODEL), lambda i, e: (e, 0, 0)),
                  _full((D_MODEL, 2 * EXPERT_FF)), _full((EXPERT_FF, D_MODEL)),
                  _full((D_MODEL, D_MODEL)), _full((PLE_DIM, D_MODEL)),
                  _full((1, D_MODEL)), _full((1, D_MODEL))],
        out_specs=row(D_MODEL),
        out_shape=jax.ShapeDtypeStruct((n, D_MODEL), F32),
        scratch_shapes=[pltpu.VMEM((tm, D_MODEL), F32), pltpu.VMEM((tm, D_MODEL), BF16)],
        compiler_params=_params(("parallel", "arbitrary")),
        name="moe_ffn",
    )(x1, gates, p, w["w1"], w["w3"], w["w2"], w["ws13"], w["ws2"], w["w_ple_gate"], w["w_ple"],
      w["ln2_g"], w["ln2_b"])


def _route_kernel(idx_ref, pstart_ref, earlier_ref, slot_ref, base_sc):
    @pl.when(pl.program_id(0) == 0)
    def _():
        base_sc[...] = jnp.zeros(base_sc.shape, F32)

    idx = idx_ref[...]
    tm = idx.shape[1]
    expert = lax.broadcasted_iota(jnp.int32, (N_EXPERTS, tm), 0).astype(F32)
    hits = [expert == idx[k:k + 1, :] for k in range(TOP_K)]
    member = jnp.zeros((N_EXPERTS, tm), F32)
    for hit in hits:
        member = member + jnp.where(hit, 1.0, 0.0)
    row = (jnp.dot(member.astype(BF16), earlier_ref[...], preferred_element_type=F32)
           + base_sc[...] + pstart_ref[...])
    slots = [jnp.sum(jnp.where(hit, row, 0.0), axis=0, keepdims=True) for hit in hits]
    slot_ref[...] = jnp.concatenate(slots, axis=0).astype(jnp.int32)
    base_sc[...] += jnp.sum(member, axis=1, keepdims=True)


def _route(top_idx, pstart, tm):
    n = top_idx.shape[1]
    earlier = jnp.triu(jnp.ones((tm, tm), F32), k=1).astype(BF16)
    return pl.pallas_call(
        _route_kernel,
        grid=(n // tm,),
        in_specs=[pl.BlockSpec((TOP_K, tm), lambda i: (0, i)), _full((N_EXPERTS, 1)), _full((tm, tm))],
        out_specs=pl.BlockSpec((TOP_K, tm), lambda i: (0, i)),
        out_shape=jax.ShapeDtypeStruct((TOP_K, n), jnp.int32),
        scratch_shapes=[pltpu.VMEM((N_EXPERTS, 1), F32)],
        compiler_params=_params(("arbitrary",)),
        name="route",
    )(top_idx, pstart, earlier)


def _sc_mesh():
    return plsc.VectorSubcoreMesh(core_axis_name="c", subcore_axis_name="s",
                                  num_cores=SC_CORES, num_subcores=SC_SUBCORES)


def _sc_dispatch(x_tiles, slots, n_rows):
    n = x_tiles.shape[0]
    wins_per_worker = n // SC_WINDOW // (SC_CORES * SC_SUBCORES)

    def body(x_hbm, slot_hbm, xs_hbm, idx_v, rows_v, sem):
        wid = lax.axis_index("s") * SC_CORES + lax.axis_index("c")

        @pl.loop(0, wins_per_worker)
        def _(i):
            win = wid * wins_per_worker + i
            pltpu.sync_copy(slot_hbm.at[win], idx_v)
            pltpu.sync_copy(x_hbm.at[pl.ds(win * SC_WINDOW, SC_WINDOW)], rows_v)
            copies = [pltpu.async_copy(rows_v, xs_hbm.at[idx_v.at[k]], sem) for k in range(TOP_K)]
            for copy in copies:
                copy.wait()

    return pl.kernel(
        body, out_type=jax.ShapeDtypeStruct((n_rows, ROW_TILE, LANES), jnp.int32), mesh=_sc_mesh(),
        scratch_types=[pltpu.VMEM((TOP_K, SC_WINDOW), jnp.int32),
                       pltpu.VMEM((SC_WINDOW, ROW_TILE, LANES), jnp.int32),
                       pltpu.SemaphoreType.DMA],
        name="sc_dispatch",
    )(x_tiles, slots)


def _sc_combine(y_tiles, slots, n):
    wins_per_worker = n // SC_WINDOW // (SC_CORES * SC_SUBCORES)

    def body(ys_hbm, slot_hbm, yg_hbm, idx_v, rows_a, rows_b, gather_sems, write_sems):
        wid = lax.axis_index("s") * SC_CORES + lax.axis_index("c")
        bufs = (rows_a, rows_b)

        @pl.loop(0, wins_per_worker)
        def _(i):
            win = wid * wins_per_worker + i
            pltpu.sync_copy(slot_hbm.at[win], idx_v)

            def gather(k):
                return pltpu.async_copy(ys_hbm.at[idx_v.at[k]], bufs[k % 2], gather_sems.at[k % 2])

            def write(k):
                return pltpu.async_copy(bufs[k % 2], yg_hbm.at[k, pl.ds(win * SC_WINDOW, SC_WINDOW)],
                                        write_sems.at[k % 2])

            gathers = {0: gather(0)}
            writes = {}
            for k in range(TOP_K):
                gathers[k].wait()
                if k + 1 < TOP_K:
                    if k >= 1:
                        writes[k - 1].wait()
                    gathers[k + 1] = gather(k + 1)
                writes[k] = write(k)
            writes[TOP_K - 2].wait()
            writes[TOP_K - 1].wait()

    return pl.kernel(
        body, out_type=jax.ShapeDtypeStruct((TOP_K, n, ROW_TILE, LANES), jnp.int32), mesh=_sc_mesh(),
        scratch_types=[pltpu.VMEM((TOP_K, SC_WINDOW), jnp.int32),
                       pltpu.VMEM((SC_WINDOW, ROW_TILE, LANES), jnp.int32),
                       pltpu.VMEM((SC_WINDOW, ROW_TILE, LANES), jnp.int32),
                       pltpu.SemaphoreType.DMA((2,)), pltpu.SemaphoreType.DMA((2,))],
        name="sc_combine",
    )(y_tiles, slots)


def _expert_ffn_kernel(bexp_ref, valid_ref, xs_ref, w1_ref, w3_ref, w2_ref, anchor_ref, ys_ref, w13_sc, w2_sc):
    del anchor_ref
    i = pl.program_id(0)
    valid = valid_ref[i]

    @pl.when(jnp.logical_or(i == 0, bexp_ref[i] != bexp_ref[jnp.maximum(i - 1, 0)]))
    def _():
        w13_sc[:, :EXPERT_FF] = w1_ref[0].astype(BF16)
        w13_sc[:, EXPERT_FF:] = w3_ref[0].astype(BF16)
        w2_sc[...] = w2_ref[0].astype(BF16)

    def ffn(r0, rows):
        x = jnp.concatenate(_load_packed_chunks(xs_ref, rows, row0=r0), axis=1)
        h13 = jnp.dot(x.astype(BF16), w13_sc[...], preferred_element_type=F32)
        h = jax.nn.silu(h13[:, :EXPERT_FF]) * h13[:, EXPERT_FF:]
        _store_packed_rows(ys_ref, jnp.dot(h.astype(BF16), w2_sc[...], preferred_element_type=F32), r0)

    @pl.when(valid == MOE_BLOCK)
    def _():
        ffn(0, MOE_BLOCK)

    @pl.when(jnp.logical_and(valid > 0, valid < MOE_BLOCK))
    def _():
        sub = MOE_BLOCK // EXPERT_FFN_SUBTILES
        for r0 in range(0, MOE_BLOCK, sub):
            pl.when(valid > r0)(functools.partial(ffn, r0, sub))


def _expert_ffn(xs_rows, block_expert, block_valid, w, anchor):
    n_blocks = block_expert.shape[0]
    blk = (MOE_BLOCK * ROW_TILE, LANES)
    return pl.pallas_call(
        _expert_ffn_kernel,
        grid_spec=pltpu.PrefetchScalarGridSpec(
            num_scalar_prefetch=2, grid=(n_blocks,),
            in_specs=[pl.BlockSpec(blk, lambda i, be, nu: (i, 0)),
                      pl.BlockSpec((1, D_MODEL, EXPERT_FF), lambda i, be, nu: (be[i], 0, 0)),
                      pl.BlockSpec((1, D_MODEL, EXPERT_FF), lambda i, be, nu: (be[i], 0, 0)),
                      pl.BlockSpec((1, EXPERT_FF, D_MODEL), lambda i, be, nu: (be[i], 0, 0)),
                      pl.BlockSpec((8, LANES), lambda i, be, nu: (0, 0))],
            out_specs=pl.BlockSpec(blk, lambda i, be, nu: (i, 0)),
            scratch_shapes=[pltpu.VMEM((D_MODEL, 2 * EXPERT_FF), BF16), pltpu.VMEM((EXPERT_FF, D_MODEL), BF16)]),
        out_shape=jax.ShapeDtypeStruct(xs_rows.shape, jnp.int32),
        compiler_params=_params(("arbitrary",)),
        name="expert_ffn",
    )(block_expert, block_valid, xs_rows, w["w1"], w["w3"], w["w2"], anchor)


def _moe_out_kernel(x_ref, g_ref, p_ref, yg_ref, ws13_ref, ws2_ref, wpg_ref, wple_ref, lng_ref, lnb_ref, o_ref):
    x = x_ref[...]
    xb = x.astype(BF16)
    tm = x.shape[0]
    g = g_ref[...]
    parts = None
    for k in range(TOP_K):
        chunks = [g[:, k:k + 1] * c for c in _load_packed_chunks(yg_ref, tm, lead=k)]
        parts = chunks if parts is None else [a + c for a, c in zip(parts, chunks)]
    routed = jnp.concatenate(parts, axis=1)
    h13 = jnp.dot(xb, ws13_ref[...], preferred_element_type=F32)
    h = jax.nn.silu(h13[:, :EXPERT_FF]) * h13[:, EXPERT_FF:]
    shared = jnp.dot(h.astype(BF16), ws2_ref[...], preferred_element_type=F32)
    ple = (jax.nn.sigmoid(jnp.dot(xb, wpg_ref[...], preferred_element_type=F32))
           * jnp.dot(p_ref[...].astype(BF16), wple_ref[...], preferred_element_type=F32))
    o_ref[...] = _layer_norm(DN_ALPHA * x + routed + shared + ple, lng_ref[...], lnb_ref[...])


def _moe_out(x1, top_gates, p, yg_rows, w, tm):
    n = x1.shape[0]
    row = lambda width: pl.BlockSpec((tm, width), lambda i: (i, 0))
    return pl.pallas_call(
        _moe_out_kernel,
        grid=(n // tm,),
        in_specs=[row(D_MODEL), row(TOP_K), row(PLE_DIM),
                  pl.BlockSpec((TOP_K, tm * ROW_TILE, LANES), lambda i: (0, i, 0)),
                  _full((D_MODEL, 2 * EXPERT_FF)), _full((EXPERT_FF, D_MODEL)),
                  _full((D_MODEL, D_MODEL)), _full((PLE_DIM, D_MODEL)),
                  _full((1, D_MODEL)), _full((1, D_MODEL))],
        out_specs=row(D_MODEL),
        out_shape=jax.ShapeDtypeStruct((n, D_MODEL), F32),
        compiler_params=_params(("parallel",)),
        name="moe_out",
    )(x1, top_gates, p, yg_rows, w["ws13"], w["ws2"], w["w_ple_gate"], w["w_ple"], w["ln2_g"], w["ln2_b"])


def _moe_sorted(x1, x1_tiles, top_idx, top_gates, counts, p, w, anchor):
    n = x1.shape[0]
    n_blocks = n * TOP_K // MOE_BLOCK + N_EXPERTS
    n_rows = n_blocks * MOE_BLOCK
    cnt = counts.reshape(N_EXPERTS).astype(jnp.int32)
    padded = (cnt + MOE_BLOCK - 1) // MOE_BLOCK * MOE_BLOCK
    pend = jnp.cumsum(padded)
    pstart = (pend - padded).astype(F32).reshape(N_EXPERTS, 1)
    block_start = jnp.arange(n_blocks, dtype=jnp.int32) * MOE_BLOCK
    block_expert = jnp.minimum(jnp.sum((pend[None, :] <= block_start[:, None]).astype(jnp.int32), axis=1),
                               N_EXPERTS - 1)
    real_end = pend - padded + cnt
    own = block_expert[:, None] == jnp.arange(N_EXPERTS, dtype=jnp.int32)[None, :]
    block_valid = jnp.clip(jnp.sum(jnp.where(own, real_end[None, :], 0), axis=1) - block_start, 0, MOE_BLOCK)
    slots = _route(top_idx, pstart, 512)
    slots = jnp.transpose(slots.reshape(TOP_K, n // SC_WINDOW, SC_WINDOW), (1, 0, 2))
    xs = _sc_dispatch(x1_tiles.reshape(n, ROW_TILE, LANES), slots, n_rows)
    ys = _expert_ffn(xs.reshape(n_rows * ROW_TILE, LANES), block_expert, block_valid, w, anchor)
    yg = _sc_combine(ys.reshape(n_rows, ROW_TILE, LANES), slots, n)
    return _moe_out(x1, top_gates.T, p, yg.reshape(TOP_K, n * ROW_TILE, LANES), w, 512)


def _kv_rows(k, v, batch, seq, keep, g):
    cols = slice(g * GROUP_WIDTH, (g + 1) * GROUP_WIDTH)
    shape = (batch, keep, HEADS_PER_GROUP, HEAD_DIM)
    k_g = k.reshape(batch, seq, ATTN_WIDTH)[:, seq - keep:, cols].reshape(shape)
    v_g = v.reshape(batch, seq, ATTN_WIDTH)[:, seq - keep:, cols].reshape(shape)
    return jnp.stack([k_g, v_g], axis=2)


def _layer_prompt(x, p, w, ssm, anchor):
    batch, seq, _ = x.shape
    n = batch * seq
    x2 = x.reshape(n, D_MODEL)
    tabs = _rope_tables(jnp.arange(seq, dtype=jnp.int32))
    q, k, v, u_lo, u_hi = _in_proj(x2, w["w_in"], tabs, seq, 512)
    attn_o, attn_lse = _attn_prompt(q, k, v, batch, seq)
    zeros = jnp.zeros((batch, SSM_LANES), F32)
    y_lo, y_hi, h_re, h_im = _s5_scan(u_lo, u_hi, ssm, zeros, zeros, batch, 128)
    x1, x1_tiles, _, top_idx, top_gates, counts = _post_mixer(x2, attn_o, attn_lse, y_lo, y_hi, w, seq, 512)
    y = _moe_sorted(x1, x1_tiles, top_idx, top_gates, counts, p.reshape(n, PLE_DIM), w, anchor)
    kv = [_kv_rows(k, v, batch, seq, min(win, seq), g) for g, (win, _) in enumerate(DILATION_GROUPS)]
    h_last = jnp.stack([h_re, h_im], axis=-1).reshape(batch, SSM_GROUPS, SSM_STATE, 2)
    return y.reshape(batch, seq, D_MODEL), kv, h_last


def _sample_attention(x, caches, w):
    batch, seq, _ = x.shape
    assert seq == 1
    x2 = x.reshape(batch, D_MODEL)
    tabs = _rope_tables(jnp.full((batch,), PAST_LEN, dtype=jnp.int32))
    q, k, v, u_lo, u_hi = _in_proj(x2, w["w_in"], tabs, batch, batch)
    attn_o, attn_lse = _attn_sample(q, k, v, caches, 2)
    return x2, k, v, (u_lo, u_hi), attn_o, attn_lse


def _layer_sample(p, state, w, ssm, x2, k, v, u, attn_o, attn_lse):
    batch = x2.shape[0]
    h0 = state.reshape(batch, SSM_LANES, 2)
    y_lo, y_hi, h_re, h_im = _s5_scan(*u, ssm, h0[..., 0], h0[..., 1], batch, 1)
    x1, _, gates, _, _, _ = _post_mixer(x2, attn_o, attn_lse, y_lo, y_hi, w, batch, batch)
    y = _moe_ffn(x1, gates.T, p.reshape(batch, PLE_DIM), w, batch)
    kv = [_kv_rows(k, v, batch, 1, 1, g) for g in range(len(DILATION_GROUPS))]
    h_last = jnp.stack([h_re, h_im], axis=-1).reshape(batch, SSM_GROUPS, SSM_STATE, 2)
    return y.reshape(batch, 1, D_MODEL), kv, h_last


def _hi_lo_rows(t):
    hi = t.astype(BF16)
    return jnp.concatenate([hi, (t - hi.astype(F32)).astype(BF16)], axis=1).T


def kernel(x_prompt, x_sample, cache_kv_w128, cache_kv_w512, cache_kv_w2048, state_ssm, p_prompt, p_sample,
           w_in, a_re, a_im, log_dt, b_re, b_im, c_re, c_im, d_skip, w_glu, b_glu, w_attn_br, w_ssm_br,
           w_gate, b_gate, w_out, ln1_g, ln1_b, w_router, router_bias, w1, w3, w2, ws1, ws3, ws2,
           w_ple_gate, w_ple, ln2_g, ln2_b):
    assert w_in.shape[0] == DEPTH == 1
    l = 0
    row = lambda t: t[l].reshape(1, -1)
    w = {
        "w_in": w_in[l].astype(BF16),
        "w_glu": w_glu[l].astype(BF16), "b_glu": row(b_glu),
        "w_gate": w_gate[l].astype(BF16), "b_gate": row(b_gate),
        "w_attn_br": w_attn_br[l].astype(BF16), "w_ssm_br": w_ssm_br[l].astype(BF16),
        "w_out": w_out[l].astype(BF16), "ln1_g": row(ln1_g), "ln1_b": row(ln1_b),
        "w_router": _hi_lo_rows(w_router[l]), "router_bias": router_bias[l].reshape(N_EXPERTS, 1),
        "w1": w1[l], "w3": w3[l], "w2": w2[l],
        "ws13": jnp.concatenate([ws1[l], ws3[l]], axis=-1).astype(BF16), "ws2": ws2[l].astype(BF16),
        "w_ple_gate": w_ple_gate[l].astype(BF16), "w_ple": w_ple[l].astype(BF16),
        "ln2_g": row(ln2_g), "ln2_b": row(ln2_b),
    }
    ssm = _s5_params(a_re[l], a_im[l], log_dt[l], b_re[l], b_im[l], c_re[l], c_im[l], d_skip[l])
    caches = (cache_kv_w128[l], cache_kv_w512[l], cache_kv_w2048[l])
    sample = _sample_attention(x_sample, caches, w)
    yp, kv_p, h_p = _layer_prompt(x_prompt, p_prompt[l], w, ssm, anchor=sample[-1][:8, :LANES])
    ys, kv_s, h_s = _layer_sample(p_sample[l], state_ssm[l], w, ssm, *sample)
    return (yp, ys, kv_p[0][None], kv_s[0][None], kv_p[1][None], kv_s[1][None],
            kv_p[2][None], kv_s[2][None], h_p[None], h_s[None])
```

```python
import functools
import math

import jax
import jax.numpy as jnp
from jax import lax
from jax.experimental import pallas as pl
from jax.experimental.pallas import tpu as pltpu
from jax.experimental.pallas import tpu_sc as plsc

F32 = jnp.float32
BF16 = jnp.bfloat16

D_MODEL = 1024
HEAD_DIM = 64
HEADS_PER_GROUP = 4
DILATION_GROUPS = ((128, 1), (512, 4), (2048, 16))
N_BACK = 128
GROUP_WIDTH = HEADS_PER_GROUP * HEAD_DIM
ATTN_WIDTH = 3 * GROUP_WIDTH
ROPE_THETA = 10000.0
SSM_WIDTH = 256
SSM_GROUP = 16
SSM_GROUPS = 16
SSM_STATE = 64
SSM_LANES = SSM_GROUPS * SSM_STATE
IN_WIDTH = 3 * ATTN_WIDTH + SSM_WIDTH
N_EXPERTS = 64
TOP_K = 8
EXPERT_FF = 256
ROUTED_SCALE = 2.5
PLE_DIM = 256
DEPTH = 1
PAST_LEN = 8192
DN_ALPHA = (2.0 * DEPTH) ** 0.25
LN_EPS = 1e-5

LANES = 128
ROW_TILE = D_MODEL // LANES // 2
SC_CORES = 2
SC_SUBCORES = 16
SC_WINDOW = 64
MOE_BLOCK = 1024
POST_MIXER_SUBTILES = 2
EXPERT_FFN_SUBTILES = 4
ATTN_CHUNK = 2048
ATTN_UNROLL = 8
VMEM_LIMIT = 56 * 1024 * 1024


def _params(semantics):
    return pltpu.CompilerParams(dimension_semantics=semantics, vmem_limit_bytes=VMEM_LIMIT)


def _full(shape):
    return pl.BlockSpec(shape, lambda *_: (0,) * len(shape))


def _interleaved_rows(seq, rows, n_seq, row0=0):
    start = row0 * n_seq + seq
    return pl.ds(start, rows) if n_seq == 1 else pl.ds(start, rows, stride=n_seq)


def _in_proj_kernel(x_ref, w_ref, cos_ref, sina_ref, sinb_ref, q_ref, k_ref, v_ref, ulo_ref, uhi_ref, *, n_seq):
    xb = x_ref[...].astype(BF16)
    cos = cos_ref[...]
    sin_a = sina_ref[...]
    sin_b = sinb_ref[...]

    def rope_store(col0, out_ref, scale):
        t = jnp.dot(xb, w_ref[:, col0:col0 + ATTN_WIDTH], preferred_element_type=F32)
        for c in range(ATTN_WIDTH // LANES):
            xc = t[:, c * LANES:(c + 1) * LANES]
            r = xc * cos + pltpu.roll(xc, LANES - 32, 1) * sin_a + pltpu.roll(xc, 32, 1) * sin_b
            out_ref[:, c * LANES:(c + 1) * LANES] = r * scale if scale != 1.0 else r

    rope_store(0, q_ref, HEAD_DIM ** -0.5)
    rope_store(ATTN_WIDTH, k_ref, 1.0)
    v_ref[...] = jnp.dot(xb, w_ref[:, 2 * ATTN_WIDTH:3 * ATTN_WIDTH], preferred_element_type=F32)
    u = jnp.dot(xb, w_ref[:, 3 * ATTN_WIDTH:], preferred_element_type=F32)
    rows = _interleaved_rows(pl.program_id(1), u.shape[0], n_seq)
    ulo_ref[rows, :] = u[:, :LANES]
    uhi_ref[rows, :] = u[:, LANES:]


def _in_proj(x, w_in_bf, rope_tabs, rows_per_seq, tm):
    n = x.shape[0]
    tiles_per_seq = rows_per_seq // tm
    n_seq = n // rows_per_seq
    tab_tiles = rope_tabs[0].shape[0] // tm
    tab_spec = pl.BlockSpec((tm, LANES), lambda t, s: (t % tab_tiles, 0))
    row_spec = pl.BlockSpec((tm, ATTN_WIDTH), lambda t, s: (s * tiles_per_seq + t, 0))
    u_spec = pl.BlockSpec((tm * n_seq, LANES), lambda t, s: (t, 0))
    return pl.pallas_call(
        functools.partial(_in_proj_kernel, n_seq=n_seq),
        grid=(tiles_per_seq, n_seq),
        in_specs=[pl.BlockSpec((tm, D_MODEL), lambda t, s: (s * tiles_per_seq + t, 0)),
                  _full((D_MODEL, IN_WIDTH)), tab_spec, tab_spec, tab_spec],
        out_specs=[row_spec, row_spec, row_spec, u_spec, u_spec],
        out_shape=[jax.ShapeDtypeStruct((n, ATTN_WIDTH), F32)] * 3 + [jax.ShapeDtypeStruct((n, LANES), F32)] * 2,
        compiler_params=_params(("parallel", "arbitrary")),
        name="in_proj",
    )(x, w_in_bf, *rope_tabs)


def _rope_tables(pos):
    half = HEAD_DIM // 2
    inv = ROPE_THETA ** (-jnp.arange(half, dtype=F32) / half)
    ang = pos.astype(F32)[:, None] * inv[None, :]
    cos = jnp.tile(jnp.cos(ang), (1, LANES // half))
    sin = jnp.tile(jnp.sin(ang), (1, LANES // half))
    first_half = (jnp.arange(LANES) % HEAD_DIM) < half
    sin_a = jnp.where(first_half[None, :], -sin, 0.0)
    sin_b = jnp.where(first_half[None, :], 0.0, sin)
    return cos, sin_a, sin_b


def _band_attention(q, k, v, mask):
    head_of_lane = lax.broadcasted_iota(jnp.int32, (N_BACK, LANES), 1) // HEAD_DIM
    kb = k.astype(BF16)
    vb = v.astype(BF16)
    o = lse = None
    for h in range(LANES // HEAD_DIM):
        qh = jnp.where(head_of_lane == h, q, 0.0).astype(BF16)
        logits = lax.dot_general(qh, kb, (((1,), (1,)), ((), ())), preferred_element_type=F32) + mask
        m = jnp.max(logits, axis=1, keepdims=True)
        p = jnp.exp(logits - m)
        l = jnp.sum(p, axis=1, keepdims=True)
        o_h = jnp.dot(p.astype(BF16), vb, preferred_element_type=F32) * (1.0 / l)
        lse_h = jnp.broadcast_to(m + jnp.log(l), (N_BACK, LANES))
        o = o_h if o is None else jnp.where(head_of_lane == h, o_h, o)
        lse = lse_h if lse is None else jnp.where(head_of_lane == h, lse_h, lse)
    return o, lse


def _attn_prompt_kernel(q_ref, kp_ref, kc_ref, vp_ref, vc_ref, o_ref, lse_ref):
    c = pl.program_id(1)
    g = pl.program_id(3)
    ch = ATTN_CHUNK
    qi = lax.broadcasted_iota(jnp.int32, (N_BACK, 2 * N_BACK), 0)
    kj = lax.broadcasted_iota(jnp.int32, (N_BACK, 2 * N_BACK), 1)
    dist = qi + N_BACK - kj
    band = jnp.where(dist >= 0, jnp.where(dist <= N_BACK, 0.0, -jnp.inf), -jnp.inf)
    band_first = jnp.where(kj >= N_BACK, band, -jnp.inf)

    def group_body(d):
        span = N_BACK * d
        n_sub = ch // N_BACK

        def rows(start, size):
            return pl.ds(start, size) if d == 1 else pl.ds(start, size, stride=d)

        def store(q0, o, lse):
            o_ref[rows(q0, N_BACK), :] = o
            lse_ref[rows(q0, N_BACK), :] = lse

        def head_block(r, carry):
            k = jnp.concatenate([kp_ref[rows(ch - span + r, N_BACK), :], kc_ref[rows(r, N_BACK), :]], axis=0)
            v = jnp.concatenate([vp_ref[rows(ch - span + r, N_BACK), :], vc_ref[rows(r, N_BACK), :]], axis=0)
            mask = jnp.where(c == 0, band_first, band)
            store(r, *_band_attention(q_ref[rows(r, N_BACK), :], k, v, mask))
            return carry

        def inner_block(idx, carry):
            s = idx // d
            r = idx % d
            k0 = (s - 1) * span + r
            store(s * span + r, *_band_attention(q_ref[rows(s * span + r, N_BACK), :],
                                                 kc_ref[rows(k0, 2 * N_BACK), :],
                                                 vc_ref[rows(k0, 2 * N_BACK), :], band))
            return carry

        lax.fori_loop(0, d, head_block, 0, unroll=min(d, ATTN_UNROLL))
        if n_sub > d:
            trips = n_sub - d
            lax.fori_loop(d, n_sub, inner_block, 0,
                          unroll=max(u for u in range(1, ATTN_UNROLL + 1) if trips % u == 0))

    for gi, (_, d) in enumerate(DILATION_GROUPS):
        pl.when(g == gi)(functools.partial(group_body, d))


def _attn_prompt(q, k, v, batch, seq):
    ch = ATTN_CHUNK
    cps = seq // ch
    n = batch * seq
    pairs = GROUP_WIDTH // LANES
    cur = lambda b, c, hp, g: (b * cps + c, g * pairs + hp)
    prev = lambda b, c, hp, g: (b * cps + jnp.maximum(c - 1, 0), g * pairs + hp)
    blk = (ch, LANES)
    return pl.pallas_call(
        _attn_prompt_kernel,
        grid=(batch, cps, pairs, len(DILATION_GROUPS)),
        in_specs=[pl.BlockSpec(blk, cur), pl.BlockSpec(blk, prev), pl.BlockSpec(blk, cur),
                  pl.BlockSpec(blk, prev), pl.BlockSpec(blk, cur)],
        out_specs=[pl.BlockSpec(blk, cur), pl.BlockSpec(blk, cur)],
        out_shape=[jax.ShapeDtypeStruct((n, ATTN_WIDTH), F32)] * 2,
        compiler_params=_params(("parallel", "parallel", "parallel", "parallel")),
        name="attn_prompt",
    )(q, k, k, v, v)


def _attn_sample_kernel(q_ref, k_ref, v_ref, *refs, groups):
    *cache_refs, o_ref, lse_ref = refs
    bt = q_ref.shape[0]
    for b in range(bt):
        for slot, (g, c_ref) in enumerate(zip(groups, cache_refs)):
            win, d = DILATION_GROUPS[g]
            pos = lax.broadcasted_iota(jnp.int32, (1, win), 1)
            off_stride = (pos % d) != 0
            j0 = g * HEADS_PER_GROUP
            out0 = slot * HEADS_PER_GROUP
            heads = range(HEADS_PER_GROUP)
            qs = [q_ref[b, :, j0 + h:j0 + h + 1] for h in heads]
            s_c = jnp.concatenate([jnp.sum(c_ref[b, 0, h] * qs[h], axis=0, keepdims=True) for h in heads],
                                  axis=0)
            s_c = jnp.where(off_stride, -jnp.inf, s_c)
            s_new = jnp.concatenate([jnp.sum(k_ref[b, :, j0 + h:j0 + h + 1] * qs[h], axis=0, keepdims=True)
                                     for h in heads], axis=0)
            m = jnp.maximum(jnp.max(s_c, axis=1, keepdims=True), s_new)
            p_c = jnp.exp(s_c - m)
            p_new = jnp.exp(s_new - m)
            l = jnp.sum(p_c, axis=1, keepdims=True) + p_new
            inv_l = 1.0 / l
            lse_ref[b, out0:out0 + HEADS_PER_GROUP, :] = m + jnp.log(l)
            for h in heads:
                num = (jnp.sum(c_ref[b, 1, h] * p_c[h:h + 1, :], axis=1, keepdims=True)
                       + p_new[h:h + 1, :] * v_ref[b, :, j0 + h:j0 + h + 1])
                o_ref[b, :, out0 + h:out0 + h + 1] = num * inv_l[h:h + 1, :]


def _attn_sample(q, k, v, caches, groups, bt):
    b = q.shape[0]
    n_heads = ATTN_WIDTH // HEAD_DIM
    out_heads = HEADS_PER_GROUP * len(groups)
    views, specs = [], []
    for g in groups:
        win, d = DILATION_GROUPS[g]
        assert caches[g].shape[1] == win == N_BACK * d
        views.append(jnp.transpose(caches[g], (0, 2, 3, 4, 1)))
        specs.append(pl.BlockSpec((bt, 2, HEADS_PER_GROUP, HEAD_DIM, win), lambda i: (i, 0, 0, 0, 0)))
    col_spec = pl.BlockSpec((bt, HEAD_DIM, n_heads), lambda i: (i, 0, 0))
    cols = lambda t: jnp.transpose(t.reshape(b, n_heads, HEAD_DIM), (0, 2, 1))
    return pl.pallas_call(
        functools.partial(_attn_sample_kernel, groups=groups),
        grid=(b // bt,),
        in_specs=[col_spec, col_spec, col_spec] + specs,
        out_specs=[pl.BlockSpec((bt, HEAD_DIM, out_heads), lambda i: (i, 0, 0)),
                   pl.BlockSpec((bt, out_heads, 1), lambda i: (i, 0, 0))],
        out_shape=[jax.ShapeDtypeStruct((b, HEAD_DIM, out_heads), F32),
                   jax.ShapeDtypeStruct((b, out_heads, 1), F32)],
        compiler_params=_params(("parallel",)),
        name="attn_sample",
    )(cols(q), cols(k), cols(v), *views)


def _s5_scan_kernel(ulo_ref, uhi_ref, bmat_ref, cmat_ref, are_ref, aim_ref, d_ref, h0re_ref, h0im_ref,
                    ylo_ref, yhi_ref, hre_ref, him_ref, hist_sc, *, bg, steps):
    t_chunk = pl.program_id(0)

    @pl.when(t_chunk == 0)
    def _():
        hre_ref[...] = h0re_ref[...]
        him_ref[...] = h0im_ref[...]

    u = jnp.concatenate([ulo_ref[...], uhi_ref[...]], axis=1)
    hist_sc[...] = jnp.dot(u.astype(BF16), bmat_ref[...], preferred_element_type=F32)
    a_re = jnp.broadcast_to(are_ref[...], (bg, SSM_LANES))
    a_im = jnp.broadcast_to(aim_ref[...], (bg, SSM_LANES))

    def step(t, carry):
        h_re, h_im = carry
        rows = pl.ds(pl.multiple_of(t * bg, bg), bg)
        n_re = a_re * h_re - a_im * h_im + hist_sc[rows, 0:SSM_LANES]
        n_im = a_re * h_im + a_im * h_re + hist_sc[rows, SSM_LANES:2 * SSM_LANES]
        hist_sc[rows, 0:SSM_LANES] = n_re
        hist_sc[rows, SSM_LANES:2 * SSM_LANES] = n_im
        return n_re, n_im

    h_re, h_im = lax.fori_loop(0, steps, step, (hre_ref[...], him_ref[...]))
    hre_ref[...] = h_re
    him_ref[...] = h_im
    y = jnp.dot(hist_sc[...].astype(BF16), cmat_ref[...], preferred_element_type=F32) + d_ref[...] * u
    ylo_ref[...] = y[:, :LANES]
    yhi_ref[...] = y[:, LANES:]


def _s5_scan(u_lo, u_hi, ssm, h0_re, h0_im, bg, steps):
    rows = u_lo.shape[0]
    blk = steps * bg
    kern = functools.partial(_s5_scan_kernel, bg=bg, steps=steps)
    state_spec = _full((bg, SSM_LANES))
    half_spec = pl.BlockSpec((blk, LANES), lambda i: (i, 0))
    return pl.pallas_call(
        kern,
        grid=(rows // blk,),
        in_specs=[half_spec, half_spec,
                  _full((SSM_WIDTH, 2 * SSM_LANES)), _full((2 * SSM_LANES, SSM_WIDTH)),
                  _full((1, SSM_LANES)), _full((1, SSM_LANES)), _full((1, SSM_WIDTH)),
                  state_spec, state_spec],
        out_specs=[half_spec, half_spec, state_spec, state_spec],
        out_shape=[jax.ShapeDtypeStruct((rows, LANES), F32), jax.ShapeDtypeStruct((rows, LANES), F32),
                   jax.ShapeDtypeStruct((bg, SSM_LANES), F32), jax.ShapeDtypeStruct((bg, SSM_LANES), F32)],
        scratch_shapes=[pltpu.VMEM((blk, 2 * SSM_LANES), F32)],
        compiler_params=_params(("arbitrary",)),
        name="s5_scan",
    )(u_lo, u_hi, ssm["bmat"], ssm["cmat"], ssm["a_re"], ssm["a_im"], ssm["d_skip"], h0_re, h0_im)


def _s5_params(a_re, a_im, log_dt, b_re, b_im, c_re, c_im, d_skip):
    dt = jnp.exp(log_dt)[:, None]
    mag = jnp.exp(a_re * dt)
    abar_re = mag * jnp.cos(a_im * dt)
    abar_im = mag * jnp.sin(a_im * dt)
    a2 = a_re * a_re + a_im * a_im
    nr = abar_re - 1.0
    coef_re = (nr * a_re + abar_im * a_im) / a2
    coef_im = (abar_im * a_re - nr * a_im) / a2
    bb_re = coef_re[..., None] * b_re - coef_im[..., None] * b_im
    bb_im = coef_re[..., None] * b_im + coef_im[..., None] * b_re
    eye = jnp.eye(SSM_GROUPS, dtype=F32)
    to_b = lambda t: jnp.einsum("gpc,gh->gchp", t, eye).reshape(SSM_WIDTH, SSM_LANES)
    to_c = lambda t: jnp.einsum("gcp,gh->gphc", t, eye).reshape(SSM_LANES, SSM_WIDTH)
    return {
        "bmat": jnp.concatenate([to_b(bb_re), to_b(bb_im)], axis=1).astype(BF16),
        "cmat": jnp.concatenate([to_c(c_re), -to_c(c_im)], axis=0).astype(BF16),
        "a_re": abar_re.reshape(1, SSM_LANES), "a_im": abar_im.reshape(1, SSM_LANES),
        "d_skip": d_skip.reshape(1, SSM_WIDTH),
    }


def _layer_norm(z, g, b):
    mu = jnp.mean(z, axis=-1, keepdims=True)
    zc = z - mu
    var = jnp.mean(zc * zc, axis=-1, keepdims=True)
    return zc * lax.rsqrt(var + LN_EPS) * g + b


def _merge_groups(o, lse):
    parts = [slice(g * GROUP_WIDTH, (g + 1) * GROUP_WIDTH) for g in range(len(DILATION_GROUPS))]
    top = lse[:, parts[0]]
    for cols in parts[1:]:
        top = jnp.maximum(top, lse[:, cols])
    num = den = None
    for cols in parts:
        w = jnp.exp(lse[:, cols] - top)
        num = w * o[:, cols] if num is None else num + w * o[:, cols]
        den = w if den is None else den + w
    return num / den


def _store_packed_rows(ref, x, row0=0):
    rows = x.shape[0]
    for j in range(ROW_TILE):
        lo = x[:, j * LANES:(j + 1) * LANES].astype(BF16).astype(F32)
        hi = x[:, (j + ROW_TILE) * LANES:(j + ROW_TILE + 1) * LANES].astype(BF16).astype(F32)
        word = (lax.bitcast_convert_type(lo, jnp.uint32) >> 16) | lax.bitcast_convert_type(hi, jnp.uint32)
        ref[pl.ds(row0 * ROW_TILE + j, rows, stride=ROW_TILE), :] = lax.bitcast_convert_type(word, jnp.int32)


def _load_packed_chunks(ref, rows, lead=None, row0=0):
    lows, highs = [], []
    for j in range(ROW_TILE):
        idx = (pl.ds(row0 * ROW_TILE + j, rows, stride=ROW_TILE), slice(None))
        word = lax.bitcast_convert_type(ref[idx] if lead is None else ref[(lead,) + idx], jnp.uint32)
        lows.append(lax.bitcast_convert_type(word << 16, F32))
        highs.append(lax.bitcast_convert_type(word & jnp.uint32(0xFFFF0000), F32))
    return lows + highs


def _post_mixer_kernel(x_ref, ao_ref, lse_ref, ylo_ref, yhi_ref, wglu_ref, bglu_ref, wgate_ref, bgate_ref,
                       wab_ref, wsb_ref, wout_ref, lng_ref, lnb_ref, wr_ref, rb_ref,
                       x1_ref, x1t_ref, gate_ref, idx_ref, topg_ref, cnt_ref, *, n_seq):
    seq = pl.program_id(1)

    @pl.when(jnp.logical_and(pl.program_id(0) == 0, seq == 0))
    def _():
        cnt_ref[...] = jnp.zeros(cnt_ref.shape, F32)

    tm = x_ref.shape[0]
    sub = tm // POST_MIXER_SUBTILES if tm % (8 * POST_MIXER_SUBTILES) == 0 else tm
    for r0 in range(0, tm, sub):
        rows = slice(r0, r0 + sub)
        x = x_ref[rows, :]
        xb = x.astype(BF16)
        y_rows = _interleaved_rows(seq, sub, n_seq, r0)
        s = jax.nn.gelu(jnp.concatenate([ylo_ref[y_rows, :], yhi_ref[y_rows, :]], axis=1))
        s = s * jax.nn.sigmoid(jnp.dot(s.astype(BF16), wglu_ref[...], preferred_element_type=F32) + bglu_ref[...])
        gates = jax.nn.sigmoid(jnp.dot(xb, wgate_ref[...], preferred_element_type=F32) + bgate_ref[...])
        attn_o = _merge_groups(ao_ref[rows, :], lse_ref[rows, :])
        attn_br = jnp.dot(attn_o.astype(BF16), wab_ref[...], preferred_element_type=F32)
        ssm_br = jnp.dot(s.astype(BF16), wsb_ref[...], preferred_element_type=F32)
        merged = gates[:, :D_MODEL] * attn_br + gates[:, D_MODEL:] * ssm_br
        mix = jnp.dot(merged.astype(BF16), wout_ref[...], preferred_element_type=F32)
        x1 = _layer_norm(DN_ALPHA * x + mix, lng_ref[...], lnb_ref[...])
        x1_ref[rows, :] = x1
        _store_packed_rows(x1t_ref, x1, r0)

        x1_hi = x1.astype(BF16)
        x1_lo = (x1 - x1_hi.astype(F32)).astype(BF16)
        prod = lax.dot_general(wr_ref[...], jnp.concatenate([x1_hi, x1_lo], axis=0),
                               (((1,), (1,)), ((), ())), preferred_element_type=F32)
        logits = ((prod[:N_EXPERTS, :sub] + prod[N_EXPERTS:, :sub] + prod[:N_EXPERTS, sub:])
                  + prod[N_EXPERTS:, sub:])
        scores = jax.nn.sigmoid(logits)
        sel = scores + rb_ref[...]
        expert = lax.broadcasted_iota(jnp.int32, sel.shape, 0).astype(F32)
        chosen = jnp.zeros(sel.shape, F32)
        idx_rows, score_rows = [], []
        for _ in range(TOP_K):
            top = jnp.max(sel, axis=0, keepdims=True)
            first = jnp.min(jnp.where(sel == top, expert, float(N_EXPERTS)), axis=0, keepdims=True)
            hit = expert == first
            chosen = jnp.where(hit, 1.0, chosen)
            sel = jnp.where(hit, -jnp.inf, sel)
            idx_rows.append(first)
            score_rows.append(jnp.sum(jnp.where(hit, scores, 0.0), axis=0, keepdims=True))
        norm = ROUTED_SCALE / jnp.sum(scores * chosen, axis=0, keepdims=True)
        gate_ref[:, rows] = scores * chosen * norm
        idx_ref[:, rows] = jnp.concatenate(idx_rows, axis=0)
        topg_ref[:, rows] = jnp.concatenate(score_rows, axis=0) * norm
        cnt_ref[...] += jnp.sum(chosen, axis=1, keepdims=True)


def _post_mixer(x, attn_o, attn_lse, y_lo, y_hi, w, rows_per_seq, tm):
    n = x.shape[0]
    tiles_per_seq = rows_per_seq // tm
    n_seq = n // rows_per_seq
    row = lambda width: pl.BlockSpec((tm, width), lambda t, s: (s * tiles_per_seq + t, 0))
    col = lambda height: pl.BlockSpec((height, tm), lambda t, s: (0, s * tiles_per_seq + t))
    y_spec = pl.BlockSpec((tm * n_seq, LANES), lambda t, s: (t, 0))
    return pl.pallas_call(
        functools.partial(_post_mixer_kernel, n_seq=n_seq),
        grid=(tiles_per_seq, n_seq),
        in_specs=[row(D_MODEL), row(ATTN_WIDTH), row(ATTN_WIDTH), y_spec, y_spec,
                  _full((SSM_WIDTH, SSM_WIDTH)), _full((1, SSM_WIDTH)),
                  _full((D_MODEL, 2 * D_MODEL)), _full((1, 2 * D_MODEL)),
                  _full((GROUP_WIDTH, D_MODEL)), _full((SSM_WIDTH, D_MODEL)), _full((D_MODEL, D_MODEL)),
                  _full((1, D_MODEL)), _full((1, D_MODEL)),
                  _full((2 * N_EXPERTS, D_MODEL)), _full((N_EXPERTS, 1))],
        out_specs=[row(D_MODEL),
                   pl.BlockSpec((tm * ROW_TILE, LANES), lambda t, s: (s * tiles_per_seq + t, 0)),
                   col(N_EXPERTS), col(TOP_K), col(TOP_K), _full((N_EXPERTS, 1))],
        out_shape=[jax.ShapeDtypeStruct((n, D_MODEL), F32), jax.ShapeDtypeStruct((n * ROW_TILE, LANES), jnp.int32),
                   jax.ShapeDtypeStruct((N_EXPERTS, n), F32), jax.ShapeDtypeStruct((TOP_K, n), F32),
                   jax.ShapeDtypeStruct((TOP_K, n), F32), jax.ShapeDtypeStruct((N_EXPERTS, 1), F32)],
        compiler_params=_params(("arbitrary", "arbitrary")),
        name="post_mixer",
    )(x, attn_o, attn_lse, y_lo, y_hi, w["w_glu"], w["b_glu"], w["w_gate"], w["b_gate"], w["w_attn_br"], w["w_ssm_br"],
      w["w_out"], w["ln1_g"], w["ln1_b"], w["w_router"], w["router_bias"])


def _moe_ffn_kernel(x_ref, gate_ref, p_ref, w1_ref, w3_ref, w2_ref, ws13_ref, ws2_ref, wpg_ref, wple_ref,
                    lng_ref, lnb_ref, o_ref, acc_sc, xb_sc):
    e = pl.program_id(1)

    def glu_ffn(xb, w13, w2, row_scale):
        if isinstance(w13, tuple):
            h1 = jnp.dot(xb, w13[0].astype(BF16), preferred_element_type=F32)
            h3 = jnp.dot(xb, w13[1].astype(BF16), preferred_element_type=F32)
        else:
            h13 = jnp.dot(xb, w13, preferred_element_type=F32)
            h1, h3 = h13[:, :EXPERT_FF], h13[:, EXPERT_FF:]
        h = jax.nn.silu(h1) * h3
        if row_scale is not None:
            h = h * row_scale
        return jnp.dot(h.astype(BF16), w2.astype(BF16), preferred_element_type=F32)

    @pl.when(e == 0)
    def _():
        xb = x_ref[...].astype(BF16)
        xb_sc[...] = xb
        ple = (jax.nn.sigmoid(jnp.dot(xb, wpg_ref[...], preferred_element_type=F32))
               * jnp.dot(p_ref[...].astype(BF16), wple_ref[...], preferred_element_type=F32))
        acc_sc[...] = glu_ffn(xb, ws13_ref[...], ws2_ref[...], None) + ple

    gates = gate_ref[...]
    lane = lax.broadcasted_iota(jnp.int32, gates.shape, 1)
    g_col = jnp.sum(jnp.where(lane == e, gates, 0.0), axis=-1, keepdims=True)
    acc_sc[...] += glu_ffn(xb_sc[...], (w1_ref[0], w3_ref[0]), w2_ref[0], g_col)

    @pl.when(e == N_EXPERTS - 1)
    def _():
        o_ref[...] = _layer_norm(DN_ALPHA * x_ref[...] + acc_sc[...], lng_ref[...], lnb_ref[...])


def _moe_ffn(x1, gates, p, w, tm):
    n = x1.shape[0]
    row = lambda width: pl.BlockSpec((tm, width), lambda i, e: (i, 0))
    return pl.pallas_call(
        _moe_ffn_kernel,
        grid=(n // tm, N_EXPERTS),
        in_specs=[row(D_MODEL), row(N_EXPERTS), row(PLE_DIM),
                  pl.BlockSpec((1, D_MODEL, EXPERT_FF), lambda i, e: (e, 0, 0)),
                  pl.BlockSpec((1, D_MODEL, EXPERT_FF), lambda i, e: (e, 0, 0)),
                  pl.BlockSpec((1, EXPERT_FF, D_MODEL), lambda i, e: (e, 0, 0)),
                  _full((D_MODEL, 2 * EXPERT_FF)), _full((EXPERT_FF, D_MODEL)),
                  _full((D_MODEL, D_MODEL)), _full((PLE_DIM, D_MODEL)),
                  _full((1, D_MODEL)), _full((1, D_MODEL))],
        out_specs=row(D_MODEL),
        out_shape=jax.ShapeDtypeStruct((n, D_MODEL), F32),
        scratch_shapes=[pltpu.VMEM((tm, D_MODEL), F32), pltpu.VMEM((tm, D_MODEL), BF16)],
        compiler_params=_params(("parallel", "arbitrary")),
        name="moe_ffn",
    )(x1, gates, p, w["w1"], w["w3"], w["w2"], w["ws13"], w["ws2"], w["w_ple_gate"], w["w_ple"],
      w["ln2_g"], w["ln2_b"])


def _route_kernel(idx_ref, pstart_ref, earlier_ref, slot_ref, base_sc):
    @pl.when(pl.program_id(0) == 0)
    def _():
        base_sc[...] = jnp.zeros(base_sc.shape, F32)

    idx = idx_ref[...]
    tm = idx.shape[1]
    expert = lax.broadcasted_iota(jnp.int32, (N_EXPERTS, tm), 0).astype(F32)
    hits = [expert == idx[k:k + 1, :] for k in range(TOP_K)]
    member = jnp.zeros((N_EXPERTS, tm), F32)
    for hit in hits:
        member = member + jnp.where(hit, 1.0, 0.0)
    row = (jnp.dot(member.astype(BF16), earlier_ref[...], preferred_element_type=F32)
           + base_sc[...] + pstart_ref[...])
    slots = [jnp.sum(jnp.where(hit, row, 0.0), axis=0, keepdims=True) for hit in hits]
    slot_ref[...] = jnp.concatenate(slots, axis=0).astype(jnp.int32)
    base_sc[...] += jnp.sum(member, axis=1, keepdims=True)


def _route(top_idx, pstart, tm):
    n = top_idx.shape[1]
    earlier = jnp.triu(jnp.ones((tm, tm), F32), k=1).astype(BF16)
    return pl.pallas_call(
        _route_kernel,
        grid=(n // tm,),
        in_specs=[pl.BlockSpec((TOP_K, tm), lambda i: (0, i)), _full((N_EXPERTS, 1)), _full((tm, tm))],
        out_specs=pl.BlockSpec((TOP_K, tm), lambda i: (0, i)),
        out_shape=jax.ShapeDtypeStruct((TOP_K, n), jnp.int32),
        scratch_shapes=[pltpu.VMEM((N_EXPERTS, 1), F32)],
        compiler_params=_params(("arbitrary",)),
        name="route",
    )(top_idx, pstart, earlier)


def _sc_mesh():
    return plsc.VectorSubcoreMesh(core_axis_name="c", subcore_axis_name="s",
                                  num_cores=SC_CORES, num_subcores=SC_SUBCORES)


def _sc_dispatch(x_tiles, slots, n_rows):
    n = x_tiles.shape[0]
    wins_per_worker = n // SC_WINDOW // (SC_CORES * SC_SUBCORES)

    def body(x_hbm, slot_hbm, xs_hbm, idx_v, rows_v, sem):
        wid = lax.axis_index("s") * SC_CORES + lax.axis_index("c")

        @pl.loop(0, wins_per_worker)
        def _(i):
            win = wid * wins_per_worker + i
            pltpu.sync_copy(slot_hbm.at[win], idx_v)
            pltpu.sync_copy(x_hbm.at[pl.ds(win * SC_WINDOW, SC_WINDOW)], rows_v)
            copies = [pltpu.async_copy(rows_v, xs_hbm.at[idx_v.at[k]], sem) for k in range(TOP_K)]
            for copy in copies:
                copy.wait()

    return pl.kernel(
        body, out_type=jax.ShapeDtypeStruct((n_rows, ROW_TILE, LANES), jnp.int32), mesh=_sc_mesh(),
        scratch_types=[pltpu.VMEM((TOP_K, SC_WINDOW), jnp.int32),
                       pltpu.VMEM((SC_WINDOW, ROW_TILE, LANES), jnp.int32),
                       pltpu.SemaphoreType.DMA],
        name="sc_dispatch",
    )(x_tiles, slots)


def _sc_combine(y_tiles, slots, n):
    wins_per_worker = n // SC_WINDOW // (SC_CORES * SC_SUBCORES)

    def body(ys_hbm, slot_hbm, yg_hbm, idx_v, rows_a, rows_b, gather_sems, write_sems):
        wid = lax.axis_index("s") * SC_CORES + lax.axis_index("c")
        bufs = (rows_a, rows_b)

        @pl.loop(0, wins_per_worker)
        def _(i):
            win = wid * wins_per_worker + i
            pltpu.sync_copy(slot_hbm.at[win], idx_v)

            def gather(k):
                return pltpu.async_copy(ys_hbm.at[idx_v.at[k]], bufs[k % 2], gather_sems.at[k % 2])

            def write(k):
                return pltpu.async_copy(bufs[k % 2], yg_hbm.at[k, pl.ds(win * SC_WINDOW, SC_WINDOW)],
                                        write_sems.at[k % 2])

            gathers = {0: gather(0)}
            writes = {}
            for k in range(TOP_K):
                gathers[k].wait()
                if k + 1 < TOP_K:
                    if k >= 1:
                        writes[k - 1].wait()
                    gathers[k + 1] = gather(k + 1)
                writes[k] = write(k)
            writes[TOP_K - 2].wait()
            writes[TOP_K - 1].wait()

    return pl.kernel(
        body, out_type=jax.ShapeDtypeStruct((TOP_K, n, ROW_TILE, LANES), jnp.int32), mesh=_sc_mesh(),
        scratch_types=[pltpu.VMEM((TOP_K, SC_WINDOW), jnp.int32),
                       pltpu.VMEM((SC_WINDOW, ROW_TILE, LANES), jnp.int32),
                       pltpu.VMEM((SC_WINDOW, ROW_TILE, LANES), jnp.int32),
                       pltpu.SemaphoreType.DMA((2,)), pltpu.SemaphoreType.DMA((2,))],
        name="sc_combine",
    )(y_tiles, slots)


def _expert_ffn_kernel(bexp_ref, valid_ref, xs_ref, w1_ref, w3_ref, w2_ref, anchor_ref, ys_ref, w13_sc, w2_sc):
    del anchor_ref
    i = pl.program_id(0)
    valid = valid_ref[i]

    @pl.when(jnp.logical_or(i == 0, bexp_ref[i] != bexp_ref[jnp.maximum(i - 1, 0)]))
    def _():
        w13_sc[:, :EXPERT_FF] = w1_ref[0].astype(BF16)
        w13_sc[:, EXPERT_FF:] = w3_ref[0].astype(BF16)
        w2_sc[...] = w2_ref[0].astype(BF16)

    def ffn(r0, rows):
        x = jnp.concatenate(_load_packed_chunks(xs_ref, rows, row0=r0), axis=1)
        h13 = jnp.dot(x.astype(BF16), w13_sc[...], preferred_element_type=F32)
        h = jax.nn.silu(h13[:, :EXPERT_FF]) * h13[:, EXPERT_FF:]
        _store_packed_rows(ys_ref, jnp.dot(h.astype(BF16), w2_sc[...], preferred_element_type=F32), r0)

    @pl.when(valid == MOE_BLOCK)
    def _():
        ffn(0, MOE_BLOCK)

    @pl.when(jnp.logical_and(valid > 0, valid < MOE_BLOCK))
    def _():
        sub = MOE_BLOCK // EXPERT_FFN_SUBTILES
        for r0 in range(0, MOE_BLOCK, sub):
            pl.when(valid > r0)(functools.partial(ffn, r0, sub))


def _expert_ffn(xs_rows, block_expert, block_valid, w, anchor):
    n_blocks = block_expert.shape[0]
    blk = (MOE_BLOCK * ROW_TILE, LANES)
    return pl.pallas_call(
        _expert_ffn_kernel,
        grid_spec=pltpu.PrefetchScalarGridSpec(
            num_scalar_prefetch=2, grid=(n_blocks,),
            in_specs=[pl.BlockSpec(blk, lambda i, be, nu: (i, 0)),
                      pl.BlockSpec((1, D_MODEL, EXPERT_FF), lambda i, be, nu: (be[i], 0, 0)),
                      pl.BlockSpec((1, D_MODEL, EXPERT_FF), lambda i, be, nu: (be[i], 0, 0)),
                      pl.BlockSpec((1, EXPERT_FF, D_MODEL), lambda i, be, nu: (be[i], 0, 0)),
                      pl.BlockSpec((8, LANES), lambda i, be, nu: (0, 0))],
            out_specs=pl.BlockSpec(blk, lambda i, be, nu: (i, 0)),
            scratch_shapes=[pltpu.VMEM((D_MODEL, 2 * EXPERT_FF), BF16), pltpu.VMEM((EXPERT_FF, D_MODEL), BF16)]),
        out_shape=jax.ShapeDtypeStruct(xs_rows.shape, jnp.int32),
        compiler_params=_params(("arbitrary",)),
        name="expert_ffn",
    )(block_expert, block_valid, xs_rows, w["w1"], w["w3"], w["w2"], anchor)


def _moe_out_kernel(x_ref, g_ref, p_ref, yg_ref, ws13_ref, ws2_ref, wpg_ref, wple_ref, lng_ref, lnb_ref, o_ref):
    x = x_ref[...]
    xb = x.astype(BF16)
    tm = x.shape[0]
    g = g_ref[...]
    parts = None
    for k in range(TOP_K):
        chunks = [g[:, k:k + 1] * c for c in _load_packed_chunks(yg_ref, tm, lead=k)]
        parts = chunks if parts is None else [a + c for a, c in zip(parts, chunks)]
    routed = jnp.concatenate(parts, axis=1)
    h13 = jnp.dot(xb, ws13_ref[...], preferred_element_type=F32)
    h = jax.nn.silu(h13[:, :EXPERT_FF]) * h13[:, EXPERT_FF:]
    shared = jnp.dot(h.astype(BF16), ws2_ref[...], preferred_element_type=F32)
    ple = (jax.nn.sigmoid(jnp.dot(xb, wpg_ref[...], preferred_element_type=F32))
           * jnp.dot(p_ref[...].astype(BF16), wple_ref[...], preferred_element_type=F32))
    o_ref[...] = _layer_norm(DN_ALPHA * x + routed + shared + ple, lng_ref[...], lnb_ref[...])


def _moe_out(x1, top_gates, p, yg_rows, w, tm):
    n = x1.shape[0]
    row = lambda width: pl.BlockSpec((tm, width), lambda i: (i, 0))
    return pl.pallas_call(
        _moe_out_kernel,
        grid=(n // tm,),
        in_specs=[row(D_MODEL), row(TOP_K), row(PLE_DIM),
                  pl.BlockSpec((TOP_K, tm * ROW_TILE, LANES), lambda i: (0, i, 0)),
                  _full((D_MODEL, 2 * EXPERT_FF)), _full((EXPERT_FF, D_MODEL)),
                  _full((D_MODEL, D_MODEL)), _full((PLE_DIM, D_MODEL)),
                  _full((1, D_MODEL)), _full((1, D_MODEL))],
        out_specs=row(D_MODEL),
        out_shape=jax.ShapeDtypeStruct((n, D_MODEL), F32),
        compiler_params=_params(("parallel",)),
        name="moe_out",
    )(x1, top_gates, p, yg_rows, w["ws13"], w["ws2"], w["w_ple_gate"], w["w_ple"], w["ln2_g"], w["ln2_b"])


def _moe_sorted(x1, x1_tiles, top_idx, top_gates, counts, p, w, anchor):
    n = x1.shape[0]
    n_blocks = n * TOP_K // MOE_BLOCK + N_EXPERTS
    n_rows = n_blocks * MOE_BLOCK
    cnt = counts.reshape(N_EXPERTS).astype(jnp.int32)
    padded = (cnt + MOE_BLOCK - 1) // MOE_BLOCK * MOE_BLOCK
    pend = jnp.cumsum(padded)
    pstart = (pend - padded).astype(F32).reshape(N_EXPERTS, 1)
    block_start = jnp.arange(n_blocks, dtype=jnp.int32) * MOE_BLOCK
    block_expert = jnp.minimum(jnp.sum((pend[None, :] <= block_start[:, None]).astype(jnp.int32), axis=1),
                               N_EXPERTS - 1)
    real_end = pend - padded + cnt
    own = block_expert[:, None] == jnp.arange(N_EXPERTS, dtype=jnp.int32)[None, :]
    block_valid = jnp.clip(jnp.sum(jnp.where(own, real_end[None, :], 0), axis=1) - block_start, 0, MOE_BLOCK)
    slots = _route(top_idx, pstart, 512)
    slots = jnp.transpose(slots.reshape(TOP_K, n // SC_WINDOW, SC_WINDOW), (1, 0, 2))
    xs = _sc_dispatch(x1_tiles.reshape(n, ROW_TILE, LANES), slots, n_rows)
    ys = _expert_ffn(xs.reshape(n_rows * ROW_TILE, LANES), block_expert, block_valid, w, anchor)
    yg = _sc_combine(ys.reshape(n_rows, ROW_TILE, LANES), slots, n)
    return _moe_out(x1, top_gates.T, p, yg.reshape(TOP_K, n * ROW_TILE, LANES), w, 512)


def _kv_rows(k, v, batch, seq, keep, g):
    cols = slice(g * GROUP_WIDTH, (g + 1) * GROUP_WIDTH)
    shape = (batch, keep, HEADS_PER_GROUP, HEAD_DIM)
    k_g = k.reshape(batch, seq, ATTN_WIDTH)[:, seq - keep:, cols].reshape(shape)
    v_g = v.reshape(batch, seq, ATTN_WIDTH)[:, seq - keep:, cols].reshape(shape)
    return jnp.stack([k_g, v_g], axis=2)


def _layer_prompt(x, p, w, ssm, anchor):
    batch, seq, _ = x.shape
    n = batch * seq
    x2 = x.reshape(n, D_MODEL)
    tabs = _rope_tables(jnp.arange(seq, dtype=jnp.int32))
    q, k, v, u_lo, u_hi = _in_proj(x2, w["w_in"], tabs, seq, 512)
    attn_o, attn_lse = _attn_prompt(q, k, v, batch, seq)
    zeros = jnp.zeros((batch, SSM_LANES), F32)
    y_lo, y_hi, h_re, h_im = _s5_scan(u_lo, u_hi, ssm, zeros, zeros, batch, 128)
    x1, x1_tiles, _, top_idx, top_gates, counts = _post_mixer(x2, attn_o, attn_lse, y_lo, y_hi, w, seq, 512)
    y = _moe_sorted(x1, x1_tiles, top_idx, top_gates, counts, p.reshape(n, PLE_DIM), w, anchor)
    kv = [_kv_rows(k, v, batch, seq, min(win, seq), g) for g, (win, _) in enumerate(DILATION_GROUPS)]
    h_last = jnp.stack([h_re, h_im], axis=-1).reshape(batch, SSM_GROUPS, SSM_STATE, 2)
    return y.reshape(batch, seq, D_MODEL), kv, h_last


def _sample_attention(x, caches, w):
    batch, seq, _ = x.shape
    assert seq == 1
    x2 = x.reshape(batch, D_MODEL)
    tabs = _rope_tables(jnp.full((batch,), PAST_LEN, dtype=jnp.int32))
    q, k, v, u_lo, u_hi = _in_proj(x2, w["w_in"], tabs, batch, batch)
    wide = _attn_sample(q, k, v, caches, (2,), 2)
    narrow = _attn_sample(q, k, v, caches, (0, 1), 2)
    return x2, k, v, (u_lo, u_hi), narrow, wide


def _layer_sample(p, state, w, ssm, x2, k, v, u, narrow, wide):
    batch = x2.shape[0]
    n_heads = ATTN_WIDTH // HEAD_DIM
    attn_o = jnp.transpose(jnp.concatenate([narrow[0], wide[0]], axis=2), (0, 2, 1)).reshape(batch, ATTN_WIDTH)
    attn_lse = jnp.broadcast_to(jnp.concatenate([narrow[1], wide[1]], axis=1),
                                (batch, n_heads, HEAD_DIM)).reshape(batch, ATTN_WIDTH)
    h0 = state.reshape(batch, SSM_LANES, 2)
    y_lo, y_hi, h_re, h_im = _s5_scan(*u, ssm, h0[..., 0], h0[..., 1], batch, 1)
    x1, _, gates, _, _, _ = _post_mixer(x2, attn_o, attn_lse, y_lo, y_hi, w, batch, batch)
    y = _moe_ffn(x1, gates.T, p.reshape(batch, PLE_DIM), w, batch)
    kv = [_kv_rows(k, v, batch, 1, 1, g) for g in range(len(DILATION_GROUPS))]
    h_last = jnp.stack([h_re, h_im], axis=-1).reshape(batch, SSM_GROUPS, SSM_STATE, 2)
    return y.reshape(batch, 1, D_MODEL), kv, h_last


def _hi_lo_rows(t):
    hi = t.astype(BF16)
    return jnp.concatenate([hi, (t - hi.astype(F32)).astype(BF16)], axis=1).T


def kernel(x_prompt, x_sample, cache_kv_w128, cache_kv_w512, cache_kv_w2048, state_ssm, p_prompt, p_sample,
           w_in, a_re, a_im, log_dt, b_re, b_im, c_re, c_im, d_skip, w_glu, b_glu, w_attn_br, w_ssm_br,
           w_gate, b_gate, w_out, ln1_g, ln1_b, w_router, router_bias, w1, w3, w2, ws1, ws3, ws2,
           w_ple_gate, w_ple, ln2_g, ln2_b):
    assert w_in.shape[0] == DEPTH == 1
    l = 0
    row = lambda t: t[l].reshape(1, -1)
    w = {
        "w_in": w_in[l].astype(BF16),
        "w_glu": w_glu[l].astype(BF16), "b_glu": row(b_glu),
        "w_gate": w_gate[l].astype(BF16), "b_gate": row(b_gate),
        "w_attn_br": w_attn_br[l].astype(BF16), "w_ssm_br": w_ssm_br[l].astype(BF16),
        "w_out": w_out[l].astype(BF16), "ln1_g": row(ln1_g), "ln1_b": row(ln1_b),
        "w_router": _hi_lo_rows(w_router[l]), "router_bias": router_bias[l].reshape(N_EXPERTS, 1),
        "w1": w1[l], "w3": w3[l], "w2": w2[l],
        "ws13": jnp.concatenate([ws1[l], ws3[l]], axis=-1).astype(BF16), "ws2": ws2[l].astype(BF16),
        "w_ple_gate": w_ple_gate[l].astype(BF16), "w_ple": w_ple[l].astype(BF16),
        "ln2_g": row(ln2_g), "ln2_b": row(ln2_b),
    }
    ssm = _s5_params(a_re[l], a_im[l], log_dt[l], b_re[l], b_im[l], c_re[l], c_im[l], d_skip[l])
    caches = (cache_kv_w128[l], cache_kv_w512[l], cache_kv_w2048[l])
    sample = _sample_attention(x_sample, caches, w)
    anchor = jnp.broadcast_to(sample[-1][1][:8, 0, :], (8, LANES))
    yp, kv_p, h_p = _layer_prompt(x_prompt, p_prompt[l], w, ssm, anchor)
    ys, kv_s, h_s = _layer_sample(p_sample[l], state_ssm[l], w, ssm, *sample)
    return (yp, ys, kv_p[0][None], kv_s[0][None], kv_p[1][None], kv_s[1][None],
            kv_p[2][None], kv_s[2][None], h_p[None], h_s[None])
```

```python
import functools
import math

import jax
import jax.numpy as jnp
from jax import lax
from jax.experimental import pallas as pl
from jax.experimental.pallas import tpu as pltpu
from jax.experimental.pallas import tpu_sc as plsc

F32 = jnp.float32
BF16 = jnp.bfloat16

D_MODEL = 1024
HEAD_DIM = 64
HEADS_PER_GROUP = 4
DILATION_GROUPS = ((128, 1), (512, 4), (2048, 16))
N_BACK = 128
GROUP_WIDTH = HEADS_PER_GROUP * HEAD_DIM
ATTN_WIDTH = 3 * GROUP_WIDTH
ROPE_THETA = 10000.0
SSM_WIDTH = 256
SSM_GROUP = 16
SSM_GROUPS = 16
SSM_STATE = 64
SSM_LANES = SSM_GROUPS * SSM_STATE
IN_WIDTH = 3 * ATTN_WIDTH + SSM_WIDTH
N_EXPERTS = 64
TOP_K = 8
EXPERT_FF = 256
ROUTED_SCALE = 2.5
PLE_DIM = 256
DEPTH = 1
PAST_LEN = 8192
DN_ALPHA = (2.0 * DEPTH) ** 0.25
LN_EPS = 1e-5

LANES = 128
ROW_TILE = D_MODEL // LANES // 2
SC_CORES = 2
SC_SUBCORES = 16
SC_WINDOW = 64
MOE_BLOCK = 1024
POST_MIXER_SUBTILES = 2
EXPERT_FFN_SUBTILES = 4
ATTN_CHUNK = 2048
ATTN_UNROLL = 8
VMEM_LIMIT = 56 * 1024 * 1024


def _params(semantics):
    return pltpu.CompilerParams(dimension_semantics=semantics, vmem_limit_bytes=VMEM_LIMIT)


def _full(shape):
    return pl.BlockSpec(shape, lambda *_: (0,) * len(shape))


def _interleaved_rows(seq, rows, n_seq, row0=0):
    start = row0 * n_seq + seq
    return pl.ds(start, rows) if n_seq == 1 else pl.ds(start, rows, stride=n_seq)


def _in_proj_kernel(x_ref, w_ref, cos_ref, sina_ref, sinb_ref, q_ref, k_ref, v_ref, ulo_ref, uhi_ref, *, n_seq):
    xb = x_ref[...].astype(BF16)
    cos = cos_ref[...]
    sin_a = sina_ref[...]
    sin_b = sinb_ref[...]

    def rope_store(col0, out_ref, scale):
        t = jnp.dot(xb, w_ref[:, col0:col0 + ATTN_WIDTH], preferred_element_type=F32)
        for c in range(ATTN_WIDTH // LANES):
            xc = t[:, c * LANES:(c + 1) * LANES]
            r = xc * cos + pltpu.roll(xc, LANES - 32, 1) * sin_a + pltpu.roll(xc, 32, 1) * sin_b
            out_ref[:, c * LANES:(c + 1) * LANES] = r * scale if scale != 1.0 else r

    rope_store(0, q_ref, HEAD_DIM ** -0.5)
    rope_store(ATTN_WIDTH, k_ref, 1.0)
    v_ref[...] = jnp.dot(xb, w_ref[:, 2 * ATTN_WIDTH:3 * ATTN_WIDTH], preferred_element_type=F32)
    u = jnp.dot(xb, w_ref[:, 3 * ATTN_WIDTH:], preferred_element_type=F32)
    rows = _interleaved_rows(pl.program_id(1), u.shape[0], n_seq)
    ulo_ref[rows, :] = u[:, :LANES]
    uhi_ref[rows, :] = u[:, LANES:]


def _in_proj(x, w_in_bf, rope_tabs, rows_per_seq, tm):
    n = x.shape[0]
    tiles_per_seq = rows_per_seq // tm
    n_seq = n // rows_per_seq
    tab_tiles = rope_tabs[0].shape[0] // tm
    tab_spec = pl.BlockSpec((tm, LANES), lambda t, s: (t % tab_tiles, 0))
    row_spec = pl.BlockSpec((tm, ATTN_WIDTH), lambda t, s: (s * tiles_per_seq + t, 0))
    u_spec = pl.BlockSpec((tm * n_seq, LANES), lambda t, s: (t, 0))
    return pl.pallas_call(
        functools.partial(_in_proj_kernel, n_seq=n_seq),
        grid=(tiles_per_seq, n_seq),
        in_specs=[pl.BlockSpec((tm, D_MODEL), lambda t, s: (s * tiles_per_seq + t, 0)),
                  _full((D_MODEL, IN_WIDTH)), tab_spec, tab_spec, tab_spec],
        out_specs=[row_spec, row_spec, row_spec, u_spec, u_spec],
        out_shape=[jax.ShapeDtypeStruct((n, ATTN_WIDTH), F32)] * 3 + [jax.ShapeDtypeStruct((n, LANES), F32)] * 2,
        compiler_params=_params(("parallel", "arbitrary")),
        name="in_proj",
    )(x, w_in_bf, *rope_tabs)


def _rope_tables(pos):
    half = HEAD_DIM // 2
    inv = ROPE_THETA ** (-jnp.arange(half, dtype=F32) / half)
    ang = pos.astype(F32)[:, None] * inv[None, :]
    cos = jnp.tile(jnp.cos(ang), (1, LANES // half))
    sin = jnp.tile(jnp.sin(ang), (1, LANES // half))
    first_half = (jnp.arange(LANES) % HEAD_DIM) < half
    sin_a = jnp.where(first_half[None, :], -sin, 0.0)
    sin_b = jnp.where(first_half[None, :], 0.0, sin)
    return cos, sin_a, sin_b


def _band_attention(q, k, v, mask):
    head_of_lane = lax.broadcasted_iota(jnp.int32, (N_BACK, LANES), 1) // HEAD_DIM
    kb = k.astype(BF16)
    vb = v.astype(BF16)
    o = lse = None
    for h in range(LANES // HEAD_DIM):
        qh = jnp.where(head_of_lane == h, q, 0.0).astype(BF16)
        logits = lax.dot_general(qh, kb, (((1,), (1,)), ((), ())), preferred_element_type=F32) + mask
        m = jnp.max(logits, axis=1, keepdims=True)
        p = jnp.exp(logits - m)
        l = jnp.sum(p, axis=1, keepdims=True)
        o_h = jnp.dot(p.astype(BF16), vb, preferred_element_type=F32) * (1.0 / l)
        lse_h = jnp.broadcast_to(m + jnp.log(l), (N_BACK, LANES))
        o = o_h if o is None else jnp.where(head_of_lane == h, o_h, o)
        lse = lse_h if lse is None else jnp.where(head_of_lane == h, lse_h, lse)
    return o, lse


def _attn_prompt_kernel(q_ref, kp_ref, kc_ref, vp_ref, vc_ref, o_ref, lse_ref):
    c = pl.program_id(1)
    g = pl.program_id(3)
    ch = ATTN_CHUNK
    qi = lax.broadcasted_iota(jnp.int32, (N_BACK, 2 * N_BACK), 0)
    kj = lax.broadcasted_iota(jnp.int32, (N_BACK, 2 * N_BACK), 1)
    dist = qi + N_BACK - kj
    band = jnp.where(dist >= 0, jnp.where(dist <= N_BACK, 0.0, -jnp.inf), -jnp.inf)
    band_first = jnp.where(kj >= N_BACK, band, -jnp.inf)

    def group_body(d):
        span = N_BACK * d
        n_sub = ch // N_BACK

        def rows(start, size):
            return pl.ds(start, size) if d == 1 else pl.ds(start, size, stride=d)

        def store(q0, o, lse):
            o_ref[rows(q0, N_BACK), :] = o
            lse_ref[rows(q0, N_BACK), :] = lse

        def head_block(r, carry):
            k = jnp.concatenate([kp_ref[rows(ch - span + r, N_BACK), :], kc_ref[rows(r, N_BACK), :]], axis=0)
            v = jnp.concatenate([vp_ref[rows(ch - span + r, N_BACK), :], vc_ref[rows(r, N_BACK), :]], axis=0)
            mask = jnp.where(c == 0, band_first, band)
            store(r, *_band_attention(q_ref[rows(r, N_BACK), :], k, v, mask))
            return carry

        def inner_block(idx, carry):
            s = idx // d
            r = idx % d
            k0 = (s - 1) * span + r
            store(s * span + r, *_band_attention(q_ref[rows(s * span + r, N_BACK), :],
                                                 kc_ref[rows(k0, 2 * N_BACK), :],
                                                 vc_ref[rows(k0, 2 * N_BACK), :], band))
            return carry

        lax.fori_loop(0, d, head_block, 0, unroll=min(d, ATTN_UNROLL))
        if n_sub > d:
            trips = n_sub - d
            lax.fori_loop(d, n_sub, inner_block, 0,
                          unroll=max(u for u in range(1, ATTN_UNROLL + 1) if trips % u == 0))

    for gi, (_, d) in enumerate(DILATION_GROUPS):
        pl.when(g == gi)(functools.partial(group_body, d))


def _attn_prompt(q, k, v, batch, seq):
    ch = ATTN_CHUNK
    cps = seq // ch
    n = batch * seq
    pairs = GROUP_WIDTH // LANES
    cur = lambda b, c, hp, g: (b * cps + c, g * pairs + hp)
    prev = lambda b, c, hp, g: (b * cps + jnp.maximum(c - 1, 0), g * pairs + hp)
    blk = (ch, LANES)
    return pl.pallas_call(
        _attn_prompt_kernel,
        grid=(batch, cps, pairs, len(DILATION_GROUPS)),
        in_specs=[pl.BlockSpec(blk, cur), pl.BlockSpec(blk, prev), pl.BlockSpec(blk, cur),
                  pl.BlockSpec(blk, prev), pl.BlockSpec(blk, cur)],
        out_specs=[pl.BlockSpec(blk, cur), pl.BlockSpec(blk, cur)],
        out_shape=[jax.ShapeDtypeStruct((n, ATTN_WIDTH), F32)] * 2,
        compiler_params=_params(("parallel", "parallel", "parallel", "parallel")),
        name="attn_prompt",
    )(q, k, k, v, v)


def _attn_sample_kernel(q_ref, k_ref, v_ref, *refs, groups):
    *cache_refs, o_ref, lse_ref = refs
    bt = q_ref.shape[0]
    for b in range(bt):
        for slot, (g, c_ref) in enumerate(zip(groups, cache_refs)):
            win, d = DILATION_GROUPS[g]
            pos = lax.broadcasted_iota(jnp.int32, (1, win), 1)
            off_stride = (pos % d) != 0
            j0 = g * HEADS_PER_GROUP
            out0 = slot * HEADS_PER_GROUP
            heads = range(HEADS_PER_GROUP)
            qs = [q_ref[b, :, j0 + h:j0 + h + 1] for h in heads]
            s_c = jnp.concatenate([jnp.sum(c_ref[b, 0, h] * qs[h], axis=0, keepdims=True) for h in heads],
                                  axis=0)
            s_c = jnp.where(off_stride, -jnp.inf, s_c)
            s_new = jnp.concatenate([jnp.sum(k_ref[b, :, j0 + h:j0 + h + 1] * qs[h], axis=0, keepdims=True)
                                     for h in heads], axis=0)
            m = jnp.maximum(jnp.max(s_c, axis=1, keepdims=True), s_new)
            p_c = jnp.exp(s_c - m)
            p_new = jnp.exp(s_new - m)
            l = jnp.sum(p_c, axis=1, keepdims=True) + p_new
            inv_l = 1.0 / l
            lse_ref[b, out0:out0 + HEADS_PER_GROUP, :] = m + jnp.log(l)
            for h in heads:
                num = (jnp.sum(c_ref[b, 1, h] * p_c[h:h + 1, :], axis=1, keepdims=True)
                       + p_new[h:h + 1, :] * v_ref[b, :, j0 + h:j0 + h + 1])
                o_ref[b, :, out0 + h:out0 + h + 1] = num * inv_l[h:h + 1, :]


def _attn_sample(q, k, v, caches, groups, bt):
    b = q.shape[0]
    n_heads = ATTN_WIDTH // HEAD_DIM
    out_heads = HEADS_PER_GROUP * len(groups)
    views, specs = [], []
    for g in groups:
        win, d = DILATION_GROUPS[g]
        assert caches[g].shape[1] == win == N_BACK * d
        views.append(jnp.transpose(caches[g], (0, 2, 3, 4, 1)))
        specs.append(pl.BlockSpec((bt, 2, HEADS_PER_GROUP, HEAD_DIM, win), lambda i: (i, 0, 0, 0, 0)))
    col_spec = pl.BlockSpec((bt, HEAD_DIM, n_heads), lambda i: (i, 0, 0))
    cols = lambda t: jnp.transpose(t.reshape(b, n_heads, HEAD_DIM), (0, 2, 1))
    return pl.pallas_call(
        functools.partial(_attn_sample_kernel, groups=groups),
        grid=(b // bt,),
        in_specs=[col_spec, col_spec, col_spec] + specs,
        out_specs=[pl.BlockSpec((bt, HEAD_DIM, out_heads), lambda i: (i, 0, 0)),
                   pl.BlockSpec((bt, out_heads, 1), lambda i: (i, 0, 0))],
        out_shape=[jax.ShapeDtypeStruct((b, HEAD_DIM, out_heads), F32),
                   jax.ShapeDtypeStruct((b, out_heads, 1), F32)],
        compiler_params=_params(("parallel",)),
        name="attn_sample",
    )(cols(q), cols(k), cols(v), *views)


def _s5_scan_kernel(ulo_ref, uhi_ref, bmat_ref, cmat_ref, are_ref, aim_ref, d_ref, h0re_ref, h0im_ref,
                    ylo_ref, yhi_ref, hre_ref, him_ref, hist_sc, *, bg, steps):
    t_chunk = pl.program_id(0)

    @pl.when(t_chunk == 0)
    def _():
        hre_ref[...] = h0re_ref[...]
        him_ref[...] = h0im_ref[...]

    u = jnp.concatenate([ulo_ref[...], uhi_ref[...]], axis=1)
    hist_sc[...] = jnp.dot(u.astype(BF16), bmat_ref[...], preferred_element_type=F32)
    a_re = jnp.broadcast_to(are_ref[...], (bg, SSM_LANES))
    a_im = jnp.broadcast_to(aim_ref[...], (bg, SSM_LANES))

    def step(t, carry):
        h_re, h_im = carry
        rows = pl.ds(pl.multiple_of(t * bg, bg), bg)
        n_re = a_re * h_re - a_im * h_im + hist_sc[rows, 0:SSM_LANES]
        n_im = a_re * h_im + a_im * h_re + hist_sc[rows, SSM_LANES:2 * SSM_LANES]
        hist_sc[rows, 0:SSM_LANES] = n_re
        hist_sc[rows, SSM_LANES:2 * SSM_LANES] = n_im
        return n_re, n_im

    h_re, h_im = lax.fori_loop(0, steps, step, (hre_ref[...], him_ref[...]))
    hre_ref[...] = h_re
    him_ref[...] = h_im
    y = jnp.dot(hist_sc[...].astype(BF16), cmat_ref[...], preferred_element_type=F32) + d_ref[...] * u
    ylo_ref[...] = y[:, :LANES]
    yhi_ref[...] = y[:, LANES:]


def _s5_scan(u_lo, u_hi, ssm, h0_re, h0_im, bg, steps):
    rows = u_lo.shape[0]
    blk = steps * bg
    kern = functools.partial(_s5_scan_kernel, bg=bg, steps=steps)
    state_spec = _full((bg, SSM_LANES))
    half_spec = pl.BlockSpec((blk, LANES), lambda i: (i, 0))
    return pl.pallas_call(
        kern,
        grid=(rows // blk,),
        in_specs=[half_spec, half_spec,
                  _full((SSM_WIDTH, 2 * SSM_LANES)), _full((2 * SSM_LANES, SSM_WIDTH)),
                  _full((1, SSM_LANES)), _full((1, SSM_LANES)), _full((1, SSM_WIDTH)),
                  state_spec, state_spec],
        out_specs=[half_spec, half_spec, state_spec, state_spec],
        out_shape=[jax.ShapeDtypeStruct((rows, LANES), F32), jax.ShapeDtypeStruct((rows, LANES), F32),
                   jax.ShapeDtypeStruct((bg, SSM_LANES), F32), jax.ShapeDtypeStruct((bg, SSM_LANES), F32)],
        scratch_shapes=[pltpu.VMEM((blk, 2 * SSM_LANES), F32)],
        compiler_params=_params(("arbitrary",)),
        name="s5_scan",
    )(u_lo, u_hi, ssm["bmat"], ssm["cmat"], ssm["a_re"], ssm["a_im"], ssm["d_skip"], h0_re, h0_im)


def _s5_params(a_re, a_im, log_dt, b_re, b_im, c_re, c_im, d_skip):
    dt = jnp.exp(log_dt)[:, None]
    mag = jnp.exp(a_re * dt)
    abar_re = mag * jnp.cos(a_im * dt)
    abar_im = mag * jnp.sin(a_im * dt)
    a2 = a_re * a_re + a_im * a_im
    nr = abar_re - 1.0
    coef_re = (nr * a_re + abar_im * a_im) / a2
    coef_im = (abar_im * a_re - nr * a_im) / a2
    bb_re = coef_re[..., None] * b_re - coef_im[..., None] * b_im
    bb_im = coef_re[..., None] * b_im + coef_im[..., None] * b_re
    eye = jnp.eye(SSM_GROUPS, dtype=F32)
    to_b = lambda t: jnp.einsum("gpc,gh->gchp", t, eye).reshape(SSM_WIDTH, SSM_LANES)
    to_c = lambda t: jnp.einsum("gcp,gh->gphc", t, eye).reshape(SSM_LANES, SSM_WIDTH)
    return {
        "bmat": jnp.concatenate([to_b(bb_re), to_b(bb_im)], axis=1).astype(BF16),
        "cmat": jnp.concatenate([to_c(c_re), -to_c(c_im)], axis=0).astype(BF16),
        "a_re": abar_re.reshape(1, SSM_LANES), "a_im": abar_im.reshape(1, SSM_LANES),
        "d_skip": d_skip.reshape(1, SSM_WIDTH),
    }


def _layer_norm(z, g, b):
    mu = jnp.mean(z, axis=-1, keepdims=True)
    zc = z - mu
    var = jnp.mean(zc * zc, axis=-1, keepdims=True)
    return zc * lax.rsqrt(var + LN_EPS) * g + b


def _merge_groups(o, lse):
    parts = [slice(g * GROUP_WIDTH, (g + 1) * GROUP_WIDTH) for g in range(len(DILATION_GROUPS))]
    top = lse[:, parts[0]]
    for cols in parts[1:]:
        top = jnp.maximum(top, lse[:, cols])
    num = den = None
    for cols in parts:
        w = jnp.exp(lse[:, cols] - top)
        num = w * o[:, cols] if num is None else num + w * o[:, cols]
        den = w if den is None else den + w
    return num / den


def _store_packed_rows(ref, x, row0=0):
    rows = x.shape[0]
    for j in range(ROW_TILE):
        lo = x[:, j * LANES:(j + 1) * LANES].astype(BF16).astype(F32)
        hi = x[:, (j + ROW_TILE) * LANES:(j + ROW_TILE + 1) * LANES].astype(BF16).astype(F32)
        word = (lax.bitcast_convert_type(lo, jnp.uint32) >> 16) | lax.bitcast_convert_type(hi, jnp.uint32)
        ref[pl.ds(row0 * ROW_TILE + j, rows, stride=ROW_TILE), :] = lax.bitcast_convert_type(word, jnp.int32)


def _load_packed_chunks(ref, rows, lead=None, row0=0):
    lows, highs = [], []
    for j in range(ROW_TILE):
        idx = (pl.ds(row0 * ROW_TILE + j, rows, stride=ROW_TILE), slice(None))
        word = lax.bitcast_convert_type(ref[idx] if lead is None else ref[(lead,) + idx], jnp.uint32)
        lows.append(lax.bitcast_convert_type(word << 16, F32))
        highs.append(lax.bitcast_convert_type(word & jnp.uint32(0xFFFF0000), F32))
    return lows + highs


def _post_mixer_kernel(x_ref, ao_ref, lse_ref, ylo_ref, yhi_ref, wglu_ref, bglu_ref, wgate_ref, bgate_ref,
                       wab_ref, wsb_ref, wout_ref, lng_ref, lnb_ref, wr_ref, rb_ref,
                       x1_ref, x1t_ref, gate_ref, idx_ref, topg_ref, cnt_ref, *, n_seq):
    seq = pl.program_id(1)

    @pl.when(jnp.logical_and(pl.program_id(0) == 0, seq == 0))
    def _():
        cnt_ref[...] = jnp.zeros(cnt_ref.shape, F32)

    tm = x_ref.shape[0]
    sub = tm // POST_MIXER_SUBTILES if tm % (8 * POST_MIXER_SUBTILES) == 0 else tm
    for r0 in range(0, tm, sub):
        rows = slice(r0, r0 + sub)
        x = x_ref[rows, :]
        xb = x.astype(BF16)
        y_rows = _interleaved_rows(seq, sub, n_seq, r0)
        s = jax.nn.gelu(jnp.concatenate([ylo_ref[y_rows, :], yhi_ref[y_rows, :]], axis=1))
        s = s * jax.nn.sigmoid(jnp.dot(s.astype(BF16), wglu_ref[...], preferred_element_type=F32) + bglu_ref[...])
        gates = jax.nn.sigmoid(jnp.dot(xb, wgate_ref[...], preferred_element_type=F32) + bgate_ref[...])
        attn_o = _merge_groups(ao_ref[rows, :], lse_ref[rows, :])
        attn_br = jnp.dot(attn_o.astype(BF16), wab_ref[...], preferred_element_type=F32)
        ssm_br = jnp.dot(s.astype(BF16), wsb_ref[...], preferred_element_type=F32)
        merged = gates[:, :D_MODEL] * attn_br + gates[:, D_MODEL:] * ssm_br
        mix = jnp.dot(merged.astype(BF16), wout_ref[...], preferred_element_type=F32)
        x1 = _layer_norm(DN_ALPHA * x + mix, lng_ref[...], lnb_ref[...])
        x1_ref[rows, :] = x1
        _store_packed_rows(x1t_ref, x1, r0)

        x1_hi = x1.astype(BF16)
        x1_lo = (x1 - x1_hi.astype(F32)).astype(BF16)
        prod = lax.dot_general(wr_ref[...], jnp.concatenate([x1_hi, x1_lo], axis=0),
                               (((1,), (1,)), ((), ())), preferred_element_type=F32)
        logits = ((prod[:N_EXPERTS, :sub] + prod[N_EXPERTS:, :sub] + prod[:N_EXPERTS, sub:])
                  + prod[N_EXPERTS:, sub:])
        scores = jax.nn.sigmoid(logits)
        sel = scores + rb_ref[...]
        expert = lax.broadcasted_iota(jnp.int32, sel.shape, 0).astype(F32)
        chosen = jnp.zeros(sel.shape, F32)
        idx_rows, score_rows = [], []
        for _ in range(TOP_K):
            top = jnp.max(sel, axis=0, keepdims=True)
            first = jnp.min(jnp.where(sel == top, expert, float(N_EXPERTS)), axis=0, keepdims=True)
            hit = expert == first
            chosen = jnp.where(hit, 1.0, chosen)
            sel = jnp.where(hit, -jnp.inf, sel)
            idx_rows.append(first)
            score_rows.append(jnp.sum(jnp.where(hit, scores, 0.0), axis=0, keepdims=True))
        norm = ROUTED_SCALE / jnp.sum(scores * chosen, axis=0, keepdims=True)
        gate_ref[:, rows] = scores * chosen * norm
        idx_ref[:, rows] = jnp.concatenate(idx_rows, axis=0)
        topg_ref[:, rows] = jnp.concatenate(score_rows, axis=0) * norm
        cnt_ref[...] += jnp.sum(chosen, axis=1, keepdims=True)


def _post_mixer(x, attn_o, attn_lse, y_lo, y_hi, w, rows_per_seq, tm):
    n = x.shape[0]
    tiles_per_seq = rows_per_seq // tm
    n_seq = n // rows_per_seq
    row = lambda width: pl.BlockSpec((tm, width), lambda t, s: (s * tiles_per_seq + t, 0))
    col = lambda height: pl.BlockSpec((height, tm), lambda t, s: (0, s * tiles_per_seq + t))
    y_spec = pl.BlockSpec((tm * n_seq, LANES), lambda t, s: (t, 0))
    return pl.pallas_call(
        functools.partial(_post_mixer_kernel, n_seq=n_seq),
        grid=(tiles_per_seq, n_seq),
        in_specs=[row(D_MODEL), row(ATTN_WIDTH), row(ATTN_WIDTH), y_spec, y_spec,
                  _full((SSM_WIDTH, SSM_WIDTH)), _full((1, SSM_WIDTH)),
                  _full((D_MODEL, 2 * D_MODEL)), _full((1, 2 * D_MODEL)),
                  _full((GROUP_WIDTH, D_MODEL)), _full((SSM_WIDTH, D_MODEL)), _full((D_MODEL, D_MODEL)),
                  _full((1, D_MODEL)), _full((1, D_MODEL)),
                  _full((2 * N_EXPERTS, D_MODEL)), _full((N_EXPERTS, 1))],
        out_specs=[row(D_MODEL),
                   pl.BlockSpec((tm * ROW_TILE, LANES), lambda t, s: (s * tiles_per_seq + t, 0)),
                   col(N_EXPERTS), col(TOP_K), col(TOP_K), _full((N_EXPERTS, 1))],
        out_shape=[jax.ShapeDtypeStruct((n, D_MODEL), F32), jax.ShapeDtypeStruct((n * ROW_TILE, LANES), jnp.int32),
                   jax.ShapeDtypeStruct((N_EXPERTS, n), F32), jax.ShapeDtypeStruct((TOP_K, n), F32),
                   jax.ShapeDtypeStruct((TOP_K, n), F32), jax.ShapeDtypeStruct((N_EXPERTS, 1), F32)],
        compiler_params=_params(("arbitrary", "arbitrary")),
        name="post_mixer",
    )(x, attn_o, attn_lse, y_lo, y_hi, w["w_glu"], w["b_glu"], w["w_gate"], w["b_gate"], w["w_attn_br"], w["w_ssm_br"],
      w["w_out"], w["ln1_g"], w["ln1_b"], w["w_router"], w["router_bias"])


def _moe_ffn_kernel(x_ref, gate_ref, p_ref, w1_ref, w3_ref, w2_ref, ws13_ref, ws2_ref, wpg_ref, wple_ref,
                    lng_ref, lnb_ref, o_ref, acc_sc, xb_sc):
    e = pl.program_id(1)

    def glu_ffn(xb, w13, w2, row_scale):
        if isinstance(w13, tuple):
            h1 = jnp.dot(xb, w13[0].astype(BF16), preferred_element_type=F32)
            h3 = jnp.dot(xb, w13[1].astype(BF16), preferred_element_type=F32)
        else:
            h13 = jnp.dot(xb, w13, preferred_element_type=F32)
            h1, h3 = h13[:, :EXPERT_FF], h13[:, EXPERT_FF:]
        h = jax.nn.silu(h1) * h3
        if row_scale is not None:
            h = h * row_scale
        return jnp.dot(h.astype(BF16), w2.astype(BF16), preferred_element_type=F32)

    @pl.when(e == 0)
    def _():
        xb = x_ref[...].astype(BF16)
        xb_sc[...] = xb
        ple = (jax.nn.sigmoid(jnp.dot(xb, wpg_ref[...], preferred_element_type=F32))
               * jnp.dot(p_ref[...].astype(BF16), wple_ref[...], preferred_element_type=F32))
        acc_sc[...] = glu_ffn(xb, ws13_ref[...], ws2_ref[...], None) + ple

    gates = gate_ref[...]
    lane = lax.broadcasted_iota(jnp.int32, gates.shape, 1)
    g_col = jnp.sum(jnp.where(lane == e, gates, 0.0), axis=-1, keepdims=True)
    acc_sc[...] += glu_ffn(xb_sc[...], (w1_ref[0], w3_ref[0]), w2_ref[0], g_col)

    @pl.when(e == N_EXPERTS - 1)
    def _():
        o_ref[...] = _layer_norm(DN_ALPHA * x_ref[...] + acc_sc[...], lng_ref[...], lnb_ref[...])


def _moe_ffn(x1, gates, p, w, tm):
    n = x1.shape[0]
    row = lambda width: pl.BlockSpec((tm, width), lambda i, e: (i, 0))
    return pl.pallas_call(
        _moe_ffn_kernel,
        grid=(n // tm, N_EXPERTS),
        in_specs=[row(D_MODEL), row(N_EXPERTS), row(PLE_DIM),
                  pl.BlockSpec((1, D_MODEL, EXPERT_FF), lambda i, e: (e, 0, 0)),
                  pl.BlockSpec((1, D_MODEL, EXPERT_FF), lambda i, e: (e, 0, 0)),
                  pl.BlockSpec((1, EXPERT_FF, D_MODEL), lambda i, e: (e, 0, 0)),
                  _full((D_MODEL, 2 * EXPERT_FF)), _full((EXPERT_FF, D_MODEL)),
                  _full((D_MODEL, D_MODEL)), _full((PLE_DIM, D_MODEL)),
                  _full((1, D_MODEL)), _full((1, D_MODEL))],
        out_specs=row(D_MODEL),
        out_shape=jax.ShapeDtypeStruct((n, D_MODEL), F32),
        scratch_shapes=[pltpu.VMEM((tm, D_MODEL), F32), pltpu.VMEM((tm, D_MODEL), BF16)],
        compiler_params=_params(("parallel", "arbitrary")),
        name="moe_ffn",
    )(x1, gates, p, w["w1"], w["w3"], w["w2"], w["ws13"], w["ws2"], w["w_ple_gate"], w["w_ple"],
      w["ln2_g"], w["ln2_b"])


def _route_kernel(idx_ref, pstart_ref, earlier_ref, slot_ref, base_sc):
    @pl.when(pl.program_id(0) == 0)
    def _():
        base_sc[...] = jnp.zeros(base_sc.shape, F32)

    idx = idx_ref[...]
    tm = idx.shape[1]
    expert = lax.broadcasted_iota(jnp.int32, (N_EXPERTS, tm), 0).astype(F32)
    hits = [expert == idx[k:k + 1, :] for k in range(TOP_K)]
    member = jnp.zeros((N_EXPERTS, tm), F32)
    for hit in hits:
        member = member + jnp.where(hit, 1.0, 0.0)
    row = (jnp.dot(member.astype(BF16), earlier_ref[...], preferred_element_type=F32)
           + base_sc[...] + pstart_ref[...])
    slots = [jnp.sum(jnp.where(hit, row, 0.0), axis=0, keepdims=True) for hit in hits]
    slot_ref[...] = jnp.concatenate(slots, axis=0).astype(jnp.int32)
    base_sc[...] += jnp.sum(member, axis=1, keepdims=True)


def _route(top_idx, pstart, tm):
    n = top_idx.shape[1]
    earlier = jnp.triu(jnp.ones((tm, tm), F32), k=1).astype(BF16)
    return pl.pallas_call(
        _route_kernel,
        grid=(n // tm,),
        in_specs=[pl.BlockSpec((TOP_K, tm), lambda i: (0, i)), _full((N_EXPERTS, 1)), _full((tm, tm))],
        out_specs=pl.BlockSpec((TOP_K, tm), lambda i: (0, i)),
        out_shape=jax.ShapeDtypeStruct((TOP_K, n), jnp.int32),
        scratch_shapes=[pltpu.VMEM((N_EXPERTS, 1), F32)],
        compiler_params=_params(("arbitrary",)),
        name="route",
    )(top_idx, pstart, earlier)


def _sc_mesh():
    return plsc.VectorSubcoreMesh(core_axis_name="c", subcore_axis_name="s",
                                  num_cores=SC_CORES, num_subcores=SC_SUBCORES)


def _sc_dispatch(x_tiles, slots, n_rows):
    n = x_tiles.shape[0]
    wins_per_worker = n // SC_WINDOW // (SC_CORES * SC_SUBCORES)

    assert wins_per_worker % 2 == 0

    def body(x_hbm, slot_hbm, xs_hbm, idx_a, idx_b, rows_a, rows_b, load_sems, scatter_sem):
        wid = lax.axis_index("s") * SC_CORES + lax.axis_index("c")
        first = wid * wins_per_worker
        idx_bufs = (idx_a, idx_b)
        row_bufs = (rows_a, rows_b)

        def loads(win, b):
            return (pltpu.make_async_copy(slot_hbm.at[win], idx_bufs[b], load_sems.at[b]),
                    pltpu.make_async_copy(x_hbm.at[pl.ds(win * SC_WINDOW, SC_WINDOW)], row_bufs[b],
                                          load_sems.at[b]))

        for copy in loads(first, 0):
            copy.start()

        @pl.loop(0, wins_per_worker, step=2)
        def _(i):
            for b in range(2):
                win = first + i + b
                for copy in loads(win, b):
                    copy.wait()

                @pl.when(i + b + 1 < wins_per_worker)
                def _():
                    for copy in loads(win + 1, 1 - b):
                        copy.start()

                copies = [pltpu.async_copy(row_bufs[b], xs_hbm.at[idx_bufs[b].at[k]], scatter_sem)
                          for k in range(TOP_K)]
                for copy in copies:
                    copy.wait()

    return pl.kernel(
        body, out_type=jax.ShapeDtypeStruct((n_rows, ROW_TILE, LANES), jnp.int32), mesh=_sc_mesh(),
        scratch_types=[pltpu.VMEM((TOP_K, SC_WINDOW), jnp.int32), pltpu.VMEM((TOP_K, SC_WINDOW), jnp.int32),
                       pltpu.VMEM((SC_WINDOW, ROW_TILE, LANES), jnp.int32),
                       pltpu.VMEM((SC_WINDOW, ROW_TILE, LANES), jnp.int32),
                       pltpu.SemaphoreType.DMA((2,)), pltpu.SemaphoreType.DMA],
        name="sc_dispatch",
    )(x_tiles, slots)


def _sc_combine(y_tiles, slots, n):
    wins_per_worker = n // SC_WINDOW // (SC_CORES * SC_SUBCORES)

    n_bufs = 3

    def body(ys_hbm, slot_hbm, yg_hbm, idx_v, rows_a, rows_b, rows_c, gather_sems, write_sems):
        wid = lax.axis_index("s") * SC_CORES + lax.axis_index("c")
        bufs = (rows_a, rows_b, rows_c)

        @pl.loop(0, wins_per_worker)
        def _(i):
            win = wid * wins_per_worker + i
            pltpu.sync_copy(slot_hbm.at[win], idx_v)

            def gather(k):
                return pltpu.async_copy(ys_hbm.at[idx_v.at[k]], bufs[k % n_bufs], gather_sems.at[k % n_bufs])

            def write(k):
                return pltpu.async_copy(bufs[k % n_bufs], yg_hbm.at[k, pl.ds(win * SC_WINDOW, SC_WINDOW)],
                                        write_sems.at[k % n_bufs])

            gathers = {k: gather(k) for k in range(n_bufs - 1)}
            writes = {}
            waited = set()
            for k in range(TOP_K):
                gathers[k].wait()
                writes[k] = write(k)
                nxt = k + n_bufs - 1
                if nxt < TOP_K:
                    if nxt - n_bufs >= 0:
                        writes[nxt - n_bufs].wait()
                        waited.add(nxt - n_bufs)
                    gathers[nxt] = gather(nxt)
            for k in range(TOP_K):
                if k not in waited:
                    writes[k].wait()

    return pl.kernel(
        body, out_type=jax.ShapeDtypeStruct((TOP_K, n, ROW_TILE, LANES), jnp.int32), mesh=_sc_mesh(),
        scratch_types=[pltpu.VMEM((TOP_K, SC_WINDOW), jnp.int32)]
        + [pltpu.VMEM((SC_WINDOW, ROW_TILE, LANES), jnp.int32)] * n_bufs
        + [pltpu.SemaphoreType.DMA((n_bufs,)), pltpu.SemaphoreType.DMA((n_bufs,))],
        name="sc_combine",
    )(y_tiles, slots)


def _expert_ffn_kernel(bexp_ref, valid_ref, xs_ref, w1_ref, w3_ref, w2_ref, anchor_ref, ys_ref, w13_sc, w2_sc):
    del anchor_ref
    i = pl.program_id(0)
    valid = valid_ref[i]

    @pl.when(jnp.logical_or(i == 0, bexp_ref[i] != bexp_ref[jnp.maximum(i - 1, 0)]))
    def _():
        w13_sc[:, :EXPERT_FF] = w1_ref[0].astype(BF16)
        w13_sc[:, EXPERT_FF:] = w3_ref[0].astype(BF16)
        w2_sc[...] = w2_ref[0].astype(BF16)

    def ffn(r0, rows):
        x = jnp.concatenate(_load_packed_chunks(xs_ref, rows, row0=r0), axis=1)
        h13 = jnp.dot(x.astype(BF16), w13_sc[...], preferred_element_type=F32)
        h = jax.nn.silu(h13[:, :EXPERT_FF]) * h13[:, EXPERT_FF:]
        _store_packed_rows(ys_ref, jnp.dot(h.astype(BF16), w2_sc[...], preferred_element_type=F32), r0)

    @pl.when(valid == MOE_BLOCK)
    def _():
        ffn(0, MOE_BLOCK)

    @pl.when(jnp.logical_and(valid > 0, valid < MOE_BLOCK))
    def _():
        sub = MOE_BLOCK // EXPERT_FFN_SUBTILES
        for r0 in range(0, MOE_BLOCK, sub):
            pl.when(valid > r0)(functools.partial(ffn, r0, sub))


def _expert_ffn(xs_rows, block_expert, block_valid, w, anchor):
    n_blocks = block_expert.shape[0]
    blk = (MOE_BLOCK * ROW_TILE, LANES)
    return pl.pallas_call(
        _expert_ffn_kernel,
        grid_spec=pltpu.PrefetchScalarGridSpec(
            num_scalar_prefetch=2, grid=(n_blocks,),
            in_specs=[pl.BlockSpec(blk, lambda i, be, nu: (i, 0)),
                      pl.BlockSpec((1, D_MODEL, EXPERT_FF), lambda i, be, nu: (be[i], 0, 0)),
                      pl.BlockSpec((1, D_MODEL, EXPERT_FF), lambda i, be, nu: (be[i], 0, 0)),
                      pl.BlockSpec((1, EXPERT_FF, D_MODEL), lambda i, be, nu: (be[i], 0, 0)),
                      pl.BlockSpec((8, LANES), lambda i, be, nu: (0, 0))],
            out_specs=pl.BlockSpec(blk, lambda i, be, nu: (i, 0)),
            scratch_shapes=[pltpu.VMEM((D_MODEL, 2 * EXPERT_FF), BF16), pltpu.VMEM((EXPERT_FF, D_MODEL), BF16)]),
        out_shape=jax.ShapeDtypeStruct(xs_rows.shape, jnp.int32),
        compiler_params=_params(("arbitrary",)),
        name="expert_ffn",
    )(block_expert, block_valid, xs_rows, w["w1"], w["w3"], w["w2"], anchor)


def _moe_out_kernel(x_ref, g_ref, p_ref, yg_ref, ws13_ref, ws2_ref, wpg_ref, wple_ref, lng_ref, lnb_ref, o_ref):
    x = x_ref[...]
    xb = x.astype(BF16)
    tm = x.shape[0]
    g = g_ref[...]
    parts = None
    for k in range(TOP_K):
        chunks = [g[:, k:k + 1] * c for c in _load_packed_chunks(yg_ref, tm, lead=k)]
        parts = chunks if parts is None else [a + c for a, c in zip(parts, chunks)]
    routed = jnp.concatenate(parts, axis=1)
    h13 = jnp.dot(xb, ws13_ref[...], preferred_element_type=F32)
    h = jax.nn.silu(h13[:, :EXPERT_FF]) * h13[:, EXPERT_FF:]
    shared = jnp.dot(h.astype(BF16), ws2_ref[...], preferred_element_type=F32)
    ple = (jax.nn.sigmoid(jnp.dot(xb, wpg_ref[...], preferred_element_type=F32))
           * jnp.dot(p_ref[...].astype(BF16), wple_ref[...], preferred_element_type=F32))
    o_ref[...] = _layer_norm(DN_ALPHA * x + routed + shared + ple, lng_ref[...], lnb_ref[...])


def _moe_out(x1, top_gates, p, yg_rows, w, tm):
    n = x1.shape[0]
    row = lambda width: pl.BlockSpec((tm, width), lambda i: (i, 0))
    return pl.pallas_call(
        _moe_out_kernel,
        grid=(n // tm,),
        in_specs=[row(D_MODEL), row(TOP_K), row(PLE_DIM),
                  pl.BlockSpec((TOP_K, tm * ROW_TILE, LANES), lambda i: (0, i, 0)),
                  _full((D_MODEL, 2 * EXPERT_FF)), _full((EXPERT_FF, D_MODEL)),
                  _full((D_MODEL, D_MODEL)), _full((PLE_DIM, D_MODEL)),
                  _full((1, D_MODEL)), _full((1, D_MODEL))],
        out_specs=row(D_MODEL),
        out_shape=jax.ShapeDtypeStruct((n, D_MODEL), F32),
        compiler_params=_params(("parallel",)),
        name="moe_out",
    )(x1, top_gates, p, yg_rows, w["ws13"], w["ws2"], w["w_ple_gate"], w["w_ple"], w["ln2_g"], w["ln2_b"])


def _moe_sorted(x1, x1_tiles, top_idx, top_gates, counts, p, w, anchor):
    n = x1.shape[0]
    n_blocks = n * TOP_K // MOE_BLOCK + N_EXPERTS
    n_rows = n_blocks * MOE_BLOCK
    cnt = counts.reshape(N_EXPERTS).astype(jnp.int32)
    padded = (cnt + MOE_BLOCK - 1) // MOE_BLOCK * MOE_BLOCK
    pend = jnp.cumsum(padded)
    pstart = (pend - padded).astype(F32).reshape(N_EXPERTS, 1)
    block_start = jnp.arange(n_blocks, dtype=jnp.int32) * MOE_BLOCK
    block_expert = jnp.minimum(jnp.sum((pend[None, :] <= block_start[:, None]).astype(jnp.int32), axis=1),
                               N_EXPERTS - 1)
    real_end = pend - padded + cnt
    own = block_expert[:, None] == jnp.arange(N_EXPERTS, dtype=jnp.int32)[None, :]
    block_valid = jnp.clip(jnp.sum(jnp.where(own, real_end[None, :], 0), axis=1) - block_start, 0, MOE_BLOCK)
    slots = _route(top_idx, pstart, 512)
    slots = jnp.transpose(slots.reshape(TOP_K, n // SC_WINDOW, SC_WINDOW), (1, 0, 2))
    xs = _sc_dispatch(x1_tiles.reshape(n, ROW_TILE, LANES), slots, n_rows)
    ys = _expert_ffn(xs.reshape(n_rows * ROW_TILE, LANES), block_expert, block_valid, w, anchor)
    yg = _sc_combine(ys.reshape(n_rows, ROW_TILE, LANES), slots, n)
    return _moe_out(x1, top_gates.T, p, yg.reshape(TOP_K, n * ROW_TILE, LANES), w, 512)


def _kv_rows(k, v, batch, seq, keep, g):
    cols = slice(g * GROUP_WIDTH, (g + 1) * GROUP_WIDTH)
    shape = (batch, keep, HEADS_PER_GROUP, HEAD_DIM)
    k_g = k.reshape(batch, seq, ATTN_WIDTH)[:, seq - keep:, cols].reshape(shape)
    v_g = v.reshape(batch, seq, ATTN_WIDTH)[:, seq - keep:, cols].reshape(shape)
    return jnp.stack([k_g, v_g], axis=2)


def _layer_prompt(x, p, w, ssm, anchor):
    batch, seq, _ = x.shape
    n = batch * seq
    x2 = x.reshape(n, D_MODEL)
    tabs = _rope_tables(jnp.arange(seq, dtype=jnp.int32))
    q, k, v, u_lo, u_hi = _in_proj(x2, w["w_in"], tabs, seq, 512)
    attn_o, attn_lse = _attn_prompt(q, k, v, batch, seq)
    zeros = jnp.zeros((batch, SSM_LANES), F32)
    y_lo, y_hi, h_re, h_im = _s5_scan(u_lo, u_hi, ssm, zeros, zeros, batch, 128)
    x1, x1_tiles, _, top_idx, top_gates, counts = _post_mixer(x2, attn_o, attn_lse, y_lo, y_hi, w, seq, 512)
    y = _moe_sorted(x1, x1_tiles, top_idx, top_gates, counts, p.reshape(n, PLE_DIM), w, anchor)
    kv = [_kv_rows(k, v, batch, seq, min(win, seq), g) for g, (win, _) in enumerate(DILATION_GROUPS)]
    h_last = jnp.stack([h_re, h_im], axis=-1).reshape(batch, SSM_GROUPS, SSM_STATE, 2)
    return y.reshape(batch, seq, D_MODEL), kv, h_last


def _sample_attention(x, caches, w):
    batch, seq, _ = x.shape
    assert seq == 1
    x2 = x.reshape(batch, D_MODEL)
    tabs = _rope_tables(jnp.full((batch,), PAST_LEN, dtype=jnp.int32))
    q, k, v, u_lo, u_hi = _in_proj(x2, w["w_in"], tabs, batch, batch)
    wide = _attn_sample(q, k, v, caches, (2,), 2)
    narrow = _attn_sample(q, k, v, caches, (0, 1), 2)
    return x2, k, v, (u_lo, u_hi), narrow, wide


def _layer_sample(p, state, w, ssm, x2, k, v, u, narrow, wide):
    batch = x2.shape[0]
    n_heads = ATTN_WIDTH // HEAD_DIM
    attn_o = jnp.transpose(jnp.concatenate([narrow[0], wide[0]], axis=2), (0, 2, 1)).reshape(batch, ATTN_WIDTH)
    attn_lse = jnp.broadcast_to(jnp.concatenate([narrow[1], wide[1]], axis=1),
                                (batch, n_heads, HEAD_DIM)).reshape(batch, ATTN_WIDTH)
    h0 = state.reshape(batch, SSM_LANES, 2)
    y_lo, y_hi, h_re, h_im = _s5_scan(*u, ssm, h0[..., 0], h0[..., 1], batch, 1)
    x1, _, gates, _, _, _ = _post_mixer(x2, attn_o, attn_lse, y_lo, y_hi, w, batch, batch)
    y = _moe_ffn(x1, gates.T, p.reshape(batch, PLE_DIM), w, batch)
    kv = [_kv_rows(k, v, batch, 1, 1, g) for g in range(len(DILATION_GROUPS))]
    h_last = jnp.stack([h_re, h_im], axis=-1).reshape(batch, SSM_GROUPS, SSM_STATE, 2)
    return y.reshape(batch, 1, D_MODEL), kv, h_last


def _hi_lo_rows(t):
    hi = t.astype(BF16)
    return jnp.concatenate([hi, (t - hi.astype(F32)).astype(BF16)], axis=1).T


def kernel(x_prompt, x_sample, cache_kv_w128, cache_kv_w512, cache_kv_w2048, state_ssm, p_prompt, p_sample,
           w_in, a_re, a_im, log_dt, b_re, b_im, c_re, c_im, d_skip, w_glu, b_glu, w_attn_br, w_ssm_br,
           w_gate, b_gate, w_out, ln1_g, ln1_b, w_router, router_bias, w1, w3, w2, ws1, ws3, ws2,
           w_ple_gate, w_ple, ln2_g, ln2_b):
    assert w_in.shape[0] == DEPTH == 1
    l = 0
    row = lambda t: t[l].reshape(1, -1)
    w = {
        "w_in": w_in[l].astype(BF16),
        "w_glu": w_glu[l].astype(BF16), "b_glu": row(b_glu),
        "w_gate": w_gate[l].astype(BF16), "b_gate": row(b_gate),
        "w_attn_br": w_attn_br[l].astype(BF16), "w_ssm_br": w_ssm_br[l].astype(BF16),
        "w_out": w_out[l].astype(BF16), "ln1_g": row(ln1_g), "ln1_b": row(ln1_b),
        "w_router": _hi_lo_rows(w_router[l]), "router_bias": router_bias[l].reshape(N_EXPERTS, 1),
        "w1": w1[l], "w3": w3[l], "w2": w2[l],
        "ws13": jnp.concatenate([ws1[l], ws3[l]], axis=-1).astype(BF16), "ws2": ws2[l].astype(BF16),
        "w_ple_gate": w_ple_gate[l].astype(BF16), "w_ple": w_ple[l].astype(BF16),
        "ln2_g": row(ln2_g), "ln2_b": row(ln2_b),
    }
    ssm = _s5_params(a_re[l], a_im[l], log_dt[l], b_re[l], b_im[l], c_re[l], c_im[l], d_skip[l])
    caches = (cache_kv_w128[l], cache_kv_w512[l], cache_kv_w2048[l])
    sample = _sample_attention(x_sample, caches, w)
    anchor = jnp.broadcast_to(sample[-1][1][:8, 0, :], (8, LANES))
    yp, kv_p, h_p = _layer_prompt(x_prompt, p_prompt[l], w, ssm, anchor)
    ys, kv_s, h_s = _layer_sample(p_sample[l], state_ssm[l], w, ssm, *sample)
    return (yp, ys, kv_p[0][None], kv_s[0][None], kv_p[1][None], kv_s[1][None],
            kv_p[2][None], kv_s[2][None], h_p[None], h_s[None])
```

```python
import functools

import jax
import jax.numpy as jnp
from jax import lax
from jax.experimental import pallas as pl
from jax.experimental.pallas import tpu as pltpu
from jax.experimental.pallas import tpu_sc as plsc

F32 = jnp.float32
BF16 = jnp.bfloat16

D_MODEL = 1024
HEAD_DIM = 64
HEADS_PER_GROUP = 4
DILATION_GROUPS = ((128, 1), (512, 4), (2048, 16))
N_BACK = 128
GROUP_WIDTH = HEADS_PER_GROUP * HEAD_DIM
ATTN_WIDTH = 3 * GROUP_WIDTH
ROPE_THETA = 10000.0
SSM_WIDTH = 256
SSM_GROUP = 16
SSM_GROUPS = 16
SSM_STATE = 64
SSM_LANES = SSM_GROUPS * SSM_STATE
IN_WIDTH = 3 * ATTN_WIDTH + SSM_WIDTH
N_EXPERTS = 64
TOP_K = 8
EXPERT_FF = 256
ROUTED_SCALE = 2.5
PLE_DIM = 256
DEPTH = 1
PAST_LEN = 8192
DN_ALPHA = (2.0 * DEPTH) ** 0.25
LN_EPS = 1e-5

LANES = 128
ROW_TILE = D_MODEL // LANES // 2
SC_CORES = 2
SC_SUBCORES = 16
SC_WINDOW = 64
MOE_BLOCK = 1024
POST_MIXER_SUBTILES = 2
EXPERT_FFN_SUBTILES = 4
ATTN_CHUNK = 2048
ATTN_UNROLL = 8
VMEM_LIMIT = 56 * 1024 * 1024


def _params(semantics):
    return pltpu.CompilerParams(dimension_semantics=semantics, vmem_limit_bytes=VMEM_LIMIT)


def _full(shape):
    return pl.BlockSpec(shape, lambda *_: (0,) * len(shape))


def _interleaved_rows(seq, rows, n_seq, row0=0):
    start = row0 * n_seq + seq
    return pl.ds(start, rows) if n_seq == 1 else pl.ds(start, rows, stride=n_seq)


def _in_proj_kernel(x_ref, w_ref, cos_ref, sina_ref, sinb_ref, q_ref, k_ref, v_ref, ulo_ref, uhi_ref, *, n_seq):
    xb = x_ref[...].astype(BF16)
    cos = cos_ref[...]
    sin_a = sina_ref[...]
    sin_b = sinb_ref[...]

    def rope_store(col0, out_ref, scale):
        t = jnp.dot(xb, w_ref[:, col0:col0 + ATTN_WIDTH], preferred_element_type=F32)
        for c in range(ATTN_WIDTH // LANES):
            xc = t[:, c * LANES:(c + 1) * LANES]
            r = xc * cos + pltpu.roll(xc, LANES - 32, 1) * sin_a + pltpu.roll(xc, 32, 1) * sin_b
            out_ref[:, c * LANES:(c + 1) * LANES] = r * scale if scale != 1.0 else r

    rope_store(0, q_ref, HEAD_DIM ** -0.5)
    rope_store(ATTN_WIDTH, k_ref, 1.0)
    v_ref[...] = jnp.dot(xb, w_ref[:, 2 * ATTN_WIDTH:3 * ATTN_WIDTH], preferred_element_type=F32)
    u = jnp.dot(xb, w_ref[:, 3 * ATTN_WIDTH:], preferred_element_type=F32)
    rows = _interleaved_rows(pl.program_id(1), u.shape[0], n_seq)
    ulo_ref[rows, :] = u[:, :LANES]
    uhi_ref[rows, :] = u[:, LANES:]


def _in_proj(x, w_in_bf, rope_tabs, rows_per_seq, tm):
    n = x.shape[0]
    tiles_per_seq = rows_per_seq // tm
    n_seq = n // rows_per_seq
    tab_tiles = rope_tabs[0].shape[0] // tm
    tab_spec = pl.BlockSpec((tm, LANES), lambda t, s: (t % tab_tiles, 0))
    row_spec = pl.BlockSpec((tm, ATTN_WIDTH), lambda t, s: (s * tiles_per_seq + t, 0))
    u_spec = pl.BlockSpec((tm * n_seq, LANES), lambda t, s: (t, 0))
    return pl.pallas_call(
        functools.partial(_in_proj_kernel, n_seq=n_seq),
        grid=(tiles_per_seq, n_seq),
        in_specs=[pl.BlockSpec((tm, D_MODEL), lambda t, s: (s * tiles_per_seq + t, 0)),
                  _full((D_MODEL, IN_WIDTH)), tab_spec, tab_spec, tab_spec],
        out_specs=[row_spec, row_spec, row_spec, u_spec, u_spec],
        out_shape=[jax.ShapeDtypeStruct((n, ATTN_WIDTH), F32)] * 3 + [jax.ShapeDtypeStruct((n, LANES), F32)] * 2,
        compiler_params=_params(("parallel", "arbitrary")),
        name="in_proj",
    )(x, w_in_bf, *rope_tabs)


def _rope_tables(pos):
    half = HEAD_DIM // 2
    inv = ROPE_THETA ** (-jnp.arange(half, dtype=F32) / half)
    ang = pos.astype(F32)[:, None] * inv[None, :]
    cos = jnp.tile(jnp.cos(ang), (1, LANES // half))
    sin = jnp.tile(jnp.sin(ang), (1, LANES // half))
    first_half = (jnp.arange(LANES) % HEAD_DIM) < half
    sin_a = jnp.where(first_half[None, :], -sin, 0.0)
    sin_b = jnp.where(first_half[None, :], 0.0, sin)
    return cos, sin_a, sin_b


def _band_attention(q, k, v, mask):
    head_of_lane = lax.broadcasted_iota(jnp.int32, (N_BACK, LANES), 1) // HEAD_DIM
    kb = k.astype(BF16)
    vb = v.astype(BF16)
    o = lse = None
    for h in range(LANES // HEAD_DIM):
        qh = jnp.where(head_of_lane == h, q, 0.0).astype(BF16)
        logits = lax.dot_general(qh, kb, (((1,), (1,)), ((), ())), preferred_element_type=F32) + mask
        m = jnp.max(logits, axis=1, keepdims=True)
        p = jnp.exp(logits - m)
        l = jnp.sum(p, axis=1, keepdims=True)
        o_h = jnp.dot(p.astype(BF16), vb, preferred_element_type=F32) * (1.0 / l)
        lse_h = jnp.broadcast_to(m + jnp.log(l), (N_BACK, LANES))
        o = o_h if o is None else jnp.where(head_of_lane == h, o_h, o)
        lse = lse_h if lse is None else jnp.where(head_of_lane == h, lse_h, lse)
    return o, lse


def _attn_prompt_kernel(q_ref, kp_ref, kc_ref, vp_ref, vc_ref, o_ref, lse_ref):
    c = pl.program_id(1)
    g = pl.program_id(3)
    ch = ATTN_CHUNK
    qi = lax.broadcasted_iota(jnp.int32, (N_BACK, 2 * N_BACK), 0)
    kj = lax.broadcasted_iota(jnp.int32, (N_BACK, 2 * N_BACK), 1)
    dist = qi + N_BACK - kj
    band = jnp.where(dist >= 0, jnp.where(dist <= N_BACK, 0.0, -jnp.inf), -jnp.inf)
    band_first = jnp.where(kj >= N_BACK, band, -jnp.inf)

    def group_body(d):
        span = N_BACK * d
        n_sub = ch // N_BACK

        def rows(start, size):
            return pl.ds(start, size) if d == 1 else pl.ds(start, size, stride=d)

        def store(q0, o, lse):
            o_ref[rows(q0, N_BACK), :] = o
            lse_ref[rows(q0, N_BACK), :] = lse

        def head_block(r, carry):
            k = jnp.concatenate([kp_ref[rows(ch - span + r, N_BACK), :], kc_ref[rows(r, N_BACK), :]], axis=0)
            v = jnp.concatenate([vp_ref[rows(ch - span + r, N_BACK), :], vc_ref[rows(r, N_BACK), :]], axis=0)
            mask = jnp.where(c == 0, band_first, band)
            store(r, *_band_attention(q_ref[rows(r, N_BACK), :], k, v, mask))
            return carry

        def inner_block(idx, carry):
            s = idx // d
            r = idx % d
            k0 = (s - 1) * span + r
            store(s * span + r, *_band_attention(q_ref[rows(s * span + r, N_BACK), :],
                                                 kc_ref[rows(k0, 2 * N_BACK), :],
                                                 vc_ref[rows(k0, 2 * N_BACK), :], band))
            return carry

        lax.fori_loop(0, d, head_block, 0, unroll=min(d, ATTN_UNROLL))
        if n_sub > d:
            trips = n_sub - d
            lax.fori_loop(d, n_sub, inner_block, 0,
                          unroll=max(u for u in range(1, ATTN_UNROLL + 1) if trips % u == 0))

    for gi, (_, d) in enumerate(DILATION_GROUPS):
        pl.when(g == gi)(functools.partial(group_body, d))


def _attn_prompt(q, k, v, batch, seq):
    ch = ATTN_CHUNK
    cps = seq // ch
    n = batch * seq
    pairs = GROUP_WIDTH // LANES
    cur = lambda b, c, hp, g: (b * cps + c, g * pairs + hp)
    prev = lambda b, c, hp, g: (b * cps + jnp.maximum(c - 1, 0), g * pairs + hp)
    blk = (ch, LANES)
    return pl.pallas_call(
        _attn_prompt_kernel,
        grid=(batch, cps, pairs, len(DILATION_GROUPS)),
        in_specs=[pl.BlockSpec(blk, cur), pl.BlockSpec(blk, prev), pl.BlockSpec(blk, cur),
                  pl.BlockSpec(blk, prev), pl.BlockSpec(blk, cur)],
        out_specs=[pl.BlockSpec(blk, cur), pl.BlockSpec(blk, cur)],
        out_shape=[jax.ShapeDtypeStruct((n, ATTN_WIDTH), F32)] * 2,
        compiler_params=_params(("parallel", "parallel", "parallel", "parallel")),
        name="attn_prompt",
    )(q, k, k, v, v)


def _attn_sample_kernel(q_ref, k_ref, v_ref, *refs, groups):
    *cache_refs, o_ref, lse_ref = refs
    bt = q_ref.shape[0]
    for b in range(bt):
        for slot, (g, c_ref) in enumerate(zip(groups, cache_refs)):
            win, d = DILATION_GROUPS[g]
            pos = lax.broadcasted_iota(jnp.int32, (1, win), 1)
            off_stride = (pos % d) != 0
            j0 = g * HEADS_PER_GROUP
            out0 = slot * HEADS_PER_GROUP
            heads = range(HEADS_PER_GROUP)
            qs = [q_ref[b, :, j0 + h:j0 + h + 1] for h in heads]
            s_c = jnp.concatenate([jnp.sum(c_ref[b, 0, h] * qs[h], axis=0, keepdims=True) for h in heads],
                                  axis=0)
            s_c = jnp.where(off_stride, -jnp.inf, s_c)
            s_new = jnp.concatenate([jnp.sum(k_ref[b, :, j0 + h:j0 + h + 1] * qs[h], axis=0, keepdims=True)
                                     for h in heads], axis=0)
            m = jnp.maximum(jnp.max(s_c, axis=1, keepdims=True), s_new)
            p_c = jnp.exp(s_c - m)
            p_new = jnp.exp(s_new - m)
            l = jnp.sum(p_c, axis=1, keepdims=True) + p_new
            inv_l = 1.0 / l
            lse_ref[b, out0:out0 + HEADS_PER_GROUP, :] = m + jnp.log(l)
            for h in heads:
                num = (jnp.sum(c_ref[b, 1, h] * p_c[h:h + 1, :], axis=1, keepdims=True)
                       + p_new[h:h + 1, :] * v_ref[b, :, j0 + h:j0 + h + 1])
                o_ref[b, :, out0 + h:out0 + h + 1] = num * inv_l[h:h + 1, :]


def _attn_sample(q, k, v, caches, groups, bt):
    b = q.shape[0]
    n_heads = ATTN_WIDTH // HEAD_DIM
    out_heads = HEADS_PER_GROUP * len(groups)
    views, specs = [], []
    for g in groups:
        win, d = DILATION_GROUPS[g]
        assert caches[g].shape[1] == win == N_BACK * d
        views.append(jnp.transpose(caches[g], (0, 2, 3, 4, 1)))
        specs.append(pl.BlockSpec((bt, 2, HEADS_PER_GROUP, HEAD_DIM, win), lambda i: (i, 0, 0, 0, 0)))
    col_spec = pl.BlockSpec((bt, HEAD_DIM, n_heads), lambda i: (i, 0, 0))
    cols = lambda t: jnp.transpose(t.reshape(b, n_heads, HEAD_DIM), (0, 2, 1))
    return pl.pallas_call(
        functools.partial(_attn_sample_kernel, groups=groups),
        grid=(b // bt,),
        in_specs=[col_spec, col_spec, col_spec] + specs,
        out_specs=[pl.BlockSpec((bt, HEAD_DIM, out_heads), lambda i: (i, 0, 0)),
                   pl.BlockSpec((bt, out_heads, 1), lambda i: (i, 0, 0))],
        out_shape=[jax.ShapeDtypeStruct((b, HEAD_DIM, out_heads), F32),
                   jax.ShapeDtypeStruct((b, out_heads, 1), F32)],
        compiler_params=_params(("parallel",)),
        name="attn_sample",
    )(cols(q), cols(k), cols(v), *views)


def _s5_scan_kernel(ulo_ref, uhi_ref, bmat_ref, cmat_ref, are_ref, aim_ref, d_ref, h0re_ref, h0im_ref,
                    ylo_ref, yhi_ref, hre_ref, him_ref, hist_sc, *, bg, steps):
    t_chunk = pl.program_id(0)

    @pl.when(t_chunk == 0)
    def _():
        hre_ref[...] = h0re_ref[...]
        him_ref[...] = h0im_ref[...]

    u = jnp.concatenate([ulo_ref[...], uhi_ref[...]], axis=1)
    hist_sc[...] = jnp.dot(u.astype(BF16), bmat_ref[...], preferred_element_type=F32)
    a_re = jnp.broadcast_to(are_ref[...], (bg, SSM_LANES))
    a_im = jnp.broadcast_to(aim_ref[...], (bg, SSM_LANES))

    def step(t, carry):
        h_re, h_im = carry
        rows = pl.ds(pl.multiple_of(t * bg, bg), bg)
        n_re = a_re * h_re - a_im * h_im + hist_sc[rows, 0:SSM_LANES]
        n_im = a_re * h_im + a_im * h_re + hist_sc[rows, SSM_LANES:2 * SSM_LANES]
        hist_sc[rows, 0:SSM_LANES] = n_re
        hist_sc[rows, SSM_LANES:2 * SSM_LANES] = n_im
        return n_re, n_im

    h_re, h_im = lax.fori_loop(0, steps, step, (hre_ref[...], him_ref[...]))
    hre_ref[...] = h_re
    him_ref[...] = h_im
    y = jnp.dot(hist_sc[...].astype(BF16), cmat_ref[...], preferred_element_type=F32) + d_ref[...] * u
    ylo_ref[...] = y[:, :LANES]
    yhi_ref[...] = y[:, LANES:]


def _s5_scan(u_lo, u_hi, ssm, h0_re, h0_im, bg, steps):
    rows = u_lo.shape[0]
    blk = steps * bg
    kern = functools.partial(_s5_scan_kernel, bg=bg, steps=steps)
    state_spec = _full((bg, SSM_LANES))
    half_spec = pl.BlockSpec((blk, LANES), lambda i: (i, 0))
    return pl.pallas_call(
        kern,
        grid=(rows // blk,),
        in_specs=[half_spec, half_spec,
                  _full((SSM_WIDTH, 2 * SSM_LANES)), _full((2 * SSM_LANES, SSM_WIDTH)),
                  _full((1, SSM_LANES)), _full((1, SSM_LANES)), _full((1, SSM_WIDTH)),
                  state_spec, state_spec],
        out_specs=[half_spec, half_spec, state_spec, state_spec],
        out_shape=[jax.ShapeDtypeStruct((rows, LANES), F32), jax.ShapeDtypeStruct((rows, LANES), F32),
                   jax.ShapeDtypeStruct((bg, SSM_LANES), F32), jax.ShapeDtypeStruct((bg, SSM_LANES), F32)],
        scratch_shapes=[pltpu.VMEM((blk, 2 * SSM_LANES), F32)],
        compiler_params=_params(("arbitrary",)),
        name="s5_scan",
    )(u_lo, u_hi, ssm["bmat"], ssm["cmat"], ssm["a_re"], ssm["a_im"], ssm["d_skip"], h0_re, h0_im)


def _s5_params(a_re, a_im, log_dt, b_re, b_im, c_re, c_im, d_skip):
    dt = jnp.exp(log_dt)[:, None]
    mag = jnp.exp(a_re * dt)
    abar_re = mag * jnp.cos(a_im * dt)
    abar_im = mag * jnp.sin(a_im * dt)
    a2 = a_re * a_re + a_im * a_im
    nr = abar_re - 1.0
    coef_re = (nr * a_re + abar_im * a_im) / a2
    coef_im = (abar_im * a_re - nr * a_im) / a2
    bb_re = coef_re[..., None] * b_re - coef_im[..., None] * b_im
    bb_im = coef_re[..., None] * b_im + coef_im[..., None] * b_re
    eye = jnp.eye(SSM_GROUPS, dtype=F32)
    to_b = lambda t: jnp.einsum("gpc,gh->gchp", t, eye).reshape(SSM_WIDTH, SSM_LANES)
    to_c = lambda t: jnp.einsum("gcp,gh->gphc", t, eye).reshape(SSM_LANES, SSM_WIDTH)
    return {
        "bmat": jnp.concatenate([to_b(bb_re), to_b(bb_im)], axis=1).astype(BF16),
        "cmat": jnp.concatenate([to_c(c_re), -to_c(c_im)], axis=0).astype(BF16),
        "a_re": abar_re.reshape(1, SSM_LANES), "a_im": abar_im.reshape(1, SSM_LANES),
        "d_skip": d_skip.reshape(1, SSM_WIDTH),
    }


def _layer_norm(z, g, b):
    mu = jnp.mean(z, axis=-1, keepdims=True)
    zc = z - mu
    var = jnp.mean(zc * zc, axis=-1, keepdims=True)
    return zc * lax.rsqrt(var + LN_EPS) * g + b


def _merge_groups(o, lse):
    parts = [slice(g * GROUP_WIDTH, (g + 1) * GROUP_WIDTH) for g in range(len(DILATION_GROUPS))]
    top = lse[:, parts[0]]
    for cols in parts[1:]:
        top = jnp.maximum(top, lse[:, cols])
    num = den = None
    for cols in parts:
        w = jnp.exp(lse[:, cols] - top)
        num = w * o[:, cols] if num is None else num + w * o[:, cols]
        den = w if den is None else den + w
    return num / den


def _store_packed_rows(ref, x, row0=0):
    rows = x.shape[0]
    for j in range(ROW_TILE):
        lo = x[:, j * LANES:(j + 1) * LANES].astype(BF16).astype(F32)
        hi = x[:, (j + ROW_TILE) * LANES:(j + ROW_TILE + 1) * LANES].astype(BF16).astype(F32)
        word = (lax.bitcast_convert_type(lo, jnp.uint32) >> 16) | lax.bitcast_convert_type(hi, jnp.uint32)
        ref[pl.ds(row0 * ROW_TILE + j, rows, stride=ROW_TILE), :] = lax.bitcast_convert_type(word, jnp.int32)


def _load_packed_chunks(ref, rows, lead=None, row0=0):
    lows, highs = [], []
    for j in range(ROW_TILE):
        idx = (pl.ds(row0 * ROW_TILE + j, rows, stride=ROW_TILE), slice(None))
        word = lax.bitcast_convert_type(ref[idx] if lead is None else ref[(lead,) + idx], jnp.uint32)
        lows.append(lax.bitcast_convert_type(word << 16, F32))
        highs.append(lax.bitcast_convert_type(word & jnp.uint32(0xFFFF0000), F32))
    return lows + highs


def _post_mixer_kernel(x_ref, ao_ref, lse_ref, ylo_ref, yhi_ref, wglu_ref, bglu_ref, wgate_ref, bgate_ref,
                       wab_ref, wsb_ref, wout_ref, lng_ref, lnb_ref, wr_ref, rb_ref,
                       x1_ref, x1t_ref, gate_ref, idx_ref, topg_ref, cnt_ref, *, n_seq):
    seq = pl.program_id(1)

    @pl.when(jnp.logical_and(pl.program_id(0) == 0, seq == 0))
    def _():
        cnt_ref[...] = jnp.zeros(cnt_ref.shape, F32)

    tm = x_ref.shape[0]
    sub = tm // POST_MIXER_SUBTILES if tm % (8 * POST_MIXER_SUBTILES) == 0 else tm
    for r0 in range(0, tm, sub):
        rows = slice(r0, r0 + sub)
        x = x_ref[rows, :]
        xb = x.astype(BF16)
        y_rows = _interleaved_rows(seq, sub, n_seq, r0)
        s = jax.nn.gelu(jnp.concatenate([ylo_ref[y_rows, :], yhi_ref[y_rows, :]], axis=1))
        s = s * jax.nn.sigmoid(jnp.dot(s.astype(BF16), wglu_ref[...], preferred_element_type=F32) + bglu_ref[...])
        gates = jax.nn.sigmoid(jnp.dot(xb, wgate_ref[...], preferred_element_type=F32) + bgate_ref[...])
        attn_o = _merge_groups(ao_ref[rows, :], lse_ref[rows, :])
        attn_br = jnp.dot(attn_o.astype(BF16), wab_ref[...], preferred_element_type=F32)
        ssm_br = jnp.dot(s.astype(BF16), wsb_ref[...], preferred_element_type=F32)
        merged = gates[:, :D_MODEL] * attn_br + gates[:, D_MODEL:] * ssm_br
        mix = jnp.dot(merged.astype(BF16), wout_ref[...], preferred_element_type=F32)
        x1 = _layer_norm(DN_ALPHA * x + mix, lng_ref[...], lnb_ref[...])
        x1_ref[rows, :] = x1
        _store_packed_rows(x1t_ref, x1, r0)

        x1_hi = x1.astype(BF16)
        x1_lo = (x1 - x1_hi.astype(F32)).astype(BF16)
        prod = lax.dot_general(wr_ref[...], jnp.concatenate([x1_hi, x1_lo], axis=0),
                               (((1,), (1,)), ((), ())), preferred_element_type=F32)
        logits = ((prod[:N_EXPERTS, :sub] + prod[N_EXPERTS:, :sub] + prod[:N_EXPERTS, sub:])
                  + prod[N_EXPERTS:, sub:])
        scores = jax.nn.sigmoid(logits)
        sel = scores + rb_ref[...]
        expert = lax.broadcasted_iota(jnp.int32, sel.shape, 0).astype(F32)
        chosen = jnp.zeros(sel.shape, F32)
        idx_rows, score_rows = [], []
        for _ in range(TOP_K):
            top = jnp.max(sel, axis=0, keepdims=True)
            first = jnp.min(jnp.where(sel == top, expert, float(N_EXPERTS)), axis=0, keepdims=True)
            hit = expert == first
            chosen = jnp.where(hit, 1.0, chosen)
            sel = jnp.where(hit, -jnp.inf, sel)
            idx_rows.append(first)
            score_rows.append(jnp.sum(jnp.where(hit, scores, 0.0), axis=0, keepdims=True))
        norm = ROUTED_SCALE / jnp.sum(scores * chosen, axis=0, keepdims=True)
        gate_ref[:, rows] = scores * chosen * norm
        idx_ref[:, rows] = jnp.concatenate(idx_rows, axis=0)
        topg_ref[:, rows] = jnp.concatenate(score_rows, axis=0) * norm
        cnt_ref[...] += jnp.sum(chosen, axis=1, keepdims=True)


def _post_mixer(x, attn_o, attn_lse, y_lo, y_hi, w, rows_per_seq, tm):
    n = x.shape[0]
    tiles_per_seq = rows_per_seq // tm
    n_seq = n // rows_per_seq
    row = lambda width: pl.BlockSpec((tm, width), lambda t, s: (s * tiles_per_seq + t, 0))
    col = lambda height: pl.BlockSpec((height, tm), lambda t, s: (0, s * tiles_per_seq + t))
    y_spec = pl.BlockSpec((tm * n_seq, LANES), lambda t, s: (t, 0))
    return pl.pallas_call(
        functools.partial(_post_mixer_kernel, n_seq=n_seq),
        grid=(tiles_per_seq, n_seq),
        in_specs=[row(D_MODEL), row(ATTN_WIDTH), row(ATTN_WIDTH), y_spec, y_spec,
                  _full((SSM_WIDTH, SSM_WIDTH)), _full((1, SSM_WIDTH)),
                  _full((D_MODEL, 2 * D_MODEL)), _full((1, 2 * D_MODEL)),
                  _full((GROUP_WIDTH, D_MODEL)), _full((SSM_WIDTH, D_MODEL)), _full((D_MODEL, D_MODEL)),
                  _full((1, D_MODEL)), _full((1, D_MODEL)),
                  _full((2 * N_EXPERTS, D_MODEL)), _full((N_EXPERTS, 1))],
        out_specs=[row(D_MODEL),
                   pl.BlockSpec((tm * ROW_TILE, LANES), lambda t, s: (s * tiles_per_seq + t, 0)),
                   col(N_EXPERTS), col(TOP_K), col(TOP_K), _full((N_EXPERTS, 1))],
        out_shape=[jax.ShapeDtypeStruct((n, D_MODEL), F32), jax.ShapeDtypeStruct((n * ROW_TILE, LANES), jnp.int32),
                   jax.ShapeDtypeStruct((N_EXPERTS, n), F32), jax.ShapeDtypeStruct((TOP_K, n), F32),
                   jax.ShapeDtypeStruct((TOP_K, n), F32), jax.ShapeDtypeStruct((N_EXPERTS, 1), F32)],
        compiler_params=_params(("arbitrary", "arbitrary")),
        name="post_mixer",
    )(x, attn_o, attn_lse, y_lo, y_hi, w["w_glu"], w["b_glu"], w["w_gate"], w["b_gate"], w["w_attn_br"], w["w_ssm_br"],
      w["w_out"], w["ln1_g"], w["ln1_b"], w["w_router"], w["router_bias"])


def _moe_ffn_kernel(x_ref, gate_ref, p_ref, w1_ref, w3_ref, w2_ref, ws13_ref, ws2_ref, wpg_ref, wple_ref,
                    lng_ref, lnb_ref, o_ref, acc_sc, xb_sc):
    e = pl.program_id(1)

    def glu_ffn(xb, w13, w2, row_scale):
        if isinstance(w13, tuple):
            h1 = jnp.dot(xb, w13[0].astype(BF16), preferred_element_type=F32)
            h3 = jnp.dot(xb, w13[1].astype(BF16), preferred_element_type=F32)
        else:
            h13 = jnp.dot(xb, w13, preferred_element_type=F32)
            h1, h3 = h13[:, :EXPERT_FF], h13[:, EXPERT_FF:]
        h = jax.nn.silu(h1) * h3
        if row_scale is not None:
            h = h * row_scale
        return jnp.dot(h.astype(BF16), w2.astype(BF16), preferred_element_type=F32)

    @pl.when(e == 0)
    def _():
        xb = x_ref[...].astype(BF16)
        xb_sc[...] = xb
        ple = (jax.nn.sigmoid(jnp.dot(xb, wpg_ref[...], preferred_element_type=F32))
               * jnp.dot(p_ref[...].astype(BF16), wple_ref[...], preferred_element_type=F32))
        acc_sc[...] = glu_ffn(xb, ws13_ref[...], ws2_ref[...], None) + ple

    gates = gate_ref[...]
    lane = lax.broadcasted_iota(jnp.int32, gates.shape, 1)
    g_col = jnp.sum(jnp.where(lane == e, gates, 0.0), axis=-1, keepdims=True)
    acc_sc[...] += glu_ffn(xb_sc[...], (w1_ref[0], w3_ref[0]), w2_ref[0], g_col)

    @pl.when(e == N_EXPERTS - 1)
    def _():
        o_ref[...] = _layer_norm(DN_ALPHA * x_ref[...] + acc_sc[...], lng_ref[...], lnb_ref[...])


def _moe_ffn(x1, gates, p, w, tm):
    n = x1.shape[0]
    row = lambda width: pl.BlockSpec((tm, width), lambda i, e: (i, 0))
    return pl.pallas_call(
        _moe_ffn_kernel,
        grid=(n // tm, N_EXPERTS),
        in_specs=[row(D_MODEL), row(N_EXPERTS), row(PLE_DIM),
                  pl.BlockSpec((1, D_MODEL, EXPERT_FF), lambda i, e: (e, 0, 0)),
                  pl.BlockSpec((1, D_MODEL, EXPERT_FF), lambda i, e: (e, 0, 0)),
                  pl.BlockSpec((1, EXPERT_FF, D_MODEL), lambda i, e: (e, 0, 0)),
                  _full((D_MODEL, 2 * EXPERT_FF)), _full((EXPERT_FF, D_MODEL)),
                  _full((D_MODEL, D_MODEL)), _full((PLE_DIM, D_MODEL)),
                  _full((1, D_MODEL)), _full((1, D_MODEL))],
        out_specs=row(D_MODEL),
        out_shape=jax.ShapeDtypeStruct((n, D_MODEL), F32),
        scratch_shapes=[pltpu.VMEM((tm, D_MODEL), F32), pltpu.VMEM((tm, D_MODEL), BF16)],
        compiler_params=_params(("parallel", "arbitrary")),
        name="moe_ffn",
    )(x1, gates, p, w["w1"], w["w3"], w["w2"], w["ws13"], w["ws2"], w["w_ple_gate"], w["w_ple"],
      w["ln2_g"], w["ln2_b"])


def _route_kernel(idx_ref, pstart_ref, earlier_ref, slot_ref, base_sc):
    @pl.when(pl.program_id(0) == 0)
    def _():
        base_sc[...] = jnp.zeros(base_sc.shape, F32)

    idx = idx_ref[...]
    tm = idx.shape[1]
    expert = lax.broadcasted_iota(jnp.int32, (N_EXPERTS, tm), 0).astype(F32)
    hits = [expert == idx[k:k + 1, :] for k in range(TOP_K)]
    member = jnp.zeros((N_EXPERTS, tm), F32)
    for hit in hits:
        member = member + jnp.where(hit, 1.0, 0.0)
    row = (jnp.dot(member.astype(BF16), earlier_ref[...], preferred_element_type=F32)
           + base_sc[...] + pstart_ref[...])
    slots = [jnp.sum(jnp.where(hit, row, 0.0), axis=0, keepdims=True) for hit in hits]
    slot_ref[...] = jnp.concatenate(slots, axis=0).astype(jnp.int32)
    base_sc[...] += jnp.sum(member, axis=1, keepdims=True)


def _route(top_idx, pstart, tm):
    n = top_idx.shape[1]
    earlier = jnp.triu(jnp.ones((tm, tm), F32), k=1).astype(BF16)
    return pl.pallas_call(
        _route_kernel,
        grid=(n // tm,),
        in_specs=[pl.BlockSpec((TOP_K, tm), lambda i: (0, i)), _full((N_EXPERTS, 1)), _full((tm, tm))],
        out_specs=pl.BlockSpec((TOP_K, tm), lambda i: (0, i)),
        out_shape=jax.ShapeDtypeStruct((TOP_K, n), jnp.int32),
        scratch_shapes=[pltpu.VMEM((N_EXPERTS, 1), F32)],
        compiler_params=_params(("arbitrary",)),
        name="route",
    )(top_idx, pstart, earlier)


def _sc_mesh():
    return plsc.VectorSubcoreMesh(core_axis_name="c", subcore_axis_name="s",
                                  num_cores=SC_CORES, num_subcores=SC_SUBCORES)


def _sc_dispatch(x_tiles, slots, n_rows):
    n = x_tiles.shape[0]
    wins_per_worker = n // SC_WINDOW // (SC_CORES * SC_SUBCORES)

    def body(x_hbm, slot_hbm, xs_hbm, idx_v, rows_v, sem):
        wid = lax.axis_index("s") * SC_CORES + lax.axis_index("c")

        @pl.loop(0, wins_per_worker)
        def _(i):
            win = wid * wins_per_worker + i
            pltpu.sync_copy(slot_hbm.at[win], idx_v)
            pltpu.sync_copy(x_hbm.at[pl.ds(win * SC_WINDOW, SC_WINDOW)], rows_v)
            copies = [pltpu.async_copy(rows_v, xs_hbm.at[idx_v.at[k]], sem) for k in range(TOP_K)]
            for copy in copies:
                copy.wait()

    return pl.kernel(
        body, out_type=jax.ShapeDtypeStruct((n_rows, ROW_TILE, LANES), jnp.int32), mesh=_sc_mesh(),
        scratch_types=[pltpu.VMEM((TOP_K, SC_WINDOW), jnp.int32),
                       pltpu.VMEM((SC_WINDOW, ROW_TILE, LANES), jnp.int32),
                       pltpu.SemaphoreType.DMA],
        name="sc_dispatch",
    )(x_tiles, slots)


def _sc_combine(y_tiles, slots, n):
    wins_per_worker = n // SC_WINDOW // (SC_CORES * SC_SUBCORES)

    def body(ys_hbm, slot_hbm, yg_hbm, idx_v, rows_a, rows_b, gather_sems, write_sems):
        wid = lax.axis_index("s") * SC_CORES + lax.axis_index("c")
        bufs = (rows_a, rows_b)

        @pl.loop(0, wins_per_worker)
        def _(i):
            win = wid * wins_per_worker + i
            pltpu.sync_copy(slot_hbm.at[win], idx_v)

            def gather(k):
                return pltpu.async_copy(ys_hbm.at[idx_v.at[k]], bufs[k % 2], gather_sems.at[k % 2])

            def write(k):
                return pltpu.async_copy(bufs[k % 2], yg_hbm.at[k, pl.ds(win * SC_WINDOW, SC_WINDOW)],
                                        write_sems.at[k % 2])

            gathers = {0: gather(0)}
            writes = {}
            for k in range(TOP_K):
                gathers[k].wait()
                if k + 1 < TOP_K:
                    if k >= 1:
                        writes[k - 1].wait()
                    gathers[k + 1] = gather(k + 1)
                writes[k] = write(k)
            writes[TOP_K - 2].wait()
            writes[TOP_K - 1].wait()

    return pl.kernel(
        body, out_type=jax.ShapeDtypeStruct((TOP_K, n, ROW_TILE, LANES), jnp.int32), mesh=_sc_mesh(),
        scratch_types=[pltpu.VMEM((TOP_K, SC_WINDOW), jnp.int32),
                       pltpu.VMEM((SC_WINDOW, ROW_TILE, LANES), jnp.int32),
                       pltpu.VMEM((SC_WINDOW, ROW_TILE, LANES), jnp.int32),
                       pltpu.SemaphoreType.DMA((2,)), pltpu.SemaphoreType.DMA((2,))],
        name="sc_combine",
    )(y_tiles, slots)


def _expert_ffn_kernel(bexp_ref, valid_ref, xs_ref, w1_ref, w3_ref, w2_ref, anchor_ref, ys_ref, w13_sc, w2_sc):
    del anchor_ref
    i = pl.program_id(0)
    valid = valid_ref[i]

    @pl.when(jnp.logical_or(i == 0, bexp_ref[i] != bexp_ref[jnp.maximum(i - 1, 0)]))
    def _():
        w13_sc[:, :EXPERT_FF] = w1_ref[0].astype(BF16)
        w13_sc[:, EXPERT_FF:] = w3_ref[0].astype(BF16)
        w2_sc[...] = w2_ref[0].astype(BF16)

    def ffn(r0, rows):
        x = jnp.concatenate(_load_packed_chunks(xs_ref, rows, row0=r0), axis=1)
        h13 = jnp.dot(x.astype(BF16), w13_sc[...], preferred_element_type=F32)
        h = jax.nn.silu(h13[:, :EXPERT_FF]) * h13[:, EXPERT_FF:]
        _store_packed_rows(ys_ref, jnp.dot(h.astype(BF16), w2_sc[...], preferred_element_type=F32), r0)

    @pl.when(valid == MOE_BLOCK)
    def _():
        ffn(0, MOE_BLOCK)

    @pl.when(jnp.logical_and(valid > 0, valid < MOE_BLOCK))
    def _():
        sub = MOE_BLOCK // EXPERT_FFN_SUBTILES
        for r0 in range(0, MOE_BLOCK, sub):
            pl.when(valid > r0)(functools.partial(ffn, r0, sub))


def _expert_ffn(xs_rows, block_expert, block_valid, w, anchor):
    n_blocks = block_expert.shape[0]
    blk = (MOE_BLOCK * ROW_TILE, LANES)
    return pl.pallas_call(
        _expert_ffn_kernel,
        grid_spec=pltpu.PrefetchScalarGridSpec(
            num_scalar_prefetch=2, grid=(n_blocks,),
            in_specs=[pl.BlockSpec(blk, lambda i, be, nu: (i, 0)),
                      pl.BlockSpec((1, D_MODEL, EXPERT_FF), lambda i, be, nu: (be[i], 0, 0)),
                      pl.BlockSpec((1, D_MODEL, EXPERT_FF), lambda i, be, nu: (be[i], 0, 0)),
                      pl.BlockSpec((1, EXPERT_FF, D_MODEL), lambda i, be, nu: (be[i], 0, 0)),
                      pl.BlockSpec((8, LANES), lambda i, be, nu: (0, 0))],
            out_specs=pl.BlockSpec(blk, lambda i, be, nu: (i, 0)),
            scratch_shapes=[pltpu.VMEM((D_MODEL, 2 * EXPERT_FF), BF16), pltpu.VMEM((EXPERT_FF, D_MODEL), BF16)]),
        out_shape=jax.ShapeDtypeStruct(xs_rows.shape, jnp.int32),
        compiler_params=_params(("arbitrary",)),
        name="expert_ffn",
    )(block_expert, block_valid, xs_rows, w["w1"], w["w3"], w["w2"], anchor)


def _moe_out_kernel(x_ref, g_ref, p_ref, yg_ref, ws13_ref, ws2_ref, wpg_ref, wple_ref, lng_ref, lnb_ref, o_ref):
    x = x_ref[...]
    xb = x.astype(BF16)
    tm = x.shape[0]
    g = g_ref[...]
    parts = None
    for k in range(TOP_K):
        chunks = [g[:, k:k + 1] * c for c in _load_packed_chunks(yg_ref, tm, lead=k)]
        parts = chunks if parts is None else [a + c for a, c in zip(parts, chunks)]
    routed = jnp.concatenate(parts, axis=1)
    h13 = jnp.dot(xb, ws13_ref[...], preferred_element_type=F32)
    h = jax.nn.silu(h13[:, :EXPERT_FF]) * h13[:, EXPERT_FF:]
    shared = jnp.dot(h.astype(BF16), ws2_ref[...], preferred_element_type=F32)
    ple = (jax.nn.sigmoid(jnp.dot(xb, wpg_ref[...], preferred_element_type=F32))
           * jnp.dot(p_ref[...].astype(BF16), wple_ref[...], preferred_element_type=F32))
    o_ref[...] = _layer_norm(DN_ALPHA * x + routed + shared + ple, lng_ref[...], lnb_ref[...])


def _moe_out(x1, top_gates, p, yg_rows, w, tm):
    n = x1.shape[0]
    row = lambda width: pl.BlockSpec((tm, width), lambda i: (i, 0))
    return pl.pallas_call(
        _moe_out_kernel,
        grid=(n // tm,),
        in_specs=[row(D_MODEL), row(TOP_K), row(PLE_DIM),
                  pl.BlockSpec((TOP_K, tm * ROW_TILE, LANES), lambda i: (0, i, 0)),
                  _full((D_MODEL, 2 * EXPERT_FF)), _full((EXPERT_FF, D_MODEL)),
                  _full((D_MODEL, D_MODEL)), _full((PLE_DIM, D_MODEL)),
                  _full((1, D_MODEL)), _full((1, D_MODEL))],
        out_specs=row(D_MODEL),
        out_shape=jax.ShapeDtypeStruct((n, D_MODEL), F32),
        compiler_params=_params(("parallel",)),
        name="moe_out",
    )(x1, top_gates, p, yg_rows, w["ws13"], w["ws2"], w["w_ple_gate"], w["w_ple"], w["ln2_g"], w["ln2_b"])


def _moe_sorted(x1, x1_tiles, top_idx, top_gates, counts, p, w, anchor):
    n = x1.shape[0]
    n_blocks = n * TOP_K // MOE_BLOCK + N_EXPERTS
    n_rows = n_blocks * MOE_BLOCK
    cnt = counts.reshape(N_EXPERTS).astype(jnp.int32)
    padded = (cnt + MOE_BLOCK - 1) // MOE_BLOCK * MOE_BLOCK
    pend = jnp.cumsum(padded)
    pstart = (pend - padded).astype(F32).reshape(N_EXPERTS, 1)
    block_start = jnp.arange(n_blocks, dtype=jnp.int32) * MOE_BLOCK
    block_expert = jnp.minimum(jnp.sum((pend[None, :] <= block_start[:, None]).astype(jnp.int32), axis=1),
                               N_EXPERTS - 1)
    real_end = pend - padded + cnt
    own = block_expert[:, None] == jnp.arange(N_EXPERTS, dtype=jnp.int32)[None, :]
    block_valid = jnp.clip(jnp.sum(jnp.where(own, real_end[None, :], 0), axis=1) - block_start, 0, MOE_BLOCK)
    slots = _route(top_idx, pstart, 512)
    slots = jnp.transpose(slots.reshape(TOP_K, n // SC_WINDOW, SC_WINDOW), (1, 0, 2))
    xs = _sc_dispatch(x1_tiles.reshape(n, ROW_TILE, LANES), slots, n_rows)
    ys = _expert_ffn(xs.reshape(n_rows * ROW_TILE, LANES), block_expert, block_valid, w, anchor)
    yg = _sc_combine(ys.reshape(n_rows, ROW_TILE, LANES), slots, n)
    return _moe_out(x1, top_gates.T, p, yg.reshape(TOP_K, n * ROW_TILE, LANES), w, 512)


def _kv_rows(k, v, batch, seq, keep, g):
    cols = slice(g * GROUP_WIDTH, (g + 1) * GROUP_WIDTH)
    shape = (batch, keep, HEADS_PER_GROUP, HEAD_DIM)
    k_g = k.reshape(batch, seq, ATTN_WIDTH)[:, seq - keep:, cols].reshape(shape)
    v_g = v.reshape(batch, seq, ATTN_WIDTH)[:, seq - keep:, cols].reshape(shape)
    return jnp.stack([k_g, v_g], axis=2)


def _layer_prompt(x, p, w, ssm, anchor):
    batch, seq, _ = x.shape
    n = batch * seq
    x2 = x.reshape(n, D_MODEL)
    tabs = _rope_tables(jnp.arange(seq, dtype=jnp.int32))
    q, k, v, u_lo, u_hi = _in_proj(x2, w["w_in"], tabs, seq, 512)
    attn_o, attn_lse = _attn_prompt(q, k, v, batch, seq)
    zeros = jnp.zeros((batch, SSM_LANES), F32)
    y_lo, y_hi, h_re, h_im = _s5_scan(u_lo, u_hi, ssm, zeros, zeros, batch, 128)
    x1, x1_tiles, _, top_idx, top_gates, counts = _post_mixer(x2, attn_o, attn_lse, y_lo, y_hi, w, seq, 512)
    y = _moe_sorted(x1, x1_tiles, top_idx, top_gates, counts, p.reshape(n, PLE_DIM), w, anchor)
    kv = [_kv_rows(k, v, batch, seq, min(win, seq), g) for g, (win, _) in enumerate(DILATION_GROUPS)]
    h_last = jnp.stack([h_re, h_im], axis=-1).reshape(batch, SSM_GROUPS, SSM_STATE, 2)
    return y.reshape(batch, seq, D_MODEL), kv, h_last


def _sample_attention(x, caches, w):
    batch, seq, _ = x.shape
    assert seq == 1
    x2 = x.reshape(batch, D_MODEL)
    tabs = _rope_tables(jnp.full((batch,), PAST_LEN, dtype=jnp.int32))
    q, k, v, u_lo, u_hi = _in_proj(x2, w["w_in"], tabs, batch, batch)
    wide = _attn_sample(q, k, v, caches, (2,), 2)
    narrow = _attn_sample(q, k, v, caches, (0, 1), 2)
    return x2, k, v, (u_lo, u_hi), narrow, wide


def _layer_sample(p, state, w, ssm, x2, k, v, u, narrow, wide):
    batch = x2.shape[0]
    n_heads = ATTN_WIDTH // HEAD_DIM
    attn_o = jnp.transpose(jnp.concatenate([narrow[0], wide[0]], axis=2), (0, 2, 1)).reshape(batch, ATTN_WIDTH)
    attn_lse = jnp.broadcast_to(jnp.concatenate([narrow[1], wide[1]], axis=1),
                                (batch, n_heads, HEAD_DIM)).reshape(batch, ATTN_WIDTH)
    h0 = state.reshape(batch, SSM_LANES, 2)
    y_lo, y_hi, h_re, h_im = _s5_scan(*u, ssm, h0[..., 0], h0[..., 1], batch, 1)
    x1, _, gates, _, _, _ = _post_mixer(x2, attn_o, attn_lse, y_lo, y_hi, w, batch, batch)
    y = _moe_ffn(x1, gates.T, p.reshape(batch, PLE_DIM), w, batch)
    kv = [_kv_rows(k, v, batch, 1, 1, g) for g in range(len(DILATION_GROUPS))]
    h_last = jnp.stack([h_re, h_im], axis=-1).reshape(batch, SSM_GROUPS, SSM_STATE, 2)
    return y.reshape(batch, 1, D_MODEL), kv, h_last


def _hi_lo_rows(t):
    hi = t.astype(BF16)
    return jnp.concatenate([hi, (t - hi.astype(F32)).astype(BF16)], axis=1).T


def kernel(x_prompt, x_sample, cache_kv_w128, cache_kv_w512, cache_kv_w2048, state_ssm, p_prompt, p_sample,
           w_in, a_re, a_im, log_dt, b_re, b_im, c_re, c_im, d_skip, w_glu, b_glu, w_attn_br, w_ssm_br,
           w_gate, b_gate, w_out, ln1_g, ln1_b, w_router, router_bias, w1, w3, w2, ws1, ws3, ws2,
           w_ple_gate, w_ple, ln2_g, ln2_b):
    assert w_in.shape[0] == DEPTH == 1
    l = 0
    row = lambda t: t[l].reshape(1, -1)
    w = {
        "w_in": w_in[l].astype(BF16),
        "w_glu": w_glu[l].astype(BF16), "b_glu": row(b_glu),
        "w_gate": w_gate[l].astype(BF16), "b_gate": row(b_gate),
        "w_attn_br": w_attn_br[l].astype(BF16), "w_ssm_br": w_ssm_br[l].astype(BF16),
        "w_out": w_out[l].astype(BF16), "ln1_g": row(ln1_g), "ln1_b": row(ln1_b),
        "w_router": _hi_lo_rows(w_router[l]), "router_bias": router_bias[l].reshape(N_EXPERTS, 1),
        "w1": w1[l], "w3": w3[l], "w2": w2[l],
        "ws13": jnp.concatenate([ws1[l], ws3[l]], axis=-1).astype(BF16), "ws2": ws2[l].astype(BF16),
        "w_ple_gate": w_ple_gate[l].astype(BF16), "w_ple": w_ple[l].astype(BF16),
        "ln2_g": row(ln2_g), "ln2_b": row(ln2_b),
    }
    ssm = _s5_params(a_re[l], a_im[l], log_dt[l], b_re[l], b_im[l], c_re[l], c_im[l], d_skip[l])
    caches = (cache_kv_w128[l], cache_kv_w512[l], cache_kv_w2048[l])
    sample = _sample_attention(x_sample, caches, w)
    anchor = jnp.broadcast_to(sample[-1][1][:8, 0, :], (8, LANES))
    yp, kv_p, h_p = _layer_prompt(x_prompt, p_prompt[l], w, ssm, anchor)
    ys, kv_s, h_s = _layer_sample(p_sample[l], state_ssm[l], w, ssm, *sample)
    return (yp, ys, kv_p[0][None], kv_s[0][None], kv_p[1][None], kv_s[1][None],
            kv_p[2][None], kv_s[2][None], h_p[None], h_s[None])
```

```python
import functools

import jax
import jax.numpy as jnp
from jax import lax
from jax.experimental import pallas as pl
from jax.experimental.pallas import tpu as pltpu
from jax.experimental.pallas import tpu_sc as plsc

F32 = jnp.float32
BF16 = jnp.bfloat16

D_MODEL = 1024
HEAD_DIM = 64
HEADS_PER_GROUP = 4
DILATION_GROUPS = ((128, 1), (512, 4), (2048, 16))
N_BACK = 128
GROUP_WIDTH = HEADS_PER_GROUP * HEAD_DIM
ATTN_WIDTH = 3 * GROUP_WIDTH
ROPE_THETA = 10000.0
SSM_WIDTH = 256
SSM_GROUP = 16
SSM_GROUPS = 16
SSM_STATE = 64
SSM_LANES = SSM_GROUPS * SSM_STATE
IN_WIDTH = 3 * ATTN_WIDTH + SSM_WIDTH
N_EXPERTS = 64
TOP_K = 8
EXPERT_FF = 256
ROUTED_SCALE = 2.5
PLE_DIM = 256
DEPTH = 1
PAST_LEN = 8192
DN_ALPHA = (2.0 * DEPTH) ** 0.25
LN_EPS = 1e-5

LANES = 128
ROW_TILE = D_MODEL // LANES // 2
SC_CORES = 2
SC_SUBCORES = 16
SC_WINDOW = 64
MOE_BLOCK = 1024
POST_MIXER_SUBTILES = 2
EXPERT_FFN_SUBTILES = 4
ATTN_CHUNK = 2048
ATTN_UNROLL = 8
VMEM_LIMIT = 56 * 1024 * 1024


def _params(semantics):
    return pltpu.CompilerParams(dimension_semantics=semantics, vmem_limit_bytes=VMEM_LIMIT)


def _full(shape):
    return pl.BlockSpec(shape, lambda *_: (0,) * len(shape))


def _interleaved_rows(seq, rows, n_seq, row0=0):
    start = row0 * n_seq + seq
    return pl.ds(start, rows) if n_seq == 1 else pl.ds(start, rows, stride=n_seq)


def _in_proj_kernel(x_ref, w_ref, cos_ref, sina_ref, sinb_ref, q_ref, k_ref, v_ref, ulo_ref, uhi_ref, *, n_seq):
    xb = x_ref[...].astype(BF16)
    cos = cos_ref[...]
    sin_a = sina_ref[...]
    sin_b = sinb_ref[...]

    def rope_store(col0, out_ref, scale):
        t = jnp.dot(xb, w_ref[:, col0:col0 + ATTN_WIDTH], preferred_element_type=F32)
        for c in range(ATTN_WIDTH // LANES):
            xc = t[:, c * LANES:(c + 1) * LANES]
            r = xc * cos + pltpu.roll(xc, LANES - 32, 1) * sin_a + pltpu.roll(xc, 32, 1) * sin_b
            out_ref[:, c * LANES:(c + 1) * LANES] = r * scale if scale != 1.0 else r

    rope_store(0, q_ref, HEAD_DIM ** -0.5)
    rope_store(ATTN_WIDTH, k_ref, 1.0)
    v_ref[...] = jnp.dot(xb, w_ref[:, 2 * ATTN_WIDTH:3 * ATTN_WIDTH], preferred_element_type=F32)
    u = jnp.dot(xb, w_ref[:, 3 * ATTN_WIDTH:], preferred_element_type=F32)
    rows = _interleaved_rows(pl.program_id(1), u.shape[0], n_seq)
    ulo_ref[rows, :] = u[:, :LANES]
    uhi_ref[rows, :] = u[:, LANES:]


def _in_proj(x, w_in_bf, rope_tabs, rows_per_seq, tm):
    n = x.shape[0]
    tiles_per_seq = rows_per_seq // tm
    n_seq = n // rows_per_seq
    tab_tiles = rope_tabs[0].shape[0] // tm
    tab_spec = pl.BlockSpec((tm, LANES), lambda t, s: (t % tab_tiles, 0))
    row_spec = pl.BlockSpec((tm, ATTN_WIDTH), lambda t, s: (s * tiles_per_seq + t, 0))
    u_spec = pl.BlockSpec((tm * n_seq, LANES), lambda t, s: (t, 0))
    return pl.pallas_call(
        functools.partial(_in_proj_kernel, n_seq=n_seq),
        grid=(tiles_per_seq, n_seq),
        in_specs=[pl.BlockSpec((tm, D_MODEL), lambda t, s: (s * tiles_per_seq + t, 0)),
                  _full((D_MODEL, IN_WIDTH)), tab_spec, tab_spec, tab_spec],
        out_specs=[row_spec, row_spec, row_spec, u_spec, u_spec],
        out_shape=[jax.ShapeDtypeStruct((n, ATTN_WIDTH), F32)] * 3 + [jax.ShapeDtypeStruct((n, LANES), F32)] * 2,
        compiler_params=_params(("parallel", "arbitrary")),
        name="in_proj",
    )(x, w_in_bf, *rope_tabs)


def _rope_tables(pos):
    half = HEAD_DIM // 2
    inv = ROPE_THETA ** (-jnp.arange(half, dtype=F32) / half)
    ang = pos.astype(F32)[:, None] * inv[None, :]
    cos = jnp.tile(jnp.cos(ang), (1, LANES // half))
    sin = jnp.tile(jnp.sin(ang), (1, LANES // half))
    first_half = (jnp.arange(LANES) % HEAD_DIM) < half
    sin_a = jnp.where(first_half[None, :], -sin, 0.0)
    sin_b = jnp.where(first_half[None, :], 0.0, sin)
    return cos, sin_a, sin_b


def _band_attention(q, k, v, mask):
    head_of_lane = lax.broadcasted_iota(jnp.int32, (N_BACK, LANES), 1) // HEAD_DIM
    kb = k.astype(BF16)
    vb = v.astype(BF16)
    o = lse = None
    for h in range(LANES // HEAD_DIM):
        qh = jnp.where(head_of_lane == h, q, 0.0).astype(BF16)
        logits = lax.dot_general(qh, kb, (((1,), (1,)), ((), ())), preferred_element_type=F32) + mask
        m = jnp.max(logits, axis=1, keepdims=True)
        p = jnp.exp(logits - m)
        l = jnp.sum(p, axis=1, keepdims=True)
        o_h = jnp.dot(p.astype(BF16), vb, preferred_element_type=F32) * (1.0 / l)
        lse_h = jnp.broadcast_to(m + jnp.log(l), (N_BACK, LANES))
        o = o_h if o is None else jnp.where(head_of_lane == h, o_h, o)
        lse = lse_h if lse is None else jnp.where(head_of_lane == h, lse_h, lse)
    return o, lse


def _attn_prompt_kernel(q_ref, kp_ref, kc_ref, vp_ref, vc_ref, o_ref, lse_ref):
    c = pl.program_id(1)
    g = pl.program_id(3)
    ch = ATTN_CHUNK
    qi = lax.broadcasted_iota(jnp.int32, (N_BACK, 2 * N_BACK), 0)
    kj = lax.broadcasted_iota(jnp.int32, (N_BACK, 2 * N_BACK), 1)
    dist = qi + N_BACK - kj
    band = jnp.where(dist >= 0, jnp.where(dist <= N_BACK, 0.0, -jnp.inf), -jnp.inf)
    band_first = jnp.where(kj >= N_BACK, band, -jnp.inf)

    def group_body(d):
        span = N_BACK * d
        n_sub = ch // N_BACK

        def rows(start, size):
            return pl.ds(start, size) if d == 1 else pl.ds(start, size, stride=d)

        def store(q0, o, lse):
            o_ref[rows(q0, N_BACK), :] = o
            lse_ref[rows(q0, N_BACK), :] = lse

        def head_block(r, carry):
            k = jnp.concatenate([kp_ref[rows(ch - span + r, N_BACK), :], kc_ref[rows(r, N_BACK), :]], axis=0)
            v = jnp.concatenate([vp_ref[rows(ch - span + r, N_BACK), :], vc_ref[rows(r, N_BACK), :]], axis=0)
            mask = jnp.where(c == 0, band_first, band)
            store(r, *_band_attention(q_ref[rows(r, N_BACK), :], k, v, mask))
            return carry

        def inner_block(idx, carry):
            s = idx // d
            r = idx % d
            k0 = (s - 1) * span + r
            store(s * span + r, *_band_attention(q_ref[rows(s * span + r, N_BACK), :],
                                                 kc_ref[rows(k0, 2 * N_BACK), :],
                                                 vc_ref[rows(k0, 2 * N_BACK), :], band))
            return carry

        lax.fori_loop(0, d, head_block, 0, unroll=min(d, ATTN_UNROLL))
        if n_sub > d:
            trips = n_sub - d
            lax.fori_loop(d, n_sub, inner_block, 0,
                          unroll=max(u for u in range(1, ATTN_UNROLL + 1) if trips % u == 0))

    for gi, (_, d) in enumerate(DILATION_GROUPS):
        pl.when(g == gi)(functools.partial(group_body, d))


def _attn_prompt(q, k, v, batch, seq):
    ch = ATTN_CHUNK
    cps = seq // ch
    n = batch * seq
    pairs = GROUP_WIDTH // LANES
    cur = lambda b, c, hp, g: (b * cps + c, g * pairs + hp)
    prev = lambda b, c, hp, g: (b * cps + jnp.maximum(c - 1, 0), g * pairs + hp)
    blk = (ch, LANES)
    return pl.pallas_call(
        _attn_prompt_kernel,
        grid=(batch, cps, pairs, len(DILATION_GROUPS)),
        in_specs=[pl.BlockSpec(blk, cur), pl.BlockSpec(blk, prev), pl.BlockSpec(blk, cur),
                  pl.BlockSpec(blk, prev), pl.BlockSpec(blk, cur)],
        out_specs=[pl.BlockSpec(blk, cur), pl.BlockSpec(blk, cur)],
        out_shape=[jax.ShapeDtypeStruct((n, ATTN_WIDTH), F32)] * 2,
        compiler_params=_params(("parallel", "parallel", "parallel", "parallel")),
        name="attn_prompt",
    )(q, k, k, v, v)


def _attn_sample_kernel(q_ref, k_ref, v_ref, *refs, groups):
    *cache_refs, o_ref, lse_ref = refs
    for slot, (g, c_ref) in enumerate(zip(groups, cache_refs)):
        win, d = DILATION_GROUPS[g]
        pos = lax.broadcasted_iota(jnp.int32, (1, 1, win), 2)
        off_stride = (pos % d) != 0
        j0 = g * HEADS_PER_GROUP
        out0 = slot * HEADS_PER_GROUP
        heads = range(HEADS_PER_GROUP)
        qs = [q_ref[:, :, j0 + h:j0 + h + 1] for h in heads]
        s_c = jnp.concatenate([jnp.sum(c_ref[:, 0, h] * qs[h], axis=1, keepdims=True) for h in heads],
                              axis=1)
        s_c = jnp.where(off_stride, -jnp.inf, s_c)
        s_new = jnp.concatenate([jnp.sum(k_ref[:, :, j0 + h:j0 + h + 1] * qs[h], axis=1, keepdims=True)
                                 for h in heads], axis=1)
        m = jnp.maximum(jnp.max(s_c, axis=2, keepdims=True), s_new)
        p_c = jnp.exp(s_c - m)
        p_new = jnp.exp(s_new - m)
        l = jnp.sum(p_c, axis=2, keepdims=True) + p_new
        inv_l = 1.0 / l
        lse_ref[:, out0:out0 + HEADS_PER_GROUP, :] = m + jnp.log(l)
        for h in heads:
            num = (jnp.sum(c_ref[:, 1, h] * p_c[:, h:h + 1, :], axis=2, keepdims=True)
                   + p_new[:, h:h + 1, :] * v_ref[:, :, j0 + h:j0 + h + 1])
            o_ref[:, :, out0 + h:out0 + h + 1] = num * inv_l[:, h:h + 1, :]


def _attn_sample(q, k, v, caches, groups, bt):
    b = q.shape[0]
    n_heads = ATTN_WIDTH // HEAD_DIM
    out_heads = HEADS_PER_GROUP * len(groups)
    views, specs = [], []
    for g in groups:
        win, d = DILATION_GROUPS[g]
        assert caches[g].shape[1] == win == N_BACK * d
        views.append(jnp.transpose(caches[g], (0, 2, 3, 4, 1)))
        specs.append(pl.BlockSpec((bt, 2, HEADS_PER_GROUP, HEAD_DIM, win), lambda i: (i, 0, 0, 0, 0)))
    col_spec = pl.BlockSpec((bt, HEAD_DIM, n_heads), lambda i: (i, 0, 0))
    cols = lambda t: jnp.transpose(t.reshape(b, n_heads, HEAD_DIM), (0, 2, 1))
    return pl.pallas_call(
        functools.partial(_attn_sample_kernel, groups=groups),
        grid=(b // bt,),
        in_specs=[col_spec, col_spec, col_spec] + specs,
        out_specs=[pl.BlockSpec((bt, HEAD_DIM, out_heads), lambda i: (i, 0, 0)),
                   pl.BlockSpec((bt, out_heads, 1), lambda i: (i, 0, 0))],
        out_shape=[jax.ShapeDtypeStruct((b, HEAD_DIM, out_heads), F32),
                   jax.ShapeDtypeStruct((b, out_heads, 1), F32)],
        compiler_params=_params(("parallel",)),
        name="attn_sample",
    )(cols(q), cols(k), cols(v), *views)


def _s5_scan_kernel(ulo_ref, uhi_ref, bmat_ref, cmat_ref, are_ref, aim_ref, d_ref, h0re_ref, h0im_ref,
                    ylo_ref, yhi_ref, hre_ref, him_ref, hist_sc, *, bg, steps):
    t_chunk = pl.program_id(0)

    @pl.when(t_chunk == 0)
    def _():
        hre_ref[...] = h0re_ref[...]
        him_ref[...] = h0im_ref[...]

    u = jnp.concatenate([ulo_ref[...], uhi_ref[...]], axis=1)
    hist_sc[...] = jnp.dot(u.astype(BF16), bmat_ref[...], preferred_element_type=F32)
    a_re = jnp.broadcast_to(are_ref[...], (bg, SSM_LANES))
    a_im = jnp.broadcast_to(aim_ref[...], (bg, SSM_LANES))

    def step(t, carry):
        h_re, h_im = carry
        rows = pl.ds(pl.multiple_of(t * bg, bg), bg)
        n_re = a_re * h_re - a_im * h_im + hist_sc[rows, 0:SSM_LANES]
        n_im = a_re * h_im + a_im * h_re + hist_sc[rows, SSM_LANES:2 * SSM_LANES]
        hist_sc[rows, 0:SSM_LANES] = n_re
        hist_sc[rows, SSM_LANES:2 * SSM_LANES] = n_im
        return n_re, n_im

    h_re, h_im = lax.fori_loop(0, steps, step, (hre_ref[...], him_ref[...]))
    hre_ref[...] = h_re
    him_ref[...] = h_im
    y = jnp.dot(hist_sc[...].astype(BF16), cmat_ref[...], preferred_element_type=F32) + d_ref[...] * u
    ylo_ref[...] = y[:, :LANES]
    yhi_ref[...] = y[:, LANES:]


def _s5_scan(u_lo, u_hi, ssm, h0_re, h0_im, bg, steps):
    rows = u_lo.shape[0]
    blk = steps * bg
    kern = functools.partial(_s5_scan_kernel, bg=bg, steps=steps)
    state_spec = _full((bg, SSM_LANES))
    half_spec = pl.BlockSpec((blk, LANES), lambda i: (i, 0))
    return pl.pallas_call(
        kern,
        grid=(rows // blk,),
        in_specs=[half_spec, half_spec,
                  _full((SSM_WIDTH, 2 * SSM_LANES)), _full((2 * SSM_LANES, SSM_WIDTH)),
                  _full((1, SSM_LANES)), _full((1, SSM_LANES)), _full((1, SSM_WIDTH)),
                  state_spec, state_spec],
        out_specs=[half_spec, half_spec, state_spec, state_spec],
        out_shape=[jax.ShapeDtypeStruct((rows, LANES), F32), jax.ShapeDtypeStruct((rows, LANES), F32),
                   jax.ShapeDtypeStruct((bg, SSM_LANES), F32), jax.ShapeDtypeStruct((bg, SSM_LANES), F32)],
        scratch_shapes=[pltpu.VMEM((blk, 2 * SSM_LANES), F32)],
        compiler_params=_params(("arbitrary",)),
        name="s5_scan",
    )(u_lo, u_hi, ssm["bmat"], ssm["cmat"], ssm["a_re"], ssm["a_im"], ssm["d_skip"], h0_re, h0_im)


def _s5_params(a_re, a_im, log_dt, b_re, b_im, c_re, c_im, d_skip):
    dt = jnp.exp(log_dt)[:, None]
    mag = jnp.exp(a_re * dt)
    abar_re = mag * jnp.cos(a_im * dt)
    abar_im = mag * jnp.sin(a_im * dt)
    a2 = a_re * a_re + a_im * a_im
    nr = abar_re - 1.0
    coef_re = (nr * a_re + abar_im * a_im) / a2
    coef_im = (abar_im * a_re - nr * a_im) / a2
    bb_re = coef_re[..., None] * b_re - coef_im[..., None] * b_im
    bb_im = coef_re[..., None] * b_im + coef_im[..., None] * b_re
    eye = jnp.eye(SSM_GROUPS, dtype=F32)
    to_b = lambda t: jnp.einsum("gpc,gh->gchp", t, eye).reshape(SSM_WIDTH, SSM_LANES)
    to_c = lambda t: jnp.einsum("gcp,gh->gphc", t, eye).reshape(SSM_LANES, SSM_WIDTH)
    return {
        "bmat": jnp.concatenate([to_b(bb_re), to_b(bb_im)], axis=1).astype(BF16),
        "cmat": jnp.concatenate([to_c(c_re), -to_c(c_im)], axis=0).astype(BF16),
        "a_re": abar_re.reshape(1, SSM_LANES), "a_im": abar_im.reshape(1, SSM_LANES),
        "d_skip": d_skip.reshape(1, SSM_WIDTH),
    }


def _layer_norm(z, g, b):
    mu = jnp.mean(z, axis=-1, keepdims=True)
    zc = z - mu
    var = jnp.mean(zc * zc, axis=-1, keepdims=True)
    return zc * lax.rsqrt(var + LN_EPS) * g + b


def _merge_groups(o, lse):
    parts = [slice(g * GROUP_WIDTH, (g + 1) * GROUP_WIDTH) for g in range(len(DILATION_GROUPS))]
    top = lse[:, parts[0]]
    for cols in parts[1:]:
        top = jnp.maximum(top, lse[:, cols])
    num = den = None
    for cols in parts:
        w = jnp.exp(lse[:, cols] - top)
        num = w * o[:, cols] if num is None else num + w * o[:, cols]
        den = w if den is None else den + w
    return num / den


def _store_packed_rows(ref, x, row0=0):
    rows = x.shape[0]
    for j in range(ROW_TILE):
        lo = x[:, j * LANES:(j + 1) * LANES].astype(BF16).astype(F32)
        hi = x[:, (j + ROW_TILE) * LANES:(j + ROW_TILE + 1) * LANES].astype(BF16).astype(F32)
        word = (lax.bitcast_convert_type(lo, jnp.uint32) >> 16) | lax.bitcast_convert_type(hi, jnp.uint32)
        ref[pl.ds(row0 * ROW_TILE + j, rows, stride=ROW_TILE), :] = lax.bitcast_convert_type(word, jnp.int32)


def _load_packed_chunks(ref, rows, lead=None, row0=0):
    lows, highs = [], []
    for j in range(ROW_TILE):
        idx = (pl.ds(row0 * ROW_TILE + j, rows, stride=ROW_TILE), slice(None))
        word = lax.bitcast_convert_type(ref[idx] if lead is None else ref[(lead,) + idx], jnp.uint32)
        lows.append(lax.bitcast_convert_type(word << 16, F32))
        highs.append(lax.bitcast_convert_type(word & jnp.uint32(0xFFFF0000), F32))
    return lows + highs


def _post_mixer_kernel(x_ref, ao_ref, lse_ref, ylo_ref, yhi_ref, wglu_ref, bglu_ref, wgate_ref, bgate_ref,
                       wab_ref, wsb_ref, wout_ref, lng_ref, lnb_ref, wr_ref, rb_ref,
                       x1_ref, x1t_ref, gate_ref, idx_ref, topg_ref, cnt_ref, *, n_seq):
    seq = pl.program_id(1)

    @pl.when(jnp.logical_and(pl.program_id(0) == 0, seq == 0))
    def _():
        cnt_ref[...] = jnp.zeros(cnt_ref.shape, F32)

    tm = x_ref.shape[0]
    sub = tm // POST_MIXER_SUBTILES if tm % (8 * POST_MIXER_SUBTILES) == 0 else tm
    for r0 in range(0, tm, sub):
        rows = slice(r0, r0 + sub)
        x = x_ref[rows, :]
        xb = x.astype(BF16)
        y_rows = _interleaved_rows(seq, sub, n_seq, r0)
        s = jax.nn.gelu(jnp.concatenate([ylo_ref[y_rows, :], yhi_ref[y_rows, :]], axis=1))
        s = s * jax.nn.sigmoid(jnp.dot(s.astype(BF16), wglu_ref[...], preferred_element_type=F32) + bglu_ref[...])
        gates = jax.nn.sigmoid(jnp.dot(xb, wgate_ref[...], preferred_element_type=F32) + bgate_ref[...])
        attn_o = _merge_groups(ao_ref[rows, :], lse_ref[rows, :])
        attn_br = jnp.dot(attn_o.astype(BF16), wab_ref[...], preferred_element_type=F32)
        ssm_br = jnp.dot(s.astype(BF16), wsb_ref[...], preferred_element_type=F32)
        merged = gates[:, :D_MODEL] * attn_br + gates[:, D_MODEL:] * ssm_br
        mix = jnp.dot(merged.astype(BF16), wout_ref[...], preferred_element_type=F32)
        x1 = _layer_norm(DN_ALPHA * x + mix, lng_ref[...], lnb_ref[...])
        x1_ref[rows, :] = x1
        _store_packed_rows(x1t_ref, x1, r0)

        x1_hi = x1.astype(BF16)
        x1_lo = (x1 - x1_hi.astype(F32)).astype(BF16)
        prod = lax.dot_general(wr_ref[...], jnp.concatenate([x1_hi, x1_lo], axis=0),
                               (((1,), (1,)), ((), ())), preferred_element_type=F32)
        logits = ((prod[:N_EXPERTS, :sub] + prod[N_EXPERTS:, :sub] + prod[:N_EXPERTS, sub:])
                  + prod[N_EXPERTS:, sub:])
        scores = jax.nn.sigmoid(logits)
        sel = scores + rb_ref[...]
        expert = lax.broadcasted_iota(jnp.int32, sel.shape, 0).astype(F32)
        chosen = jnp.zeros(sel.shape, F32)
        idx_rows, score_rows = [], []
        for _ in range(TOP_K):
            top = jnp.max(sel, axis=0, keepdims=True)
            first = jnp.min(jnp.where(sel == top, expert, float(N_EXPERTS)), axis=0, keepdims=True)
            hit = expert == first
            chosen = jnp.where(hit, 1.0, chosen)
            sel = jnp.where(hit, -jnp.inf, sel)
            idx_rows.append(first)
            score_rows.append(jnp.sum(jnp.where(hit, scores, 0.0), axis=0, keepdims=True))
        norm = ROUTED_SCALE / jnp.sum(scores * chosen, axis=0, keepdims=True)
        gate_ref[:, rows] = scores * chosen * norm
        idx_ref[:, rows] = jnp.concatenate(idx_rows, axis=0)
        topg_ref[:, rows] = jnp.concatenate(score_rows, axis=0) * norm
        cnt_ref[...] += jnp.sum(chosen, axis=1, keepdims=True)


def _post_mixer(x, attn_o, attn_lse, y_lo, y_hi, w, rows_per_seq, tm):
    n = x.shape[0]
    tiles_per_seq = rows_per_seq // tm
    n_seq = n // rows_per_seq
    row = lambda width: pl.BlockSpec((tm, width), lambda t, s: (s * tiles_per_seq + t, 0))
    col = lambda height: pl.BlockSpec((height, tm), lambda t, s: (0, s * tiles_per_seq + t))
    y_spec = pl.BlockSpec((tm * n_seq, LANES), lambda t, s: (t, 0))
    return pl.pallas_call(
        functools.partial(_post_mixer_kernel, n_seq=n_seq),
        grid=(tiles_per_seq, n_seq),
        in_specs=[row(D_MODEL), row(ATTN_WIDTH), row(ATTN_WIDTH), y_spec, y_spec,
                  _full((SSM_WIDTH, SSM_WIDTH)), _full((1, SSM_WIDTH)),
                  _full((D_MODEL, 2 * D_MODEL)), _full((1, 2 * D_MODEL)),
                  _full((GROUP_WIDTH, D_MODEL)), _full((SSM_WIDTH, D_MODEL)), _full((D_MODEL, D_MODEL)),
                  _full((1, D_MODEL)), _full((1, D_MODEL)),
                  _full((2 * N_EXPERTS, D_MODEL)), _full((N_EXPERTS, 1))],
        out_specs=[row(D_MODEL),
                   pl.BlockSpec((tm * ROW_TILE, LANES), lambda t, s: (s * tiles_per_seq + t, 0)),
                   col(N_EXPERTS), col(TOP_K), col(TOP_K), _full((N_EXPERTS, 1))],
        out_shape=[jax.ShapeDtypeStruct((n, D_MODEL), F32), jax.ShapeDtypeStruct((n * ROW_TILE, LANES), jnp.int32),
                   jax.ShapeDtypeStruct((N_EXPERTS, n), F32), jax.ShapeDtypeStruct((TOP_K, n), F32),
                   jax.ShapeDtypeStruct((TOP_K, n), F32), jax.ShapeDtypeStruct((N_EXPERTS, 1), F32)],
        compiler_params=_params(("arbitrary", "arbitrary")),
        name="post_mixer",
    )(x, attn_o, attn_lse, y_lo, y_hi, w["w_glu"], w["b_glu"], w["w_gate"], w["b_gate"], w["w_attn_br"], w["w_ssm_br"],
      w["w_out"], w["ln1_g"], w["ln1_b"], w["w_router"], w["router_bias"])


def _moe_ffn_kernel(x_ref, gate_ref, p_ref, w1_ref, w3_ref, w2_ref, ws13_ref, ws2_ref, wpg_ref, wple_ref,
                    lng_ref, lnb_ref, o_ref, acc_sc, xb_sc):
    e = pl.program_id(1)

    def glu_ffn(xb, w13, w2, row_scale):
        if isinstance(w13, tuple):
            h1 = jnp.dot(xb, w13[0].astype(BF16), preferred_element_type=F32)
            h3 = jnp.dot(xb, w13[1].astype(BF16), preferred_element_type=F32)
        else:
            h13 = jnp.dot(xb, w13, preferred_element_type=F32)
            h1, h3 = h13[:, :EXPERT_FF], h13[:, EXPERT_FF:]
        h = jax.nn.silu(h1) * h3
        if row_scale is not None:
            h = h * row_scale
        return jnp.dot(h.astype(BF16), w2.astype(BF16), preferred_element_type=F32)

    @pl.when(e == 0)
    def _():
        xb = x_ref[...].astype(BF16)
        xb_sc[...] = xb
        ple = (jax.nn.sigmoid(jnp.dot(xb, wpg_ref[...], preferred_element_type=F32))
               * jnp.dot(p_ref[...].astype(BF16), wple_ref[...], preferred_element_type=F32))
        acc_sc[...] = glu_ffn(xb, ws13_ref[...], ws2_ref[...], None) + ple

    gates = gate_ref[...]
    lane = lax.broadcasted_iota(jnp.int32, gates.shape, 1)
    g_col = jnp.sum(jnp.where(lane == e, gates, 0.0), axis=-1, keepdims=True)
    acc_sc[...] += glu_ffn(xb_sc[...], (w1_ref[0], w3_ref[0]), w2_ref[0], g_col)

    @pl.when(e == N_EXPERTS - 1)
    def _():
        o_ref[...] = _layer_norm(DN_ALPHA * x_ref[...] + acc_sc[...], lng_ref[...], lnb_ref[...])


def _moe_ffn(x1, gates, p, w, tm):
    n = x1.shape[0]
    row = lambda width: pl.BlockSpec((tm, width), lambda i, e: (i, 0))
    return pl.pallas_call(
        _moe_ffn_kernel,
        grid=(n // tm, N_EXPERTS),
        in_specs=[row(D_MODEL), row(N_EXPERTS), row(PLE_DIM),
                  pl.BlockSpec((1, D_MODEL, EXPERT_FF), lambda i, e: (e, 0, 0)),
                  pl.BlockSpec((1, D_MODEL, EXPERT_FF), lambda i, e: (e, 0, 0)),
                  pl.BlockSpec((1, EXPERT_FF, D_MODEL), lambda i, e: (e, 0, 0)),
                  _full((D_MODEL, 2 * EXPERT_FF)), _full((EXPERT_FF, D_MODEL)),
                  _full((D_MODEL, D_MODEL)), _full((PLE_DIM, D_MODEL)),
                  _full((1, D_MODEL)), _full((1, D_MODEL))],
        out_specs=row(D_MODEL),
        out_shape=jax.ShapeDtypeStruct((n, D_MODEL), F32),
        scratch_shapes=[pltpu.VMEM((tm, D_MODEL), F32), pltpu.VMEM((tm, D_MODEL), BF16)],
        compiler_params=_params(("parallel", "arbitrary")),
        name="moe_ffn",
    )(x1, gates, p, w["w1"], w["w3"], w["w2"], w["ws13"], w["ws2"], w["w_ple_gate"], w["w_ple"],
      w["ln2_g"], w["ln2_b"])


def _route_kernel(idx_ref, pstart_ref, earlier_ref, slot_ref, base_sc):
    @pl.when(pl.program_id(0) == 0)
    def _():
        base_sc[...] = jnp.zeros(base_sc.shape, F32)

    idx = idx_ref[...]
    tm = idx.shape[1]
    expert = lax.broadcasted_iota(jnp.int32, (N_EXPERTS, tm), 0).astype(F32)
    hits = [expert == idx[k:k + 1, :] for k in range(TOP_K)]
    member = jnp.zeros((N_EXPERTS, tm), F32)
    for hit in hits:
        member = member + jnp.where(hit, 1.0, 0.0)
    row = (jnp.dot(member.astype(BF16), earlier_ref[...], preferred_element_type=F32)
           + base_sc[...] + pstart_ref[...])
    slots = [jnp.sum(jnp.where(hit, row, 0.0), axis=0, keepdims=True) for hit in hits]
    slot_ref[...] = jnp.concatenate(slots, axis=0).astype(jnp.int32)
    base_sc[...] += jnp.sum(member, axis=1, keepdims=True)


def _route(top_idx, pstart, tm):
    n = top_idx.shape[1]
    earlier = jnp.triu(jnp.ones((tm, tm), F32), k=1).astype(BF16)
    return pl.pallas_call(
        _route_kernel,
        grid=(n // tm,),
        in_specs=[pl.BlockSpec((TOP_K, tm), lambda i: (0, i)), _full((N_EXPERTS, 1)), _full((tm, tm))],
        out_specs=pl.BlockSpec((TOP_K, tm), lambda i: (0, i)),
        out_shape=jax.ShapeDtypeStruct((TOP_K, n), jnp.int32),
        scratch_shapes=[pltpu.VMEM((N_EXPERTS, 1), F32)],
        compiler_params=_params(("arbitrary",)),
        name="route",
    )(top_idx, pstart, earlier)


def _sc_mesh():
    return plsc.VectorSubcoreMesh(core_axis_name="c", subcore_axis_name="s",
                                  num_cores=SC_CORES, num_subcores=SC_SUBCORES)


def _sc_dispatch(x_tiles, slots, n_rows):
    n = x_tiles.shape[0]
    wins_per_worker = n // SC_WINDOW // (SC_CORES * SC_SUBCORES)

    def body(x_hbm, slot_hbm, xs_hbm, idx_v, rows_v, sem):
        wid = lax.axis_index("s") * SC_CORES + lax.axis_index("c")

        @pl.loop(0, wins_per_worker)
        def _(i):
            win = wid * wins_per_worker + i
            pltpu.sync_copy(slot_hbm.at[win], idx_v)
            pltpu.sync_copy(x_hbm.at[pl.ds(win * SC_WINDOW, SC_WINDOW)], rows_v)
            copies = [pltpu.async_copy(rows_v, xs_hbm.at[idx_v.at[k]], sem) for k in range(TOP_K)]
            for copy in copies:
                copy.wait()

    return pl.kernel(
        body, out_type=jax.ShapeDtypeStruct((n_rows, ROW_TILE, LANES), jnp.int32), mesh=_sc_mesh(),
        scratch_types=[pltpu.VMEM((TOP_K, SC_WINDOW), jnp.int32),
                       pltpu.VMEM((SC_WINDOW, ROW_TILE, LANES), jnp.int32),
                       pltpu.SemaphoreType.DMA],
        name="sc_dispatch",
    )(x_tiles, slots)


def _sc_combine(y_tiles, slots, n):
    wins_per_worker = n // SC_WINDOW // (SC_CORES * SC_SUBCORES)

    def body(ys_hbm, slot_hbm, yg_hbm, idx_v, rows_a, rows_b, gather_sems, write_sems):
        wid = lax.axis_index("s") * SC_CORES + lax.axis_index("c")
        bufs = (rows_a, rows_b)

        @pl.loop(0, wins_per_worker)
        def _(i):
            win = wid * wins_per_worker + i
            pltpu.sync_copy(slot_hbm.at[win], idx_v)

            def gather(k):
                return pltpu.async_copy(ys_hbm.at[idx_v.at[k]], bufs[k % 2], gather_sems.at[k % 2])

            def write(k):
                return pltpu.async_copy(bufs[k % 2], yg_hbm.at[k, pl.ds(win * SC_WINDOW, SC_WINDOW)],
                                        write_sems.at[k % 2])

            gathers = {0: gather(0)}
            writes = {}
            for k in range(TOP_K):
                gathers[k].wait()
                if k + 1 < TOP_K:
                    if k >= 1:
                        writes[k - 1].wait()
                    gathers[k + 1] = gather(k + 1)
                writes[k] = write(k)
            writes[TOP_K - 2].wait()
            writes[TOP_K - 1].wait()

    return pl.kernel(
        body, out_type=jax.ShapeDtypeStruct((TOP_K, n, ROW_TILE, LANES), jnp.int32), mesh=_sc_mesh(),
        scratch_types=[pltpu.VMEM((TOP_K, SC_WINDOW), jnp.int32),
                       pltpu.VMEM((SC_WINDOW, ROW_TILE, LANES), jnp.int32),
                       pltpu.VMEM((SC_WINDOW, ROW_TILE, LANES), jnp.int32),
                       pltpu.SemaphoreType.DMA((2,)), pltpu.SemaphoreType.DMA((2,))],
        name="sc_combine",
    )(y_tiles, slots)


def _expert_ffn_kernel(bexp_ref, valid_ref, xs_ref, w1_ref, w3_ref, w2_ref, anchor_ref, ys_ref, w13_sc, w2_sc):
    del anchor_ref
    i = pl.program_id(0)
    valid = valid_ref[i]

    @pl.when(jnp.logical_or(i == 0, bexp_ref[i] != bexp_ref[jnp.maximum(i - 1, 0)]))
    def _():
        w13_sc[:, :EXPERT_FF] = w1_ref[0].astype(BF16)
        w13_sc[:, EXPERT_FF:] = w3_ref[0].astype(BF16)
        w2_sc[...] = w2_ref[0].astype(BF16)

    def ffn(r0, rows):
        x = jnp.concatenate(_load_packed_chunks(xs_ref, rows, row0=r0), axis=1)
        h13 = jnp.dot(x.astype(BF16), w13_sc[...], preferred_element_type=F32)
        h = jax.nn.silu(h13[:, :EXPERT_FF]) * h13[:, EXPERT_FF:]
        _store_packed_rows(ys_ref, jnp.dot(h.astype(BF16), w2_sc[...], preferred_element_type=F32), r0)

    @pl.when(valid == MOE_BLOCK)
    def _():
        ffn(0, MOE_BLOCK)

    @pl.when(jnp.logical_and(valid > 0, valid < MOE_BLOCK))
    def _():
        sub = MOE_BLOCK // EXPERT_FFN_SUBTILES
        for r0 in range(0, MOE_BLOCK, sub):
            pl.when(valid > r0)(functools.partial(ffn, r0, sub))


def _expert_ffn(xs_rows, block_expert, block_valid, w, anchor):
    n_blocks = block_expert.shape[0]
    blk = (MOE_BLOCK * ROW_TILE, LANES)
    return pl.pallas_call(
        _expert_ffn_kernel,
        grid_spec=pltpu.PrefetchScalarGridSpec(
            num_scalar_prefetch=2, grid=(n_blocks,),
            in_specs=[pl.BlockSpec(blk, lambda i, be, nu: (i, 0)),
                      pl.BlockSpec((1, D_MODEL, EXPERT_FF), lambda i, be, nu: (be[i], 0, 0)),
                      pl.BlockSpec((1, D_MODEL, EXPERT_FF), lambda i, be, nu: (be[i], 0, 0)),
                      pl.BlockSpec((1, EXPERT_FF, D_MODEL), lambda i, be, nu: (be[i], 0, 0)),
                      pl.BlockSpec((8, LANES), lambda i, be, nu: (0, 0))],
            out_specs=pl.BlockSpec(blk, lambda i, be, nu: (i, 0)),
            scratch_shapes=[pltpu.VMEM((D_MODEL, 2 * EXPERT_FF), BF16), pltpu.VMEM((EXPERT_FF, D_MODEL), BF16)]),
        out_shape=jax.ShapeDtypeStruct(xs_rows.shape, jnp.int32),
        compiler_params=_params(("arbitrary",)),
        name="expert_ffn",
    )(block_expert, block_valid, xs_rows, w["w1"], w["w3"], w["w2"], anchor)


def _moe_out_kernel(x_ref, g_ref, p_ref, yg_ref, ws13_ref, ws2_ref, wpg_ref, wple_ref, lng_ref, lnb_ref, o_ref):
    x = x_ref[...]
    xb = x.astype(BF16)
    tm = x.shape[0]
    g = g_ref[...]
    parts = None
    for k in range(TOP_K):
        chunks = [g[:, k:k + 1] * c for c in _load_packed_chunks(yg_ref, tm, lead=k)]
        parts = chunks if parts is None else [a + c for a, c in zip(parts, chunks)]
    routed = jnp.concatenate(parts, axis=1)
    h13 = jnp.dot(xb, ws13_ref[...], preferred_element_type=F32)
    h = jax.nn.silu(h13[:, :EXPERT_FF]) * h13[:, EXPERT_FF:]
    shared = jnp.dot(h.astype(BF16), ws2_ref[...], preferred_element_type=F32)
    ple = (jax.nn.sigmoid(jnp.dot(xb, wpg_ref[...], preferred_element_type=F32))
           * jnp.dot(p_ref[...].astype(BF16), wple_ref[...], preferred_element_type=F32))
    o_ref[...] = _layer_norm(DN_ALPHA * x + routed + shared + ple, lng_ref[...], lnb_ref[...])


def _moe_out(x1, top_gates, p, yg_rows, w, tm):
    n = x1.shape[0]
    row = lambda width: pl.BlockSpec((tm, width), lambda i: (i, 0))
    return pl.pallas_call(
        _moe_out_kernel,
        grid=(n // tm,),
        in_specs=[row(D_MODEL), row(TOP_K), row(PLE_DIM),
                  pl.BlockSpec((TOP_K, tm * ROW_TILE, LANES), lambda i: (0, i, 0)),
                  _full((D_MODEL, 2 * EXPERT_FF)), _full((EXPERT_FF, D_MODEL)),
                  _full((D_MODEL, D_MODEL)), _full((PLE_DIM, D_MODEL)),
                  _full((1, D_MODEL)), _full((1, D_MODEL))],
        out_specs=row(D_MODEL),
        out_shape=jax.ShapeDtypeStruct((n, D_MODEL), F32),
        compiler_params=_params(("parallel",)),
        name="moe_out",
    )(x1, top_gates, p, yg_rows, w["ws13"], w["ws2"], w["w_ple_gate"], w["w_ple"], w["ln2_g"], w["ln2_b"])


def _moe_sorted(x1, x1_tiles, top_idx, top_gates, counts, p, w, anchor):
    n = x1.shape[0]
    n_blocks = n * TOP_K // MOE_BLOCK + N_EXPERTS
    n_rows = n_blocks * MOE_BLOCK
    cnt = counts.reshape(N_EXPERTS).astype(jnp.int32)
    padded = (cnt + MOE_BLOCK - 1) // MOE_BLOCK * MOE_BLOCK
    pend = jnp.cumsum(padded)
    pstart = (pend - padded).astype(F32).reshape(N_EXPERTS, 1)
    block_start = jnp.arange(n_blocks, dtype=jnp.int32) * MOE_BLOCK
    block_expert = jnp.minimum(jnp.sum((pend[None, :] <= block_start[:, None]).astype(jnp.int32), axis=1),
                               N_EXPERTS - 1)
    real_end = pend - padded + cnt
    own = block_expert[:, None] == jnp.arange(N_EXPERTS, dtype=jnp.int32)[None, :]
    block_valid = jnp.clip(jnp.sum(jnp.where(own, real_end[None, :], 0), axis=1) - block_start, 0, MOE_BLOCK)
    slots = _route(top_idx, pstart, 512)
    slots = jnp.transpose(slots.reshape(TOP_K, n // SC_WINDOW, SC_WINDOW), (1, 0, 2))
    xs = _sc_dispatch(x1_tiles.reshape(n, ROW_TILE, LANES), slots, n_rows)
    ys = _expert_ffn(xs.reshape(n_rows * ROW_TILE, LANES), block_expert, block_valid, w, anchor)
    yg = _sc_combine(ys.reshape(n_rows, ROW_TILE, LANES), slots, n)
    return _moe_out(x1, top_gates.T, p, yg.reshape(TOP_K, n * ROW_TILE, LANES), w, 512)


def _kv_rows(k, v, batch, seq, keep, g):
    cols = slice(g * GROUP_WIDTH, (g + 1) * GROUP_WIDTH)
    shape = (batch, keep, HEADS_PER_GROUP, HEAD_DIM)
    k_g = k.reshape(batch, seq, ATTN_WIDTH)[:, seq - keep:, cols].reshape(shape)
    v_g = v.reshape(batch, seq, ATTN_WIDTH)[:, seq - keep:, cols].reshape(shape)
    return jnp.stack([k_g, v_g], axis=2)


def _layer_prompt(x, p, w, ssm, anchor):
    batch, seq, _ = x.shape
    n = batch * seq
    x2 = x.reshape(n, D_MODEL)
    tabs = _rope_tables(jnp.arange(seq, dtype=jnp.int32))
    q, k, v, u_lo, u_hi = _in_proj(x2, w["w_in"], tabs, seq, 512)
    attn_o, attn_lse = _attn_prompt(q, k, v, batch, seq)
    zeros = jnp.zeros((batch, SSM_LANES), F32)
    y_lo, y_hi, h_re, h_im = _s5_scan(u_lo, u_hi, ssm, zeros, zeros, batch, 128)
    x1, x1_tiles, _, top_idx, top_gates, counts = _post_mixer(x2, attn_o, attn_lse, y_lo, y_hi, w, seq, 512)
    y = _moe_sorted(x1, x1_tiles, top_idx, top_gates, counts, p.reshape(n, PLE_DIM), w, anchor)
    kv = [_kv_rows(k, v, batch, seq, min(win, seq), g) for g, (win, _) in enumerate(DILATION_GROUPS)]
    h_last = jnp.stack([h_re, h_im], axis=-1).reshape(batch, SSM_GROUPS, SSM_STATE, 2)
    return y.reshape(batch, seq, D_MODEL), kv, h_last


def _sample_attention(x, caches, w):
    batch, seq, _ = x.shape
    assert seq == 1
    x2 = x.reshape(batch, D_MODEL)
    tabs = _rope_tables(jnp.full((batch,), PAST_LEN, dtype=jnp.int32))
    q, k, v, u_lo, u_hi = _in_proj(x2, w["w_in"], tabs, batch, batch)
    wide = _attn_sample(q, k, v, caches, (2,), 2)
    narrow = _attn_sample(q, k, v, caches, (0, 1), 8)
    return x2, k, v, (u_lo, u_hi), narrow, wide


def _layer_sample(p, state, w, ssm, x2, k, v, u, narrow, wide):
    batch = x2.shape[0]
    n_heads = ATTN_WIDTH // HEAD_DIM
    attn_o = jnp.transpose(jnp.concatenate([narrow[0], wide[0]], axis=2), (0, 2, 1)).reshape(batch, ATTN_WIDTH)
    attn_lse = jnp.broadcast_to(jnp.concatenate([narrow[1], wide[1]], axis=1),
                                (batch, n_heads, HEAD_DIM)).reshape(batch, ATTN_WIDTH)
    h0 = state.reshape(batch, SSM_LANES, 2)
    y_lo, y_hi, h_re, h_im = _s5_scan(*u, ssm, h0[..., 0], h0[..., 1], batch, 1)
    x1, _, gates, _, _, _ = _post_mixer(x2, attn_o, attn_lse, y_lo, y_hi, w, batch, batch)
    y = _moe_ffn(x1, gates.T, p.reshape(batch, PLE_DIM), w, batch)
    kv = [_kv_rows(k, v, batch, 1, 1, g) for g in range(len(DILATION_GROUPS))]
    h_last = jnp.stack([h_re, h_im], axis=-1).reshape(batch, SSM_GROUPS, SSM_STATE, 2)
    return y.reshape(batch, 1, D_MODEL), kv, h_last


def _hi_lo_rows(t):
    hi = t.astype(BF16)
    return jnp.concatenate([hi, (t - hi.astype(F32)).astype(BF16)], axis=1).T


def kernel(x_prompt, x_sample, cache_kv_w128, cache_kv_w512, cache_kv_w2048, state_ssm, p_prompt, p_sample,
           w_in, a_re, a_im, log_dt, b_re, b_im, c_re, c_im, d_skip, w_glu, b_glu, w_attn_br, w_ssm_br,
           w_gate, b_gate, w_out, ln1_g, ln1_b, w_router, router_bias, w1, w3, w2, ws1, ws3, ws2,
           w_ple_gate, w_ple, ln2_g, ln2_b):
    assert w_in.shape[0] == DEPTH == 1
    l = 0
    row = lambda t: t[l].reshape(1, -1)
    w = {
        "w_in": w_in[l].astype(BF16),
        "w_glu": w_glu[l].astype(BF16), "b_glu": row(b_glu),
        "w_gate": w_gate[l].astype(BF16), "b_gate": row(b_gate),
        "w_attn_br": w_attn_br[l].astype(BF16), "w_ssm_br": w_ssm_br[l].astype(BF16),
        "w_out": w_out[l].astype(BF16), "ln1_g": row(ln1_g), "ln1_b": row(ln1_b),
        "w_router": _hi_lo_rows(w_router[l]), "router_bias": router_bias[l].reshape(N_EXPERTS, 1),
        "w1": w1[l], "w3": w3[l], "w2": w2[l],
        "ws13": jnp.concatenate([ws1[l], ws3[l]], axis=-1).astype(BF16), "ws2": ws2[l].astype(BF16),
        "w_ple_gate": w_ple_gate[l].astype(BF16), "w_ple": w_ple[l].astype(BF16),
        "ln2_g": row(ln2_g), "ln2_b": row(ln2_b),
    }
    ssm = _s5_params(a_re[l], a_im[l], log_dt[l], b_re[l], b_im[l], c_re[l], c_im[l], d_skip[l])
    caches = (cache_kv_w128[l], cache_kv_w512[l], cache_kv_w2048[l])
    sample = _sample_attention(x_sample, caches, w)
    anchor = jnp.broadcast_to(sample[-1][1][:8, 0, :], (8, LANES))
    yp, kv_p, h_p = _layer_prompt(x_prompt, p_prompt[l], w, ssm, anchor)
    ys, kv_s, h_s = _layer_sample(p_sample[l], state_ssm[l], w, ssm, *sample)
    return (yp, ys, kv_p[0][None], kv_s[0][None], kv_p[1][None], kv_s[1][None],
            kv_p[2][None], kv_s[2][None], h_p[None], h_s[None])
```
